```python
import math
import jax, jax.numpy as jnp
from jax import lax
import numpy as np

D_MODEL = 2048
BATCH = 2
SEQ = 4096
DEPTH = 2

N_META = 16
CHUNK = 128
SUB = 16
EPS = 1e-6

RET_HEADS = 8
RET_DK = 128
RET_DV = 256
RET_QK = RET_HEADS * RET_DK
RET_W = RET_HEADS * RET_DV
ROPE_BASE = 10000.0

S5_W = 1024
S5_GH = 16
S5_G = S5_W // S5_GH
S5_P = 64
DT_MIN = 1e-3
DT_MAX = 1e-1

GLA_HEADS = 4
GLA_DK = 256
GLA_DV = 512
GLA_QK = GLA_HEADS * GLA_DK
GLA_W = GLA_HEADS * GLA_DV
GLA_RANK = 16
GLA_TAU = 16.0

IN_AB = 2 * RET_QK + 2 * RET_W + 2 * S5_W
OUT_AB = RET_W + S5_W
IN_C = 2 * GLA_QK + 2 * GLA_W + GLA_RANK
N_EVEN = (DEPTH + 1) // 2
N_ODD = DEPTH // 2

kernel_name = "hybrid_retention_s5_gla_meta"


def _rmsnorm(x, w):
    xf = x.astype(jnp.float32)
    y = xf * lax.rsqrt(jnp.mean(xf * xf, axis=-1, keepdims=True) + EPS)
    return (y * w.astype(jnp.float32)).astype(x.dtype)


def _head_rmsnorm(o, w):
    y = o * lax.rsqrt(jnp.mean(o * o, axis=-1, keepdims=True) + EPS)
    b, l, hh, d = o.shape
    return y.reshape(b, l, hh * d) * w.astype(jnp.float32)


def _rope(t, cos, sin):
    half = t.shape[-1] // 2
    t1, t2 = t[..., :half], t[..., half:]
    c, s = cos[None, :, None, :], sin[None, :, None, :]
    return jnp.concatenate([t1 * c - t2 * s, t1 * s + t2 * c], axis=-1)


def _to_chunks(t):
    t = jnp.pad(t, ((0, 0), (CHUNK - N_META, 0), (0, 0), (0, 0)))
    b, lp, hh, d = t.shape
    return t.reshape(b, lp // CHUNK, CHUNK, hh, d).transpose(0, 3, 1, 2, 4)


def _from_chunks(t):
    b, hh, n, c, d = t.shape
    return t.transpose(0, 2, 3, 1, 4).reshape(b, n * c, hh, d)[:, CHUNK - N_META:]


def _retention(q, k, v):
    qc, kc, vc = _to_chunks(q), _to_chunks(k), _to_chunks(v)
    bsz = qc.shape[0]
    log_g = jnp.log1p(-jnp.exp2(-5.0 - jnp.arange(RET_HEADS, dtype=jnp.float32)))
    idx = jnp.arange(CHUNK, dtype=jnp.float32)
    diff = idx[:, None] - idx[None, :]
    causal = diff >= 0
    decay = jnp.where(causal, jnp.exp(log_g[:, None, None] * jnp.maximum(diff, 0.0)), 0.0)
    scores = jnp.einsum('bhnid,bhnjd->bhnij', qc, kc) * decay[None, :, None]
    o_intra = jnp.einsum('bhnij,bhnjv->bhniv', scores, vc)
    k_w = kc * jnp.exp(log_g[:, None] * (CHUNK - 1 - idx))[None, :, None, :, None]
    kv = jnp.einsum('bhnjd,bhnjv->nbhdv', k_w, vc)
    g_chunk = jnp.exp(log_g * CHUNK)[None, :, None, None]

    def step(s, kv_n):
        return s * g_chunk + kv_n, s

    s0 = jnp.zeros((bsz, RET_HEADS, RET_DK, RET_DV), jnp.float32)
    _, s_prev = lax.scan(step, s0, kv)
    q_w = qc * jnp.exp(log_g[:, None] * (idx + 1.0))[None, :, None, :, None]
    o_inter = jnp.einsum('bhnid,nbhdv->bhniv', q_w, s_prev)
    return _from_chunks(o_intra + o_inter)


def _s5(u, lam_re, lam_im, log_dt, b_re, b_im, c_re, c_im, d, w_glu):
    bsz, L, _ = u.shape
    lam_re = lam_re.astype(jnp.float32); lam_im = lam_im.astype(jnp.float32)
    dt = jnp.exp(log_dt.astype(jnp.float32))[:, None]
    mag = jnp.exp(lam_re * dt)
    ab_re, ab_im = mag * jnp.cos(lam_im * dt), mag * jnp.sin(lam_im * dt)
    den = lam_re * lam_re + lam_im * lam_im
    nr, ni = ab_re - 1.0, ab_im
    f_re = (nr * lam_re + ni * lam_im) / den
    f_im = (ni * lam_re - nr * lam_im) / den
    b_re = b_re.astype(jnp.float32); b_im = b_im.astype(jnp.float32)
    bb_re = f_re[..., None] * b_re - f_im[..., None] * b_im
    bb_im = f_re[..., None] * b_im + f_im[..., None] * b_re
    ug = u.reshape(bsz, L, S5_G, S5_GH)
    bu_re = jnp.einsum('blgh,gph->lbgp', ug, bb_re)
    bu_im = jnp.einsum('blgh,gph->lbgp', ug, bb_im)
    a_re = jnp.broadcast_to(ab_re, bu_re.shape)
    a_im = jnp.broadcast_to(ab_im, bu_im.shape)

    def combine(e1, e2):
        a1r, a1i, b1r, b1i = e1
        a2r, a2i, b2r, b2i = e2
        return (a2r * a1r - a2i * a1i,
                a2r * a1i + a2i * a1r,
                a2r * b1r - a2i * b1i + b2r,
                a2r * b1i + a2i * b1r + b2i)

    _, _, x_re, x_im = lax.associative_scan(combine, (a_re, a_im, bu_re, bu_im), axis=0)
    y = (jnp.einsum('lbgp,ghp->blgh', x_re, c_re.astype(jnp.float32))
         - jnp.einsum('lbgp,ghp->blgh', x_im, c_im.astype(jnp.float32)))
    y = y.reshape(bsz, L, S5_W) + d.astype(jnp.float32) * u
    y = jax.nn.gelu(y)
    return y * jax.nn.sigmoid(y @ w_glu.astype(jnp.float32))


def _gla(q, k, v, log_a):
    qc, kc, vc, gc = _to_chunks(q), _to_chunks(k), _to_chunks(v), _to_chunks(log_a)
    bsz, hh, n, c, dk = qc.shape
    dv = vc.shape[-1]
    nsub = CHUNK // SUB
    b = jnp.cumsum(gc, axis=3)
    b_last = b[:, :, :, -1]
    kv = jnp.einsum('bhnjd,bhnjv->nbhdv', kc * jnp.exp(b_last[:, :, :, None] - b), vc)
    dec = jnp.exp(b_last).transpose(2, 0, 1, 3)

    def step(s, inp):
        kv_n, dec_n = inp
        return s * dec_n[..., None] + kv_n, s

    s0 = jnp.zeros((bsz, hh, dk, dv), jnp.float32)
    _, s_prev = lax.scan(step, s0, (kv, dec))
    o_inter = jnp.einsum('bhnid,nbhdv->bhniv', qc * jnp.exp(b), s_prev)
    bs = b.reshape(bsz, hh, n, nsub, SUB, dk)
    qs = qc.reshape(bsz, hh, n, nsub, SUB, dk)
    ks = kc.reshape(bsz, hh, n, nsub, SUB, dk)
    vs = vc.reshape(bsz, hh, n, nsub, SUB, dv)
    b_ref = jnp.concatenate([jnp.zeros_like(bs[:, :, :, :1, 0]), bs[:, :, :, :-1, -1]], axis=3)
    q_hat = qs * jnp.exp(bs - b_ref[:, :, :, :, None])
    j_pos = jnp.arange(CHUNK)
    before = j_pos[None, :] < (jnp.arange(nsub) * SUB)[:, None]
    expo = jnp.where(before[:, :, None], b_ref[:, :, :, :, None] - b[:, :, :, None], -jnp.inf)
    k_hat = kc[:, :, :, None] * jnp.exp(expo)
    s_cross = jnp.einsum('bhnsid,bhnsjd->bhnsij', q_hat, k_hat)
    o_cross = jnp.einsum('bhnsij,bhnjv->bhnsiv', s_cross, vc)
    tri = jnp.arange(SUB)[:, None] >= jnp.arange(SUB)[None, :]
    expo_d = jnp.where(tri[:, :, None], bs[:, :, :, :, :, None] - bs[:, :, :, :, None], -jnp.inf)
    s_diag = jnp.einsum('bhnsid,bhnsjd,bhnsijd->bhnsij', qs, ks, jnp.exp(expo_d))
    o_diag = jnp.einsum('bhnsij,bhnsjv->bhnsiv', s_diag, vs)
    o_intra = (o_cross + o_diag).reshape(bsz, hh, n, c, dv)
    return _from_chunks(o_intra + o_inter)


def _mixer_ab(h, w_in, ret_norm_w, lam_re, lam_im, log_dt, b_re, b_im, c_re, c_im, d, w_glu, w_out, cos, sin):
    bsz, L, _ = h.shape
    proj = (h @ w_in).astype(jnp.float32)
    q, k, v, z_a, u, z_b = jnp.split(
        proj, [RET_QK, 2 * RET_QK, 2 * RET_QK + RET_W, 2 * RET_QK + 2 * RET_W,
               2 * RET_QK + 2 * RET_W + S5_W], axis=-1)
    q = _rope(q.reshape(bsz, L, RET_HEADS, RET_DK), cos, sin)
    k = _rope(k.reshape(bsz, L, RET_HEADS, RET_DK), cos, sin) * (RET_DK ** -0.5)
    v = v.reshape(bsz, L, RET_HEADS, RET_DV)
    o_a = _head_rmsnorm(_retention(q, k, v), ret_norm_w) * jax.nn.silu(z_a)
    o_b = _s5(u, lam_re, lam_im, log_dt, b_re, b_im, c_re, c_im, d, w_glu) * jax.nn.silu(z_b)
    return jnp.concatenate([o_a, o_b], axis=-1).astype(h.dtype) @ w_out


def _mixer_c(h, w_in, w_gate, b_gate, norm_w, w_out):
    bsz, L, _ = h.shape
    proj = (h @ w_in).astype(jnp.float32)
    q, k, v, z, g_low = jnp.split(
        proj, [GLA_QK, 2 * GLA_QK, 2 * GLA_QK + GLA_W, 2 * GLA_QK + 2 * GLA_W], axis=-1)
    log_a = jax.nn.log_sigmoid(g_low @ w_gate.astype(jnp.float32) + b_gate.astype(jnp.float32)) / GLA_TAU
    o = _gla(q.reshape(bsz, L, GLA_HEADS, GLA_DK) * (GLA_DK ** -0.5),
             k.reshape(bsz, L, GLA_HEADS, GLA_DK),
             v.reshape(bsz, L, GLA_HEADS, GLA_DV),
             log_a.reshape(bsz, L, GLA_HEADS, GLA_DK))
    o = _head_rmsnorm(o, norm_w) * jax.nn.silu(z)
    return o.astype(h.dtype) @ w_out


def setup_inputs(seed: int = 0) -> dict:
    key = jax.random.key(seed)
    ks = jax.random.split(key, 24)
    f32 = jnp.float32

    def nrm(k, shape, scale):
        return jax.random.normal(k, shape, f32) * scale

    return {
        "x": nrm(ks[0], (BATCH, SEQ, D_MODEL), 1.0),
        "meta": nrm(ks[1], (N_META, D_MODEL), 1.0),
        "norm_ab_w": 1.0 + nrm(ks[2], (N_EVEN, D_MODEL), 0.02),
        "w_in_ab": nrm(ks[3], (N_EVEN, D_MODEL, IN_AB), D_MODEL ** -0.5),
        "ret_norm_w": 1.0 + nrm(ks[4], (N_EVEN, RET_W), 0.02),
        "s5_lam_re": -0.5 + nrm(ks[5], (N_EVEN, S5_G, S5_P), 0.01),
        "s5_lam_im": math.pi * jnp.arange(S5_P, dtype=f32) + nrm(ks[6], (N_EVEN, S5_G, S5_P), 0.01),
        "s5_log_dt": jax.random.uniform(ks[7], (N_EVEN, S5_G), f32, math.log(DT_MIN), math.log(DT_MAX)),
        "s5_b_re": nrm(ks[8], (N_EVEN, S5_G, S5_P, S5_GH), (2 * S5_GH) ** -0.5),
        "s5_b_im": nrm(ks[9], (N_EVEN, S5_G, S5_P, S5_GH), (2 * S5_GH) ** -0.5),
        "s5_c_re": nrm(ks[10], (N_EVEN, S5_G, S5_GH, S5_P), S5_P ** -0.5),
        "s5_c_im": nrm(ks[11], (N_EVEN, S5_G, S5_GH, S5_P), S5_P ** -0.5),
        "s5_d": nrm(ks[12], (N_EVEN, S5_W), 1.0),
        "s5_w_glu": nrm(ks[13], (N_EVEN, S5_W, S5_W), S5_W ** -0.5),
        "w_out_ab": nrm(ks[14], (N_EVEN, OUT_AB, D_MODEL), OUT_AB ** -0.5),
        "norm_c_w": 1.0 + nrm(ks[15], (N_ODD, D_MODEL), 0.02),
        "w_in_c": nrm(ks[16], (N_ODD, D_MODEL, IN_C), D_MODEL ** -0.5),
        "gla_w_gate": nrm(ks[17], (N_ODD, GLA_RANK, GLA_QK), GLA_RANK ** -0.5),
        "gla_b_gate": nrm(ks[18], (N_ODD, GLA_QK), 0.1),
        "gla_norm_w": 1.0 + nrm(ks[19], (N_ODD, GLA_W), 0.02),
        "w_out_c": nrm(ks[20], (N_ODD, GLA_W, D_MODEL), GLA_W ** -0.5),
        "final_norm_w": 1.0 + nrm(ks[21], (D_MODEL,), 0.02),
    }


def reference(x, meta, norm_ab_w, w_in_ab, ret_norm_w, s5_lam_re, s5_lam_im, s5_log_dt,
              s5_b_re, s5_b_im, s5_c_re, s5_c_im, s5_d, s5_w_glu, w_out_ab,
              norm_c_w, w_in_c, gla_w_gate, gla_b_gate, gla_norm_w, w_out_c, final_norm_w):
    bsz = x.shape[0]
    h = jnp.concatenate(
        [jnp.broadcast_to(meta.astype(x.dtype)[None], (bsz, N_META, D_MODEL)), x], axis=1)
    L = h.shape[1]
    pos = jnp.arange(L, dtype=jnp.float32)
    inv_freq = jnp.power(ROPE_BASE, -jnp.arange(0, RET_DK, 2, dtype=jnp.float32) / RET_DK)
    ang = pos[:, None] * inv_freq[None, :]
    cos, sin = jnp.cos(ang), jnp.sin(ang)
    for layer in range(DEPTH):
        i = layer // 2
        if layer % 2 == 0:
            h = h + _mixer_ab(_rmsnorm(h, norm_ab_w[i]), w_in_ab[i], ret_norm_w[i],
                              s5_lam_re[i], s5_lam_im[i], s5_log_dt[i], s5_b_re[i], s5_b_im[i],
                              s5_c_re[i], s5_c_im[i], s5_d[i], s5_w_glu[i], w_out_ab[i], cos, sin)
        else:
            h = h + _mixer_c(_rmsnorm(h, norm_c_w[i]), w_in_c[i], gla_w_gate[i], gla_b_gate[i],
                             gla_norm_w[i], w_out_c[i])
    return _rmsnorm(h, final_norm_w)[:, N_META:]
```

```python
import functools
import math

import numpy as np
import jax
import jax.numpy as jnp
from jax import lax
from jax.experimental import pallas as pl
from jax.experimental.pallas import tpu as pltpu

F32 = jnp.float32
BF16 = jnp.bfloat16

N_META = 16
CHUNK = 128
PAD = CHUNK - N_META
EPS = 1e-6

RET_HEADS = 8
RET_DK = 128
RET_DV = 256
RET_QK = RET_HEADS * RET_DK
RET_W = RET_HEADS * RET_DV
ROPE_BASE = 10000.0

S5_W = 1024
S5_GH = 16
S5_G = 64
S5_P = 64
S5_F = S5_G * S5_P
S5_SLABS = S5_F // 128
S5_JB = 4
S5_SEGS = 8

GLA_HEADS = 4
GLA_DK = 256
GLA_DV = 512
GLA_QK = GLA_HEADS * GLA_DK
GLA_W = GLA_HEADS * GLA_DV
GLA_RANK = 16
GLA_TAU = 16.0
GLA_LEVELS = 7

VMEM_LIMIT = 56 * 1024 * 1024


def _cparams(sem):
    return pltpu.CompilerParams(dimension_semantics=sem, vmem_limit_bytes=VMEM_LIMIT)


def _silu(x):
    return x * (1.0 / (1.0 + jnp.exp(-x)))


def _pick_tile(n, candidates):
    for c in candidates:
        if n % c == 0:
            return c
    raise ValueError(f"no tile for {n}")


def _rmsnorm_kernel(x_ref, w_ref, o_ref):
    x = x_ref[...]
    ms = jnp.mean(x * x, axis=-1, keepdims=True)
    o_ref[...] = (x * lax.rsqrt(ms + EPS) * w_ref[...]).astype(o_ref.dtype)


def _rmsnorm(x, w, tm):
    m, d = x.shape
    return pl.pallas_call(
        _rmsnorm_kernel,
        grid=(m // tm,),
        in_specs=[pl.BlockSpec((tm, d), lambda i: (i, 0)),
                  pl.BlockSpec((1, d), lambda i: (0, 0))],
        out_specs=pl.BlockSpec((tm, d), lambda i: (i, 0)),
        out_shape=jax.ShapeDtypeStruct((m, d), BF16),
        compiler_params=_cparams(("parallel",)),
        name="rmsnorm",
    )(x, w.reshape(1, d))


def _matmul_kernel(x_ref, w_ref, o_ref):
    o_ref[...] = jnp.dot(x_ref[...], w_ref[...], preferred_element_type=F32).astype(o_ref.dtype)


def _matmul(x, w, tm, tn, out_dtype=F32):
    m, k = x.shape
    n = w.shape[1]
    return pl.pallas_call(
        _matmul_kernel,
        grid=(n // tn, m // tm),
        in_specs=[pl.BlockSpec((tm, k), lambda j, i: (i, 0)),
                  pl.BlockSpec((k, tn), lambda j, i: (0, j))],
        out_specs=pl.BlockSpec((tm, tn), lambda j, i: (i, j)),
        out_shape=jax.ShapeDtypeStruct((m, n), out_dtype),
        compiler_params=_cparams(("parallel", "parallel")),
        name="in_proj",
    )(x, w)


def _ret_log_decay(h):
    return math.log1p(-(2.0 ** (-5.0 - h)))


def _retention_kernel(q_ref, k_ref, v_ref, z_ref, cos_ref, sin_ref, nw_ref, o_ref, s_ref, *, nc):
    c = pl.program_id(0)

    @pl.when(c % nc == 0)
    def _():
        s_ref[...] = jnp.zeros_like(s_ref)

    ii = lax.broadcasted_iota(jnp.int32, (CHUNK, CHUNK), 0)
    jj = lax.broadcasted_iota(jnp.int32, (CHUNK, CHUNK), 1)
    diff = (ii - jj).astype(F32)
    causal = ii >= jj
    idx = lax.broadcasted_iota(jnp.int32, (CHUNK, 1), 0).astype(F32)
    cos = cos_ref[...]
    sin = sin_ref[...]
    for h in range(RET_HEADS):
        lg = _ret_log_decay(h)
        q = q_ref[:, h * RET_DK:(h + 1) * RET_DK]
        k = k_ref[:, h * RET_DK:(h + 1) * RET_DK]
        q = q * cos + pltpu.roll(q, RET_DK // 2, 1) * sin
        k = (k * cos + pltpu.roll(k, RET_DK // 2, 1) * sin) * (RET_DK ** -0.5)
        v = v_ref[:, h * RET_DV:(h + 1) * RET_DV].astype(BF16)
        decay = jnp.where(causal, jnp.exp(lg * jnp.maximum(diff, 0.0)), 0.0)
        scores = lax.dot_general(q.astype(BF16), k.astype(BF16), (((1,), (1,)), ((), ())),
                                 preferred_element_type=F32) * decay
        o = jnp.dot(scores.astype(BF16), v, preferred_element_type=F32)
        s_prev = s_ref[h]
        q_w = q * jnp.exp(lg * (idx + 1.0))
        o = o + jnp.dot(q_w.astype(BF16), s_prev.astype(BF16), preferred_element_type=F32)
        k_w = k * jnp.exp(lg * (CHUNK - 1.0 - idx))
        kv = lax.dot_general(k_w.astype(BF16), v, (((0,), (0,)), ((), ())),
                             preferred_element_type=F32)
        s_ref[h] = s_prev * math.exp(lg * CHUNK) + kv
        y = o * lax.rsqrt(jnp.mean(o * o, axis=-1, keepdims=True) + EPS)
        y = y * nw_ref[:, h * RET_DV:(h + 1) * RET_DV]
        z = z_ref[:, h * RET_DV:(h + 1) * RET_DV]
        o_ref[:, h * RET_DV:(h + 1) * RET_DV] = (y * _silu(z)).astype(o_ref.dtype)


def _retention(proj, cos2, sin2, norm_w, nc):
    m = proj.shape[0]
    return pl.pallas_call(
        functools.partial(_retention_kernel, nc=nc),
        grid=(m // CHUNK,),
        in_specs=[pl.BlockSpec((CHUNK, RET_QK), lambda c: (c, 0)),
                  pl.BlockSpec((CHUNK, RET_QK), lambda c: (c, 1)),
                  pl.BlockSpec((CHUNK, RET_W), lambda c: (c, 2 * RET_QK // RET_W)),
                  pl.BlockSpec((CHUNK, RET_W), lambda c: (c, 2 * RET_QK // RET_W + 1)),
                  pl.BlockSpec((CHUNK, RET_DK), lambda c: (c % nc, 0)),
                  pl.BlockSpec((CHUNK, RET_DK), lambda c: (c % nc, 0)),
                  pl.BlockSpec((1, RET_W), lambda c: (0, 0))],
        out_specs=pl.BlockSpec((CHUNK, RET_W), lambda c: (c, 0)),
        out_shape=jax.ShapeDtypeStruct((m, RET_W), BF16),
        scratch_shapes=[pltpu.VMEM((RET_HEADS, RET_DK, RET_DV), F32)],
        compiler_params=_cparams(("arbitrary",)),
        name="retention",
    )(proj, proj, proj, proj, cos2, sin2, norm_w.reshape(1, RET_W))


def _s5_prep_kernel(lre_ref, lim_ref, ldt_ref, bre_ref, bim_ref, pre_ref, pim_ref, bbre_ref, bbim_ref):
    lre = lre_ref[...]
    lim = lim_ref[...]
    dt = jnp.exp(ldt_ref[...])
    tseg = pre_ref.shape[0]
    n = lax.broadcasted_iota(jnp.int32, (tseg, 1), 0).astype(F32) + 1.0
    mag = jnp.exp(n * (lre * dt))
    ang = n * (lim * dt)
    p_re = mag * jnp.cos(ang)
    p_im = mag * jnp.sin(ang)
    pre_ref[...] = p_re
    pim_ref[...] = p_im
    a_re = p_re[0:1]
    a_im = p_im[0:1]
    den = lre * lre + lim * lim
    nr = a_re - 1.0
    f_re = (nr * lre + a_im * lim) / den
    f_im = (a_im * lre - nr * lim) / den
    b_re = bre_ref[...]
    b_im = bim_ref[...]
    bbre_ref[...] = f_re * b_re - f_im * b_im
    bbim_ref[...] = f_re * b_im + f_im * b_re


def _s5_prep(lam_re, lam_im, log_dt, b_re, b_im, tseg):
    flat = lambda a: a.reshape(1, S5_F)
    ldt = jnp.broadcast_to(log_dt[:, None], (S5_G, S5_P))
    bt = lambda a: a.reshape(S5_F, S5_GH).T
    out = jax.ShapeDtypeStruct
    return pl.pallas_call(
        _s5_prep_kernel,
        out_shape=(out((tseg, S5_F), F32), out((tseg, S5_F), F32),
                   out((S5_GH, S5_F), F32), out((S5_GH, S5_F), F32)),
        compiler_params=pltpu.CompilerParams(vmem_limit_bytes=VMEM_LIMIT),
        name="s5_prep",
    )(flat(lam_re), flat(lam_im), flat(ldt), bt(b_re), bt(b_im))


def _gelu_tanh(x):
    return 0.5 * x * (1.0 + jnp.tanh(math.sqrt(2.0 / math.pi) * (x + 0.044715 * (x * x * x))))


def _s5_kernel(u_ref, z_ref, wb_ref, wcre_ref, wcim_ref, pre_ref, pim_ref, d_ref, wglu_ref,
               o_ref, bure_ref, buim_ref, cre_ref, cim_ref, xinre_ref, xinim_ref, *, tseg, slab_group):
    ci = pl.program_id(1)

    @pl.when(ci == 0)
    def _():
        cre_ref[...] = jnp.zeros_like(cre_ref)
        cim_ref[...] = jnp.zeros_like(cim_ref)

    u = u_ref[...]
    u_bf = u.astype(BF16)
    spb = S5_SLABS // S5_JB
    for j in range(S5_JB):
        res = jnp.dot(u_bf[:, j * 256:(j + 1) * 256], wb_ref[j], preferred_element_type=F32)
        for s in range(spb):
            bure_ref[j * spb + s] = res[:, s * 128:(s + 1) * 128]
            buim_ref[j * spb + s] = res[:, (spb + s) * 128:(spb + s + 1) * 128]

    for g0 in range(0, S5_SLABS, slab_group):
        slabs = list(range(g0, g0 + slab_group))
        a_re = [jnp.broadcast_to(pre_ref[sl, 0:1, :], (S5_SEGS, 128)) for sl in slabs]
        a_im = [jnp.broadcast_to(pim_ref[sl, 0:1, :], (S5_SEGS, 128)) for sl in slabs]

        def scan_body(tau, carry, slabs=slabs, a_re=a_re, a_im=a_im):
            xr, xi = carry
            nr, ni = [], []
            for n, sl in enumerate(slabs):
                rows = pl.ds(tau, S5_SEGS, stride=tseg)
                br = bure_ref[sl, rows, :]
                bi = buim_ref[sl, rows, :]
                r = a_re[n] * xr[n] - a_im[n] * xi[n] + br
                i = a_re[n] * xi[n] + a_im[n] * xr[n] + bi
                bure_ref[sl, rows, :] = r
                buim_ref[sl, rows, :] = i
                nr.append(r)
                ni.append(i)
            return tuple(nr), tuple(ni)

        zero = tuple(jnp.zeros((S5_SEGS, 128), F32) for _ in slabs)
        end_re, end_im = lax.fori_loop(0, tseg, scan_body, (zero, zero))

        for n, sl in enumerate(slabs):
            at_re = pre_ref[sl, tseg - 1:tseg, :]
            at_im = pim_ref[sl, tseg - 1:tseg, :]
            xr = cre_ref[sl, 0:1, :]
            xi = cim_ref[sl, 0:1, :]
            rows_re, rows_im = [], []
            for s in range(S5_SEGS):
                rows_re.append(xr)
                rows_im.append(xi)
                er = end_re[n][s:s + 1]
                ei = end_im[n][s:s + 1]
                xr, xi = er + at_re * xr - at_im * xi, ei + at_re * xi + at_im * xr
            cre_ref[sl] = jnp.broadcast_to(xr, (S5_SEGS, 128))
            cim_ref[sl] = jnp.broadcast_to(xi, (S5_SEGS, 128))
            xinre_ref[sl] = jnp.concatenate(rows_re, axis=0)
            xinim_ref[sl] = jnp.concatenate(rows_im, axis=0)

    def fix_body(sl, carry):
        xin_re = xinre_ref[sl]
        xin_im = xinim_ref[sl]
        for tau in range(tseg):
            rows = pl.ds(tau, S5_SEGS, stride=tseg)
            pr = pre_ref[sl, tau:tau + 1, :]
            pi = pim_ref[sl, tau:tau + 1, :]
            bure_ref[sl, rows, :] = bure_ref[sl, rows, :] + (pr * xin_re - pi * xin_im)
            buim_ref[sl, rows, :] = buim_ref[sl, rows, :] + (pr * xin_im + pi * xin_re)
        return carry

    lax.fori_loop(0, S5_SLABS, fix_body, 0)

    ys = []
    for j in range(S5_JB):
        x_re = jnp.concatenate([bure_ref[j * spb + s] for s in range(spb)], axis=1).astype(BF16)
        x_im = jnp.concatenate([buim_ref[j * spb + s] for s in range(spb)], axis=1).astype(BF16)
        ys.append(jnp.dot(x_re, wcre_ref[j], preferred_element_type=F32)
                  - jnp.dot(x_im, wcim_ref[j], preferred_element_type=F32))
    y = jnp.concatenate(ys, axis=1) + d_ref[...] * u
    y = _gelu_tanh(y)
    gate = jnp.dot(y.astype(BF16), wglu_ref[...], preferred_element_type=F32)
    y = y * (1.0 / (1.0 + jnp.exp(-gate)))
    o_ref[...] = (y * _silu(z_ref[...])).astype(o_ref.dtype)


def _s5_time_tile(rows_per_batch):
    for tseg in (66, 44, 132, 12, 4, 6, 2):
        t = S5_SEGS * tseg
        if rows_per_batch % t == 0 and t % 16 == 0:
            return t, tseg
    raise ValueError(f"no S5 time tile for {rows_per_batch}")


def _s5(proj, bsz, wb, wcre, wcim, p_re, p_im, d, wglu, t, tseg):
    m = proj.shape[0]
    nt = m // bsz // t
    ucol = (2 * RET_QK + 2 * RET_W) // S5_W
    const = lambda *shape: pl.BlockSpec(shape, lambda b, i: (0,) * len(shape))
    slabs = lambda p: p.reshape(tseg, S5_SLABS, 128).transpose(1, 0, 2)
    return pl.pallas_call(
        functools.partial(_s5_kernel, tseg=tseg, slab_group=8),
        grid=(bsz, nt),
        in_specs=[pl.BlockSpec((t, S5_W), lambda b, i: (b * nt + i, ucol)),
                  pl.BlockSpec((t, S5_W), lambda b, i: (b * nt + i, ucol + 1)),
                  const(S5_JB, 256, 2048), const(S5_JB, 1024, 256), const(S5_JB, 1024, 256),
                  const(S5_SLABS, tseg, 128), const(S5_SLABS, tseg, 128), const(1, S5_W),
                  const(S5_W, S5_W)],
        out_specs=pl.BlockSpec((t, S5_W), lambda b, i: (b * nt + i, 0)),
        out_shape=jax.ShapeDtypeStruct((m, S5_W), BF16),
        scratch_shapes=[pltpu.VMEM((S5_SLABS, t, 128), F32), pltpu.VMEM((S5_SLABS, t, 128), F32)]
        + [pltpu.VMEM((S5_SLABS, S5_SEGS, 128), F32)] * 4,
        compiler_params=_cparams(("arbitrary", "arbitrary")),
        name="s5",
    )(proj, proj, wb, wcre, wcim, slabs(p_re), slabs(p_im), d.reshape(1, S5_W), wglu)


def _s5_block_weights(bb_re, bb_im, c_re, c_im):
    eye = jnp.eye(16, dtype=F32)

    def wb_part(bb):
        a = bb.reshape(S5_GH, S5_JB, 16, S5_P)
        w = jnp.einsum('hjgp,gk->jghkp', a, eye)
        return w.reshape(S5_JB, 256, 1024)

    wb = jnp.concatenate([wb_part(bb_re), wb_part(bb_im)], axis=2).astype(BF16)

    def wc_part(c):
        a = c.astype(F32).reshape(S5_JB, 16, S5_GH, S5_P)
        w = jnp.einsum('jghp,gk->jgpkh', a, eye)
        return w.reshape(S5_JB, 1024, 256).astype(BF16)

    return wb, wc_part(c_re), wc_part(c_im)


def _out_ab_kernel(h_ref, oa_ref, ob_ref, wa_ref, wb_ref, nw_ref, wg_ref, h1_ref, xn_ref, gl_ref):
    acc = jnp.dot(oa_ref[...], wa_ref[...], preferred_element_type=F32)
    acc = acc + jnp.dot(ob_ref[...], wb_ref[...], preferred_element_type=F32)
    h1 = h_ref[...] + acc
    h1_ref[...] = h1
    ms = jnp.mean(h1 * h1, axis=-1, keepdims=True)
    xn = (h1 * lax.rsqrt(ms + EPS) * nw_ref[...]).astype(BF16)
    xn_ref[...] = xn
    gl_ref[...] = jnp.dot(xn, wg_ref[...], preferred_element_type=F32)


def _out_ab(h, oa, ob, wa, wb, norm_w, wg, tm):
    m, d = h.shape
    row = lambda w: pl.BlockSpec((tm, w), lambda i: (i, 0))
    const = lambda a: pl.BlockSpec(a.shape, lambda i: (0, 0))
    out = jax.ShapeDtypeStruct
    return pl.pallas_call(
        _out_ab_kernel,
        grid=(m // tm,),
        in_specs=[row(d), row(RET_W), row(S5_W), const(wa), const(wb),
                  pl.BlockSpec((1, d), lambda i: (0, 0)), const(wg)],
        out_specs=[row(d), row(d), row(128)],
        out_shape=(out((m, d), F32), out((m, d), BF16), out((m, 128), F32)),
        compiler_params=_cparams(("parallel",)),
        name="out_proj_ab",
    )(h, oa, ob, wa, wb, norm_w.reshape(1, d), wg)


def _out_c_kernel(h_ref, o_ref, w_ref, nw_ref, y_ref):
    h2 = h_ref[...] + jnp.dot(o_ref[...], w_ref[...], preferred_element_type=F32)
    ms = jnp.mean(h2 * h2, axis=-1, keepdims=True)
    y_ref[0] = h2 * lax.rsqrt(ms + EPS) * nw_ref[...]


def _out_c(h, o, w, norm_w, bsz, nc):
    m, d = h.shape
    real = nc - 1

    def rows(c):
        return ((c // real) * nc + c % real + 1, 0)

    return pl.pallas_call(
        _out_c_kernel,
        grid=(bsz * real,),
        in_specs=[pl.BlockSpec((CHUNK, d), rows), pl.BlockSpec((CHUNK, GLA_W), rows),
                  pl.BlockSpec(w.shape, lambda c: (0, 0)), pl.BlockSpec((1, d), lambda c: (0, 0))],
        out_specs=pl.BlockSpec((1, CHUNK, d), lambda c: (c // real, c % real, 0)),
        out_shape=jax.ShapeDtypeStruct((bsz, real * CHUNK, d), F32),
        compiler_params=_cparams(("parallel",)),
        name="out_proj_c",
    )(h, o, w, norm_w.reshape(1, d))


def _split3(x):
    x1 = x.astype(BF16)
    r1 = x - x1.astype(F32)
    x2 = r1.astype(BF16)
    x3 = (r1 - x2.astype(F32)).astype(BF16)
    return x1, x2, x3


def _block_ref_rows(b, level):
    rows, width = b.shape
    half = 1 << level
    blk = 2 * half
    if blk >= 8:
        b3 = b.reshape(rows // blk, blk, width)
        ref = jnp.broadcast_to(b3[:, half - 1:half, :], b3.shape)
        return ref.reshape(rows, width)
    b3 = b.reshape(rows // 8, 8, width)
    sub = lax.broadcasted_iota(jnp.int32, b3.shape, 1)
    pick = lambda r: jnp.broadcast_to(b3[:, r:r + 1, :], b3.shape)
    if blk == 4:
        ref = jnp.where(sub < 4, pick(1), pick(5))
    else:
        ref = jnp.where(sub < 2, pick(0), jnp.where(sub < 4, pick(2), jnp.where(sub < 6, pick(4), pick(6))))
    return ref.reshape(rows, width)


def _gla_kernel(q_ref, k_ref, v_ref, z_ref, gl_ref, wg_ref, bg_ref, nw_ref, o_ref, st_ref, *, nc):
    c = pl.program_id(0)
    n = c % nc

    @pl.when(n == 0)
    def _():
        st_ref[...] = jnp.zeros_like(st_ref)

    ridx = lax.broadcasted_iota(jnp.int32, (CHUNK, 1), 0)
    ii = lax.broadcasted_iota(jnp.int32, (CHUNK, CHUNK), 0)
    jj = lax.broadcasted_iota(jnp.int32, (CHUNK, CHUNK), 1)
    pair_code = jnp.where(ii > jj, ii ^ jj, 0)
    eye = ii == jj
    tri = (ii >= jj).astype(BF16)

    gl1, gl2, _ = _split3(gl_ref[:, :GLA_RANK])
    wg1, wg2, _ = _split3(wg_ref[...])
    x = (jnp.dot(gl1, wg1, preferred_element_type=F32) + jnp.dot(gl1, wg2, preferred_element_type=F32)
         + jnp.dot(gl2, wg1, preferred_element_type=F32)) + bg_ref[...]
    log_a = (jnp.minimum(x, 0.0) - jnp.log1p(jnp.exp(-jnp.abs(x)))) * (1.0 / GLA_TAU)
    log_a = jnp.where(ridx >= jnp.where(n > 0, 0, PAD), log_a, 0.0)
    g1, g2, g3 = _split3(log_a)
    b_all = (jnp.dot(tri, g1, preferred_element_type=F32) + jnp.dot(tri, g2, preferred_element_type=F32)
             + jnp.dot(tri, g3, preferred_element_type=F32))
    for h in range(GLA_HEADS):
        kc = slice(h * GLA_DK, (h + 1) * GLA_DK)
        vc = slice(h * GLA_DV, (h + 1) * GLA_DV)
        q = q_ref[:, kc] * (GLA_DK ** -0.5)
        k = k_ref[:, kc]
        v = v_ref[:, vc].astype(BF16)
        b = b_all[:, kc]
        b_last = b[CHUNK - 1:CHUNK, :]
        scores = jnp.where(eye, jnp.sum(q * k, axis=1, keepdims=True), 0.0)
        for lv in range(GLA_LEVELS):
            upper = ((ridx >> lv) & 1) == 1
            fac = jnp.exp(-jnp.abs(b - _block_ref_rows(b, lv)))
            zz = (jnp.where(upper, q, k) * fac).astype(BF16)
            gram = lax.dot_general(zz, zz, (((1,), (1,)), ((), ())), preferred_element_type=F32)
            scores = scores + jnp.where((pair_code >> lv) == 1, gram, 0.0)
        o = jnp.dot(scores.astype(BF16), v, preferred_element_type=F32)
        st = st_ref[h]
        qe = (q * jnp.exp(b)).astype(BF16)
        o = o + lax.dot_general(qe, st.astype(BF16), (((1,), (1,)), ((), ())), preferred_element_type=F32)
        ke = (k * jnp.exp(b_last - b)).astype(BF16)
        kv_t = lax.dot_general(v, ke, (((0,), (0,)), ((), ())), preferred_element_type=F32)
        st_ref[h] = st * jnp.exp(b_last) + kv_t
        y = o * lax.rsqrt(jnp.mean(o * o, axis=-1, keepdims=True) + EPS) * nw_ref[:, vc]
        o_ref[:, vc] = (y * _silu(z_ref[:, vc])).astype(o_ref.dtype)


def _gla(proj, glow, w_gate, b_gate, norm_w, nc):
    m = proj.shape[0]
    return pl.pallas_call(
        functools.partial(_gla_kernel, nc=nc),
        grid=(m // CHUNK,),
        in_specs=[pl.BlockSpec((CHUNK, GLA_QK), lambda c: (c, 0)),
                  pl.BlockSpec((CHUNK, GLA_QK), lambda c: (c, 1)),
                  pl.BlockSpec((CHUNK, GLA_W), lambda c: (c, 1)),
                  pl.BlockSpec((CHUNK, GLA_W), lambda c: (c, 2)),
                  pl.BlockSpec((CHUNK, 128), lambda c: (c, 0)),
                  pl.BlockSpec((GLA_RANK, GLA_QK), lambda c: (0, 0)),
                  pl.BlockSpec((1, GLA_QK), lambda c: (0, 0)),
                  pl.BlockSpec((1, GLA_W), lambda c: (0, 0))],
        out_specs=pl.BlockSpec((CHUNK, GLA_W), lambda c: (c, 0)),
        out_shape=jax.ShapeDtypeStruct((m, GLA_W), BF16),
        scratch_shapes=[pltpu.VMEM((GLA_HEADS, GLA_DV, GLA_DK), F32)],
        compiler_params=_cparams(("arbitrary",)),
        name="gla",
    )(proj, proj, proj, proj, glow, w_gate.astype(F32), b_gate.reshape(1, GLA_QK).astype(F32),
      norm_w.reshape(1, GLA_W).astype(F32))


def _rope_tables(rows_per_batch):
    pos = jnp.maximum(jnp.arange(rows_per_batch, dtype=F32) - PAD, 0.0)
    inv_freq = jnp.power(ROPE_BASE, -jnp.arange(0, RET_DK, 2, dtype=F32) / RET_DK)
    ang = pos[:, None] * inv_freq[None, :]
    cos, sin = jnp.cos(ang), jnp.sin(ang)
    return jnp.concatenate([cos, cos], axis=1), jnp.concatenate([-sin, sin], axis=1)


def kernel(x, meta, norm_ab_w, w_in_ab, ret_norm_w, s5_lam_re, s5_lam_im, s5_log_dt, s5_b_re, s5_b_im,
           s5_c_re, s5_c_im, s5_d, s5_w_glu, w_out_ab, norm_c_w, w_in_c, gla_w_gate, gla_b_gate,
           gla_norm_w, w_out_c, final_norm_w):
    bsz, seq, d = x.shape
    assert seq % CHUNK == 0 and w_in_ab.shape[0] == 1 and w_in_c.shape[0] == 1
    rpb = seq + CHUNK
    nc = rpb // CHUNK
    m = bsz * rpb
    tm = _pick_tile(m, (768, 384, 256, 128))

    lead = jnp.concatenate([jnp.zeros((PAD, d), x.dtype), meta.astype(x.dtype)], axis=0)
    h0 = jnp.concatenate([jnp.broadcast_to(lead[None], (bsz, CHUNK, d)), x], axis=1).reshape(m, d)

    xn0 = _rmsnorm(h0, norm_ab_w[0], tm)
    proj0 = _matmul(xn0, w_in_ab[0].astype(BF16), tm, 1024)
    cos2, sin2 = _rope_tables(rpb)
    o_a = _retention(proj0, cos2, sin2, ret_norm_w[0], nc)
    t, tseg = _s5_time_tile(rpb)
    p_re, p_im, bb_re, bb_im = _s5_prep(s5_lam_re[0], s5_lam_im[0], s5_log_dt[0], s5_b_re[0], s5_b_im[0], tseg)
    wb, wcre, wcim = _s5_block_weights(bb_re, bb_im, s5_c_re[0], s5_c_im[0])
    o_b = _s5(proj0, bsz, wb, wcre, wcim, p_re, p_im, s5_d[0], s5_w_glu[0].astype(BF16), t, tseg)

    w_out = w_out_ab[0].astype(BF16)
    n_main = 2 * GLA_QK + 2 * GLA_W
    wg = jnp.pad(w_in_c[0][:, n_main:], ((0, 0), (0, 128 - GLA_RANK))).astype(BF16)
    h1, xn1, glow = _out_ab(h0, o_a, o_b, w_out[:RET_W], w_out[RET_W:], norm_c_w[0], wg,
                            _pick_tile(m, (384, 256, 128)))

    proj1 = _matmul(xn1, w_in_c[0][:, :n_main].astype(BF16), tm, 1024)
    o_c = _gla(proj1, glow, gla_w_gate[0], gla_b_gate[0], gla_norm_w[0], nc)
    return _out_c(h1, o_c, w_out_c[0].astype(BF16), final_norm_w, bsz, nc)
```

```python
import functools
import math

import jax
import jax.numpy as jnp
from jax import lax
from jax.experimental import pallas as pl
from jax.experimental.pallas import tpu as pltpu

F32 = jnp.float32
BF16 = jnp.bfloat16

N_META = 16
CHUNK = 128
PAD = CHUNK - N_META
EPS = 1e-6
ROW_CHUNKS = 3

RET_HEADS = 8
RET_DK = 128
RET_DV = 256
RET_QK = RET_HEADS * RET_DK
RET_W = RET_HEADS * RET_DV
ROPE_BASE = 10000.0

S5_W = 1024
S5_GH = 16
S5_G = 64
S5_P = 64
S5_F = S5_G * S5_P
S5_SLABS = S5_F // 128
S5_JB = 4
S5_SEGS = 8

GLA_HEADS = 4
GLA_DK = 256
GLA_DV = 512
GLA_QK = GLA_HEADS * GLA_DK
GLA_W = GLA_HEADS * GLA_DV
GLA_RANK = 16
GLA_TAU = 16.0
GLA_LEVELS = 7

PROJ_TN = 1024
VMEM_LIMIT = 56 * 1024 * 1024


def _cparams(sem):
    return pltpu.CompilerParams(dimension_semantics=sem, vmem_limit_bytes=VMEM_LIMIT)


def _silu(x):
    return x * (1.0 / (1.0 + jnp.exp(-x)))


def _pick_tile(n, candidates):
    for c in candidates:
        if n % c == 0:
            return c
    raise ValueError(f"no tile for {n}")


def _rms_scale(x, w):
    return x * lax.rsqrt(jnp.mean(x * x, axis=-1, keepdims=True) + EPS) * w


def _chunk_specs(d, nt, n_before):
    del nt
    return [pl.BlockSpec((None, CHUNK, d),
                         lambda b, t, k=k: (b, jnp.maximum(ROW_CHUNKS * t + k - n_before, 0), 0))
            for k in range(ROW_CHUNKS)]


def _embed_norm_kernel(lead_ref, xa_ref, xb_ref, xc_ref, w_ref, o_ref):
    t = pl.program_id(1)
    first = jnp.where(t == 0, lead_ref[...], xa_ref[...])
    for r, rows in enumerate((first, xb_ref[...], xc_ref[...])):
        o_ref[r * CHUNK:(r + 1) * CHUNK, :] = _rms_scale(rows, w_ref[...]).astype(o_ref.dtype)


def _embed_norm(x, lead, w, nt):
    bsz, _, d = x.shape
    tile = ROW_CHUNKS * CHUNK
    return pl.pallas_call(
        _embed_norm_kernel,
        grid=(bsz, nt),
        in_specs=[pl.BlockSpec((CHUNK, d), lambda b, t: (0, 0))] + _chunk_specs(d, nt, 1)
        + [pl.BlockSpec((1, d), lambda b, t: (0, 0))],
        out_specs=pl.BlockSpec((tile, d), lambda b, t: (b * nt + t, 0)),
        out_shape=jax.ShapeDtypeStruct((bsz * nt * tile, d), BF16),
        compiler_params=_cparams(("parallel", "parallel")),
        name="embed_norm",
    )(lead, x, x, x, w.reshape(1, d))


def _rope_rows(a, cos, sin, scale):
    outs = []
    for h in range(a.shape[1] // RET_DK):
        blk = a[:, h * RET_DK:(h + 1) * RET_DK]
        r = blk * cos + pltpu.roll(blk, RET_DK // 2, 1) * sin
        outs.append(r * scale if scale != 1.0 else r)
    return jnp.concatenate(outs, axis=1)


def _in_proj_kernel(*refs, modes, has_rope):
    if has_rope:
        x_ref, w_ref, cos_ref, sin_ref, o_ref, wbf_ref = refs
    else:
        x_ref, w_ref, o_ref, wbf_ref = refs
    j = pl.program_id(0)

    @pl.when(pl.program_id(1) == 0)
    def _():
        wbf_ref[...] = w_ref[...].astype(BF16)

    for mode in sorted(set(modes)):
        cond = functools.reduce(jnp.logical_or, [j == jj for jj, mm in enumerate(modes) if mm == mode])

        @pl.when(cond)
        def _(mode=mode):
            kind, scale = mode
            acc = jnp.dot(x_ref[...], wbf_ref[...], preferred_element_type=F32)
            if kind == "rope":
                acc = _rope_rows(acc, cos_ref[...], sin_ref[...], scale)
            elif kind == "silu":
                acc = _silu(acc)
            elif scale != 1.0:
                acc = acc * scale
            o_ref[...] = acc.astype(o_ref.dtype)


def _in_proj(x, w, modes, tm, rope=None):
    m, k = x.shape
    nt = len(modes)
    in_specs = [pl.BlockSpec((tm, k), lambda j, i: (i, 0)),
                pl.BlockSpec((k, PROJ_TN), lambda j, i: (0, j))]
    args = [x, w]
    if rope is not None:
        in_specs += [pl.BlockSpec((tm, RET_DK), lambda j, i: (i, 0))] * 2
        args += list(rope)
    return pl.pallas_call(
        functools.partial(_in_proj_kernel, modes=tuple(modes), has_rope=rope is not None),
        grid=(nt, m // tm),
        in_specs=in_specs,
        out_specs=pl.BlockSpec((tm, PROJ_TN), lambda j, i: (i, j)),
        out_shape=jax.ShapeDtypeStruct((m, nt * PROJ_TN), BF16),
        scratch_shapes=[pltpu.VMEM((k, PROJ_TN), BF16)],
        compiler_params=_cparams(("arbitrary", "arbitrary")),
        name="in_proj",
    )(*args)


def _ret_log_decay(h):
    return math.log1p(-(2.0 ** (-5.0 - h)))


def _retention_kernel(q_ref, k_ref, v_ref, z_ref, nw_ref, o_ref, s_ref, dec_ref, *, nc):
    c = pl.program_id(0)

    @pl.when(c == 0)
    def _():
        ii = lax.broadcasted_iota(jnp.int32, (CHUNK, CHUNK), 0)
        jj = lax.broadcasted_iota(jnp.int32, (CHUNK, CHUNK), 1)
        diff = (ii - jj).astype(F32)
        for h in range(RET_HEADS):
            dec_ref[h] = jnp.where(ii >= jj, jnp.exp(_ret_log_decay(h) * jnp.maximum(diff, 0.0)), 0.0)

    @pl.when(c % nc == 0)
    def _():
        s_ref[...] = jnp.zeros_like(s_ref)

    idx = lax.broadcasted_iota(jnp.int32, (CHUNK, 1), 0).astype(F32)
    for h in range(RET_HEADS):
        lg = _ret_log_decay(h)
        q = q_ref[:, h * RET_DK:(h + 1) * RET_DK]
        k = k_ref[:, h * RET_DK:(h + 1) * RET_DK]
        v = v_ref[:, h * RET_DV:(h + 1) * RET_DV]
        scores = lax.dot_general(q, k, (((1,), (1,)), ((), ())), preferred_element_type=F32) * dec_ref[h]
        o = jnp.dot(scores.astype(BF16), v, preferred_element_type=F32)
        s_prev = s_ref[h]
        o = o + jnp.exp(lg * (idx + 1.0)) * jnp.dot(q, s_prev.astype(BF16), preferred_element_type=F32)
        k_w = (k.astype(F32) * jnp.exp(lg * (CHUNK - 1.0 - idx))).astype(BF16)
        kv = lax.dot_general(k_w, v, (((0,), (0,)), ((), ())), preferred_element_type=F32)
        s_ref[h] = s_prev * math.exp(lg * CHUNK) + kv
        y = _rms_scale(o, nw_ref[:, h * RET_DV:(h + 1) * RET_DV])
        z = z_ref[:, h * RET_DV:(h + 1) * RET_DV].astype(F32)
        o_ref[:, h * RET_DV:(h + 1) * RET_DV] = (y * z).astype(o_ref.dtype)


def _retention(proj, norm_w, nc):
    m = proj.shape[0]
    return pl.pallas_call(
        functools.partial(_retention_kernel, nc=nc),
        grid=(m // CHUNK,),
        in_specs=[pl.BlockSpec((CHUNK, RET_QK), lambda c: (c, 0)),
                  pl.BlockSpec((CHUNK, RET_QK), lambda c: (c, 1)),
                  pl.BlockSpec((CHUNK, RET_W), lambda c: (c, 1)),
                  pl.BlockSpec((CHUNK, RET_W), lambda c: (c, 2)),
                  pl.BlockSpec((1, RET_W), lambda c: (0, 0))],
        out_specs=pl.BlockSpec((CHUNK, RET_W), lambda c: (c, 0)),
        out_shape=jax.ShapeDtypeStruct((m, RET_W), BF16),
        scratch_shapes=[pltpu.VMEM((RET_HEADS, RET_DK, RET_DV), F32),
                        pltpu.VMEM((RET_HEADS, CHUNK, CHUNK), F32)],
        compiler_params=_cparams(("arbitrary",)),
        name="retention",
    )(proj, proj, proj, proj, norm_w.reshape(1, RET_W))


def _s5_prep_kernel(lre_ref, lim_ref, ldt_ref, bre_ref, bim_ref, pre_ref, pim_ref, bbre_ref, bbim_ref):
    lre = lre_ref[...]
    lim = lim_ref[...]
    dt = jnp.exp(ldt_ref[...])
    tseg = pre_ref.shape[0]
    n = lax.broadcasted_iota(jnp.int32, (tseg, 1), 0).astype(F32) + 1.0
    mag = jnp.exp(n * (lre * dt))
    ang = n * (lim * dt)
    p_re = mag * jnp.cos(ang)
    p_im = mag * jnp.sin(ang)
    pre_ref[...] = p_re
    pim_ref[...] = p_im
    a_re = p_re[0:1]
    a_im = p_im[0:1]
    den = lre * lre + lim * lim
    nr = a_re - 1.0
    f_re = (nr * lre + a_im * lim) / den
    f_im = (a_im * lre - nr * lim) / den
    b_re = bre_ref[...]
    b_im = bim_ref[...]
    bbre_ref[...] = f_re * b_re - f_im * b_im
    bbim_ref[...] = f_re * b_im + f_im * b_re


def _s5_prep(lam_re, lam_im, log_dt, b_re, b_im, tseg):
    flat = lambda a: a.reshape(1, S5_F)
    ldt = jnp.broadcast_to(log_dt[:, None], (S5_G, S5_P))
    bt = lambda a: a.reshape(S5_F, S5_GH).T
    out = jax.ShapeDtypeStruct
    return pl.pallas_call(
        _s5_prep_kernel,
        out_shape=(out((tseg, S5_F), F32), out((tseg, S5_F), F32),
                   out((S5_GH, S5_F), F32), out((S5_GH, S5_F), F32)),
        compiler_params=pltpu.CompilerParams(vmem_limit_bytes=VMEM_LIMIT),
        name="s5_prep",
    )(flat(lam_re), flat(lam_im), flat(ldt), bt(b_re), bt(b_im))


def _gelu_tanh(x):
    return 0.5 * x * (1.0 + jnp.tanh(math.sqrt(2.0 / math.pi) * (x + 0.044715 * (x * x * x))))


def _s5_kernel(u_ref, z_ref, wb_ref, wcre_ref, wcim_ref, pre_ref, pim_ref, d_ref, wglu_ref,
               o_ref, bure_ref, buim_ref, cre_ref, cim_ref, xinre_ref, xinim_ref, *, tseg, slab_group):
    ci = pl.program_id(1)

    @pl.when(ci == 0)
    def _():
        cre_ref[...] = jnp.zeros_like(cre_ref)
        cim_ref[...] = jnp.zeros_like(cim_ref)

    u_bf = u_ref[...]
    spb = S5_SLABS // S5_JB
    for j in range(S5_JB):
        res = jnp.dot(u_bf[:, j * 256:(j + 1) * 256], wb_ref[j], preferred_element_type=F32)
        for s in range(spb):
            bure_ref[j * spb + s] = res[:, s * 128:(s + 1) * 128]
            buim_ref[j * spb + s] = res[:, (spb + s) * 128:(spb + s + 1) * 128]

    for g0 in range(0, S5_SLABS, slab_group):
        slabs = list(range(g0, g0 + slab_group))
        a_re = [jnp.broadcast_to(pre_ref[sl, 0:1, :], (S5_SEGS, 128)) for sl in slabs]
        a_im = [jnp.broadcast_to(pim_ref[sl, 0:1, :], (S5_SEGS, 128)) for sl in slabs]

        def scan_body(tau, carry, slabs=slabs, a_re=a_re, a_im=a_im):
            xr, xi = carry
            nr, ni = [], []
            for n, sl in enumerate(slabs):
                rows = pl.ds(tau, S5_SEGS, stride=tseg)
                br = bure_ref[sl, rows, :]
                bi = buim_ref[sl, rows, :]
                r = a_re[n] * xr[n] - a_im[n] * xi[n] + br
                i = a_re[n] * xi[n] + a_im[n] * xr[n] + bi
                bure_ref[sl, rows, :] = r
                buim_ref[sl, rows, :] = i
                nr.append(r)
                ni.append(i)
            return tuple(nr), tuple(ni)

        zero = tuple(jnp.zeros((S5_SEGS, 128), F32) for _ in slabs)
        end_re, end_im = lax.fori_loop(0, tseg, scan_body, (zero, zero))

        for n, sl in enumerate(slabs):
            at_re = pre_ref[sl, tseg - 1:tseg, :]
            at_im = pim_ref[sl, tseg - 1:tseg, :]
            xr = cre_ref[sl, 0:1, :]
            xi = cim_ref[sl, 0:1, :]
            rows_re, rows_im = [], []
            for s in range(S5_SEGS):
                rows_re.append(xr)
                rows_im.append(xi)
                er = end_re[n][s:s + 1]
                ei = end_im[n][s:s + 1]
                xr, xi = er + at_re * xr - at_im * xi, ei + at_re * xi + at_im * xr
            cre_ref[sl] = jnp.broadcast_to(xr, (S5_SEGS, 128))
            cim_ref[sl] = jnp.broadcast_to(xi, (S5_SEGS, 128))
            xinre_ref[sl] = jnp.concatenate(rows_re, axis=0)
            xinim_ref[sl] = jnp.concatenate(rows_im, axis=0)

    def fix_body(sl, carry):
        xin_re = xinre_ref[sl]
        xin_im = xinim_ref[sl]
        for tau in range(tseg):
            rows = pl.ds(tau, S5_SEGS, stride=tseg)
            pr = pre_ref[sl, tau:tau + 1, :]
            pi = pim_ref[sl, tau:tau + 1, :]
            bure_ref[sl, rows, :] = bure_ref[sl, rows, :] + (pr * xin_re - pi * xin_im)
            buim_ref[sl, rows, :] = buim_ref[sl, rows, :] + (pr * xin_im + pi * xin_re)
        return carry

    lax.fori_loop(0, S5_SLABS, fix_body, 0)

    ys = []
    for j in range(S5_JB):
        x_re = jnp.concatenate([bure_ref[j * spb + s] for s in range(spb)], axis=1).astype(BF16)
        x_im = jnp.concatenate([buim_ref[j * spb + s] for s in range(spb)], axis=1).astype(BF16)
        ys.append(jnp.dot(x_re, wcre_ref[j], preferred_element_type=F32)
                  - jnp.dot(x_im, wcim_ref[j], preferred_element_type=F32))
    y = jnp.concatenate(ys, axis=1) + d_ref[...] * u_bf.astype(F32)
    y = _gelu_tanh(y)
    gate = jnp.dot(y.astype(BF16), wglu_ref[...], preferred_element_type=F32)
    y = y * (1.0 / (1.0 + jnp.exp(-gate)))
    o_ref[...] = (y * z_ref[...].astype(F32)).astype(o_ref.dtype)


def _s5_time_tile(rows_per_batch):
    for tseg in (66, 44, 132, 12, 4, 6, 2):
        t = S5_SEGS * tseg
        if rows_per_batch % t == 0 and t % 16 == 0:
            return t, tseg
    raise ValueError(f"no S5 time tile for {rows_per_batch}")


def _s5(proj, bsz, wb, wcre, wcim, p_re, p_im, d, wglu, t, tseg):
    m = proj.shape[0]
    nt = m // bsz // t
    ucol = (2 * RET_QK + 2 * RET_W) // S5_W
    const = lambda *shape: pl.BlockSpec(shape, lambda b, i: (0,) * len(shape))
    slabs = lambda p: p.reshape(tseg, S5_SLABS, 128).transpose(1, 0, 2)
    return pl.pallas_call(
        functools.partial(_s5_kernel, tseg=tseg, slab_group=8),
        grid=(bsz, nt),
        in_specs=[pl.BlockSpec((t, S5_W), lambda b, i: (b * nt + i, ucol)),
                  pl.BlockSpec((t, S5_W), lambda b, i: (b * nt + i, ucol + 1)),
                  const(S5_JB, 256, 2048), const(S5_JB, 1024, 256), const(S5_JB, 1024, 256),
                  const(S5_SLABS, tseg, 128), const(S5_SLABS, tseg, 128), const(1, S5_W),
                  const(S5_W, S5_W)],
        out_specs=pl.BlockSpec((t, S5_W), lambda b, i: (b * nt + i, 0)),
        out_shape=jax.ShapeDtypeStruct((m, S5_W), BF16),
        scratch_shapes=[pltpu.VMEM((S5_SLABS, t, 128), F32), pltpu.VMEM((S5_SLABS, t, 128), F32)]
        + [pltpu.VMEM((S5_SLABS, S5_SEGS, 128), F32)] * 4,
        compiler_params=_cparams(("arbitrary", "arbitrary")),
        name="s5",
    )(proj, proj, wb, wcre, wcim, slabs(p_re), slabs(p_im), d.reshape(1, S5_W), wglu)


def _s5_block_weights(bb_re, bb_im, c_re, c_im):
    eye = jnp.eye(16, dtype=F32)

    def wb_part(bb):
        a = bb.reshape(S5_GH, S5_JB, 16, S5_P)
        w = jnp.einsum('hjgp,gk->jghkp', a, eye)
        return w.reshape(S5_JB, 256, 1024)

    wb = jnp.concatenate([wb_part(bb_re), wb_part(bb_im)], axis=2).astype(BF16)

    def wc_part(c):
        a = c.astype(F32).reshape(S5_JB, 16, S5_GH, S5_P)
        w = jnp.einsum('jghp,gk->jgpkh', a, eye)
        return w.reshape(S5_JB, 1024, 256).astype(BF16)

    return wb, wc_part(c_re), wc_part(c_im)


def _out_ab_kernel(lead_ref, xa_ref, xb_ref, xc_ref, oa_ref, ob_ref, w_ref, nw_ref, wg_ref,
                   h1_ref, xn_ref, gl_ref):
    t = pl.program_id(1)
    first = jnp.where(t == 0, lead_ref[...], xa_ref[...])
    for r, rows in enumerate((first, xb_ref[...], xc_ref[...])):
        sl = slice(r * CHUNK, (r + 1) * CHUNK)
        acc = jnp.dot(oa_ref[sl, :], w_ref[:RET_W, :], preferred_element_type=F32)
        acc = acc + jnp.dot(ob_ref[sl, :], w_ref[RET_W:, :], preferred_element_type=F32)
        h1 = rows + acc
        h1_ref[sl, :] = h1
        xn = _rms_scale(h1, nw_ref[...]).astype(BF16)
        xn_ref[sl, :] = xn
        gl_ref[sl, :] = jnp.dot(xn, wg_ref[...], preferred_element_type=F32)


def _out_ab(x, lead, oa, ob, w, norm_w, wg, nt):
    bsz, _, d = x.shape
    tile = ROW_CHUNKS * CHUNK
    m = bsz * nt * tile
    row = lambda width: pl.BlockSpec((tile, width), lambda b, t: (b * nt + t, 0))
    const = lambda a: pl.BlockSpec(a.shape, lambda b, t: (0, 0))
    out = jax.ShapeDtypeStruct
    return pl.pallas_call(
        _out_ab_kernel,
        grid=(bsz, nt),
        in_specs=[pl.BlockSpec((CHUNK, d), lambda b, t: (0, 0))] + _chunk_specs(d, nt, 1)
        + [row(RET_W), row(S5_W), const(w), pl.BlockSpec((1, d), lambda b, t: (0, 0)), const(wg)],
        out_specs=[row(d), row(d), row(128)],
        out_shape=(out((m, d), F32), out((m, d), BF16), out((m, 128), F32)),
        compiler_params=_cparams(("parallel", "parallel")),
        name="out_proj_ab",
    )(lead, x, x, x, oa, ob, w, norm_w.reshape(1, d), wg)


def _out_c_kernel(*refs, cpt):
    h_refs, o_refs = refs[:cpt], refs[cpt:2 * cpt]
    w_ref, nw_ref, y_ref = refs[2 * cpt:]
    for r in range(cpt):
        h2 = h_refs[r][...] + jnp.dot(o_refs[r][...], w_ref[...], preferred_element_type=F32)
        y_ref[0, r * CHUNK:(r + 1) * CHUNK, :] = _rms_scale(h2, nw_ref[...])


def _out_c(h, o, w, norm_w, bsz, nc):
    d = h.shape[1]
    real = nc - 1
    cpt = _pick_tile(real, (4, 2, 1))
    nt = real // cpt

    def chunk(width, r):
        return pl.BlockSpec((CHUNK, width), lambda b, t: (b * nc + 1 + cpt * t + r, 0))

    return pl.pallas_call(
        functools.partial(_out_c_kernel, cpt=cpt),
        grid=(bsz, nt),
        in_specs=[chunk(d, r) for r in range(cpt)] + [chunk(GLA_W, r) for r in range(cpt)]
        + [pl.BlockSpec(w.shape, lambda b, t: (0, 0)), pl.BlockSpec((1, d), lambda b, t: (0, 0))],
        out_specs=pl.BlockSpec((1, cpt * CHUNK, d), lambda b, t: (b, t, 0)),
        out_shape=jax.ShapeDtypeStruct((bsz, real * CHUNK, d), F32),
        compiler_params=_cparams(("parallel", "parallel")),
        name="out_proj_c",
    )(*([h] * cpt + [o] * cpt + [w, norm_w.reshape(1, d)]))


def _split3(x):
    x1 = x.astype(BF16)
    r1 = x - x1.astype(F32)
    x2 = r1.astype(BF16)
    x3 = (r1 - x2.astype(F32)).astype(BF16)
    return x1, x2, x3


def _block_ref_rows(b, level):
    rows, width = b.shape
    half = 1 << level
    blk = 2 * half
    if blk >= 8:
        b3 = b.reshape(rows // blk, blk, width)
        ref = jnp.broadcast_to(b3[:, half - 1:half, :], b3.shape)
        return ref.reshape(rows, width)
    b3 = b.reshape(rows // 8, 8, width)
    sub = lax.broadcasted_iota(jnp.int32, b3.shape, 1)
    pick = lambda r: jnp.broadcast_to(b3[:, r:r + 1, :], b3.shape)
    if blk == 4:
        ref = jnp.where(sub < 4, pick(1), pick(5))
    else:
        ref = jnp.where(sub < 2, pick(0), jnp.where(sub < 4, pick(2), jnp.where(sub < 6, pick(4), pick(6))))
    return ref.reshape(rows, width)


def _gla_kernel(q_ref, k_ref, v_ref, z_ref, gl_ref, wg_ref, bg_ref, nw_ref, o_ref, st_ref, *, nc):
    c = pl.program_id(0)
    n = c % nc

    @pl.when(n == 0)
    def _():
        st_ref[...] = jnp.zeros_like(st_ref)

    ridx = lax.broadcasted_iota(jnp.int32, (CHUNK, 1), 0)
    ii = lax.broadcasted_iota(jnp.int32, (CHUNK, CHUNK), 0)
    jj = lax.broadcasted_iota(jnp.int32, (CHUNK, CHUNK), 1)
    pair_code = jnp.where(ii > jj, ii ^ jj, 0)
    eye = ii == jj
    tri = (ii >= jj).astype(BF16)

    gl1, gl2, _ = _split3(gl_ref[:, :GLA_RANK])
    wg1, wg2, _ = _split3(wg_ref[...])
    x = (jnp.dot(gl1, wg1, preferred_element_type=F32) + jnp.dot(gl1, wg2, preferred_element_type=F32)
         + jnp.dot(gl2, wg1, preferred_element_type=F32)) + bg_ref[...]
    log_a = (jnp.minimum(x, 0.0) - jnp.log1p(jnp.exp(-jnp.abs(x)))) * (1.0 / GLA_TAU)
    log_a = jnp.where(ridx >= jnp.where(n > 0, 0, PAD), log_a, 0.0)
    g1, g2, g3 = _split3(log_a)
    b_all = (jnp.dot(tri, g1, preferred_element_type=F32) + jnp.dot(tri, g2, preferred_element_type=F32)
             + jnp.dot(tri, g3, preferred_element_type=F32))
    for h in range(GLA_HEADS):
        kc = slice(h * GLA_DK, (h + 1) * GLA_DK)
        vc = slice(h * GLA_DV, (h + 1) * GLA_DV)
        q = q_ref[:, kc].astype(F32)
        k = k_ref[:, kc].astype(F32)
        v = v_ref[:, vc]
        b = b_all[:, kc]
        b_last = b[CHUNK - 1:CHUNK, :]
        scores = jnp.where(eye, jnp.sum(q * k, axis=1, keepdims=True), 0.0)
        for lv in range(GLA_LEVELS):
            upper = ((ridx >> lv) & 1) == 1
            fac = jnp.exp(-jnp.abs(b - _block_ref_rows(b, lv)))
            zz = (jnp.where(upper, q, k) * fac).astype(BF16)
            gram = lax.dot_general(zz, zz, (((1,), (1,)), ((), ())), preferred_element_type=F32)
            scores = scores + jnp.where((pair_code >> lv) == 1, gram, 0.0)
        o = jnp.dot(scores.astype(BF16), v, preferred_element_type=F32)
        st = st_ref[h]
        qe = (q * jnp.exp(b)).astype(BF16)
        o = o + lax.dot_general(qe, st.astype(BF16), (((1,), (1,)), ((), ())), preferred_element_type=F32)
        ke = (k * jnp.exp(b_last - b)).astype(BF16)
        kv_t = lax.dot_general(v, ke, (((0,), (0,)), ((), ())), preferred_element_type=F32)
        st_ref[h] = st * jnp.exp(b_last) + kv_t
        y = _rms_scale(o, nw_ref[:, vc])
        o_ref[:, vc] = (y * z_ref[:, vc].astype(F32)).astype(o_ref.dtype)


def _gla(proj, glow, w_gate, b_gate, norm_w, nc):
    m = proj.shape[0]
    return pl.pallas_call(
        functools.partial(_gla_kernel, nc=nc),
        grid=(m // CHUNK,),
        in_specs=[pl.BlockSpec((CHUNK, GLA_QK), lambda c: (c, 0)),
                  pl.BlockSpec((CHUNK, GLA_QK), lambda c: (c, 1)),
                  pl.BlockSpec((CHUNK, GLA_W), lambda c: (c, 1)),
                  pl.BlockSpec((CHUNK, GLA_W), lambda c: (c, 2)),
                  pl.BlockSpec((CHUNK, 128), lambda c: (c, 0)),
                  pl.BlockSpec((GLA_RANK, GLA_QK), lambda c: (0, 0)),
                  pl.BlockSpec((1, GLA_QK), lambda c: (0, 0)),
                  pl.BlockSpec((1, GLA_W), lambda c: (0, 0))],
        out_specs=pl.BlockSpec((CHUNK, GLA_W), lambda c: (c, 0)),
        out_shape=jax.ShapeDtypeStruct((m, GLA_W), BF16),
        scratch_shapes=[pltpu.VMEM((GLA_HEADS, GLA_DV, GLA_DK), F32)],
        compiler_params=_cparams(("arbitrary",)),
        name="gla",
    )(proj, proj, proj, proj, glow, w_gate.astype(F32), b_gate.reshape(1, GLA_QK).astype(F32),
      norm_w.reshape(1, GLA_W).astype(F32))


def _rope_tables(bsz, rows_per_batch):
    pos = jnp.maximum(jnp.arange(rows_per_batch, dtype=F32) - PAD, 0.0)
    inv_freq = jnp.power(ROPE_BASE, -jnp.arange(0, RET_DK, 2, dtype=F32) / RET_DK)
    ang = pos[:, None] * inv_freq[None, :]
    cos, sin = jnp.cos(ang), jnp.sin(ang)
    cos2 = jnp.concatenate([cos, cos], axis=1)
    sin2 = jnp.concatenate([-sin, sin], axis=1)
    return jnp.tile(cos2, (bsz, 1)), jnp.tile(sin2, (bsz, 1))


def kernel(x, meta, norm_ab_w, w_in_ab, ret_norm_w, s5_lam_re, s5_lam_im, s5_log_dt, s5_b_re, s5_b_im,
           s5_c_re, s5_c_im, s5_d, s5_w_glu, w_out_ab, norm_c_w, w_in_c, gla_w_gate, gla_b_gate,
           gla_norm_w, w_out_c, final_norm_w):
    bsz, seq, d = x.shape
    assert seq % CHUNK == 0 and w_in_ab.shape[0] == 1 and w_in_c.shape[0] == 1
    rpb = seq + CHUNK
    nc = rpb // CHUNK
    assert nc % ROW_CHUNKS == 0
    nt = nc // ROW_CHUNKS
    m = bsz * rpb
    tm = _pick_tile(m, (768, 384, 256, 128))
    lead = jnp.concatenate([jnp.zeros((PAD, d), x.dtype), meta.astype(x.dtype)], axis=0)

    xn0 = _embed_norm(x, lead, norm_ab_w[0], nt)
    plain, silu = ("plain", 1.0), ("silu", 1.0)
    modes_ab = [("rope", 1.0), ("rope", RET_DK ** -0.5), plain, plain, silu, silu, plain, silu]
    proj0 = _in_proj(xn0, w_in_ab[0], modes_ab, tm, rope=_rope_tables(bsz, rpb))
    o_a = _retention(proj0, ret_norm_w[0], nc)
    t, tseg = _s5_time_tile(rpb)
    p_re, p_im, bb_re, bb_im = _s5_prep(s5_lam_re[0], s5_lam_im[0], s5_log_dt[0], s5_b_re[0], s5_b_im[0], tseg)
    wb, wcre, wcim = _s5_block_weights(bb_re, bb_im, s5_c_re[0], s5_c_im[0])
    o_b = _s5(proj0, bsz, wb, wcre, wcim, p_re, p_im, s5_d[0], s5_w_glu[0].astype(BF16), t, tseg)

    n_main = 2 * GLA_QK + 2 * GLA_W
    wg = jnp.pad(w_in_c[0][:, n_main:], ((0, 0), (0, 128 - GLA_RANK))).astype(BF16)
    h1, xn1, glow = _out_ab(x, lead, o_a, o_b, w_out_ab[0].astype(BF16), norm_c_w[0], wg, nt)

    modes_c = [("plain", GLA_DK ** -0.5), plain, plain, plain, silu, silu]
    proj1 = _in_proj(xn1, w_in_c[0], modes_c, tm)
    o_c = _gla(proj1, glow, gla_w_gate[0], gla_b_gate[0], gla_norm_w[0], nc)
    return _out_c(h1, o_c, w_out_c[0].astype(BF16), final_norm_w, bsz, nc)
```

```python
import functools
import math

import jax
import jax.numpy as jnp
from jax import lax
from jax.experimental import pallas as pl
from jax.experimental.pallas import tpu as pltpu

F32 = jnp.float32
BF16 = jnp.bfloat16

N_META = 16
CHUNK = 128
PAD = CHUNK - N_META
EPS = 1e-6
ROW_CHUNKS = 3

RET_HEADS = 8
RET_DK = 128
RET_DV = 256
RET_QK = RET_HEADS * RET_DK
RET_W = RET_HEADS * RET_DV
ROPE_BASE = 10000.0

S5_W = 1024
S5_GH = 16
S5_G = 64
S5_P = 64
S5_F = S5_G * S5_P
S5_SLABS = S5_F // 128
S5_JB = 4
S5_SEGS = 8

GLA_HEADS = 4
GLA_DK = 256
GLA_DV = 512
GLA_QK = GLA_HEADS * GLA_DK
GLA_W = GLA_HEADS * GLA_DV
GLA_RANK = 16
GLA_TAU = 16.0
GLA_LEVELS = 7

PROJ_TN = 1024
VMEM_LIMIT = 56 * 1024 * 1024


def _cparams(sem):
    return pltpu.CompilerParams(dimension_semantics=sem, vmem_limit_bytes=VMEM_LIMIT)


def _silu(x):
    return x * (1.0 / (1.0 + jnp.exp(-x)))


def _pick_tile(n, candidates):
    for c in candidates:
        if n % c == 0:
            return c
    raise ValueError(f"no tile for {n}")


def _rms_scale(x, w):
    return x * lax.rsqrt(jnp.mean(x * x, axis=-1, keepdims=True) + EPS) * w


def _chunk_specs(d, nt, n_before):
    del nt
    return [pl.BlockSpec((None, CHUNK, d),
                         lambda b, t, k=k: (b, jnp.maximum(ROW_CHUNKS * t + k - n_before, 0), 0))
            for k in range(ROW_CHUNKS)]


def _embed_norm_kernel(lead_ref, xa_ref, xb_ref, xc_ref, w_ref, o_ref):
    t = pl.program_id(1)
    first = jnp.where(t == 0, lead_ref[...], xa_ref[...])
    for r, rows in enumerate((first, xb_ref[...], xc_ref[...])):
        o_ref[r * CHUNK:(r + 1) * CHUNK, :] = _rms_scale(rows, w_ref[...]).astype(o_ref.dtype)


def _embed_norm(x, lead, w, nt):
    bsz, _, d = x.shape
    tile = ROW_CHUNKS * CHUNK
    return pl.pallas_call(
        _embed_norm_kernel,
        grid=(bsz, nt),
        in_specs=[pl.BlockSpec((CHUNK, d), lambda b, t: (0, 0))] + _chunk_specs(d, nt, 1)
        + [pl.BlockSpec((1, d), lambda b, t: (0, 0))],
        out_specs=pl.BlockSpec((tile, d), lambda b, t: (b * nt + t, 0)),
        out_shape=jax.ShapeDtypeStruct((bsz * nt * tile, d), BF16),
        compiler_params=_cparams(("parallel", "parallel")),
        name="embed_norm",
    )(lead, x, x, x, w.reshape(1, d))


def _rope_rows(a, cos, sin, scale):
    outs = []
    for h in range(a.shape[1] // RET_DK):
        blk = a[:, h * RET_DK:(h + 1) * RET_DK]
        r = blk * cos + pltpu.roll(blk, RET_DK // 2, 1) * sin
        outs.append(r * scale if scale != 1.0 else r)
    return jnp.concatenate(outs, axis=1)


def _in_proj_kernel(*refs, modes, has_rope):
    if has_rope:
        x_ref, w_ref, cos_ref, sin_ref, o_ref, wbf_ref = refs
    else:
        x_ref, w_ref, o_ref, wbf_ref = refs
    j = pl.program_id(0)

    @pl.when(pl.program_id(1) == 0)
    def _():
        wbf_ref[...] = w_ref[...].astype(BF16)

    for mode in sorted(set(modes)):
        cond = functools.reduce(jnp.logical_or, [j == jj for jj, mm in enumerate(modes) if mm == mode])

        @pl.when(cond)
        def _(mode=mode):
            kind, scale = mode
            acc = jnp.dot(x_ref[...], wbf_ref[...], preferred_element_type=F32)
            if kind == "rope":
                acc = _rope_rows(acc, cos_ref[...], sin_ref[...], scale)
            elif kind == "silu":
                acc = _silu(acc)
            elif scale != 1.0:
                acc = acc * scale
            o_ref[...] = acc.astype(o_ref.dtype)


def _in_proj(x, w, modes, tm, rope=None):
    m, k = x.shape
    nt = len(modes)
    in_specs = [pl.BlockSpec((tm, k), lambda j, i: (i, 0)),
                pl.BlockSpec((None, k, PROJ_TN), lambda j, i: (0, 0, j))]
    args = [x, w]
    if rope is not None:
        in_specs += [pl.BlockSpec((tm, RET_DK), lambda j, i: (i, 0))] * 2
        args += list(rope)
    return pl.pallas_call(
        functools.partial(_in_proj_kernel, modes=tuple(modes), has_rope=rope is not None),
        grid=(nt, m // tm),
        in_specs=in_specs,
        out_specs=pl.BlockSpec((tm, PROJ_TN), lambda j, i: (i, j)),
        out_shape=jax.ShapeDtypeStruct((m, nt * PROJ_TN), BF16),
        scratch_shapes=[pltpu.VMEM((k, PROJ_TN), BF16)],
        compiler_params=_cparams(("arbitrary", "arbitrary")),
        name="in_proj",
    )(*args)


def _ret_log_decay(h):
    return math.log1p(-(2.0 ** (-5.0 - h)))


def _retention_kernel(q_ref, k_ref, v_ref, z_ref, nw_ref, o_ref, s_ref, dec_ref, *, nc):
    c = pl.program_id(0)

    @pl.when(c == 0)
    def _():
        ii = lax.broadcasted_iota(jnp.int32, (CHUNK, CHUNK), 0)
        jj = lax.broadcasted_iota(jnp.int32, (CHUNK, CHUNK), 1)
        diff = (ii - jj).astype(F32)
        for h in range(RET_HEADS):
            dec_ref[h] = jnp.where(ii >= jj, jnp.exp(_ret_log_decay(h) * jnp.maximum(diff, 0.0)), 0.0)

    @pl.when(c % nc == 0)
    def _():
        s_ref[...] = jnp.zeros_like(s_ref)

    idx = lax.broadcasted_iota(jnp.int32, (CHUNK, 1), 0).astype(F32)
    for h in range(RET_HEADS):
        lg = _ret_log_decay(h)
        q = q_ref[:, h * RET_DK:(h + 1) * RET_DK]
        k = k_ref[:, h * RET_DK:(h + 1) * RET_DK]
        v = v_ref[:, h * RET_DV:(h + 1) * RET_DV]
        scores = lax.dot_general(q, k, (((1,), (1,)), ((), ())), preferred_element_type=F32) * dec_ref[h]
        o = jnp.dot(scores.astype(BF16), v, preferred_element_type=F32)
        s_prev = s_ref[h]
        o = o + jnp.exp(lg * (idx + 1.0)) * jnp.dot(q, s_prev.astype(BF16), preferred_element_type=F32)
        k_w = (k.astype(F32) * jnp.exp(lg * (CHUNK - 1.0 - idx))).astype(BF16)
        kv = lax.dot_general(k_w, v, (((0,), (0,)), ((), ())), preferred_element_type=F32)
        s_ref[h] = s_prev * math.exp(lg * CHUNK) + kv
        y = _rms_scale(o, nw_ref[:, h * RET_DV:(h + 1) * RET_DV])
        z = z_ref[:, h * RET_DV:(h + 1) * RET_DV].astype(F32)
        o_ref[:, h * RET_DV:(h + 1) * RET_DV] = (y * z).astype(o_ref.dtype)


def _retention(proj, norm_w, nc):
    m = proj.shape[0]
    return pl.pallas_call(
        functools.partial(_retention_kernel, nc=nc),
        grid=(m // CHUNK,),
        in_specs=[pl.BlockSpec((CHUNK, RET_QK), lambda c: (c, 0)),
                  pl.BlockSpec((CHUNK, RET_QK), lambda c: (c, 1)),
                  pl.BlockSpec((CHUNK, RET_W), lambda c: (c, 1)),
                  pl.BlockSpec((CHUNK, RET_W), lambda c: (c, 2)),
                  pl.BlockSpec((1, RET_W), lambda c: (0, 0))],
        out_specs=pl.BlockSpec((CHUNK, RET_W), lambda c: (c, 0)),
        out_shape=jax.ShapeDtypeStruct((m, RET_W), BF16),
        scratch_shapes=[pltpu.VMEM((RET_HEADS, RET_DK, RET_DV), F32),
                        pltpu.VMEM((RET_HEADS, CHUNK, CHUNK), F32)],
        compiler_params=_cparams(("arbitrary",)),
        name="retention",
    )(proj, proj, proj, proj, norm_w.reshape(1, RET_W))


def _s5_prep_kernel(lre_ref, lim_ref, ldt_ref, bre_ref, bim_ref, pre_ref, pim_ref, bbre_ref, bbim_ref):
    lre = lre_ref[...]
    lim = lim_ref[...]
    dt = jnp.exp(ldt_ref[...])
    tseg = pre_ref.shape[0]
    n = lax.broadcasted_iota(jnp.int32, (tseg, 1), 0).astype(F32) + 1.0
    mag = jnp.exp(n * (lre * dt))
    ang = n * (lim * dt)
    p_re = mag * jnp.cos(ang)
    p_im = mag * jnp.sin(ang)
    pre_ref[...] = p_re
    pim_ref[...] = p_im
    a_re = p_re[0:1]
    a_im = p_im[0:1]
    den = lre * lre + lim * lim
    nr = a_re - 1.0
    f_re = (nr * lre + a_im * lim) / den
    f_im = (a_im * lre - nr * lim) / den
    b_re = bre_ref[...]
    b_im = bim_ref[...]
    bbre_ref[...] = f_re * b_re - f_im * b_im
    bbim_ref[...] = f_re * b_im + f_im * b_re


def _s5_prep(lam_re, lam_im, log_dt, b_re, b_im, tseg):
    flat = lambda a: a.reshape(1, S5_F)
    ldt = jnp.broadcast_to(log_dt[:, None], (S5_G, S5_P))
    bt = lambda a: a.reshape(S5_F, S5_GH).T
    out = jax.ShapeDtypeStruct
    return pl.pallas_call(
        _s5_prep_kernel,
        out_shape=(out((tseg, S5_F), F32), out((tseg, S5_F), F32),
                   out((S5_GH, S5_F), F32), out((S5_GH, S5_F), F32)),
        compiler_params=pltpu.CompilerParams(vmem_limit_bytes=VMEM_LIMIT),
        name="s5_prep",
    )(flat(lam_re), flat(lam_im), flat(ldt), bt(b_re), bt(b_im))


def _gelu_tanh(x):
    return 0.5 * x * (1.0 + jnp.tanh(math.sqrt(2.0 / math.pi) * (x + 0.044715 * (x * x * x))))


def _s5_kernel(u_ref, z_ref, wb_ref, wcre_ref, wcim_ref, pre_ref, pim_ref, d_ref, wglu_ref,
               o_ref, bure_ref, buim_ref, cre_ref, cim_ref, xinre_ref, xinim_ref, *, tseg, slab_group):
    ci = pl.program_id(1)

    @pl.when(ci == 0)
    def _():
        cre_ref[...] = jnp.zeros_like(cre_ref)
        cim_ref[...] = jnp.zeros_like(cim_ref)

    u_bf = u_ref[...]
    spb = S5_SLABS // S5_JB
    for j in range(S5_JB):
        res = jnp.dot(u_bf[:, j * 256:(j + 1) * 256], wb_ref[j], preferred_element_type=F32)
        for s in range(spb):
            bure_ref[j * spb + s] = res[:, s * 128:(s + 1) * 128]
            buim_ref[j * spb + s] = res[:, (spb + s) * 128:(spb + s + 1) * 128]

    for g0 in range(0, S5_SLABS, slab_group):
        slabs = list(range(g0, g0 + slab_group))
        a_re = [jnp.broadcast_to(pre_ref[sl, 0:1, :], (S5_SEGS, 128)) for sl in slabs]
        a_im = [jnp.broadcast_to(pim_ref[sl, 0:1, :], (S5_SEGS, 128)) for sl in slabs]

        def scan_body(tau, carry, slabs=slabs, a_re=a_re, a_im=a_im):
            xr, xi = carry
            nr, ni = [], []
            for n, sl in enumerate(slabs):
                rows = pl.ds(tau, S5_SEGS, stride=tseg)
                br = bure_ref[sl, rows, :]
                bi = buim_ref[sl, rows, :]
                r = a_re[n] * xr[n] - a_im[n] * xi[n] + br
                i = a_re[n] * xi[n] + a_im[n] * xr[n] + bi
                bure_ref[sl, rows, :] = r
                buim_ref[sl, rows, :] = i
                nr.append(r)
                ni.append(i)
            return tuple(nr), tuple(ni)

        zero = tuple(jnp.zeros((S5_SEGS, 128), F32) for _ in slabs)
        end_re, end_im = lax.fori_loop(0, tseg, scan_body, (zero, zero))

        for n, sl in enumerate(slabs):
            at_re = pre_ref[sl, tseg - 1:tseg, :]
            at_im = pim_ref[sl, tseg - 1:tseg, :]
            xr = cre_ref[sl, 0:1, :]
            xi = cim_ref[sl, 0:1, :]
            rows_re, rows_im = [], []
            for s in range(S5_SEGS):
                rows_re.append(xr)
                rows_im.append(xi)
                er = end_re[n][s:s + 1]
                ei = end_im[n][s:s + 1]
                xr, xi = er + at_re * xr - at_im * xi, ei + at_re * xi + at_im * xr
            cre_ref[sl] = jnp.broadcast_to(xr, (S5_SEGS, 128))
            cim_ref[sl] = jnp.broadcast_to(xi, (S5_SEGS, 128))
            xinre_ref[sl] = jnp.concatenate(rows_re, axis=0)
            xinim_ref[sl] = jnp.concatenate(rows_im, axis=0)

    def fix_body(sl, carry):
        xin_re = xinre_ref[sl]
        xin_im = xinim_ref[sl]
        for tau in range(tseg):
            rows = pl.ds(tau, S5_SEGS, stride=tseg)
            pr = pre_ref[sl, tau:tau + 1, :]
            pi = pim_ref[sl, tau:tau + 1, :]
            bure_ref[sl, rows, :] = bure_ref[sl, rows, :] + (pr * xin_re - pi * xin_im)
            buim_ref[sl, rows, :] = buim_ref[sl, rows, :] + (pr * xin_im + pi * xin_re)
        return carry

    lax.fori_loop(0, S5_SLABS, fix_body, 0)

    ys = []
    for j in range(S5_JB):
        x_re = jnp.concatenate([bure_ref[j * spb + s] for s in range(spb)], axis=1).astype(BF16)
        x_im = jnp.concatenate([buim_ref[j * spb + s] for s in range(spb)], axis=1).astype(BF16)
        ys.append(jnp.dot(x_re, wcre_ref[j], preferred_element_type=F32)
                  - jnp.dot(x_im, wcim_ref[j], preferred_element_type=F32))
    y = jnp.concatenate(ys, axis=1) + d_ref[...] * u_bf.astype(F32)
    y = _gelu_tanh(y)
    gate = jnp.dot(y.astype(BF16), wglu_ref[...], preferred_element_type=F32)
    y = y * (1.0 / (1.0 + jnp.exp(-gate)))
    o_ref[...] = (y * z_ref[...].astype(F32)).astype(o_ref.dtype)


def _s5_time_tile(rows_per_batch):
    for tseg in (66, 44, 132, 12, 4, 6, 2):
        t = S5_SEGS * tseg
        if rows_per_batch % t == 0 and t % 16 == 0:
            return t, tseg
    raise ValueError(f"no S5 time tile for {rows_per_batch}")


def _s5(proj, bsz, wb, wcre, wcim, p_re, p_im, d, wglu, t, tseg):
    m = proj.shape[0]
    nt = m // bsz // t
    ucol = (2 * RET_QK + 2 * RET_W) // S5_W
    const = lambda *shape: pl.BlockSpec(shape, lambda b, i: (0,) * len(shape))
    slabs = lambda p: p.reshape(tseg, S5_SLABS, 128).transpose(1, 0, 2)
    return pl.pallas_call(
        functools.partial(_s5_kernel, tseg=tseg, slab_group=8),
        grid=(bsz, nt),
        in_specs=[pl.BlockSpec((t, S5_W), lambda b, i: (b * nt + i, ucol)),
                  pl.BlockSpec((t, S5_W), lambda b, i: (b * nt + i, ucol + 1)),
                  const(S5_JB, 256, 2048), const(S5_JB, 1024, 256), const(S5_JB, 1024, 256),
                  const(S5_SLABS, tseg, 128), const(S5_SLABS, tseg, 128), const(1, S5_W),
                  const(S5_W, S5_W)],
        out_specs=pl.BlockSpec((t, S5_W), lambda b, i: (b * nt + i, 0)),
        out_shape=jax.ShapeDtypeStruct((m, S5_W), BF16),
        scratch_shapes=[pltpu.VMEM((S5_SLABS, t, 128), F32), pltpu.VMEM((S5_SLABS, t, 128), F32)]
        + [pltpu.VMEM((S5_SLABS, S5_SEGS, 128), F32)] * 4,
        compiler_params=_cparams(("arbitrary", "arbitrary")),
        name="s5",
    )(proj, proj, wb, wcre, wcim, slabs(p_re), slabs(p_im), d.reshape(1, S5_W), wglu)


def _s5_block_weights(bb_re, bb_im, c_re, c_im):
    eye = jnp.eye(16, dtype=F32)

    def wb_part(bb):
        a = bb.reshape(S5_GH, S5_JB, 16, S5_P)
        w = jnp.einsum('hjgp,gk->jghkp', a, eye)
        return w.reshape(S5_JB, 256, 1024)

    wb = jnp.concatenate([wb_part(bb_re), wb_part(bb_im)], axis=2).astype(BF16)

    def wc_part(c):
        a = c.astype(F32).reshape(S5_JB, 16, S5_GH, S5_P)
        w = jnp.einsum('jghp,gk->jgpkh', a, eye)
        return w.reshape(S5_JB, 1024, 256).astype(BF16)

    return wb, wc_part(c_re), wc_part(c_im)


def _out_ab_kernel(lead_ref, xa_ref, xb_ref, xc_ref, oa_ref, ob_ref, w_ref, nw_ref, wg_ref,
                   h1_ref, xn_ref, gl_ref):
    t = pl.program_id(1)
    first = jnp.where(t == 0, lead_ref[...], xa_ref[...])
    for r, rows in enumerate((first, xb_ref[...], xc_ref[...])):
        sl = slice(r * CHUNK, (r + 1) * CHUNK)
        acc = jnp.dot(oa_ref[sl, :], w_ref[:RET_W, :], preferred_element_type=F32)
        acc = acc + jnp.dot(ob_ref[sl, :], w_ref[RET_W:, :], preferred_element_type=F32)
        h1 = rows + acc
        h1_ref[sl, :] = h1
        xn = _rms_scale(h1, nw_ref[...]).astype(BF16)
        xn_ref[sl, :] = xn
        gl_ref[sl, :] = jnp.dot(xn, wg_ref[...], preferred_element_type=F32)


def _out_ab(x, lead, oa, ob, w, norm_w, wg, nt):
    bsz, _, d = x.shape
    tile = ROW_CHUNKS * CHUNK
    m = bsz * nt * tile
    row = lambda width: pl.BlockSpec((tile, width), lambda b, t: (b * nt + t, 0))
    const = lambda a: pl.BlockSpec(a.shape, lambda b, t: (0, 0))
    out = jax.ShapeDtypeStruct
    return pl.pallas_call(
        _out_ab_kernel,
        grid=(bsz, nt),
        in_specs=[pl.BlockSpec((CHUNK, d), lambda b, t: (0, 0))] + _chunk_specs(d, nt, 1)
        + [row(RET_W), row(S5_W), const(w), pl.BlockSpec((1, d), lambda b, t: (0, 0)), const(wg)],
        out_specs=[row(d), row(d), row(128)],
        out_shape=(out((m, d), F32), out((m, d), BF16), out((m, 128), F32)),
        compiler_params=_cparams(("parallel", "parallel")),
        name="out_proj_ab",
    )(lead, x, x, x, oa, ob, w, norm_w.reshape(1, d), wg)


LOG2E = 1.4426950408889634


def _split2(x):
    x1 = x.astype(BF16)
    return x1, (x - x1.astype(F32)).astype(BF16)


def _neg_abs(x):
    return pltpu.bitcast(pltpu.bitcast(x, jnp.uint32) | jnp.uint32(0x80000000), F32)


def _block_ref_rows(b, level):
    rows, width = b.shape
    half = 1 << level
    blk = 2 * half
    if blk >= 8:
        b3 = b.reshape(rows // blk, blk, width)
        ref = jnp.broadcast_to(b3[:, half - 1:half, :], b3.shape)
        return ref.reshape(rows, width)
    b3 = b.reshape(rows // 8, 8, width)
    sub = lax.broadcasted_iota(jnp.int32, b3.shape, 1)
    pick = lambda r: jnp.broadcast_to(b3[:, r:r + 1, :], b3.shape)
    if blk == 4:
        ref = jnp.where(sub < 4, pick(1), pick(5))
    else:
        ref = jnp.where(sub < 2, pick(0), jnp.where(sub < 4, pick(2), jnp.where(sub < 6, pick(4), pick(6))))
    return ref.reshape(rows, width)


def _gla_kernel(q_ref, k_ref, v_ref, z_ref, gl_ref, wg_ref, bg_ref, nw_ref, h_ref, wo_ref, fw_ref,
                y_ref, st_ref, mix_ref, *, nc, nchunks):
    c = pl.program_id(0)
    n = jnp.minimum(c, nchunks - 1) % nc

    @pl.when(c == 0)
    def _():
        mix_ref[...] = jnp.zeros_like(mix_ref)

    @pl.when(n == 0)
    def _():
        st_ref[...] = jnp.zeros_like(st_ref)


    ridx = lax.broadcasted_iota(jnp.int32, (CHUNK, 1), 0)
    ii = lax.broadcasted_iota(jnp.int32, (CHUNK, CHUNK), 0)
    jj = lax.broadcasted_iota(jnp.int32, (CHUNK, CHUNK), 1)
    pair_code = jnp.where(ii > jj, ii ^ jj, 0)
    eye = ii == jj
    tri = (ii >= jj).astype(BF16)

    gl1, gl2 = _split2(gl_ref[:, :GLA_RANK])
    wg1, wg2 = _split2(wg_ref[...])
    x = (jnp.dot(gl1, wg1, preferred_element_type=F32) + jnp.dot(gl1, wg2, preferred_element_type=F32)
         + jnp.dot(gl2, wg1, preferred_element_type=F32)) + bg_ref[...]
    log_a = (jnp.minimum(x, 0.0) - jnp.log(1.0 + jnp.exp(-jnp.abs(x)))) * (LOG2E / GLA_TAU)
    log_a = jnp.where(ridx >= jnp.where(n > 0, 0, PAD), log_a, 0.0)
    g1, g2 = _split2(log_a)
    b_all = jnp.dot(tri, g1, preferred_element_type=F32) + jnp.dot(tri, g2, preferred_element_type=F32)
    h2 = h_ref[...]
    for h in range(GLA_HEADS):
        kc = slice(h * GLA_DK, (h + 1) * GLA_DK)
        vc = slice(h * GLA_DV, (h + 1) * GLA_DV)
        h2 = h2 + jnp.dot(mix_ref[:, vc], wo_ref[vc, :], preferred_element_type=F32)
        q = q_ref[:, kc].astype(F32)
        k = k_ref[:, kc].astype(F32)
        v = v_ref[:, vc]
        b = b_all[:, kc]
        b_last = b[CHUNK - 1:CHUNK, :]
        scores = jnp.where(eye, jnp.sum(q * k, axis=1, keepdims=True), 0.0)
        for lv in range(GLA_LEVELS):
            upper = ((ridx >> lv) & 1) == 1
            fac = jnp.exp2(_neg_abs(b - _block_ref_rows(b, lv)))
            zz = (jnp.where(upper, q, k) * fac).astype(BF16)
            gram = lax.dot_general(zz, zz, (((1,), (1,)), ((), ())), preferred_element_type=F32)
            scores = scores + jnp.where((pair_code >> lv) == 1, gram, 0.0)
        o = jnp.dot(scores.astype(BF16), v, preferred_element_type=F32)
        st = st_ref[h]
        qe = (q * jnp.exp2(b)).astype(BF16)
        o = o + lax.dot_general(qe, st.astype(BF16), (((1,), (1,)), ((), ())), preferred_element_type=F32)
        ke = (k * jnp.exp2(b_last - b)).astype(BF16)
        kv_t = lax.dot_general(v, ke, (((0,), (0,)), ((), ())), preferred_element_type=F32)
        st_ref[h] = st * jnp.exp2(b_last) + kv_t
        mix_ref[:, vc] = (_rms_scale(o, nw_ref[:, vc]) * z_ref[:, vc].astype(F32)).astype(BF16)

    y_ref[...] = _rms_scale(h2, fw_ref[...])


def _gla(proj, glow, w_gate, b_gate, norm_w, h1, w_out, final_w, bsz, nc):
    m, d = h1.shape
    nchunks = m // CHUNK
    const = lambda *shape: pl.BlockSpec(shape, lambda c: (0,) * len(shape))
    cur = lambda width, col: pl.BlockSpec((CHUNK, width), lambda c: (jnp.minimum(c, nchunks - 1), col))
    prev = lambda c: jnp.maximum(c - 1, 0)
    return pl.pallas_call(
        functools.partial(_gla_kernel, nc=nc, nchunks=nchunks),
        grid=(nchunks + 1,),
        in_specs=[cur(GLA_QK, 0), cur(GLA_QK, 1), cur(GLA_W, 1), cur(GLA_W, 2), cur(128, 0),
                  const(GLA_RANK, GLA_QK), const(1, GLA_QK), const(1, GLA_W),
                  pl.BlockSpec((CHUNK, d), lambda c: (prev(c), 0)),
                  const(GLA_W, d), const(1, d)],
        out_specs=pl.BlockSpec((None, CHUNK, d),
                               lambda c: (prev(c) // nc, jnp.maximum(prev(c) % nc - 1, 0), 0)),
        out_shape=jax.ShapeDtypeStruct((bsz, (nc - 1) * CHUNK, d), F32),
        scratch_shapes=[pltpu.VMEM((GLA_HEADS, GLA_DV, GLA_DK), F32), pltpu.VMEM((CHUNK, GLA_W), BF16)],
        compiler_params=_cparams(("arbitrary",)),
        name="gla_out",
    )(proj, proj, proj, proj, glow, w_gate.astype(F32), b_gate.reshape(1, GLA_QK).astype(F32),
      norm_w.reshape(1, GLA_W).astype(F32), h1, w_out, final_w.reshape(1, d).astype(F32))


def _rope_tables(bsz, rows_per_batch):
    pos = jnp.maximum(jnp.arange(rows_per_batch, dtype=F32) - PAD, 0.0)
    inv_freq = jnp.power(ROPE_BASE, -jnp.arange(0, RET_DK, 2, dtype=F32) / RET_DK)
    ang = pos[:, None] * inv_freq[None, :]
    cos, sin = jnp.cos(ang), jnp.sin(ang)
    cos2 = jnp.concatenate([cos, cos], axis=1)
    sin2 = jnp.concatenate([-sin, sin], axis=1)
    return jnp.tile(cos2, (bsz, 1)), jnp.tile(sin2, (bsz, 1))


def kernel(x, meta, norm_ab_w, w_in_ab, ret_norm_w, s5_lam_re, s5_lam_im, s5_log_dt, s5_b_re, s5_b_im,
           s5_c_re, s5_c_im, s5_d, s5_w_glu, w_out_ab, norm_c_w, w_in_c, gla_w_gate, gla_b_gate,
           gla_norm_w, w_out_c, final_norm_w):
    bsz, seq, d = x.shape
    assert seq % CHUNK == 0 and w_in_ab.shape[0] == 1 and w_in_c.shape[0] == 1
    rpb = seq + CHUNK
    nc = rpb // CHUNK
    assert nc % ROW_CHUNKS == 0
    nt = nc // ROW_CHUNKS
    m = bsz * rpb
    tm = _pick_tile(m, (768, 384, 256, 128))
    lead = jnp.concatenate([jnp.zeros((PAD, d), x.dtype), meta.astype(x.dtype)], axis=0)

    xn0 = _embed_norm(x, lead, norm_ab_w[0], nt)
    plain, silu = ("plain", 1.0), ("silu", 1.0)
    modes_ab = [("rope", 1.0), ("rope", RET_DK ** -0.5), plain, plain, silu, silu, plain, silu]
    proj0 = _in_proj(xn0, w_in_ab, modes_ab, tm, rope=_rope_tables(bsz, rpb))
    o_a = _retention(proj0, ret_norm_w[0], nc)
    t, tseg = _s5_time_tile(rpb)
    p_re, p_im, bb_re, bb_im = _s5_prep(s5_lam_re[0], s5_lam_im[0], s5_log_dt[0], s5_b_re[0], s5_b_im[0], tseg)
    wb, wcre, wcim = _s5_block_weights(bb_re, bb_im, s5_c_re[0], s5_c_im[0])
    o_b = _s5(proj0, bsz, wb, wcre, wcim, p_re, p_im, s5_d[0], s5_w_glu[0].astype(BF16), t, tseg)

    n_main = 2 * GLA_QK + 2 * GLA_W
    wg = jnp.pad(w_in_c[0][:, n_main:], ((0, 0), (0, 128 - GLA_RANK))).astype(BF16)
    h1, xn1, glow = _out_ab(x, lead, o_a, o_b, w_out_ab[0].astype(BF16), norm_c_w[0], wg, nt)

    modes_c = [("plain", GLA_DK ** -0.5), plain, plain, plain, silu, silu]
    proj1 = _in_proj(xn1, w_in_c, modes_c, tm)
    return _gla(proj1, glow, gla_w_gate[0], gla_b_gate[0], gla_norm_w[0], h1, w_out_c[0].astype(BF16),
                final_norm_w, bsz, nc)
```

```python
import functools
import math

import jax
import jax.numpy as jnp
from jax import lax
from jax.experimental import pallas as pl
from jax.experimental.pallas import tpu as pltpu

F32 = jnp.float32
BF16 = jnp.bfloat16

N_META = 16
CHUNK = 128
PAD = CHUNK - N_META
EPS = 1e-6
ROW_CHUNKS = 3

RET_HEADS = 8
RET_DK = 128
RET_DV = 256
RET_QK = RET_HEADS * RET_DK
RET_W = RET_HEADS * RET_DV
ROPE_BASE = 10000.0

S5_W = 1024
S5_GH = 16
S5_G = 64
S5_P = 64
S5_F = S5_G * S5_P
S5_SLABS = S5_F // 128
S5_JB = 4
S5_SEGS = 8

GLA_HEADS = 4
GLA_DK = 256
GLA_DV = 512
GLA_QK = GLA_HEADS * GLA_DK
GLA_W = GLA_HEADS * GLA_DV
GLA_RANK = 16
GLA_TAU = 16.0
GLA_LEVELS = 7

PROJ_TN = 1024
VMEM_LIMIT = 56 * 1024 * 1024


def _cparams(sem):
    return pltpu.CompilerParams(dimension_semantics=sem, vmem_limit_bytes=VMEM_LIMIT)


def _silu(x):
    return x * (1.0 / (1.0 + jnp.exp(-x)))


def _pick_tile(n, candidates):
    for c in candidates:
        if n % c == 0:
            return c
    raise ValueError(f"no tile for {n}")


def _rms_scale(x, w):
    return x * lax.rsqrt(jnp.mean(x * x, axis=-1, keepdims=True) + EPS) * w


def _chunk_specs(d, nt, n_before):
    del nt
    return [pl.BlockSpec((None, CHUNK, d),
                         lambda b, t, k=k: (b, jnp.maximum(ROW_CHUNKS * t + k - n_before, 0), 0))
            for k in range(ROW_CHUNKS)]


def _embed_norm_kernel(lead_ref, xa_ref, xb_ref, xc_ref, w_ref, o_ref):
    t = pl.program_id(1)
    first = jnp.where(t == 0, lead_ref[...], xa_ref[...])
    for r, rows in enumerate((first, xb_ref[...], xc_ref[...])):
        o_ref[r * CHUNK:(r + 1) * CHUNK, :] = _rms_scale(rows, w_ref[...]).astype(o_ref.dtype)


def _embed_norm(x, lead, w, nt):
    bsz, _, d = x.shape
    tile = ROW_CHUNKS * CHUNK
    return pl.pallas_call(
        _embed_norm_kernel,
        grid=(bsz, nt),
        in_specs=[pl.BlockSpec((CHUNK, d), lambda b, t: (0, 0))] + _chunk_specs(d, nt, 1)
        + [pl.BlockSpec((1, d), lambda b, t: (0, 0))],
        out_specs=pl.BlockSpec((tile, d), lambda b, t: (b * nt + t, 0)),
        out_shape=jax.ShapeDtypeStruct((bsz * nt * tile, d), BF16),
        compiler_params=_cparams(("parallel", "parallel")),
        name="embed_norm",
    )(lead, x, x, x, w.reshape(1, d))


def _rope_rows(a, cos, sin, scale):
    outs = []
    for h in range(a.shape[1] // RET_DK):
        blk = a[:, h * RET_DK:(h + 1) * RET_DK]
        r = blk * cos + pltpu.roll(blk, RET_DK // 2, 1) * sin
        outs.append(r * scale if scale != 1.0 else r)
    return jnp.concatenate(outs, axis=1)


def _in_proj_kernel(*refs, modes, has_rope, w_is_transposed):
    if has_rope:
        x_ref, w_ref, cos_ref, sin_ref, o_ref, wbf_ref = refs
    else:
        x_ref, w_ref, o_ref, wbf_ref = refs
    j = pl.program_id(0)

    @pl.when(pl.program_id(1) == 0)
    def _():
        wbf_ref[...] = w_ref[...].astype(BF16)

    contract = (((1,), (1 if w_is_transposed else 0,)), ((), ()))
    for mode in sorted(set(modes)):
        cond = functools.reduce(jnp.logical_or, [j == jj for jj, mm in enumerate(modes) if mm == mode])

        @pl.when(cond)
        def _(mode=mode):
            kind, scale = mode
            acc = lax.dot_general(x_ref[...], wbf_ref[...], contract, preferred_element_type=F32)
            if kind == "rope":
                acc = _rope_rows(acc, cos_ref[...], sin_ref[...], scale)
            elif kind == "silu":
                acc = _silu(acc)
            elif scale != 1.0:
                acc = acc * scale
            o_ref[...] = acc.astype(o_ref.dtype)


def _in_proj(x, w, modes, tm, rope=None, w_is_transposed=False):
    m, k = x.shape
    nt = len(modes)
    if w_is_transposed:
        w_spec = pl.BlockSpec((None, PROJ_TN, k), lambda j, i: (0, j, 0))
        w_scratch = pltpu.VMEM((PROJ_TN, k), BF16)
    else:
        w_spec = pl.BlockSpec((None, k, PROJ_TN), lambda j, i: (0, 0, j))
        w_scratch = pltpu.VMEM((k, PROJ_TN), BF16)
    in_specs = [pl.BlockSpec((tm, k), lambda j, i: (i, 0)), w_spec]
    args = [x, w]
    if rope is not None:
        in_specs += [pl.BlockSpec((tm, RET_DK), lambda j, i: (i, 0))] * 2
        args += list(rope)
    return pl.pallas_call(
        functools.partial(_in_proj_kernel, modes=tuple(modes), has_rope=rope is not None,
                          w_is_transposed=w_is_transposed),
        grid=(nt, m // tm),
        in_specs=in_specs,
        out_specs=pl.BlockSpec((tm, PROJ_TN), lambda j, i: (i, j)),
        out_shape=jax.ShapeDtypeStruct((m, nt * PROJ_TN), BF16),
        scratch_shapes=[w_scratch],
        compiler_params=_cparams(("arbitrary", "arbitrary")),
        name="in_proj",
    )(*args)


def _ret_log_decay(h):
    return math.log1p(-(2.0 ** (-5.0 - h)))


def _retention_kernel(q_ref, k_ref, v_ref, z_ref, nw_ref, o_ref, s_ref, dec_ref, *, nc):
    c = pl.program_id(0)

    @pl.when(c == 0)
    def _():
        ii = lax.broadcasted_iota(jnp.int32, (CHUNK, CHUNK), 0)
        jj = lax.broadcasted_iota(jnp.int32, (CHUNK, CHUNK), 1)
        diff = (ii - jj).astype(F32)
        for h in range(RET_HEADS):
            dec_ref[h] = jnp.where(ii >= jj, jnp.exp(_ret_log_decay(h) * jnp.maximum(diff, 0.0)), 0.0)

    @pl.when(c % nc == 0)
    def _():
        s_ref[...] = jnp.zeros_like(s_ref)

    idx = lax.broadcasted_iota(jnp.int32, (CHUNK, 1), 0).astype(F32)
    for h in range(RET_HEADS):
        lg = _ret_log_decay(h)
        q = q_ref[:, h * RET_DK:(h + 1) * RET_DK]
        k = k_ref[:, h * RET_DK:(h + 1) * RET_DK]
        v = v_ref[:, h * RET_DV:(h + 1) * RET_DV]
        scores = lax.dot_general(q, k, (((1,), (1,)), ((), ())), preferred_element_type=F32) * dec_ref[h]
        o = jnp.dot(scores.astype(BF16), v, preferred_element_type=F32)
        s_prev = s_ref[h]
        o = o + jnp.exp(lg * (idx + 1.0)) * jnp.dot(q, s_prev.astype(BF16), preferred_element_type=F32)
        k_w = (k.astype(F32) * jnp.exp(lg * (CHUNK - 1.0 - idx))).astype(BF16)
        kv = lax.dot_general(k_w, v, (((0,), (0,)), ((), ())), preferred_element_type=F32)
        s_ref[h] = s_prev * math.exp(lg * CHUNK) + kv
        y = _rms_scale(o, nw_ref[:, h * RET_DV:(h + 1) * RET_DV])
        z = z_ref[:, h * RET_DV:(h + 1) * RET_DV].astype(F32)
        o_ref[:, h * RET_DV:(h + 1) * RET_DV] = (y * z).astype(o_ref.dtype)


def _retention(proj, norm_w, nc):
    m = proj.shape[0]
    return pl.pallas_call(
        functools.partial(_retention_kernel, nc=nc),
        grid=(m // CHUNK,),
        in_specs=[pl.BlockSpec((CHUNK, RET_QK), lambda c: (c, 0)),
                  pl.BlockSpec((CHUNK, RET_QK), lambda c: (c, 1)),
                  pl.BlockSpec((CHUNK, RET_W), lambda c: (c, 1)),
                  pl.BlockSpec((CHUNK, RET_W), lambda c: (c, 2)),
                  pl.BlockSpec((1, RET_W), lambda c: (0, 0))],
        out_specs=pl.BlockSpec((CHUNK, RET_W), lambda c: (c, 0)),
        out_shape=jax.ShapeDtypeStruct((m, RET_W), BF16),
        scratch_shapes=[pltpu.VMEM((RET_HEADS, RET_DK, RET_DV), F32),
                        pltpu.VMEM((RET_HEADS, CHUNK, CHUNK), F32)],
        compiler_params=_cparams(("arbitrary",)),
        name="retention",
    )(proj, proj, proj, proj, norm_w.reshape(1, RET_W))


def _s5_prep_kernel(lre_ref, lim_ref, ldt_ref, bre_ref, bim_ref, pre_ref, pim_ref, bbre_ref, bbim_ref):
    lre = lre_ref[...]
    lim = lim_ref[...]
    dt = jnp.exp(ldt_ref[...])
    tseg = pre_ref.shape[0]
    n = lax.broadcasted_iota(jnp.int32, (tseg, 1), 0).astype(F32) + 1.0
    mag = jnp.exp(n * (lre * dt))
    ang = n * (lim * dt)
    p_re = mag * jnp.cos(ang)
    p_im = mag * jnp.sin(ang)
    pre_ref[...] = p_re
    pim_ref[...] = p_im
    a_re = p_re[0:1]
    a_im = p_im[0:1]
    den = lre * lre + lim * lim
    nr = a_re - 1.0
    f_re = (nr * lre + a_im * lim) / den
    f_im = (a_im * lre - nr * lim) / den
    b_re = bre_ref[...]
    b_im = bim_ref[...]
    bbre_ref[...] = f_re * b_re - f_im * b_im
    bbim_ref[...] = f_re * b_im + f_im * b_re


def _s5_prep(lam_re, lam_im, log_dt, b_re, b_im, tseg):
    flat = lambda a: a.reshape(1, S5_F)
    ldt = jnp.broadcast_to(log_dt[:, None], (S5_G, S5_P))
    bt = lambda a: a.reshape(S5_F, S5_GH).T
    out = jax.ShapeDtypeStruct
    return pl.pallas_call(
        _s5_prep_kernel,
        out_shape=(out((tseg, S5_F), F32), out((tseg, S5_F), F32),
                   out((S5_GH, S5_F), F32), out((S5_GH, S5_F), F32)),
        compiler_params=pltpu.CompilerParams(vmem_limit_bytes=VMEM_LIMIT),
        name="s5_prep",
    )(flat(lam_re), flat(lam_im), flat(ldt), bt(b_re), bt(b_im))


def _gelu_tanh(x):
    return 0.5 * x * (1.0 + jnp.tanh(math.sqrt(2.0 / math.pi) * (x + 0.044715 * (x * x * x))))


def _s5_kernel(u_ref, z_ref, wb_ref, wcre_ref, wcim_ref, pre_ref, pim_ref, d_ref, wglu_ref,
               o_ref, bure_ref, buim_ref, cre_ref, cim_ref, xinre_ref, xinim_ref, *, tseg, slab_group):
    ci = pl.program_id(1)

    @pl.when(ci == 0)
    def _():
        cre_ref[...] = jnp.zeros_like(cre_ref)
        cim_ref[...] = jnp.zeros_like(cim_ref)

    u_bf = u_ref[...]
    spb = S5_SLABS // S5_JB
    for j in range(S5_JB):
        res = jnp.dot(u_bf[:, j * 256:(j + 1) * 256], wb_ref[j], preferred_element_type=F32)
        for s in range(spb):
            bure_ref[j * spb + s] = res[:, s * 128:(s + 1) * 128]
            buim_ref[j * spb + s] = res[:, (spb + s) * 128:(spb + s + 1) * 128]

    for g0 in range(0, S5_SLABS, slab_group):
        slabs = list(range(g0, g0 + slab_group))
        a_re = [jnp.broadcast_to(pre_ref[sl, 0:1, :], (S5_SEGS, 128)) for sl in slabs]
        a_im = [jnp.broadcast_to(pim_ref[sl, 0:1, :], (S5_SEGS, 128)) for sl in slabs]

        def scan_body(tau, carry, slabs=slabs, a_re=a_re, a_im=a_im):
            xr, xi = carry
            nr, ni = [], []
            for n, sl in enumerate(slabs):
                rows = pl.ds(tau, S5_SEGS, stride=tseg)
                br = bure_ref[sl, rows, :]
                bi = buim_ref[sl, rows, :]
                r = a_re[n] * xr[n] - a_im[n] * xi[n] + br
                i = a_re[n] * xi[n] + a_im[n] * xr[n] + bi
                bure_ref[sl, rows, :] = r
                buim_ref[sl, rows, :] = i
                nr.append(r)
                ni.append(i)
            return tuple(nr), tuple(ni)

        zero = tuple(jnp.zeros((S5_SEGS, 128), F32) for _ in slabs)
        end_re, end_im = lax.fori_loop(0, tseg, scan_body, (zero, zero))

        for n, sl in enumerate(slabs):
            at_re = pre_ref[sl, tseg - 1:tseg, :]
            at_im = pim_ref[sl, tseg - 1:tseg, :]
            xr = cre_ref[sl, 0:1, :]
            xi = cim_ref[sl, 0:1, :]
            rows_re, rows_im = [], []
            for s in range(S5_SEGS):
                rows_re.append(xr)
                rows_im.append(xi)
                er = end_re[n][s:s + 1]
                ei = end_im[n][s:s + 1]
                xr, xi = er + at_re * xr - at_im * xi, ei + at_re * xi + at_im * xr
            cre_ref[sl] = jnp.broadcast_to(xr, (S5_SEGS, 128))
            cim_ref[sl] = jnp.broadcast_to(xi, (S5_SEGS, 128))
            xinre_ref[sl] = jnp.concatenate(rows_re, axis=0)
            xinim_ref[sl] = jnp.concatenate(rows_im, axis=0)

    def fix_body(sl, carry):
        xin_re = xinre_ref[sl]
        xin_im = xinim_ref[sl]
        for tau in range(tseg):
            rows = pl.ds(tau, S5_SEGS, stride=tseg)
            pr = pre_ref[sl, tau:tau + 1, :]
            pi = pim_ref[sl, tau:tau + 1, :]
            bure_ref[sl, rows, :] = bure_ref[sl, rows, :] + (pr * xin_re - pi * xin_im)
            buim_ref[sl, rows, :] = buim_ref[sl, rows, :] + (pr * xin_im + pi * xin_re)
        return carry

    lax.fori_loop(0, S5_SLABS, fix_body, 0)

    ys = []
    for j in range(S5_JB):
        x_re = jnp.concatenate([bure_ref[j * spb + s] for s in range(spb)], axis=1).astype(BF16)
        x_im = jnp.concatenate([buim_ref[j * spb + s] for s in range(spb)], axis=1).astype(BF16)
        ys.append(jnp.dot(x_re, wcre_ref[j], preferred_element_type=F32)
                  - jnp.dot(x_im, wcim_ref[j], preferred_element_type=F32))
    y = jnp.concatenate(ys, axis=1) + d_ref[...] * u_bf.astype(F32)
    y = _gelu_tanh(y)
    gate = jnp.dot(y.astype(BF16), wglu_ref[...], preferred_element_type=F32)
    y = y * (1.0 / (1.0 + jnp.exp(-gate)))
    o_ref[...] = (y * z_ref[...].astype(F32)).astype(o_ref.dtype)


def _s5_time_tile(rows_per_batch):
    for tseg in (66, 44, 132, 12, 4, 6, 2):
        t = S5_SEGS * tseg
        if rows_per_batch % t == 0 and t % 16 == 0:
            return t, tseg
    raise ValueError(f"no S5 time tile for {rows_per_batch}")


def _s5(proj, bsz, wb, wcre, wcim, p_re, p_im, d, wglu, t, tseg):
    m = proj.shape[0]
    nt = m // bsz // t
    ucol = (2 * RET_QK + 2 * RET_W) // S5_W
    const = lambda *shape: pl.BlockSpec(shape, lambda b, i: (0,) * len(shape))
    slabs = lambda p: p.reshape(tseg, S5_SLABS, 128).transpose(1, 0, 2)
    return pl.pallas_call(
        functools.partial(_s5_kernel, tseg=tseg, slab_group=8),
        grid=(bsz, nt),
        in_specs=[pl.BlockSpec((t, S5_W), lambda b, i: (b * nt + i, ucol)),
                  pl.BlockSpec((t, S5_W), lambda b, i: (b * nt + i, ucol + 1)),
                  const(S5_JB, 256, 2048), const(S5_JB, 1024, 256), const(S5_JB, 1024, 256),
                  const(S5_SLABS, tseg, 128), const(S5_SLABS, tseg, 128), const(1, S5_W),
                  const(S5_W, S5_W)],
        out_specs=pl.BlockSpec((t, S5_W), lambda b, i: (b * nt + i, 0)),
        out_shape=jax.ShapeDtypeStruct((m, S5_W), BF16),
        scratch_shapes=[pltpu.VMEM((S5_SLABS, t, 128), F32), pltpu.VMEM((S5_SLABS, t, 128), F32)]
        + [pltpu.VMEM((S5_SLABS, S5_SEGS, 128), F32)] * 4,
        compiler_params=_cparams(("arbitrary", "arbitrary")),
        name="s5",
    )(proj, proj, wb, wcre, wcim, slabs(p_re), slabs(p_im), d.reshape(1, S5_W), wglu)


def _s5_block_weights(bb_re, bb_im, c_re, c_im):
    eye = jnp.eye(16, dtype=F32)

    def wb_part(bb):
        a = bb.reshape(S5_GH, S5_JB, 16, S5_P)
        w = jnp.einsum('hjgp,gk->jghkp', a, eye)
        return w.reshape(S5_JB, 256, 1024)

    wb = jnp.concatenate([wb_part(bb_re), wb_part(bb_im)], axis=2).astype(BF16)

    def wc_part(c):
        a = c.astype(F32).reshape(S5_JB, 16, S5_GH, S5_P)
        w = jnp.einsum('jghp,gk->jgpkh', a, eye)
        return w.reshape(S5_JB, 1024, 256).astype(BF16)

    return wb, wc_part(c_re), wc_part(c_im)


def _out_ab_kernel(lead_ref, xa_ref, xb_ref, xc_ref, oa_ref, ob_ref, w_ref, nw_ref, wg_ref,
                   h1_ref, xn_ref, gl_ref):
    t = pl.program_id(1)
    first = jnp.where(t == 0, lead_ref[...], xa_ref[...])
    for r, rows in enumerate((first, xb_ref[...], xc_ref[...])):
        sl = slice(r * CHUNK, (r + 1) * CHUNK)
        acc = jnp.dot(oa_ref[sl, :], w_ref[:RET_W, :], preferred_element_type=F32)
        acc = acc + jnp.dot(ob_ref[sl, :], w_ref[RET_W:, :], preferred_element_type=F32)
        h1 = rows + acc
        h1_ref[sl, :] = h1
        xn = _rms_scale(h1, nw_ref[...]).astype(BF16)
        xn_ref[sl, :] = xn
        gl_ref[sl, :] = lax.dot_general(xn, wg_ref[...], (((1,), (1,)), ((), ())),
                                        preferred_element_type=F32)


def _out_ab(x, lead, oa, ob, w, norm_w, wg, nt):
    bsz, _, d = x.shape
    tile = ROW_CHUNKS * CHUNK
    m = bsz * nt * tile
    row = lambda width: pl.BlockSpec((tile, width), lambda b, t: (b * nt + t, 0))
    const = lambda a: pl.BlockSpec(a.shape, lambda b, t: (0, 0))
    out = jax.ShapeDtypeStruct
    return pl.pallas_call(
        _out_ab_kernel,
        grid=(bsz, nt),
        in_specs=[pl.BlockSpec((CHUNK, d), lambda b, t: (0, 0))] + _chunk_specs(d, nt, 1)
        + [row(RET_W), row(S5_W), const(w), pl.BlockSpec((1, d), lambda b, t: (0, 0)), const(wg)],
        out_specs=[row(d), row(d), row(128)],
        out_shape=(out((m, d), F32), out((m, d), BF16), out((m, 128), F32)),
        compiler_params=_cparams(("parallel", "parallel")),
        name="out_proj_ab",
    )(lead, x, x, x, oa, ob, w, norm_w.reshape(1, d), wg)


LOG2E = 1.4426950408889634


def _split2(x):
    x1 = x.astype(BF16)
    return x1, (x - x1.astype(F32)).astype(BF16)


def _neg_abs(x):
    return -jnp.abs(x)


def _block_ref_rows(b, level):
    rows, width = b.shape
    half = 1 << level
    blk = 2 * half
    if blk >= 8:
        b3 = b.reshape(rows // blk, blk, width)
        ref = jnp.broadcast_to(b3[:, half - 1:half, :], b3.shape)
        return ref.reshape(rows, width)
    b3 = b.reshape(rows // 8, 8, width)
    sub = lax.broadcasted_iota(jnp.int32, b3.shape, 1)
    pick = lambda r: jnp.broadcast_to(b3[:, r:r + 1, :], b3.shape)
    if blk == 4:
        ref = jnp.where(sub < 4, pick(1), pick(5))
    else:
        ref = jnp.where(sub < 2, pick(0), jnp.where(sub < 4, pick(2), jnp.where(sub < 6, pick(4), pick(6))))
    return ref.reshape(rows, width)


def _gla_kernel(q_ref, k_ref, v_ref, z_ref, gl_ref, wg_ref, bg_ref, nw_ref, h_ref, wo_ref, fw_ref,
                y_ref, st_ref, mix_ref, *, nc, nchunks):
    c = pl.program_id(0)
    n = jnp.minimum(c, nchunks - 1) % nc

    @pl.when(c == 0)
    def _():
        mix_ref[...] = jnp.zeros_like(mix_ref)

    @pl.when(n == 0)
    def _():
        st_ref[...] = jnp.zeros_like(st_ref)


    ridx = lax.broadcasted_iota(jnp.int32, (CHUNK, 1), 0)
    ii = lax.broadcasted_iota(jnp.int32, (CHUNK, CHUNK), 0)
    jj = lax.broadcasted_iota(jnp.int32, (CHUNK, CHUNK), 1)
    pair_code = jnp.where(ii > jj, ii ^ jj, 0)
    eye = ii == jj
    tri = (ii >= jj).astype(BF16)

    gl1, gl2 = _split2(gl_ref[:, :GLA_RANK])
    wg1, wg2 = _split2(wg_ref[...])
    x = (jnp.dot(gl1, wg1, preferred_element_type=F32) + jnp.dot(gl1, wg2, preferred_element_type=F32)
         + jnp.dot(gl2, wg1, preferred_element_type=F32)) + bg_ref[...]
    log_a = (jnp.minimum(x, 0.0) - jnp.log(1.0 + jnp.exp(-jnp.abs(x)))) * (LOG2E / GLA_TAU)
    log_a = jnp.where(ridx >= jnp.where(n > 0, 0, PAD), log_a, 0.0)
    g1, g2 = _split2(log_a)
    b_all = jnp.dot(tri, g1, preferred_element_type=F32) + jnp.dot(tri, g2, preferred_element_type=F32)
    h2 = h_ref[...]
    for h in range(GLA_HEADS):
        kc = slice(h * GLA_DK, (h + 1) * GLA_DK)
        vc = slice(h * GLA_DV, (h + 1) * GLA_DV)
        h2 = h2 + jnp.dot(mix_ref[:, vc], wo_ref[vc, :], preferred_element_type=F32)
        q = q_ref[:, kc].astype(F32)
        k = k_ref[:, kc].astype(F32)
        v = v_ref[:, vc]
        b = b_all[:, kc]
        b_last = b[CHUNK - 1:CHUNK, :]
        scores = jnp.where(eye, jnp.sum(q * k, axis=1, keepdims=True), 0.0)
        for lv in range(GLA_LEVELS):
            upper = ((ridx >> lv) & 1) == 1
            fac = jnp.exp2(_neg_abs(b - _block_ref_rows(b, lv)))
            zz = (jnp.where(upper, q, k) * fac).astype(BF16)
            gram = lax.dot_general(zz, zz, (((1,), (1,)), ((), ())), preferred_element_type=F32)
            scores = scores + jnp.where((pair_code >> lv) == 1, gram, 0.0)
        o = jnp.dot(scores.astype(BF16), v, preferred_element_type=F32)
        st = st_ref[h]
        qe = (q * jnp.exp2(b)).astype(BF16)
        o = o + lax.dot_general(qe, st.astype(BF16), (((1,), (1,)), ((), ())), preferred_element_type=F32)
        ke = (k * jnp.exp2(b_last - b)).astype(BF16)
        kv_t = lax.dot_general(v, ke, (((0,), (0,)), ((), ())), preferred_element_type=F32)
        st_ref[h] = st * jnp.exp2(b_last) + kv_t
        mix_ref[:, vc] = (_rms_scale(o, nw_ref[:, vc]) * z_ref[:, vc].astype(F32)).astype(BF16)

    y_ref[...] = _rms_scale(h2, fw_ref[...])


def _gla(proj, glow, w_gate, b_gate, norm_w, h1, w_out, final_w, bsz, nc):
    m, d = h1.shape
    nchunks = m // CHUNK
    const = lambda *shape: pl.BlockSpec(shape, lambda c: (0,) * len(shape))
    cur = lambda width, col: pl.BlockSpec((CHUNK, width), lambda c: (jnp.minimum(c, nchunks - 1), col))
    prev = lambda c: jnp.maximum(c - 1, 0)
    return pl.pallas_call(
        functools.partial(_gla_kernel, nc=nc, nchunks=nchunks),
        grid=(nchunks + 1,),
        in_specs=[cur(GLA_QK, 0), cur(GLA_QK, 1), cur(GLA_W, 1), cur(GLA_W, 2), cur(128, 0),
                  const(GLA_RANK, GLA_QK), const(1, GLA_QK), const(1, GLA_W),
                  pl.BlockSpec((CHUNK, d), lambda c: (prev(c), 0)),
                  const(GLA_W, d), const(1, d)],
        out_specs=pl.BlockSpec((None, CHUNK, d),
                               lambda c: (prev(c) // nc, jnp.maximum(prev(c) % nc - 1, 0), 0)),
        out_shape=jax.ShapeDtypeStruct((bsz, (nc - 1) * CHUNK, d), F32),
        scratch_shapes=[pltpu.VMEM((GLA_HEADS, GLA_DV, GLA_DK), F32), pltpu.VMEM((CHUNK, GLA_W), BF16)],
        compiler_params=_cparams(("arbitrary",)),
        name="gla_out",
    )(proj, proj, proj, proj, glow, w_gate.astype(F32), b_gate.reshape(1, GLA_QK).astype(F32),
      norm_w.reshape(1, GLA_W).astype(F32), h1, w_out, final_w.reshape(1, d).astype(F32))


def _rope_tables(bsz, rows_per_batch):
    pos = jnp.maximum(jnp.arange(rows_per_batch, dtype=F32) - PAD, 0.0)
    inv_freq = jnp.power(ROPE_BASE, -jnp.arange(0, RET_DK, 2, dtype=F32) / RET_DK)
    ang = pos[:, None] * inv_freq[None, :]
    cos, sin = jnp.cos(ang), jnp.sin(ang)
    cos2 = jnp.concatenate([cos, cos], axis=1)
    sin2 = jnp.concatenate([-sin, sin], axis=1)
    return jnp.tile(cos2, (bsz, 1)), jnp.tile(sin2, (bsz, 1))


def kernel(x, meta, norm_ab_w, w_in_ab, ret_norm_w, s5_lam_re, s5_lam_im, s5_log_dt, s5_b_re, s5_b_im,
           s5_c_re, s5_c_im, s5_d, s5_w_glu, w_out_ab, norm_c_w, w_in_c, gla_w_gate, gla_b_gate,
           gla_norm_w, w_out_c, final_norm_w):
    bsz, seq, d = x.shape
    assert seq % CHUNK == 0 and w_in_ab.shape[0] == 1 and w_in_c.shape[0] == 1
    rpb = seq + CHUNK
    nc = rpb // CHUNK
    assert nc % ROW_CHUNKS == 0
    nt = nc // ROW_CHUNKS
    m = bsz * rpb
    tm = _pick_tile(m, (768, 384, 256, 128))
    lead = jnp.concatenate([jnp.zeros((PAD, d), x.dtype), meta.astype(x.dtype)], axis=0)

    xn0 = _embed_norm(x, lead, norm_ab_w[0], nt)
    plain, silu = ("plain", 1.0), ("silu", 1.0)
    modes_ab = [("rope", 1.0), ("rope", RET_DK ** -0.5), plain, plain, silu, silu, plain, silu]
    proj0 = _in_proj(xn0, w_in_ab, modes_ab, tm, rope=_rope_tables(bsz, rpb))
    o_a = _retention(proj0, ret_norm_w[0], nc)
    t, tseg = _s5_time_tile(rpb)
    p_re, p_im, bb_re, bb_im = _s5_prep(s5_lam_re[0], s5_lam_im[0], s5_log_dt[0], s5_b_re[0], s5_b_im[0], tseg)
    wb, wcre, wcim = _s5_block_weights(bb_re, bb_im, s5_c_re[0], s5_c_im[0])
    o_b = _s5(proj0, bsz, wb, wcre, wcim, p_re, p_im, s5_d[0], s5_w_glu[0].astype(BF16), t, tseg)

    w_in_c_t = jnp.swapaxes(w_in_c, 1, 2)
    n_main = 2 * GLA_QK + 2 * GLA_W
    wg_t = jnp.pad(w_in_c_t[0, n_main:, :], ((0, 128 - GLA_RANK), (0, 0))).astype(BF16)
    h1, xn1, glow = _out_ab(x, lead, o_a, o_b, w_out_ab[0].astype(BF16), norm_c_w[0], wg_t, nt)

    modes_c = [("plain", GLA_DK ** -0.5), plain, plain, plain, silu, silu]
    proj1 = _in_proj(xn1, w_in_c_t, modes_c, tm, w_is_transposed=True)
    return _gla(proj1, glow, gla_w_gate[0], gla_b_gate[0], gla_norm_w[0], h1, w_out_c[0].astype(BF16),
                final_norm_w, bsz, nc)
```

```python
import functools
import math

import jax
import jax.numpy as jnp
from jax import lax
from jax.experimental import pallas as pl
from jax.experimental.pallas import tpu as pltpu

F32 = jnp.float32
BF16 = jnp.bfloat16

N_META = 16
CHUNK = 128
PAD = CHUNK - N_META
EPS = 1e-6
ROW_CHUNKS = 3

RET_HEADS = 8
RET_DK = 128
RET_DV = 256
RET_QK = RET_HEADS * RET_DK
RET_W = RET_HEADS * RET_DV
ROPE_BASE = 10000.0

S5_W = 1024
S5_GH = 16
S5_G = 64
S5_P = 64
S5_F = S5_G * S5_P
S5_SLABS = S5_F // 128
S5_JB = 4
S5_SEGS = 8

GLA_HEADS = 4
GLA_DK = 256
GLA_DV = 512
GLA_QK = GLA_HEADS * GLA_DK
GLA_W = GLA_HEADS * GLA_DV
GLA_RANK = 16
GLA_TAU = 16.0
GLA_LEVELS = 7

PROJ_TN = 1024
VMEM_LIMIT = 56 * 1024 * 1024


def _cparams(sem):
    return pltpu.CompilerParams(dimension_semantics=sem, vmem_limit_bytes=VMEM_LIMIT)


def _silu(x):
    return x * (1.0 / (1.0 + jnp.exp(-x)))


def _pick_tile(n, candidates):
    for c in candidates:
        if n % c == 0:
            return c
    raise ValueError(f"no tile for {n}")


def _rms_scale(x, w):
    return x * lax.rsqrt(jnp.mean(x * x, axis=-1, keepdims=True) + EPS) * w


def _chunk_specs(d, nt, n_before):
    del nt
    return [pl.BlockSpec((None, CHUNK, d),
                         lambda b, t, k=k: (b, jnp.maximum(ROW_CHUNKS * t + k - n_before, 0), 0))
            for k in range(ROW_CHUNKS)]


def _embed_norm_kernel(lead_ref, xa_ref, xb_ref, xc_ref, w_ref, o_ref):
    t = pl.program_id(1)
    first = jnp.where(t == 0, lead_ref[...], xa_ref[...])
    for r, rows in enumerate((first, xb_ref[...], xc_ref[...])):
        o_ref[r * CHUNK:(r + 1) * CHUNK, :] = _rms_scale(rows, w_ref[...]).astype(o_ref.dtype)


def _embed_norm(x, lead, w, nt):
    bsz, _, d = x.shape
    tile = ROW_CHUNKS * CHUNK
    return pl.pallas_call(
        _embed_norm_kernel,
        grid=(bsz, nt),
        in_specs=[pl.BlockSpec((CHUNK, d), lambda b, t: (0, 0))] + _chunk_specs(d, nt, 1)
        + [pl.BlockSpec((1, d), lambda b, t: (0, 0))],
        out_specs=pl.BlockSpec((tile, d), lambda b, t: (b * nt + t, 0)),
        out_shape=jax.ShapeDtypeStruct((bsz * nt * tile, d), BF16),
        compiler_params=_cparams(("parallel", "parallel")),
        name="embed_norm",
    )(lead, x, x, x, w.reshape(1, d))


def _rope_rows(a, cos, sin, scale):
    outs = []
    for h in range(a.shape[1] // RET_DK):
        blk = a[:, h * RET_DK:(h + 1) * RET_DK]
        r = blk * cos + pltpu.roll(blk, RET_DK // 2, 1) * sin
        outs.append(r * scale if scale != 1.0 else r)
    return jnp.concatenate(outs, axis=1)


def _in_proj_kernel(*refs, modes, has_rope, w_is_transposed):
    if has_rope:
        x_ref, w_ref, cos_ref, sin_ref, o_ref, wbf_ref = refs
    else:
        x_ref, w_ref, o_ref, wbf_ref = refs
    j = pl.program_id(0)

    @pl.when(pl.program_id(1) == 0)
    def _():
        wbf_ref[...] = w_ref[...].astype(BF16)

    contract = (((1,), (1 if w_is_transposed else 0,)), ((), ()))
    for mode in sorted(set(modes)):
        cond = functools.reduce(jnp.logical_or, [j == jj for jj, mm in enumerate(modes) if mm == mode])

        @pl.when(cond)
        def _(mode=mode):
            kind, scale = mode
            acc = lax.dot_general(x_ref[...], wbf_ref[...], contract, preferred_element_type=F32)
            if kind == "rope":
                acc = _rope_rows(acc, cos_ref[...], sin_ref[...], scale)
            elif kind == "silu":
                acc = _silu(acc)
            elif scale != 1.0:
                acc = acc * scale
            o_ref[...] = acc.astype(o_ref.dtype)


def _in_proj(x, w, modes, tm, rope=None, w_is_transposed=False):
    m, k = x.shape
    nt = len(modes)
    if w_is_transposed:
        w_spec = pl.BlockSpec((None, PROJ_TN, k), lambda j, i: (0, j, 0))
        w_scratch = pltpu.VMEM((PROJ_TN, k), BF16)
    else:
        w_spec = pl.BlockSpec((None, k, PROJ_TN), lambda j, i: (0, 0, j))
        w_scratch = pltpu.VMEM((k, PROJ_TN), BF16)
    in_specs = [pl.BlockSpec((tm, k), lambda j, i: (i, 0)), w_spec]
    args = [x, w]
    if rope is not None:
        in_specs += [pl.BlockSpec((tm, RET_DK), lambda j, i: (i, 0))] * 2
        args += list(rope)
    return pl.pallas_call(
        functools.partial(_in_proj_kernel, modes=tuple(modes), has_rope=rope is not None,
                          w_is_transposed=w_is_transposed),
        grid=(nt, m // tm),
        in_specs=in_specs,
        out_specs=pl.BlockSpec((tm, PROJ_TN), lambda j, i: (i, j)),
        out_shape=jax.ShapeDtypeStruct((m, nt * PROJ_TN), BF16),
        scratch_shapes=[w_scratch],
        compiler_params=_cparams(("arbitrary", "arbitrary")),
        name="in_proj",
    )(*args)


def _ret_log_decay(h):
    return math.log1p(-(2.0 ** (-5.0 - h)))


def _retention_kernel(q_ref, k_ref, v_ref, z_ref, nw_ref, lead_ref, x_ref, wo_ref, ha_ref,
                      s_ref, dec_ref, mix_ref, *, nc, nchunks):
    c = pl.program_id(0)
    n = jnp.minimum(c, nchunks - 1) % nc

    @pl.when(c == 0)
    def _():
        mix_ref[...] = jnp.zeros_like(mix_ref)
        ii = lax.broadcasted_iota(jnp.int32, (CHUNK, CHUNK), 0)
        jj = lax.broadcasted_iota(jnp.int32, (CHUNK, CHUNK), 1)
        diff = (ii - jj).astype(F32)
        for h in range(RET_HEADS):
            dec_ref[h] = jnp.where(ii >= jj, jnp.exp(_ret_log_decay(h) * jnp.maximum(diff, 0.0)), 0.0)

    @pl.when(n == 0)
    def _():
        s_ref[...] = jnp.zeros_like(s_ref)

    h2 = jnp.where(jnp.maximum(c - 1, 0) % nc == 0, lead_ref[...], x_ref[...])
    idx = lax.broadcasted_iota(jnp.int32, (CHUNK, 1), 0).astype(F32)
    for h in range(RET_HEADS):
        lg = _ret_log_decay(h)
        vc = slice(h * RET_DV, (h + 1) * RET_DV)
        h2 = h2 + jnp.dot(mix_ref[:, vc], wo_ref[vc, :], preferred_element_type=F32)
        q = q_ref[:, h * RET_DK:(h + 1) * RET_DK]
        k = k_ref[:, h * RET_DK:(h + 1) * RET_DK]
        v = v_ref[:, vc]
        scores = lax.dot_general(q, k, (((1,), (1,)), ((), ())), preferred_element_type=F32) * dec_ref[h]
        o = jnp.dot(scores.astype(BF16), v, preferred_element_type=F32)
        s_prev = s_ref[h]
        o = o + jnp.exp(lg * (idx + 1.0)) * jnp.dot(q, s_prev.astype(BF16), preferred_element_type=F32)
        k_w = (k.astype(F32) * jnp.exp(lg * (CHUNK - 1.0 - idx))).astype(BF16)
        kv = lax.dot_general(k_w, v, (((0,), (0,)), ((), ())), preferred_element_type=F32)
        s_ref[h] = s_prev * math.exp(lg * CHUNK) + kv
        mix_ref[:, vc] = (_rms_scale(o, nw_ref[:, vc]) * z_ref[:, vc].astype(F32)).astype(BF16)
    ha_ref[...] = h2


def _retention(proj, norm_w, lead, x, w_out, nc):
    m = proj.shape[0]
    d = x.shape[2]
    nchunks = m // CHUNK
    cur = lambda width, col: pl.BlockSpec((CHUNK, width), lambda c: (jnp.minimum(c, nchunks - 1), col))
    prev = lambda c: jnp.maximum(c - 1, 0)
    const = lambda *shape: pl.BlockSpec(shape, lambda c: (0,) * len(shape))
    return pl.pallas_call(
        functools.partial(_retention_kernel, nc=nc, nchunks=nchunks),
        grid=(nchunks + 1,),
        in_specs=[cur(RET_QK, 0), cur(RET_QK, 1), cur(RET_W, 1), cur(RET_W, 2), const(1, RET_W),
                  const(CHUNK, d),
                  pl.BlockSpec((None, CHUNK, d), lambda c: (prev(c) // nc, jnp.maximum(prev(c) % nc - 1, 0), 0)),
                  const(RET_W, d)],
        out_specs=pl.BlockSpec((CHUNK, d), lambda c: (prev(c), 0)),
        out_shape=jax.ShapeDtypeStruct((m, d), F32),
        scratch_shapes=[pltpu.VMEM((RET_HEADS, RET_DK, RET_DV), F32),
                        pltpu.VMEM((RET_HEADS, CHUNK, CHUNK), F32),
                        pltpu.VMEM((CHUNK, RET_W), BF16)],
        compiler_params=_cparams(("arbitrary",)),
        name="retention_out",
    )(proj, proj, proj, proj, norm_w.reshape(1, RET_W), lead, x, w_out)


def _s5_prep_kernel(lre_ref, lim_ref, ldt_ref, bre_ref, bim_ref, pre_ref, pim_ref, bbre_ref, bbim_ref):
    lre = lre_ref[...]
    lim = lim_ref[...]
    dt = jnp.exp(ldt_ref[...])
    tseg = pre_ref.shape[0]
    n = lax.broadcasted_iota(jnp.int32, (tseg, 1), 0).astype(F32) + 1.0
    mag = jnp.exp(n * (lre * dt))
    ang = n * (lim * dt)
    p_re = mag * jnp.cos(ang)
    p_im = mag * jnp.sin(ang)
    pre_ref[...] = p_re
    pim_ref[...] = p_im
    a_re = p_re[0:1]
    a_im = p_im[0:1]
    den = lre * lre + lim * lim
    nr = a_re - 1.0
    f_re = (nr * lre + a_im * lim) / den
    f_im = (a_im * lre - nr * lim) / den
    b_re = bre_ref[...]
    b_im = bim_ref[...]
    bbre_ref[...] = f_re * b_re - f_im * b_im
    bbim_ref[...] = f_re * b_im + f_im * b_re


def _s5_prep(lam_re, lam_im, log_dt, b_re, b_im, tseg):
    flat = lambda a: a.reshape(1, S5_F)
    ldt = jnp.broadcast_to(log_dt[:, None], (S5_G, S5_P))
    bt = lambda a: a.reshape(S5_F, S5_GH).T
    out = jax.ShapeDtypeStruct
    return pl.pallas_call(
        _s5_prep_kernel,
        out_shape=(out((tseg, S5_F), F32), out((tseg, S5_F), F32),
                   out((S5_GH, S5_F), F32), out((S5_GH, S5_F), F32)),
        compiler_params=pltpu.CompilerParams(vmem_limit_bytes=VMEM_LIMIT),
        name="s5_prep",
    )(flat(lam_re), flat(lam_im), flat(ldt), bt(b_re), bt(b_im))


def _gelu_tanh(x):
    return 0.5 * x * (1.0 + jnp.tanh(math.sqrt(2.0 / math.pi) * (x + 0.044715 * (x * x * x))))


def _s5_kernel(u_ref, z_ref, wb_ref, wcre_ref, wcim_ref, pre_ref, pim_ref, d_ref, wglu_ref,
               o_ref, bure_ref, buim_ref, cre_ref, cim_ref, xinre_ref, xinim_ref, *, tseg, slab_group):
    ci = pl.program_id(1)

    @pl.when(ci == 0)
    def _():
        cre_ref[...] = jnp.zeros_like(cre_ref)
        cim_ref[...] = jnp.zeros_like(cim_ref)

    u_bf = u_ref[...]
    spb = S5_SLABS // S5_JB
    for j in range(S5_JB):
        res = jnp.dot(u_bf[:, j * 256:(j + 1) * 256], wb_ref[j], preferred_element_type=F32)
        for s in range(spb):
            bure_ref[j * spb + s] = res[:, s * 128:(s + 1) * 128]
            buim_ref[j * spb + s] = res[:, (spb + s) * 128:(spb + s + 1) * 128]

    for g0 in range(0, S5_SLABS, slab_group):
        slabs = list(range(g0, g0 + slab_group))
        a_re = [jnp.broadcast_to(pre_ref[sl, 0:1, :], (S5_SEGS, 128)) for sl in slabs]
        a_im = [jnp.broadcast_to(pim_ref[sl, 0:1, :], (S5_SEGS, 128)) for sl in slabs]

        def scan_body(tau, carry, slabs=slabs, a_re=a_re, a_im=a_im):
            xr, xi = carry
            nr, ni = [], []
            for n, sl in enumerate(slabs):
                rows = pl.ds(tau, S5_SEGS, stride=tseg)
                br = bure_ref[sl, rows, :]
                bi = buim_ref[sl, rows, :]
                r = a_re[n] * xr[n] - a_im[n] * xi[n] + br
                i = a_re[n] * xi[n] + a_im[n] * xr[n] + bi
                bure_ref[sl, rows, :] = r
                buim_ref[sl, rows, :] = i
                nr.append(r)
                ni.append(i)
            return tuple(nr), tuple(ni)

        zero = tuple(jnp.zeros((S5_SEGS, 128), F32) for _ in slabs)
        end_re, end_im = lax.fori_loop(0, tseg, scan_body, (zero, zero))

        for n, sl in enumerate(slabs):
            at_re = pre_ref[sl, tseg - 1:tseg, :]
            at_im = pim_ref[sl, tseg - 1:tseg, :]
            xr = cre_ref[sl, 0:1, :]
            xi = cim_ref[sl, 0:1, :]
            rows_re, rows_im = [], []
            for s in range(S5_SEGS):
                rows_re.append(xr)
                rows_im.append(xi)
                er = end_re[n][s:s + 1]
                ei = end_im[n][s:s + 1]
                xr, xi = er + at_re * xr - at_im * xi, ei + at_re * xi + at_im * xr
            cre_ref[sl] = jnp.broadcast_to(xr, (S5_SEGS, 128))
            cim_ref[sl] = jnp.broadcast_to(xi, (S5_SEGS, 128))
            xinre_ref[sl] = jnp.concatenate(rows_re, axis=0)
            xinim_ref[sl] = jnp.concatenate(rows_im, axis=0)

    def fix_body(sl, carry):
        xin_re = xinre_ref[sl]
        xin_im = xinim_ref[sl]
        for tau in range(tseg):
            rows = pl.ds(tau, S5_SEGS, stride=tseg)
            pr = pre_ref[sl, tau:tau + 1, :]
            pi = pim_ref[sl, tau:tau + 1, :]
            bure_ref[sl, rows, :] = bure_ref[sl, rows, :] + (pr * xin_re - pi * xin_im)
            buim_ref[sl, rows, :] = buim_ref[sl, rows, :] + (pr * xin_im + pi * xin_re)
        return carry

    lax.fori_loop(0, S5_SLABS, fix_body, 0)

    ys = []
    for j in range(S5_JB):
        x_re = jnp.concatenate([bure_ref[j * spb + s] for s in range(spb)], axis=1).astype(BF16)
        x_im = jnp.concatenate([buim_ref[j * spb + s] for s in range(spb)], axis=1).astype(BF16)
        ys.append(jnp.dot(x_re, wcre_ref[j], preferred_element_type=F32)
                  - jnp.dot(x_im, wcim_ref[j], preferred_element_type=F32))
    y = jnp.concatenate(ys, axis=1) + d_ref[...] * u_bf.astype(F32)
    y = _gelu_tanh(y)
    gate = jnp.dot(y.astype(BF16), wglu_ref[...], preferred_element_type=F32)
    y = y * (1.0 / (1.0 + jnp.exp(-gate)))
    o_ref[...] = (y * z_ref[...].astype(F32)).astype(o_ref.dtype)


def _s5_time_tile(rows_per_batch):
    for tseg in (66, 44, 132, 12, 4, 6, 2):
        t = S5_SEGS * tseg
        if rows_per_batch % t == 0 and t % 16 == 0:
            return t, tseg
    raise ValueError(f"no S5 time tile for {rows_per_batch}")


def _s5(proj, bsz, wb, wcre, wcim, p_re, p_im, d, wglu, t, tseg):
    m = proj.shape[0]
    nt = m // bsz // t
    ucol = (2 * RET_QK + 2 * RET_W) // S5_W
    const = lambda *shape: pl.BlockSpec(shape, lambda b, i: (0,) * len(shape))
    slabs = lambda p: p.reshape(tseg, S5_SLABS, 128).transpose(1, 0, 2)
    return pl.pallas_call(
        functools.partial(_s5_kernel, tseg=tseg, slab_group=8),
        grid=(bsz, nt),
        in_specs=[pl.BlockSpec((t, S5_W), lambda b, i: (b * nt + i, ucol)),
                  pl.BlockSpec((t, S5_W), lambda b, i: (b * nt + i, ucol + 1)),
                  const(S5_JB, 256, 2048), const(S5_JB, 1024, 256), const(S5_JB, 1024, 256),
                  const(S5_SLABS, tseg, 128), const(S5_SLABS, tseg, 128), const(1, S5_W),
                  const(S5_W, S5_W)],
        out_specs=pl.BlockSpec((t, S5_W), lambda b, i: (b * nt + i, 0)),
        out_shape=jax.ShapeDtypeStruct((m, S5_W), BF16),
        scratch_shapes=[pltpu.VMEM((S5_SLABS, t, 128), F32), pltpu.VMEM((S5_SLABS, t, 128), F32)]
        + [pltpu.VMEM((S5_SLABS, S5_SEGS, 128), F32)] * 4,
        compiler_params=_cparams(("arbitrary", "arbitrary")),
        name="s5",
    )(proj, proj, wb, wcre, wcim, slabs(p_re), slabs(p_im), d.reshape(1, S5_W), wglu)


def _s5_block_weights(bb_re, bb_im, c_re, c_im):
    eye = jnp.eye(16, dtype=F32)

    def wb_part(bb):
        a = bb.reshape(S5_GH, S5_JB, 16, S5_P)
        w = jnp.einsum('hjgp,gk->jghkp', a, eye)
        return w.reshape(S5_JB, 256, 1024)

    wb = jnp.concatenate([wb_part(bb_re), wb_part(bb_im)], axis=2).astype(BF16)

    def wc_part(c):
        a = c.astype(F32).reshape(S5_JB, 16, S5_GH, S5_P)
        w = jnp.einsum('jghp,gk->jgpkh', a, eye)
        return w.reshape(S5_JB, 1024, 256).astype(BF16)

    return wb, wc_part(c_re), wc_part(c_im)


def _out_b_kernel(ha_ref, ob_ref, w_ref, nw_ref, wg_ref, h1_ref, xn_ref, gl_ref):
    for r in range(ha_ref.shape[0] // CHUNK):
        sl = slice(r * CHUNK, (r + 1) * CHUNK)
        h1 = ha_ref[sl, :] + jnp.dot(ob_ref[sl, :], w_ref[...], preferred_element_type=F32)
        h1_ref[sl, :] = h1
        xn = _rms_scale(h1, nw_ref[...]).astype(BF16)
        xn_ref[sl, :] = xn
        gl_ref[sl, :] = jnp.dot(xn, wg_ref[...], preferred_element_type=F32)


def _out_b(ha, ob, w, norm_w, wg, tm):
    m, d = ha.shape
    row = lambda width: pl.BlockSpec((tm, width), lambda i: (i, 0))
    const = lambda a: pl.BlockSpec(a.shape, lambda i: (0, 0))
    out = jax.ShapeDtypeStruct
    return pl.pallas_call(
        _out_b_kernel,
        grid=(m // tm,),
        in_specs=[row(d), row(S5_W), const(w), pl.BlockSpec((1, d), lambda i: (0, 0)), const(wg)],
        out_specs=[row(d), row(d), row(128)],
        out_shape=(out((m, d), F32), out((m, d), BF16), out((m, 128), F32)),
        compiler_params=_cparams(("parallel",)),
        name="out_proj_b",
    )(ha, ob, w, norm_w.reshape(1, d), wg)


LOG2E = 1.4426950408889634


def _split2(x):
    x1 = x.astype(BF16)
    return x1, (x - x1.astype(F32)).astype(BF16)


def _neg_abs(x):
    return -jnp.abs(x)


def _block_ref_rows(b, level):
    rows, width = b.shape
    half = 1 << level
    blk = 2 * half
    if blk >= 8:
        b3 = b.reshape(rows // blk, blk, width)
        ref = jnp.broadcast_to(b3[:, half - 1:half, :], b3.shape)
        return ref.reshape(rows, width)
    b3 = b.reshape(rows // 8, 8, width)
    sub = lax.broadcasted_iota(jnp.int32, b3.shape, 1)
    pick = lambda r: jnp.broadcast_to(b3[:, r:r + 1, :], b3.shape)
    if blk == 4:
        ref = jnp.where(sub < 4, pick(1), pick(5))
    else:
        ref = jnp.where(sub < 2, pick(0), jnp.where(sub < 4, pick(2), jnp.where(sub < 6, pick(4), pick(6))))
    return ref.reshape(rows, width)


def _gla_kernel(q_ref, k_ref, v_ref, z_ref, gl_ref, wg_ref, bg_ref, nw_ref, h_ref, wo_ref, fw_ref,
                y_ref, st_ref, mix_ref, *, nc, nchunks):
    c = pl.program_id(0)
    n = jnp.minimum(c, nchunks - 1) % nc

    @pl.when(c == 0)
    def _():
        mix_ref[...] = jnp.zeros_like(mix_ref)

    @pl.when(n == 0)
    def _():
        st_ref[...] = jnp.zeros_like(st_ref)


    ridx = lax.broadcasted_iota(jnp.int32, (CHUNK, 1), 0)
    ii = lax.broadcasted_iota(jnp.int32, (CHUNK, CHUNK), 0)
    jj = lax.broadcasted_iota(jnp.int32, (CHUNK, CHUNK), 1)
    pair_code = jnp.where(ii > jj, ii ^ jj, 0)
    eye = ii == jj
    tri = (ii >= jj).astype(BF16)

    gl1, gl2 = _split2(gl_ref[:, :GLA_RANK])
    wg1, wg2 = _split2(wg_ref[...])
    x = (jnp.dot(gl1, wg1, preferred_element_type=F32) + jnp.dot(gl1, wg2, preferred_element_type=F32)
         + jnp.dot(gl2, wg1, preferred_element_type=F32)) + bg_ref[...]
    log_a = (jnp.minimum(x, 0.0) - jnp.log(1.0 + jnp.exp(-jnp.abs(x)))) * (LOG2E / GLA_TAU)
    log_a = jnp.where(ridx >= jnp.where(n > 0, 0, PAD), log_a, 0.0)
    g1, g2 = _split2(log_a)
    b_all = jnp.dot(tri, g1, preferred_element_type=F32) + jnp.dot(tri, g2, preferred_element_type=F32)
    h2 = h_ref[...]
    for h in range(GLA_HEADS):
        kc = slice(h * GLA_DK, (h + 1) * GLA_DK)
        vc = slice(h * GLA_DV, (h + 1) * GLA_DV)
        h2 = h2 + jnp.dot(mix_ref[:, vc], wo_ref[vc, :], preferred_element_type=F32)
        q = q_ref[:, kc].astype(F32)
        k = k_ref[:, kc].astype(F32)
        v = v_ref[:, vc]
        b = b_all[:, kc]
        b_last = b[CHUNK - 1:CHUNK, :]
        scores = jnp.where(eye, jnp.sum(q * k, axis=1, keepdims=True), 0.0)
        for lv in range(GLA_LEVELS):
            upper = ((ridx >> lv) & 1) == 1
            fac = jnp.exp2(_neg_abs(b - _block_ref_rows(b, lv)))
            zz = (jnp.where(upper, q, k) * fac).astype(BF16)
            gram = lax.dot_general(zz, zz, (((1,), (1,)), ((), ())), preferred_element_type=F32)
            scores = scores + jnp.where((pair_code >> lv) == 1, gram, 0.0)
        o = jnp.dot(scores.astype(BF16), v, preferred_element_type=F32)
        st = st_ref[h]
        qe = (q * jnp.exp2(b)).astype(BF16)
        o = o + lax.dot_general(qe, st.astype(BF16), (((1,), (1,)), ((), ())), preferred_element_type=F32)
        ke = (k * jnp.exp2(b_last - b)).astype(BF16)
        kv_t = lax.dot_general(v, ke, (((0,), (0,)), ((), ())), preferred_element_type=F32)
        st_ref[h] = st * jnp.exp2(b_last) + kv_t
        mix_ref[:, vc] = (_rms_scale(o, nw_ref[:, vc]) * z_ref[:, vc].astype(F32)).astype(BF16)

    y_ref[...] = _rms_scale(h2, fw_ref[...])


def _gla(proj, glow, w_gate, b_gate, norm_w, h1, w_out, final_w, bsz, nc):
    m, d = h1.shape
    nchunks = m // CHUNK
    const = lambda *shape: pl.BlockSpec(shape, lambda c: (0,) * len(shape))
    cur = lambda width, col: pl.BlockSpec((CHUNK, width), lambda c: (jnp.minimum(c, nchunks - 1), col))
    prev = lambda c: jnp.maximum(c - 1, 0)
    return pl.pallas_call(
        functools.partial(_gla_kernel, nc=nc, nchunks=nchunks),
        grid=(nchunks + 1,),
        in_specs=[cur(GLA_QK, 0), cur(GLA_QK, 1), cur(GLA_W, 1), cur(GLA_W, 2), cur(128, 0),
                  const(GLA_RANK, GLA_QK), const(1, GLA_QK), const(1, GLA_W),
                  pl.BlockSpec((CHUNK, d), lambda c: (prev(c), 0)),
                  const(GLA_W, d), const(1, d)],
        out_specs=pl.BlockSpec((None, CHUNK, d),
                               lambda c: (prev(c) // nc, jnp.maximum(prev(c) % nc - 1, 0), 0)),
        out_shape=jax.ShapeDtypeStruct((bsz, (nc - 1) * CHUNK, d), F32),
        scratch_shapes=[pltpu.VMEM((GLA_HEADS, GLA_DV, GLA_DK), F32), pltpu.VMEM((CHUNK, GLA_W), BF16)],
        compiler_params=_cparams(("arbitrary",)),
        name="gla_out",
    )(proj, proj, proj, proj, glow, w_gate.astype(F32), b_gate.reshape(1, GLA_QK).astype(F32),
      norm_w.reshape(1, GLA_W).astype(F32), h1, w_out, final_w.reshape(1, d).astype(F32))


def _rope_tables(bsz, rows_per_batch):
    pos = jnp.maximum(jnp.arange(rows_per_batch, dtype=F32) - PAD, 0.0)
    inv_freq = jnp.power(ROPE_BASE, -jnp.arange(0, RET_DK, 2, dtype=F32) / RET_DK)
    ang = pos[:, None] * inv_freq[None, :]
    cos, sin = jnp.cos(ang), jnp.sin(ang)
    cos2 = jnp.concatenate([cos, cos], axis=1)
    sin2 = jnp.concatenate([-sin, sin], axis=1)
    return jnp.tile(cos2, (bsz, 1)), jnp.tile(sin2, (bsz, 1))


def kernel(x, meta, norm_ab_w, w_in_ab, ret_norm_w, s5_lam_re, s5_lam_im, s5_log_dt, s5_b_re, s5_b_im,
           s5_c_re, s5_c_im, s5_d, s5_w_glu, w_out_ab, norm_c_w, w_in_c, gla_w_gate, gla_b_gate,
           gla_norm_w, w_out_c, final_norm_w):
    bsz, seq, d = x.shape
    assert seq % CHUNK == 0 and w_in_ab.shape[0] == 1 and w_in_c.shape[0] == 1
    rpb = seq + CHUNK
    nc = rpb // CHUNK
    assert nc % ROW_CHUNKS == 0
    nt = nc // ROW_CHUNKS
    m = bsz * rpb
    tm = _pick_tile(m, (768, 384, 256, 128))
    lead = jnp.concatenate([jnp.zeros((PAD, d), x.dtype), meta.astype(x.dtype)], axis=0)

    xn0 = _embed_norm(x, lead, norm_ab_w[0], nt)
    plain, silu = ("plain", 1.0), ("silu", 1.0)
    modes_ab = [("rope", 1.0), ("rope", RET_DK ** -0.5), plain, plain, silu, silu, plain, silu]
    proj0 = _in_proj(xn0, w_in_ab, modes_ab, tm, rope=_rope_tables(bsz, rpb))
    w_out = w_out_ab[0].astype(BF16)
    h_a = _retention(proj0, ret_norm_w[0], lead, x, w_out[:RET_W], nc)
    t, tseg = _s5_time_tile(rpb)
    p_re, p_im, bb_re, bb_im = _s5_prep(s5_lam_re[0], s5_lam_im[0], s5_log_dt[0], s5_b_re[0], s5_b_im[0], tseg)
    wb, wcre, wcim = _s5_block_weights(bb_re, bb_im, s5_c_re[0], s5_c_im[0])
    o_b = _s5(proj0, bsz, wb, wcre, wcim, p_re, p_im, s5_d[0], s5_w_glu[0].astype(BF16), t, tseg)

    w_in_c_t = jnp.swapaxes(w_in_c, 1, 2)
    n_main = 2 * GLA_QK + 2 * GLA_W
    wg = jnp.pad(w_in_c_t[0, n_main:, :], ((0, 128 - GLA_RANK), (0, 0))).astype(BF16).T
    h1, xn1, glow = _out_b(h_a, o_b, w_out[RET_W:], norm_c_w[0], wg, ROW_CHUNKS * CHUNK)

    modes_c = [("plain", GLA_DK ** -0.5), plain, plain, plain, silu, silu]
    proj1 = _in_proj(xn1, w_in_c_t, modes_c, tm, w_is_transposed=True)
    return _gla(proj1, glow, gla_w_gate[0], gla_b_gate[0], gla_norm_w[0], h1, w_out_c[0].astype(BF16),
                final_norm_w, bsz, nc)
```

```python
import functools
import math

import jax
import jax.numpy as jnp
from jax import lax
from jax.experimental import pallas as pl
from jax.experimental.pallas import tpu as pltpu

F32 = jnp.float32
BF16 = jnp.bfloat16

N_META = 16
CHUNK = 128
PAD = CHUNK - N_META
EPS = 1e-6
ROW_CHUNKS = 3

RET_HEADS = 8
RET_DK = 128
RET_DV = 256
RET_QK = RET_HEADS * RET_DK
RET_W = RET_HEADS * RET_DV
ROPE_BASE = 10000.0

S5_W = 1024
S5_GH = 16
S5_G = 64
S5_P = 64
S5_F = S5_G * S5_P
S5_SLABS = S5_F // 128
S5_JB = 4
S5_SEGS = 8

GLA_HEADS = 4
GLA_DK = 256
GLA_DV = 512
GLA_QK = GLA_HEADS * GLA_DK
GLA_W = GLA_HEADS * GLA_DV
GLA_RANK = 16
GLA_TAU = 16.0
GLA_LEVELS = 7

PROJ_TN = 1024
VMEM_LIMIT = 56 * 1024 * 1024


def _cparams(sem):
    return pltpu.CompilerParams(dimension_semantics=sem, vmem_limit_bytes=VMEM_LIMIT)


def _silu(x):
    return x * (1.0 / (1.0 + jnp.exp(-x)))


def _pick_tile(n, candidates):
    for c in candidates:
        if n % c == 0:
            return c
    raise ValueError(f"no tile for {n}")


def _rms_scale(x, w):
    return x * lax.rsqrt(jnp.mean(x * x, axis=-1, keepdims=True) + EPS) * w


def _chunk_specs(d, nt, n_before):
    del nt
    return [pl.BlockSpec((None, CHUNK, d),
                         lambda b, t, k=k: (b, jnp.maximum(ROW_CHUNKS * t + k - n_before, 0), 0))
            for k in range(ROW_CHUNKS)]


def _embed_norm_kernel(lead_ref, xa_ref, xb_ref, xc_ref, w_ref, o_ref):
    t = pl.program_id(1)
    first = jnp.where(t == 0, lead_ref[...], xa_ref[...])
    for r, rows in enumerate((first, xb_ref[...], xc_ref[...])):
        o_ref[r * CHUNK:(r + 1) * CHUNK, :] = _rms_scale(rows, w_ref[...]).astype(o_ref.dtype)


def _embed_norm(x, lead, w, nt):
    bsz, _, d = x.shape
    tile = ROW_CHUNKS * CHUNK
    return pl.pallas_call(
        _embed_norm_kernel,
        grid=(bsz, nt),
        in_specs=[pl.BlockSpec((CHUNK, d), lambda b, t: (0, 0))] + _chunk_specs(d, nt, 1)
        + [pl.BlockSpec((1, d), lambda b, t: (0, 0))],
        out_specs=pl.BlockSpec((tile, d), lambda b, t: (b * nt + t, 0)),
        out_shape=jax.ShapeDtypeStruct((bsz * nt * tile, d), BF16),
        compiler_params=_cparams(("parallel", "parallel")),
        name="embed_norm",
    )(lead, x, x, x, w.reshape(1, d))


def _rope_rows(a, cos, sin, scale):
    outs = []
    for h in range(a.shape[1] // RET_DK):
        blk = a[:, h * RET_DK:(h + 1) * RET_DK]
        r = blk * cos + pltpu.roll(blk, RET_DK // 2, 1) * sin
        outs.append(r * scale if scale != 1.0 else r)
    return jnp.concatenate(outs, axis=1)


def _in_proj_kernel(*refs, modes, has_rope, w_is_transposed):
    if has_rope:
        x_ref, w_ref, cos_ref, sin_ref, o_ref, wbf_ref = refs
    else:
        x_ref, w_ref, o_ref, wbf_ref = refs
    j = pl.program_id(0)

    @pl.when(pl.program_id(1) == 0)
    def _():
        wbf_ref[...] = w_ref[...].astype(BF16)

    contract = (((1,), (1 if w_is_transposed else 0,)), ((), ()))
    for mode in sorted(set(modes)):
        cond = functools.reduce(jnp.logical_or, [j == jj for jj, mm in enumerate(modes) if mm == mode])

        @pl.when(cond)
        def _(mode=mode):
            kind, scale = mode
            acc = lax.dot_general(x_ref[...], wbf_ref[...], contract, preferred_element_type=F32)
            if kind == "rope":
                acc = _rope_rows(acc, cos_ref[...], sin_ref[...], scale)
            elif kind == "silu":
                acc = _silu(acc)
            elif scale != 1.0:
                acc = acc * scale
            o_ref[...] = acc.astype(o_ref.dtype)


def _in_proj(x, w, modes, tm, rope=None, w_is_transposed=False):
    m, k = x.shape
    nt = len(modes)
    if w_is_transposed:
        w_spec = pl.BlockSpec((None, PROJ_TN, k), lambda j, i: (0, j, 0))
        w_scratch = pltpu.VMEM((PROJ_TN, k), BF16)
    else:
        w_spec = pl.BlockSpec((None, k, PROJ_TN), lambda j, i: (0, 0, j))
        w_scratch = pltpu.VMEM((k, PROJ_TN), BF16)
    in_specs = [pl.BlockSpec((tm, k), lambda j, i: (i, 0)), w_spec]
    args = [x, w]
    if rope is not None:
        in_specs += [pl.BlockSpec((tm, RET_DK), lambda j, i: (i, 0))] * 2
        args += list(rope)
    return pl.pallas_call(
        functools.partial(_in_proj_kernel, modes=tuple(modes), has_rope=rope is not None,
                          w_is_transposed=w_is_transposed),
        grid=(nt, m // tm),
        in_specs=in_specs,
        out_specs=pl.BlockSpec((tm, PROJ_TN), lambda j, i: (i, j)),
        out_shape=jax.ShapeDtypeStruct((m, nt * PROJ_TN), BF16),
        scratch_shapes=[w_scratch],
        compiler_params=_cparams(("arbitrary", "arbitrary")),
        name="in_proj",
    )(*args)


def _ret_log_decay(h):
    return math.log1p(-(2.0 ** (-5.0 - h)))


def _retention_kernel(q_ref, k_ref, v_ref, z_ref, nw_ref, o_ref, s_ref, dec_ref, qs_ref, ks_ref, *, nc):
    c = pl.program_id(0)

    @pl.when(c == 0)
    def _():
        ii = lax.broadcasted_iota(jnp.int32, (CHUNK, CHUNK), 0)
        jj = lax.broadcasted_iota(jnp.int32, (CHUNK, CHUNK), 1)
        diff = (ii - jj).astype(F32)
        row = ii.astype(F32)
        for h in range(RET_HEADS):
            lg = _ret_log_decay(h)
            dec_ref[h] = jnp.where(ii >= jj, jnp.exp(lg * jnp.maximum(diff, 0.0)), 0.0)
            qs_ref[h] = jnp.exp(lg * (row + 1.0))
            ks_ref[h] = jnp.exp(lg * (CHUNK - 1.0 - row))

    @pl.when(c % nc == 0)
    def _():
        s_ref[...] = jnp.zeros_like(s_ref)

    for h in range(RET_HEADS):
        lg = _ret_log_decay(h)
        vc = slice(h * RET_DV, (h + 1) * RET_DV)
        q = q_ref[:, h * RET_DK:(h + 1) * RET_DK]
        k = k_ref[:, h * RET_DK:(h + 1) * RET_DK]
        v = v_ref[:, vc]
        scores = lax.dot_general(q, k, (((1,), (1,)), ((), ())), preferred_element_type=F32) * dec_ref[h]
        s_prev = s_ref[h]
        lhs = jnp.concatenate([scores.astype(BF16), (q.astype(F32) * qs_ref[h]).astype(BF16)], axis=1)
        rhs = jnp.concatenate([v, s_prev.astype(BF16)], axis=0)
        o = jnp.dot(lhs, rhs, preferred_element_type=F32)
        k_w = (k.astype(F32) * ks_ref[h]).astype(BF16)
        kv = lax.dot_general(k_w, v, (((0,), (0,)), ((), ())), preferred_element_type=F32)
        s_ref[h] = s_prev * math.exp(lg * CHUNK) + kv
        o_ref[:, vc] = (_rms_scale(o, nw_ref[:, vc]) * z_ref[:, vc].astype(F32)).astype(o_ref.dtype)


def _retention(proj, norm_w, nc):
    m = proj.shape[0]
    tab = pltpu.VMEM((RET_HEADS, CHUNK, CHUNK), F32)
    return pl.pallas_call(
        functools.partial(_retention_kernel, nc=nc),
        grid=(m // CHUNK,),
        in_specs=[pl.BlockSpec((CHUNK, RET_QK), lambda c: (c, 0)),
                  pl.BlockSpec((CHUNK, RET_QK), lambda c: (c, 1)),
                  pl.BlockSpec((CHUNK, RET_W), lambda c: (c, 1)),
                  pl.BlockSpec((CHUNK, RET_W), lambda c: (c, 2)),
                  pl.BlockSpec((1, RET_W), lambda c: (0, 0))],
        out_specs=pl.BlockSpec((CHUNK, RET_W), lambda c: (c, 0)),
        out_shape=jax.ShapeDtypeStruct((m, RET_W), BF16),
        scratch_shapes=[pltpu.VMEM((RET_HEADS, RET_DK, RET_DV), F32), tab, tab, tab],
        compiler_params=_cparams(("arbitrary",)),
        name="retention",
    )(proj, proj, proj, proj, norm_w.reshape(1, RET_W))


def _s5_prep_kernel(lre_ref, lim_ref, ldt_ref, bre_ref, bim_ref, pre_ref, pim_ref, bbre_ref, bbim_ref):
    lre = lre_ref[...]
    lim = lim_ref[...]
    dt = jnp.exp(ldt_ref[...])
    tseg = pre_ref.shape[0]
    n = lax.broadcasted_iota(jnp.int32, (tseg, 1), 0).astype(F32) + 1.0
    mag = jnp.exp(n * (lre * dt))
    ang = n * (lim * dt)
    p_re = mag * jnp.cos(ang)
    p_im = mag * jnp.sin(ang)
    pre_ref[...] = p_re
    pim_ref[...] = p_im
    a_re = p_re[0:1]
    a_im = p_im[0:1]
    den = lre * lre + lim * lim
    nr = a_re - 1.0
    f_re = (nr * lre + a_im * lim) / den
    f_im = (a_im * lre - nr * lim) / den
    b_re = bre_ref[...]
    b_im = bim_ref[...]
    bbre_ref[...] = f_re * b_re - f_im * b_im
    bbim_ref[...] = f_re * b_im + f_im * b_re


def _s5_prep(lam_re, lam_im, log_dt, b_re, b_im, tseg):
    flat = lambda a: a.reshape(1, S5_F)
    ldt = jnp.broadcast_to(log_dt[:, None], (S5_G, S5_P))
    bt = lambda a: a.reshape(S5_F, S5_GH).T
    out = jax.ShapeDtypeStruct
    return pl.pallas_call(
        _s5_prep_kernel,
        out_shape=(out((tseg, S5_F), F32), out((tseg, S5_F), F32),
                   out((S5_GH, S5_F), F32), out((S5_GH, S5_F), F32)),
        compiler_params=pltpu.CompilerParams(vmem_limit_bytes=VMEM_LIMIT),
        name="s5_prep",
    )(flat(lam_re), flat(lam_im), flat(ldt), bt(b_re), bt(b_im))


def _gelu_tanh(x):
    return 0.5 * x * (1.0 + jnp.tanh(math.sqrt(2.0 / math.pi) * (x + 0.044715 * (x * x * x))))


def _s5_kernel(u_ref, z_ref, wb_ref, wcre_ref, wcim_ref, pre_ref, pim_ref, d_ref, wglu_ref,
               o_ref, bure_ref, buim_ref, cre_ref, cim_ref, xinre_ref, xinim_ref, *, tseg, slab_group):
    ci = pl.program_id(1)

    @pl.when(ci == 0)
    def _():
        cre_ref[...] = jnp.zeros_like(cre_ref)
        cim_ref[...] = jnp.zeros_like(cim_ref)

    u_bf = u_ref[...]
    spb = S5_SLABS // S5_JB
    for j in range(S5_JB):
        res = jnp.dot(u_bf[:, j * 256:(j + 1) * 256], wb_ref[j], preferred_element_type=F32)
        for s in range(spb):
            bure_ref[j * spb + s] = res[:, s * 128:(s + 1) * 128]
            buim_ref[j * spb + s] = res[:, (spb + s) * 128:(spb + s + 1) * 128]

    for g0 in range(0, S5_SLABS, slab_group):
        slabs = list(range(g0, g0 + slab_group))
        a_re = [jnp.broadcast_to(pre_ref[sl, 0:1, :], (S5_SEGS, 128)) for sl in slabs]
        a_im = [jnp.broadcast_to(pim_ref[sl, 0:1, :], (S5_SEGS, 128)) for sl in slabs]

        def scan_body(tau, carry, slabs=slabs, a_re=a_re, a_im=a_im):
            xr, xi = carry
            nr, ni = [], []
            for n, sl in enumerate(slabs):
                rows = pl.ds(tau, S5_SEGS, stride=tseg)
                br = bure_ref[sl, rows, :]
                bi = buim_ref[sl, rows, :]
                r = a_re[n] * xr[n] - a_im[n] * xi[n] + br
                i = a_re[n] * xi[n] + a_im[n] * xr[n] + bi
                bure_ref[sl, rows, :] = r
                buim_ref[sl, rows, :] = i
                nr.append(r)
                ni.append(i)
            return tuple(nr), tuple(ni)

        zero = tuple(jnp.zeros((S5_SEGS, 128), F32) for _ in slabs)
        end_re, end_im = lax.fori_loop(0, tseg, scan_body, (zero, zero))

        for n, sl in enumerate(slabs):
            at_re = pre_ref[sl, tseg - 1:tseg, :]
            at_im = pim_ref[sl, tseg - 1:tseg, :]
            xr = cre_ref[sl, 0:1, :]
            xi = cim_ref[sl, 0:1, :]
            rows_re, rows_im = [], []
            for s in range(S5_SEGS):
                rows_re.append(xr)
                rows_im.append(xi)
                er = end_re[n][s:s + 1]
                ei = end_im[n][s:s + 1]
                xr, xi = er + at_re * xr - at_im * xi, ei + at_re * xi + at_im * xr
            cre_ref[sl] = jnp.broadcast_to(xr, (S5_SEGS, 128))
            cim_ref[sl] = jnp.broadcast_to(xi, (S5_SEGS, 128))
            xinre_ref[sl] = jnp.concatenate(rows_re, axis=0)
            xinim_ref[sl] = jnp.concatenate(rows_im, axis=0)

    def fix_body(sl, carry):
        xin_re = xinre_ref[sl]
        xin_im = xinim_ref[sl]
        for tau in range(tseg):
            rows = pl.ds(tau, S5_SEGS, stride=tseg)
            pr = pre_ref[sl, tau:tau + 1, :]
            pi = pim_ref[sl, tau:tau + 1, :]
            bure_ref[sl, rows, :] = bure_ref[sl, rows, :] + (pr * xin_re - pi * xin_im)
            buim_ref[sl, rows, :] = buim_ref[sl, rows, :] + (pr * xin_im + pi * xin_re)
        return carry

    lax.fori_loop(0, S5_SLABS, fix_body, 0)

    ys = []
    for j in range(S5_JB):
        x_re = jnp.concatenate([bure_ref[j * spb + s] for s in range(spb)], axis=1).astype(BF16)
        x_im = jnp.concatenate([buim_ref[j * spb + s] for s in range(spb)], axis=1).astype(BF16)
        ys.append(jnp.dot(x_re, wcre_ref[j], preferred_element_type=F32)
                  - jnp.dot(x_im, wcim_ref[j], preferred_element_type=F32))
    y = jnp.concatenate(ys, axis=1) + d_ref[...] * u_bf.astype(F32)
    y = _gelu_tanh(y)
    gate = jnp.dot(y.astype(BF16), wglu_ref[...], preferred_element_type=F32)
    y = y * (1.0 / (1.0 + jnp.exp(-gate)))
    o_ref[...] = (y * z_ref[...].astype(F32)).astype(o_ref.dtype)


def _s5_time_tile(rows_per_batch):
    for tseg in (66, 44, 132, 12, 4, 6, 2):
        t = S5_SEGS * tseg
        if rows_per_batch % t == 0 and t % 16 == 0:
            return t, tseg
    raise ValueError(f"no S5 time tile for {rows_per_batch}")


def _s5(proj, bsz, wb, wcre, wcim, p_re, p_im, d, wglu, t, tseg):
    m = proj.shape[0]
    nt = m // bsz // t
    ucol = (2 * RET_QK + 2 * RET_W) // S5_W
    const = lambda *shape: pl.BlockSpec(shape, lambda b, i: (0,) * len(shape))
    slabs = lambda p: p.reshape(tseg, S5_SLABS, 128).transpose(1, 0, 2)
    return pl.pallas_call(
        functools.partial(_s5_kernel, tseg=tseg, slab_group=8),
        grid=(bsz, nt),
        in_specs=[pl.BlockSpec((t, S5_W), lambda b, i: (b * nt + i, ucol)),
                  pl.BlockSpec((t, S5_W), lambda b, i: (b * nt + i, ucol + 1)),
                  const(S5_JB, 256, 2048), const(S5_JB, 1024, 256), const(S5_JB, 1024, 256),
                  const(S5_SLABS, tseg, 128), const(S5_SLABS, tseg, 128), const(1, S5_W),
                  const(S5_W, S5_W)],
        out_specs=pl.BlockSpec((t, S5_W), lambda b, i: (b * nt + i, 0)),
        out_shape=jax.ShapeDtypeStruct((m, S5_W), BF16),
        scratch_shapes=[pltpu.VMEM((S5_SLABS, t, 128), F32), pltpu.VMEM((S5_SLABS, t, 128), F32)]
        + [pltpu.VMEM((S5_SLABS, S5_SEGS, 128), F32)] * 4,
        compiler_params=_cparams(("arbitrary", "arbitrary")),
        name="s5",
    )(proj, proj, wb, wcre, wcim, slabs(p_re), slabs(p_im), d.reshape(1, S5_W), wglu)


def _s5_block_weights(bb_re, bb_im, c_re, c_im):
    eye = jnp.eye(16, dtype=F32)

    def wb_part(bb):
        a = bb.reshape(S5_GH, S5_JB, 16, S5_P)
        w = jnp.einsum('hjgp,gk->jghkp', a, eye)
        return w.reshape(S5_JB, 256, 1024)

    wb = jnp.concatenate([wb_part(bb_re), wb_part(bb_im)], axis=2).astype(BF16)

    def wc_part(c):
        a = c.astype(F32).reshape(S5_JB, 16, S5_GH, S5_P)
        w = jnp.einsum('jghp,gk->jgpkh', a, eye)
        return w.reshape(S5_JB, 1024, 256).astype(BF16)

    return wb, wc_part(c_re), wc_part(c_im)


def _out_ab_kernel(lead_ref, xa_ref, xb_ref, xc_ref, oa_ref, ob_ref, w_ref, nw_ref, wg_ref,
                   h1_ref, xn_ref, gl_ref):
    t = pl.program_id(1)
    first = jnp.where(t == 0, lead_ref[...], xa_ref[...])
    for r, rows in enumerate((first, xb_ref[...], xc_ref[...])):
        sl = slice(r * CHUNK, (r + 1) * CHUNK)
        acc = jnp.dot(oa_ref[sl, :], w_ref[:RET_W, :], preferred_element_type=F32)
        acc = acc + jnp.dot(ob_ref[sl, :], w_ref[RET_W:, :], preferred_element_type=F32)
        h1 = rows + acc
        h1_ref[sl, :] = h1
        xn = _rms_scale(h1, nw_ref[...]).astype(BF16)
        xn_ref[sl, :] = xn
        gl_ref[sl, :] = jnp.dot(xn, wg_ref[...], preferred_element_type=F32)


def _out_ab(x, lead, oa, ob, w, norm_w, wg, nt):
    bsz, _, d = x.shape
    tile = ROW_CHUNKS * CHUNK
    m = bsz * nt * tile
    row = lambda width: pl.BlockSpec((tile, width), lambda b, t: (b * nt + t, 0))
    const = lambda a: pl.BlockSpec(a.shape, lambda b, t: (0, 0))
    out = jax.ShapeDtypeStruct
    return pl.pallas_call(
        _out_ab_kernel,
        grid=(bsz, nt),
        in_specs=[pl.BlockSpec((CHUNK, d), lambda b, t: (0, 0))] + _chunk_specs(d, nt, 1)
        + [row(RET_W), row(S5_W), const(w), pl.BlockSpec((1, d), lambda b, t: (0, 0)), const(wg)],
        out_specs=[row(d), row(d), row(128)],
        out_shape=(out((m, d), F32), out((m, d), BF16), out((m, 128), F32)),
        compiler_params=_cparams(("parallel", "parallel")),
        name="out_proj_ab",
    )(lead, x, x, x, oa, ob, w, norm_w.reshape(1, d), wg)


LOG2E = 1.4426950408889634


def _split2(x):
    x1 = x.astype(BF16)
    return x1, (x - x1.astype(F32)).astype(BF16)


def _neg_abs(x):
    return -jnp.abs(x)


def _block_ref_rows(b, level):
    rows, width = b.shape
    half = 1 << level
    blk = 2 * half
    if blk >= 8:
        b3 = b.reshape(rows // blk, blk, width)
        ref = jnp.broadcast_to(b3[:, half - 1:half, :], b3.shape)
        return ref.reshape(rows, width)
    b3 = b.reshape(rows // 8, 8, width)
    sub = lax.broadcasted_iota(jnp.int32, b3.shape, 1)
    pick = lambda r: jnp.broadcast_to(b3[:, r:r + 1, :], b3.shape)
    if blk == 4:
        ref = jnp.where(sub < 4, pick(1), pick(5))
    else:
        ref = jnp.where(sub < 2, pick(0), jnp.where(sub < 4, pick(2), jnp.where(sub < 6, pick(4), pick(6))))
    return ref.reshape(rows, width)


def _gla_kernel(q_ref, k_ref, v_ref, z_ref, gl_ref, wg_ref, bg_ref, nw_ref, h_ref, wo_ref, fw_ref,
                y_ref, st_ref, mix_ref, *, nc, nchunks):
    c = pl.program_id(0)
    n = jnp.minimum(c, nchunks - 1) % nc

    @pl.when(c == 0)
    def _():
        mix_ref[...] = jnp.zeros_like(mix_ref)

    @pl.when(n == 0)
    def _():
        st_ref[...] = jnp.zeros_like(st_ref)


    ridx = lax.broadcasted_iota(jnp.int32, (CHUNK, 1), 0)
    ii = lax.broadcasted_iota(jnp.int32, (CHUNK, CHUNK), 0)
    jj = lax.broadcasted_iota(jnp.int32, (CHUNK, CHUNK), 1)
    pair_code = jnp.where(ii > jj, ii ^ jj, 0)
    eye = ii == jj
    tri = (ii >= jj).astype(BF16)

    gl1, gl2 = _split2(gl_ref[:, :GLA_RANK])
    wg1, wg2 = _split2(wg_ref[...])
    x = (jnp.dot(gl1, wg1, preferred_element_type=F32) + jnp.dot(gl1, wg2, preferred_element_type=F32)
         + jnp.dot(gl2, wg1, preferred_element_type=F32)) + bg_ref[...]
    log_a = (jnp.minimum(x, 0.0) - jnp.log(1.0 + jnp.exp(-jnp.abs(x)))) * (LOG2E / GLA_TAU)
    log_a = jnp.where(ridx >= jnp.where(n > 0, 0, PAD), log_a, 0.0)
    g1, g2 = _split2(log_a)
    b_all = jnp.dot(tri, g1, preferred_element_type=F32) + jnp.dot(tri, g2, preferred_element_type=F32)
    h2 = h_ref[...]
    for h in range(GLA_HEADS):
        kc = slice(h * GLA_DK, (h + 1) * GLA_DK)
        vc = slice(h * GLA_DV, (h + 1) * GLA_DV)
        h2 = h2 + jnp.dot(mix_ref[:, vc], wo_ref[vc, :], preferred_element_type=F32)
        q = q_ref[:, kc].astype(F32)
        k = k_ref[:, kc].astype(F32)
        v = v_ref[:, vc]
        b = b_all[:, kc]
        b_last = b[CHUNK - 1:CHUNK, :]
        scores = jnp.where(eye, jnp.sum(q * k, axis=1, keepdims=True), 0.0)
        for lv in range(GLA_LEVELS):
            upper = ((ridx >> lv) & 1) == 1
            fac = jnp.exp2(_neg_abs(b - _block_ref_rows(b, lv)))
            zz = (jnp.where(upper, q, k) * fac).astype(BF16)
            gram = lax.dot_general(zz, zz, (((1,), (1,)), ((), ())), preferred_element_type=F32)
            scores = scores + jnp.where((pair_code >> lv) == 1, gram, 0.0)
        o = jnp.dot(scores.astype(BF16), v, preferred_element_type=F32)
        st = st_ref[h]
        qe = (q * jnp.exp2(b)).astype(BF16)
        o = o + lax.dot_general(qe, st.astype(BF16), (((1,), (1,)), ((), ())), preferred_element_type=F32)
        ke = (k * jnp.exp2(b_last - b)).astype(BF16)
        kv_t = lax.dot_general(v, ke, (((0,), (0,)), ((), ())), preferred_element_type=F32)
        st_ref[h] = st * jnp.exp2(b_last) + kv_t
        mix_ref[:, vc] = (_rms_scale(o, nw_ref[:, vc]) * z_ref[:, vc].astype(F32)).astype(BF16)

    y_ref[...] = _rms_scale(h2, fw_ref[...])


def _gla(proj, glow, w_gate, b_gate, norm_w, h1, w_out, final_w, bsz, nc):
    m, d = h1.shape
    nchunks = m // CHUNK
    const = lambda *shape: pl.BlockSpec(shape, lambda c: (0,) * len(shape))
    cur = lambda width, col: pl.BlockSpec((CHUNK, width), lambda c: (jnp.minimum(c, nchunks - 1), col))
    prev = lambda c: jnp.maximum(c - 1, 0)
    return pl.pallas_call(
        functools.partial(_gla_kernel, nc=nc, nchunks=nchunks),
        grid=(nchunks + 1,),
        in_specs=[cur(GLA_QK, 0), cur(GLA_QK, 1), cur(GLA_W, 1), cur(GLA_W, 2), cur(128, 0),
                  const(GLA_RANK, GLA_QK), const(1, GLA_QK), const(1, GLA_W),
                  pl.BlockSpec((CHUNK, d), lambda c: (prev(c), 0)),
                  const(GLA_W, d), const(1, d)],
        out_specs=pl.BlockSpec((None, CHUNK, d),
                               lambda c: (prev(c) // nc, jnp.maximum(prev(c) % nc - 1, 0), 0)),
        out_shape=jax.ShapeDtypeStruct((bsz, (nc - 1) * CHUNK, d), F32),
        scratch_shapes=[pltpu.VMEM((GLA_HEADS, GLA_DV, GLA_DK), F32), pltpu.VMEM((CHUNK, GLA_W), BF16)],
        compiler_params=_cparams(("arbitrary",)),
        name="gla_out",
    )(proj, proj, proj, proj, glow, w_gate.astype(F32), b_gate.reshape(1, GLA_QK).astype(F32),
      norm_w.reshape(1, GLA_W).astype(F32), h1, w_out, final_w.reshape(1, d).astype(F32))


def _rope_tables(bsz, rows_per_batch):
    pos = jnp.maximum(jnp.arange(rows_per_batch, dtype=F32) - PAD, 0.0)
    inv_freq = jnp.power(ROPE_BASE, -jnp.arange(0, RET_DK, 2, dtype=F32) / RET_DK)
    ang = pos[:, None] * inv_freq[None, :]
    cos, sin = jnp.cos(ang), jnp.sin(ang)
    cos2 = jnp.concatenate([cos, cos], axis=1)
    sin2 = jnp.concatenate([-sin, sin], axis=1)
    return jnp.tile(cos2, (bsz, 1)), jnp.tile(sin2, (bsz, 1))


def kernel(x, meta, norm_ab_w, w_in_ab, ret_norm_w, s5_lam_re, s5_lam_im, s5_log_dt, s5_b_re, s5_b_im,
           s5_c_re, s5_c_im, s5_d, s5_w_glu, w_out_ab, norm_c_w, w_in_c, gla_w_gate, gla_b_gate,
           gla_norm_w, w_out_c, final_norm_w):
    bsz, seq, d = x.shape
    assert seq % CHUNK == 0 and w_in_ab.shape[0] == 1 and w_in_c.shape[0] == 1
    rpb = seq + CHUNK
    nc = rpb // CHUNK
    assert nc % ROW_CHUNKS == 0
    nt = nc // ROW_CHUNKS
    m = bsz * rpb
    tm = _pick_tile(m, (1408, 768, 384, 256, 128))
    lead = jnp.concatenate([jnp.zeros((PAD, d), x.dtype), meta.astype(x.dtype)], axis=0)

    xn0 = _embed_norm(x, lead, norm_ab_w[0], nt)
    plain, silu = ("plain", 1.0), ("silu", 1.0)
    modes_ab = [("rope", 1.0), ("rope", RET_DK ** -0.5), plain, plain, silu, silu, plain, silu]
    proj0 = _in_proj(xn0, w_in_ab, modes_ab, tm, rope=_rope_tables(bsz, rpb))
    o_a = _retention(proj0, ret_norm_w[0], nc)
    t, tseg = _s5_time_tile(rpb)
    p_re, p_im, bb_re, bb_im = _s5_prep(s5_lam_re[0], s5_lam_im[0], s5_log_dt[0], s5_b_re[0], s5_b_im[0], tseg)
    wb, wcre, wcim = _s5_block_weights(bb_re, bb_im, s5_c_re[0], s5_c_im[0])
    o_b = _s5(proj0, bsz, wb, wcre, wcim, p_re, p_im, s5_d[0], s5_w_glu[0].astype(BF16), t, tseg)

    w_in_c_t = jnp.swapaxes(w_in_c, 1, 2)
    n_main = 2 * GLA_QK + 2 * GLA_W
    wg = jnp.pad(w_in_c_t[0, n_main:, :], ((0, 128 - GLA_RANK), (0, 0))).astype(BF16).T
    h1, xn1, glow = _out_ab(x, lead, o_a, o_b, w_out_ab[0].astype(BF16), norm_c_w[0], wg, nt)

    modes_c = [("plain", GLA_DK ** -0.5), plain, plain, plain, silu, silu]
    proj1 = _in_proj(xn1, w_in_c_t, modes_c, tm, w_is_transposed=True)
    return _gla(proj1, glow, gla_w_gate[0], gla_b_gate[0], gla_norm_w[0], h1, w_out_c[0].astype(BF16),
                final_norm_w, bsz, nc)
```

```python
import functools
import math

import jax
import jax.numpy as jnp
from jax import lax
from jax.experimental import pallas as pl
from jax.experimental.pallas import tpu as pltpu

F32 = jnp.float32
BF16 = jnp.bfloat16

N_META = 16
CHUNK = 128
PAD = CHUNK - N_META
EPS = 1e-6
ROW_CHUNKS = 3

RET_HEADS = 8
RET_DK = 128
RET_DV = 256
RET_QK = RET_HEADS * RET_DK
RET_W = RET_HEADS * RET_DV
ROPE_BASE = 10000.0

S5_W = 1024
S5_GH = 16
S5_G = 64
S5_P = 64
S5_F = S5_G * S5_P
S5_SLABS = S5_F // 128
S5_JB = 4
S5_SEGS = 8

GLA_HEADS = 4
GLA_DK = 256
GLA_DV = 512
GLA_QK = GLA_HEADS * GLA_DK
GLA_W = GLA_HEADS * GLA_DV
GLA_RANK = 16
GLA_TAU = 16.0
GLA_LEVELS = 7

PROJ_TN = 1024
VMEM_LIMIT = 56 * 1024 * 1024


def _cparams(sem):
    return pltpu.CompilerParams(dimension_semantics=sem, vmem_limit_bytes=VMEM_LIMIT)


def _silu(x):
    return x * (1.0 / (1.0 + jnp.exp(-x)))


def _pick_tile(n, candidates):
    for c in candidates:
        if n % c == 0:
            return c
    raise ValueError(f"no tile for {n}")


def _rms_scale(x, w):
    return x * lax.rsqrt(jnp.mean(x * x, axis=-1, keepdims=True) + EPS) * w


def _chunk_specs(d, nt, n_before):
    del nt
    return [pl.BlockSpec((None, CHUNK, d),
                         lambda b, t, k=k: (b, jnp.maximum(ROW_CHUNKS * t + k - n_before, 0), 0))
            for k in range(ROW_CHUNKS)]


def _embed_norm_kernel(lead_ref, xa_ref, xb_ref, xc_ref, w_ref, o_ref):
    t = pl.program_id(1)
    first = jnp.where(t == 0, lead_ref[...], xa_ref[...])
    for r, rows in enumerate((first, xb_ref[...], xc_ref[...])):
        o_ref[r * CHUNK:(r + 1) * CHUNK, :] = _rms_scale(rows, w_ref[...]).astype(o_ref.dtype)


def _embed_norm(x, lead, w, nt):
    bsz, _, d = x.shape
    tile = ROW_CHUNKS * CHUNK
    return pl.pallas_call(
        _embed_norm_kernel,
        grid=(bsz, nt),
        in_specs=[pl.BlockSpec((CHUNK, d), lambda b, t: (0, 0))] + _chunk_specs(d, nt, 1)
        + [pl.BlockSpec((1, d), lambda b, t: (0, 0))],
        out_specs=pl.BlockSpec((tile, d), lambda b, t: (b * nt + t, 0)),
        out_shape=jax.ShapeDtypeStruct((bsz * nt * tile, d), BF16),
        compiler_params=_cparams(("parallel", "parallel")),
        name="embed_norm",
    )(lead, x, x, x, w.reshape(1, d))


def _rope_rows(a, cos, sin, scale):
    outs = []
    for h in range(a.shape[1] // RET_DK):
        blk = a[:, h * RET_DK:(h + 1) * RET_DK]
        r = blk * cos + pltpu.roll(blk, RET_DK // 2, 1) * sin
        outs.append(r * scale if scale != 1.0 else r)
    return jnp.concatenate(outs, axis=1)


def _in_proj_kernel(*refs, modes, has_rope, w_is_transposed):
    if has_rope:
        x_ref, w_ref, cos_ref, sin_ref, o_ref, wbf_ref = refs
    else:
        x_ref, w_ref, o_ref, wbf_ref = refs
    j = pl.program_id(0)

    @pl.when(pl.program_id(1) == 0)
    def _():
        wbf_ref[...] = w_ref[...].astype(BF16)

    contract = (((1,), (1 if w_is_transposed else 0,)), ((), ()))
    for mode in sorted(set(modes)):
        cond = functools.reduce(jnp.logical_or, [j == jj for jj, mm in enumerate(modes) if mm == mode])

        @pl.when(cond)
        def _(mode=mode):
            kind, scale = mode
            acc = lax.dot_general(x_ref[...], wbf_ref[...], contract, preferred_element_type=F32)
            if kind == "rope":
                acc = _rope_rows(acc, cos_ref[...], sin_ref[...], scale)
            elif kind == "silu":
                acc = _silu(acc)
            elif scale != 1.0:
                acc = acc * scale
            o_ref[...] = acc.astype(o_ref.dtype)


def _in_proj(x, w, modes, tm, rope=None, w_is_transposed=False):
    m, k = x.shape
    nt = len(modes)
    if w_is_transposed:
        w_spec = pl.BlockSpec((None, PROJ_TN, k), lambda j, i: (0, j, 0))
        w_scratch = pltpu.VMEM((PROJ_TN, k), BF16)
    else:
        w_spec = pl.BlockSpec((None, k, PROJ_TN), lambda j, i: (0, 0, j))
        w_scratch = pltpu.VMEM((k, PROJ_TN), BF16)
    in_specs = [pl.BlockSpec((tm, k), lambda j, i: (i, 0)), w_spec]
    args = [x, w]
    if rope is not None:
        in_specs += [pl.BlockSpec((tm, RET_DK), lambda j, i: (i, 0))] * 2
        args += list(rope)
    return pl.pallas_call(
        functools.partial(_in_proj_kernel, modes=tuple(modes), has_rope=rope is not None,
                          w_is_transposed=w_is_transposed),
        grid=(nt, m // tm),
        in_specs=in_specs,
        out_specs=pl.BlockSpec((tm, PROJ_TN), lambda j, i: (i, j)),
        out_shape=jax.ShapeDtypeStruct((m, nt * PROJ_TN), BF16),
        scratch_shapes=[w_scratch],
        compiler_params=_cparams(("arbitrary", "arbitrary")),
        name="in_proj",
    )(*args)


def _ret_log_decay(h):
    return math.log1p(-(2.0 ** (-5.0 - h)))


def _retention_kernel(q_ref, k_ref, v_ref, z_ref, nw_ref, o_ref, s_ref, dec_ref, qs_ref, ks_ref, *, nc):
    c = pl.program_id(0)

    @pl.when(c == 0)
    def _():
        ii = lax.broadcasted_iota(jnp.int32, (CHUNK, CHUNK), 0)
        jj = lax.broadcasted_iota(jnp.int32, (CHUNK, CHUNK), 1)
        diff = (ii - jj).astype(F32)
        row = ii.astype(F32)
        for h in range(RET_HEADS):
            lg = _ret_log_decay(h)
            dec_ref[h] = jnp.where(ii >= jj, jnp.exp(lg * jnp.maximum(diff, 0.0)), 0.0)
            qs_ref[h] = jnp.exp(lg * (row + 1.0))
            ks_ref[h] = jnp.exp(lg * (CHUNK - 1.0 - row))

    @pl.when(c % nc == 0)
    def _():
        s_ref[...] = jnp.zeros_like(s_ref)

    for h in range(RET_HEADS):
        lg = _ret_log_decay(h)
        vc = slice(h * RET_DV, (h + 1) * RET_DV)
        q = q_ref[:, h * RET_DK:(h + 1) * RET_DK]
        k = k_ref[:, h * RET_DK:(h + 1) * RET_DK]
        v = v_ref[:, vc]
        scores = lax.dot_general(q, k, (((1,), (1,)), ((), ())), preferred_element_type=F32) * dec_ref[h]
        s_prev = s_ref[h]
        lhs = jnp.concatenate([scores.astype(BF16), (q.astype(F32) * qs_ref[h]).astype(BF16)], axis=1)
        rhs = jnp.concatenate([v, s_prev.astype(BF16)], axis=0)
        o = jnp.dot(lhs, rhs, preferred_element_type=F32)
        k_w = (k.astype(F32) * ks_ref[h]).astype(BF16)
        kv = lax.dot_general(k_w, v, (((0,), (0,)), ((), ())), preferred_element_type=F32)
        s_ref[h] = s_prev * math.exp(lg * CHUNK) + kv
        o_ref[:, vc] = (_rms_scale(o, nw_ref[:, vc]) * z_ref[:, vc].astype(F32)).astype(o_ref.dtype)


def _retention(proj, norm_w, nc):
    m = proj.shape[0]
    tab = pltpu.VMEM((RET_HEADS, CHUNK, CHUNK), F32)
    return pl.pallas_call(
        functools.partial(_retention_kernel, nc=nc),
        grid=(m // CHUNK,),
        in_specs=[pl.BlockSpec((CHUNK, RET_QK), lambda c: (c, 0)),
                  pl.BlockSpec((CHUNK, RET_QK), lambda c: (c, 1)),
                  pl.BlockSpec((CHUNK, RET_W), lambda c: (c, 1)),
                  pl.BlockSpec((CHUNK, RET_W), lambda c: (c, 2)),
                  pl.BlockSpec((1, RET_W), lambda c: (0, 0))],
        out_specs=pl.BlockSpec((CHUNK, RET_W), lambda c: (c, 0)),
        out_shape=jax.ShapeDtypeStruct((m, RET_W), BF16),
        scratch_shapes=[pltpu.VMEM((RET_HEADS, RET_DK, RET_DV), F32), tab, tab, tab],
        compiler_params=_cparams(("arbitrary",)),
        name="retention",
    )(proj, proj, proj, proj, norm_w.reshape(1, RET_W))


def _s5_prep_kernel(lre_ref, lim_ref, ldt_ref, bre_ref, bim_ref, pre_ref, pim_ref, bbre_ref, bbim_ref):
    lre = lre_ref[...]
    lim = lim_ref[...]
    dt = jnp.exp(ldt_ref[...])
    tseg = pre_ref.shape[0]
    n = lax.broadcasted_iota(jnp.int32, (tseg, 1), 0).astype(F32) + 1.0
    mag = jnp.exp(n * (lre * dt))
    ang = n * (lim * dt)
    p_re = mag * jnp.cos(ang)
    p_im = mag * jnp.sin(ang)
    pre_ref[...] = p_re
    pim_ref[...] = p_im
    a_re = p_re[0:1]
    a_im = p_im[0:1]
    den = lre * lre + lim * lim
    nr = a_re - 1.0
    f_re = (nr * lre + a_im * lim) / den
    f_im = (a_im * lre - nr * lim) / den
    b_re = bre_ref[...]
    b_im = bim_ref[...]
    bbre_ref[...] = f_re * b_re - f_im * b_im
    bbim_ref[...] = f_re * b_im + f_im * b_re


def _s5_prep(lam_re, lam_im, log_dt, b_re, b_im, tseg):
    flat = lambda a: a.reshape(1, S5_F)
    ldt = jnp.broadcast_to(log_dt[:, None], (S5_G, S5_P))
    bt = lambda a: a.reshape(S5_F, S5_GH).T
    out = jax.ShapeDtypeStruct
    return pl.pallas_call(
        _s5_prep_kernel,
        out_shape=(out((tseg, S5_F), F32), out((tseg, S5_F), F32),
                   out((S5_GH, S5_F), F32), out((S5_GH, S5_F), F32)),
        compiler_params=pltpu.CompilerParams(vmem_limit_bytes=VMEM_LIMIT),
        name="s5_prep",
    )(flat(lam_re), flat(lam_im), flat(ldt), bt(b_re), bt(b_im))


def _gelu_tanh(x):
    return 0.5 * x * (1.0 + jnp.tanh(math.sqrt(2.0 / math.pi) * (x + 0.044715 * (x * x * x))))


def _s5_kernel(u_ref, z_ref, wb_ref, wcre_ref, wcim_ref, pre_ref, pim_ref, d_ref, wglu_ref,
               o_ref, bure_ref, buim_ref, cre_ref, cim_ref, xinre_ref, xinim_ref, *, tseg, slab_group):
    ci = pl.program_id(1)

    @pl.when(ci == 0)
    def _():
        cre_ref[...] = jnp.zeros_like(cre_ref)
        cim_ref[...] = jnp.zeros_like(cim_ref)

    u_bf = u_ref[...]
    spb = S5_SLABS // S5_JB
    for j in range(S5_JB):
        res = jnp.dot(u_bf[:, j * 256:(j + 1) * 256], wb_ref[j], preferred_element_type=F32)
        for s in range(spb):
            bure_ref[j * spb + s] = res[:, s * 128:(s + 1) * 128]
            buim_ref[j * spb + s] = res[:, (spb + s) * 128:(spb + s + 1) * 128]

    for g0 in range(0, S5_SLABS, slab_group):
        slabs = list(range(g0, g0 + slab_group))
        a_re = [jnp.broadcast_to(pre_ref[sl, 0:1, :], (S5_SEGS, 128)) for sl in slabs]
        a_im = [jnp.broadcast_to(pim_ref[sl, 0:1, :], (S5_SEGS, 128)) for sl in slabs]

        def scan_body(tau, carry, slabs=slabs, a_re=a_re, a_im=a_im):
            xr, xi = carry
            nr, ni = [], []
            for n, sl in enumerate(slabs):
                rows = pl.ds(tau, S5_SEGS, stride=tseg)
                br = bure_ref[sl, rows, :]
                bi = buim_ref[sl, rows, :]
                r = a_re[n] * xr[n] - a_im[n] * xi[n] + br
                i = a_re[n] * xi[n] + a_im[n] * xr[n] + bi
                bure_ref[sl, rows, :] = r
                buim_ref[sl, rows, :] = i
                nr.append(r)
                ni.append(i)
            return tuple(nr), tuple(ni)

        zero = tuple(jnp.zeros((S5_SEGS, 128), F32) for _ in slabs)
        end_re, end_im = lax.fori_loop(0, tseg, scan_body, (zero, zero))

        for n, sl in enumerate(slabs):
            at_re = pre_ref[sl, tseg - 1:tseg, :]
            at_im = pim_ref[sl, tseg - 1:tseg, :]
            xr = cre_ref[sl, 0:1, :]
            xi = cim_ref[sl, 0:1, :]
            rows_re, rows_im = [], []
            for s in range(S5_SEGS):
                rows_re.append(xr)
                rows_im.append(xi)
                er = end_re[n][s:s + 1]
                ei = end_im[n][s:s + 1]
                xr, xi = er + at_re * xr - at_im * xi, ei + at_re * xi + at_im * xr
            cre_ref[sl] = jnp.broadcast_to(xr, (S5_SEGS, 128))
            cim_ref[sl] = jnp.broadcast_to(xi, (S5_SEGS, 128))
            xinre_ref[sl] = jnp.concatenate(rows_re, axis=0)
            xinim_ref[sl] = jnp.concatenate(rows_im, axis=0)

    def fix_body(sl, carry):
        xin_re = xinre_ref[sl]
        xin_im = xinim_ref[sl]
        for tau in range(tseg):
            rows = pl.ds(tau, S5_SEGS, stride=tseg)
            pr = pre_ref[sl, tau:tau + 1, :]
            pi = pim_ref[sl, tau:tau + 1, :]
            bure_ref[sl, rows, :] = bure_ref[sl, rows, :] + (pr * xin_re - pi * xin_im)
            buim_ref[sl, rows, :] = buim_ref[sl, rows, :] + (pr * xin_im + pi * xin_re)
        return carry

    lax.fori_loop(0, S5_SLABS, fix_body, 0)

    ys = []
    for j in range(S5_JB):
        x_re = jnp.concatenate([bure_ref[j * spb + s] for s in range(spb)], axis=1).astype(BF16)
        x_im = jnp.concatenate([buim_ref[j * spb + s] for s in range(spb)], axis=1).astype(BF16)
        ys.append(jnp.dot(x_re, wcre_ref[j], preferred_element_type=F32)
                  - jnp.dot(x_im, wcim_ref[j], preferred_element_type=F32))
    y = jnp.concatenate(ys, axis=1) + d_ref[...] * u_bf.astype(F32)
    y = _gelu_tanh(y)
    gate = jnp.dot(y.astype(BF16), wglu_ref[...], preferred_element_type=F32)
    y = y * (1.0 / (1.0 + jnp.exp(-gate)))
    o_ref[...] = (y * z_ref[...].astype(F32)).astype(o_ref.dtype)


def _s5_time_tile(rows_per_batch):
    for tseg in (66, 44, 132, 12, 4, 6, 2):
        t = S5_SEGS * tseg
        if rows_per_batch % t == 0 and t % 16 == 0:
            return t, tseg
    raise ValueError(f"no S5 time tile for {rows_per_batch}")


def _s5(proj, bsz, wb, wcre, wcim, p_re, p_im, d, wglu, t, tseg):
    m = proj.shape[0]
    nt = m // bsz // t
    ucol = (2 * RET_QK + 2 * RET_W) // S5_W
    const = lambda *shape: pl.BlockSpec(shape, lambda b, i: (0,) * len(shape))
    slabs = lambda p: p.reshape(tseg, S5_SLABS, 128).transpose(1, 0, 2)
    return pl.pallas_call(
        functools.partial(_s5_kernel, tseg=tseg, slab_group=8),
        grid=(bsz, nt),
        in_specs=[pl.BlockSpec((t, S5_W), lambda b, i: (b * nt + i, ucol)),
                  pl.BlockSpec((t, S5_W), lambda b, i: (b * nt + i, ucol + 1)),
                  const(S5_JB, 256, 2048), const(S5_JB, 1024, 256), const(S5_JB, 1024, 256),
                  const(S5_SLABS, tseg, 128), const(S5_SLABS, tseg, 128), const(1, S5_W),
                  const(S5_W, S5_W)],
        out_specs=pl.BlockSpec((t, S5_W), lambda b, i: (b * nt + i, 0)),
        out_shape=jax.ShapeDtypeStruct((m, S5_W), BF16),
        scratch_shapes=[pltpu.VMEM((S5_SLABS, t, 128), F32), pltpu.VMEM((S5_SLABS, t, 128), F32)]
        + [pltpu.VMEM((S5_SLABS, S5_SEGS, 128), F32)] * 4,
        compiler_params=_cparams(("arbitrary", "arbitrary")),
        name="s5",
    )(proj, proj, wb, wcre, wcim, slabs(p_re), slabs(p_im), d.reshape(1, S5_W), wglu)


def _s5_block_weights(bb_re, bb_im, c_re, c_im):
    eye = jnp.eye(16, dtype=F32)

    def wb_part(bb):
        a = bb.reshape(S5_GH, S5_JB, 16, S5_P)
        w = jnp.einsum('hjgp,gk->jghkp', a, eye)
        return w.reshape(S5_JB, 256, 1024)

    wb = jnp.concatenate([wb_part(bb_re), wb_part(bb_im)], axis=2).astype(BF16)

    def wc_part(c):
        a = c.astype(F32).reshape(S5_JB, 16, S5_GH, S5_P)
        w = jnp.einsum('jghp,gk->jgpkh', a, eye)
        return w.reshape(S5_JB, 1024, 256).astype(BF16)

    return wb, wc_part(c_re), wc_part(c_im)


def _out_ab_kernel(lead_ref, xa_ref, xb_ref, xc_ref, oa_ref, ob_ref, w_ref, nw_ref, wg_ref,
                   h1_ref, xn_ref, gl_ref):
    t = pl.program_id(1)
    first = jnp.where(t == 0, lead_ref[...], xa_ref[...])
    for r, rows in enumerate((first, xb_ref[...], xc_ref[...])):
        sl = slice(r * CHUNK, (r + 1) * CHUNK)
        acc = jnp.dot(oa_ref[sl, :], w_ref[:RET_W, :], preferred_element_type=F32)
        acc = acc + jnp.dot(ob_ref[sl, :], w_ref[RET_W:, :], preferred_element_type=F32)
        h1 = rows + acc
        h1_ref[sl, :] = h1
        xn = _rms_scale(h1, nw_ref[...]).astype(BF16)
        xn_ref[sl, :] = xn
        gl_ref[sl, :] = jnp.dot(xn, wg_ref[...], preferred_element_type=F32)


def _out_ab(x, lead, oa, ob, w, norm_w, wg, nt):
    bsz, _, d = x.shape
    tile = ROW_CHUNKS * CHUNK
    m = bsz * nt * tile
    row = lambda width: pl.BlockSpec((tile, width), lambda b, t: (b * nt + t, 0))
    const = lambda a: pl.BlockSpec(a.shape, lambda b, t: (0, 0))
    out = jax.ShapeDtypeStruct
    return pl.pallas_call(
        _out_ab_kernel,
        grid=(bsz, nt),
        in_specs=[pl.BlockSpec((CHUNK, d), lambda b, t: (0, 0))] + _chunk_specs(d, nt, 1)
        + [row(RET_W), row(S5_W), const(w), pl.BlockSpec((1, d), lambda b, t: (0, 0)), const(wg)],
        out_specs=[row(d), row(d), row(128)],
        out_shape=(out((m, d), F32), out((m, d), BF16), out((m, 128), F32)),
        compiler_params=_cparams(("parallel", "parallel")),
        name="out_proj_ab",
    )(lead, x, x, x, oa, ob, w, norm_w.reshape(1, d), wg)


LOG2E = 1.4426950408889634


def _split2(x):
    x1 = x.astype(BF16)
    return x1, (x - x1.astype(F32)).astype(BF16)


def _neg_abs(x):
    return -jnp.abs(x)


def _block_ref_rows(b, level):
    rows, width = b.shape
    half = 1 << level
    blk = 2 * half
    if blk >= 8:
        b3 = b.reshape(rows // blk, blk, width)
        ref = jnp.broadcast_to(b3[:, half - 1:half, :], b3.shape)
        return ref.reshape(rows, width)
    b3 = b.reshape(rows // 8, 8, width)
    sub = lax.broadcasted_iota(jnp.int32, b3.shape, 1)
    pick = lambda r: jnp.broadcast_to(b3[:, r:r + 1, :], b3.shape)
    if blk == 4:
        ref = jnp.where(sub < 4, pick(1), pick(5))
    else:
        ref = jnp.where(sub < 2, pick(0), jnp.where(sub < 4, pick(2), jnp.where(sub < 6, pick(4), pick(6))))
    return ref.reshape(rows, width)


def _gla_kernel(q_ref, k_ref, v_ref, z_ref, gl_ref, wg_ref, bg_ref, nw_ref, h_ref, wo_ref, fw_ref,
                y_ref, st_ref, mix_ref, *, nc, nchunks):
    c = pl.program_id(0)
    n = jnp.minimum(c, nchunks - 1) % nc

    @pl.when(c == 0)
    def _():
        mix_ref[...] = jnp.zeros_like(mix_ref)

    @pl.when(n == 0)
    def _():
        st_ref[...] = jnp.zeros_like(st_ref)


    ridx = lax.broadcasted_iota(jnp.int32, (CHUNK, 1), 0)
    ii = lax.broadcasted_iota(jnp.int32, (CHUNK, CHUNK), 0)
    jj = lax.broadcasted_iota(jnp.int32, (CHUNK, CHUNK), 1)
    pair_code = jnp.where(ii > jj, ii ^ jj, 0)
    eye = ii == jj
    tri = (ii >= jj).astype(BF16)

    gl1, gl2 = _split2(gl_ref[:, :GLA_RANK])
    wg1, wg2 = _split2(wg_ref[...])
    x = jnp.dot(jnp.concatenate([gl1, gl2, gl1], axis=1), jnp.concatenate([wg1, wg1, wg2], axis=0),
                preferred_element_type=F32) + bg_ref[...]
    log_a = (jnp.minimum(x, 0.0) - jnp.log(1.0 + jnp.exp(-jnp.abs(x)))) * (LOG2E / GLA_TAU)
    log_a = jnp.where(ridx >= jnp.where(n > 0, 0, PAD), log_a, 0.0)
    g1, g2 = _split2(log_a)
    b_all = jnp.dot(jnp.concatenate([tri, tri], axis=1), jnp.concatenate([g1, g2], axis=0),
                    preferred_element_type=F32)
    odd = (ridx & 1) == 1
    h2 = h_ref[...]
    for h0 in range(0, GLA_HEADS, 2):
        heads = (h0, h0 + 1)
        qs, ks, bs, scores = [], [], [], []
        for h in heads:
            kc = slice(h * GLA_DK, (h + 1) * GLA_DK)
            vc = slice(h * GLA_DV, (h + 1) * GLA_DV)
            h2 = h2 + jnp.dot(mix_ref[:, vc], wo_ref[vc, :], preferred_element_type=F32)
            qs.append(q_ref[:, kc].astype(F32))
            ks.append(k_ref[:, kc].astype(F32))
            bs.append(b_all[:, kc])
            scores.append(jnp.where(eye, jnp.sum(qs[-1] * ks[-1], axis=1, keepdims=True), 0.0))
        for lv in range(GLA_LEVELS):
            half = 1 << lv
            zz = []
            for q, k, b in zip(qs, ks, bs):
                if lv == 0:
                    expo = jnp.where(odd, b - pltpu.roll(b, 1, 0), 0.0)
                else:
                    expo = _neg_abs(b - _block_ref_rows(b, lv))
                if half >= 8:
                    sel = jnp.concatenate([(q if r % 2 else k)[r * half:(r + 1) * half]
                                           for r in range(CHUNK // half)], axis=0)
                else:
                    sel = jnp.where(((ridx >> lv) & 1) == 1, q, k)
                zz.append((sel * jnp.exp2(expo)).astype(BF16))
            z2 = jnp.concatenate(zz, axis=0)
            gram = lax.dot_general(z2, z2, (((1,), (1,)), ((), ())), preferred_element_type=F32)
            mask = (pair_code >> lv) == 1
            for i in range(2):
                blk = gram[i * CHUNK:(i + 1) * CHUNK, i * CHUNK:(i + 1) * CHUNK]
                scores[i] = scores[i] + jnp.where(mask, blk, 0.0)
        for i, h in enumerate(heads):
            vc = slice(h * GLA_DV, (h + 1) * GLA_DV)
            q, k, b = qs[i], ks[i], bs[i]
            v = v_ref[:, vc]
            b_last = b[CHUNK - 1:CHUNK, :]
            o = jnp.dot(scores[i].astype(BF16), v, preferred_element_type=F32)
            st = st_ref[h]
            qe = (q * jnp.exp2(b)).astype(BF16)
            o = o + lax.dot_general(qe, st.astype(BF16), (((1,), (1,)), ((), ())), preferred_element_type=F32)
            ke = (k * jnp.exp2(b_last - b)).astype(BF16)
            kv_t = lax.dot_general(v, ke, (((0,), (0,)), ((), ())), preferred_element_type=F32)
            st_ref[h] = st * jnp.exp2(b_last) + kv_t
            mix_ref[:, vc] = (_rms_scale(o, nw_ref[:, vc]) * z_ref[:, vc].astype(F32)).astype(BF16)

    y_ref[...] = _rms_scale(h2, fw_ref[...])


def _gla(proj, glow, w_gate, b_gate, norm_w, h1, w_out, final_w, bsz, nc):
    m, d = h1.shape
    nchunks = m // CHUNK
    const = lambda *shape: pl.BlockSpec(shape, lambda c: (0,) * len(shape))
    cur = lambda width, col: pl.BlockSpec((CHUNK, width), lambda c: (jnp.minimum(c, nchunks - 1), col))
    prev = lambda c: jnp.maximum(c - 1, 0)
    return pl.pallas_call(
        functools.partial(_gla_kernel, nc=nc, nchunks=nchunks),
        grid=(nchunks + 1,),
        in_specs=[cur(GLA_QK, 0), cur(GLA_QK, 1), cur(GLA_W, 1), cur(GLA_W, 2), cur(128, 0),
                  const(GLA_RANK, GLA_QK), const(1, GLA_QK), const(1, GLA_W),
                  pl.BlockSpec((CHUNK, d), lambda c: (prev(c), 0)),
                  const(GLA_W, d), const(1, d)],
        out_specs=pl.BlockSpec((None, CHUNK, d),
                               lambda c: (prev(c) // nc, jnp.maximum(prev(c) % nc - 1, 0), 0)),
        out_shape=jax.ShapeDtypeStruct((bsz, (nc - 1) * CHUNK, d), F32),
        scratch_shapes=[pltpu.VMEM((GLA_HEADS, GLA_DV, GLA_DK), F32), pltpu.VMEM((CHUNK, GLA_W), BF16)],
        compiler_params=_cparams(("arbitrary",)),
        name="gla_out",
    )(proj, proj, proj, proj, glow, w_gate.astype(F32), b_gate.reshape(1, GLA_QK).astype(F32),
      norm_w.reshape(1, GLA_W).astype(F32), h1, w_out, final_w.reshape(1, d).astype(F32))


def _rope_tables(bsz, rows_per_batch):
    pos = jnp.maximum(jnp.arange(rows_per_batch, dtype=F32) - PAD, 0.0)
    inv_freq = jnp.power(ROPE_BASE, -jnp.arange(0, RET_DK, 2, dtype=F32) / RET_DK)
    ang = pos[:, None] * inv_freq[None, :]
    cos, sin = jnp.cos(ang), jnp.sin(ang)
    cos2 = jnp.concatenate([cos, cos], axis=1)
    sin2 = jnp.concatenate([-sin, sin], axis=1)
    return jnp.tile(cos2, (bsz, 1)), jnp.tile(sin2, (bsz, 1))


def kernel(x, meta, norm_ab_w, w_in_ab, ret_norm_w, s5_lam_re, s5_lam_im, s5_log_dt, s5_b_re, s5_b_im,
           s5_c_re, s5_c_im, s5_d, s5_w_glu, w_out_ab, norm_c_w, w_in_c, gla_w_gate, gla_b_gate,
           gla_norm_w, w_out_c, final_norm_w):
    bsz, seq, d = x.shape
    assert seq % CHUNK == 0 and w_in_ab.shape[0] == 1 and w_in_c.shape[0] == 1
    rpb = seq + CHUNK
    nc = rpb // CHUNK
    assert nc % ROW_CHUNKS == 0
    nt = nc // ROW_CHUNKS
    m = bsz * rpb
    tm = _pick_tile(m, (1408, 768, 384, 256, 128))
    lead = jnp.concatenate([jnp.zeros((PAD, d), x.dtype), meta.astype(x.dtype)], axis=0)

    xn0 = _embed_norm(x, lead, norm_ab_w[0], nt)
    plain, silu = ("plain", 1.0), ("silu", 1.0)
    modes_ab = [("rope", 1.0), ("rope", RET_DK ** -0.5), plain, plain, silu, silu, plain, silu]
    proj0 = _in_proj(xn0, w_in_ab, modes_ab, tm, rope=_rope_tables(bsz, rpb))
    o_a = _retention(proj0, ret_norm_w[0], nc)
    t, tseg = _s5_time_tile(rpb)
    p_re, p_im, bb_re, bb_im = _s5_prep(s5_lam_re[0], s5_lam_im[0], s5_log_dt[0], s5_b_re[0], s5_b_im[0], tseg)
    wb, wcre, wcim = _s5_block_weights(bb_re, bb_im, s5_c_re[0], s5_c_im[0])
    o_b = _s5(proj0, bsz, wb, wcre, wcim, p_re, p_im, s5_d[0], s5_w_glu[0].astype(BF16), t, tseg)

    w_in_c_t = jnp.swapaxes(w_in_c, 1, 2)
    n_main = 2 * GLA_QK + 2 * GLA_W
    wg = jnp.pad(w_in_c_t[0, n_main:, :], ((0, 128 - GLA_RANK), (0, 0))).astype(BF16).T
    h1, xn1, glow = _out_ab(x, lead, o_a, o_b, w_out_ab[0].astype(BF16), norm_c_w[0], wg, nt)

    modes_c = [("plain", GLA_DK ** -0.5), plain, plain, plain, silu, silu]
    proj1 = _in_proj(xn1, w_in_c_t, modes_c, tm, w_is_transposed=True)
    return _gla(proj1, glow, gla_w_gate[0], gla_b_gate[0], gla_norm_w[0], h1, w_out_c[0].astype(BF16),
                final_norm_w, bsz, nc)
```

```python
import functools
import math

import jax
import jax.numpy as jnp
from jax import lax
from jax.experimental import pallas as pl
from jax.experimental.pallas import tpu as pltpu

F32 = jnp.float32
BF16 = jnp.bfloat16

N_META = 16
CHUNK = 128
PAD = CHUNK - N_META
EPS = 1e-6
ROW_CHUNKS = 3

RET_HEADS = 8
RET_DK = 128
RET_DV = 256
RET_QK = RET_HEADS * RET_DK
RET_W = RET_HEADS * RET_DV
ROPE_BASE = 10000.0

S5_W = 1024
S5_GH = 16
S5_G = 64
S5_P = 64
S5_F = S5_G * S5_P
S5_SLABS = S5_F // 128
S5_JB = 4
S5_SEGS = 8

GLA_HEADS = 4
GLA_DK = 256
GLA_DV = 512
GLA_QK = GLA_HEADS * GLA_DK
GLA_W = GLA_HEADS * GLA_DV
GLA_RANK = 16
GLA_TAU = 16.0
GLA_LEVELS = 7

PROJ_TN = 1024
VMEM_LIMIT = 56 * 1024 * 1024


def _cparams(sem):
    return pltpu.CompilerParams(dimension_semantics=sem, vmem_limit_bytes=VMEM_LIMIT)


def _silu(x):
    return x * (1.0 / (1.0 + jnp.exp(-x)))


def _pick_tile(n, candidates):
    for c in candidates:
        if n % c == 0:
            return c
    raise ValueError(f"no tile for {n}")


def _rms_scale(x, w):
    return x * lax.rsqrt(jnp.mean(x * x, axis=-1, keepdims=True) + EPS) * w


def _chunk_specs(d, nt, n_before):
    del nt
    return [pl.BlockSpec((None, CHUNK, d),
                         lambda b, t, k=k: (b, jnp.maximum(ROW_CHUNKS * t + k - n_before, 0), 0))
            for k in range(ROW_CHUNKS)]


def _embed_norm_kernel(lead_ref, xa_ref, xb_ref, xc_ref, w_ref, o_ref):
    t = pl.program_id(1)
    first = jnp.where(t == 0, lead_ref[...], xa_ref[...])
    for r, rows in enumerate((first, xb_ref[...], xc_ref[...])):
        o_ref[r * CHUNK:(r + 1) * CHUNK, :] = _rms_scale(rows, w_ref[...]).astype(o_ref.dtype)


def _embed_norm(x, lead, w, nt):
    bsz, _, d = x.shape
    tile = ROW_CHUNKS * CHUNK
    return pl.pallas_call(
        _embed_norm_kernel,
        grid=(bsz, nt),
        in_specs=[pl.BlockSpec((CHUNK, d), lambda b, t: (0, 0))] + _chunk_specs(d, nt, 1)
        + [pl.BlockSpec((1, d), lambda b, t: (0, 0))],
        out_specs=pl.BlockSpec((tile, d), lambda b, t: (b * nt + t, 0)),
        out_shape=jax.ShapeDtypeStruct((bsz * nt * tile, d), BF16),
        compiler_params=_cparams(("parallel", "parallel")),
        name="embed_norm",
    )(lead, x, x, x, w.reshape(1, d))


def _rope_rows(a, cos, sin, scale):
    outs = []
    for h in range(a.shape[1] // RET_DK):
        blk = a[:, h * RET_DK:(h + 1) * RET_DK]
        r = blk * cos + pltpu.roll(blk, RET_DK // 2, 1) * sin
        outs.append(r * scale if scale != 1.0 else r)
    return jnp.concatenate(outs, axis=1)


def _in_proj_kernel(*refs, modes, has_rope, w_is_transposed):
    if has_rope:
        x_ref, w_ref, cos_ref, sin_ref, o_ref, wbf_ref = refs
    else:
        x_ref, w_ref, o_ref, wbf_ref = refs
    j = pl.program_id(0)

    @pl.when(pl.program_id(1) == 0)
    def _():
        wbf_ref[...] = w_ref[...].astype(BF16)

    contract = (((1,), (1 if w_is_transposed else 0,)), ((), ()))
    for mode in sorted(set(modes)):
        cond = functools.reduce(jnp.logical_or, [j == jj for jj, mm in enumerate(modes) if mm == mode])

        @pl.when(cond)
        def _(mode=mode):
            kind, scale = mode
            acc = lax.dot_general(x_ref[...], wbf_ref[...], contract, preferred_element_type=F32)
            if kind == "rope":
                acc = _rope_rows(acc, cos_ref[...], sin_ref[...], scale)
            elif kind == "silu":
                acc = _silu(acc)
            elif scale != 1.0:
                acc = acc * scale
            o_ref[...] = acc.astype(o_ref.dtype)


def _in_proj(x, w, modes, tm, rope=None, w_is_transposed=False):
    m, k = x.shape
    nt = len(modes)
    if w_is_transposed:
        w_spec = pl.BlockSpec((None, PROJ_TN, k), lambda j, i: (0, j, 0))
        w_scratch = pltpu.VMEM((PROJ_TN, k), BF16)
    else:
        w_spec = pl.BlockSpec((None, k, PROJ_TN), lambda j, i: (0, 0, j))
        w_scratch = pltpu.VMEM((k, PROJ_TN), BF16)
    in_specs = [pl.BlockSpec((tm, k), lambda j, i: (i, 0)), w_spec]
    args = [x, w]
    if rope is not None:
        in_specs += [pl.BlockSpec((tm, RET_DK), lambda j, i: (i, 0))] * 2
        args += list(rope)
    return pl.pallas_call(
        functools.partial(_in_proj_kernel, modes=tuple(modes), has_rope=rope is not None,
                          w_is_transposed=w_is_transposed),
        grid=(nt, m // tm),
        in_specs=in_specs,
        out_specs=pl.BlockSpec((tm, PROJ_TN), lambda j, i: (i, j)),
        out_shape=jax.ShapeDtypeStruct((m, nt * PROJ_TN), BF16),
        scratch_shapes=[w_scratch],
        compiler_params=_cparams(("arbitrary", "arbitrary")),
        name="in_proj",
    )(*args)


def _ret_log_decay(h):
    return math.log1p(-(2.0 ** (-5.0 - h)))


def _retention_kernel(q_ref, k_ref, v_ref, z_ref, nw_ref, o_ref, s_ref, dec_ref, qs_ref, ks_ref, *, nc):
    c = pl.program_id(0)

    @pl.when(c == 0)
    def _():
        ii = lax.broadcasted_iota(jnp.int32, (CHUNK, CHUNK), 0)
        jj = lax.broadcasted_iota(jnp.int32, (CHUNK, CHUNK), 1)
        diff = (ii - jj).astype(F32)
        row = ii.astype(F32)
        for h in range(RET_HEADS):
            lg = _ret_log_decay(h)
            dec_ref[h] = jnp.where(ii >= jj, jnp.exp(lg * jnp.maximum(diff, 0.0)), 0.0)
            qs_ref[h] = jnp.exp(lg * (row + 1.0))
            ks_ref[h] = jnp.exp(lg * (CHUNK - 1.0 - row))

    @pl.when(c % nc == 0)
    def _():
        s_ref[...] = jnp.zeros_like(s_ref)

    heads = range(RET_HEADS)
    vcs = [slice(h * RET_DV, (h + 1) * RET_DV) for h in heads]
    qs = [q_ref[:, h * RET_DK:(h + 1) * RET_DK] for h in heads]
    ks = [k_ref[:, h * RET_DK:(h + 1) * RET_DK] for h in heads]
    vs = [v_ref[:, vc] for vc in vcs]
    nt_dims = (((1,), (1,)), ((), ()))
    tn_dims = (((0,), (0,)), ((), ()))
    scores = [lax.dot_general(qs[h], ks[h], nt_dims, preferred_element_type=F32) for h in heads]
    kvs = [lax.dot_general((ks[h].astype(F32) * ks_ref[h]).astype(BF16), vs[h], tn_dims,
                           preferred_element_type=F32) for h in heads]
    states = [s_ref[h] for h in heads]
    lhs = [jnp.concatenate([(scores[h] * dec_ref[h]).astype(BF16),
                            (qs[h].astype(F32) * qs_ref[h]).astype(BF16)], axis=1) for h in heads]
    outs = [jnp.dot(lhs[h], jnp.concatenate([vs[h], states[h].astype(BF16)], axis=0),
                    preferred_element_type=F32) for h in heads]
    for h in heads:
        s_ref[h] = states[h] * math.exp(_ret_log_decay(h) * CHUNK) + kvs[h]
    for h in heads:
        o_ref[:, vcs[h]] = (_rms_scale(outs[h], nw_ref[:, vcs[h]])
                            * z_ref[:, vcs[h]].astype(F32)).astype(o_ref.dtype)


def _retention(proj, norm_w, nc):
    m = proj.shape[0]
    tab = pltpu.VMEM((RET_HEADS, CHUNK, CHUNK), F32)
    return pl.pallas_call(
        functools.partial(_retention_kernel, nc=nc),
        grid=(m // CHUNK,),
        in_specs=[pl.BlockSpec((CHUNK, RET_QK), lambda c: (c, 0)),
                  pl.BlockSpec((CHUNK, RET_QK), lambda c: (c, 1)),
                  pl.BlockSpec((CHUNK, RET_W), lambda c: (c, 1)),
                  pl.BlockSpec((CHUNK, RET_W), lambda c: (c, 2)),
                  pl.BlockSpec((1, RET_W), lambda c: (0, 0))],
        out_specs=pl.BlockSpec((CHUNK, RET_W), lambda c: (c, 0)),
        out_shape=jax.ShapeDtypeStruct((m, RET_W), BF16),
        scratch_shapes=[pltpu.VMEM((RET_HEADS, RET_DK, RET_DV), F32), tab, tab, tab],
        compiler_params=_cparams(("arbitrary",)),
        name="retention",
    )(proj, proj, proj, proj, norm_w.reshape(1, RET_W))


def _s5_prep_kernel(lre_ref, lim_ref, ldt_ref, bre_ref, bim_ref, pre_ref, pim_ref, bbre_ref, bbim_ref):
    lre = lre_ref[...]
    lim = lim_ref[...]
    dt = jnp.exp(ldt_ref[...])
    tseg = pre_ref.shape[0]
    n = lax.broadcasted_iota(jnp.int32, (tseg, 1), 0).astype(F32) + 1.0
    mag = jnp.exp(n * (lre * dt))
    ang = n * (lim * dt)
    p_re = mag * jnp.cos(ang)
    p_im = mag * jnp.sin(ang)
    pre_ref[...] = p_re
    pim_ref[...] = p_im
    a_re = p_re[0:1]
    a_im = p_im[0:1]
    den = lre * lre + lim * lim
    nr = a_re - 1.0
    f_re = (nr * lre + a_im * lim) / den
    f_im = (a_im * lre - nr * lim) / den
    b_re = bre_ref[...]
    b_im = bim_ref[...]
    bbre_ref[...] = f_re * b_re - f_im * b_im
    bbim_ref[...] = f_re * b_im + f_im * b_re


def _s5_prep(lam_re, lam_im, log_dt, b_re, b_im, tseg):
    flat = lambda a: a.reshape(1, S5_F)
    ldt = jnp.broadcast_to(log_dt[:, None], (S5_G, S5_P))
    bt = lambda a: a.reshape(S5_F, S5_GH).T
    out = jax.ShapeDtypeStruct
    return pl.pallas_call(
        _s5_prep_kernel,
        out_shape=(out((tseg, S5_F), F32), out((tseg, S5_F), F32),
                   out((S5_GH, S5_F), F32), out((S5_GH, S5_F), F32)),
        compiler_params=pltpu.CompilerParams(vmem_limit_bytes=VMEM_LIMIT),
        name="s5_prep",
    )(flat(lam_re), flat(lam_im), flat(ldt), bt(b_re), bt(b_im))


def _gelu_tanh(x):
    return 0.5 * x * (1.0 + jnp.tanh(math.sqrt(2.0 / math.pi) * (x + 0.044715 * (x * x * x))))


def _s5_kernel(u_ref, z_ref, wb_ref, wcre_ref, wcim_ref, pre_ref, pim_ref, d_ref, wglu_ref,
               o_ref, bure_ref, buim_ref, cre_ref, cim_ref, xinre_ref, xinim_ref, *, tseg, slab_group):
    ci = pl.program_id(1)

    @pl.when(ci == 0)
    def _():
        cre_ref[...] = jnp.zeros_like(cre_ref)
        cim_ref[...] = jnp.zeros_like(cim_ref)

    u_bf = u_ref[...]
    spb = S5_SLABS // S5_JB
    for j in range(S5_JB):
        res = jnp.dot(u_bf[:, j * 256:(j + 1) * 256], wb_ref[j], preferred_element_type=F32)
        for s in range(spb):
            bure_ref[j * spb + s] = res[:, s * 128:(s + 1) * 128]
            buim_ref[j * spb + s] = res[:, (spb + s) * 128:(spb + s + 1) * 128]

    for g0 in range(0, S5_SLABS, slab_group):
        slabs = list(range(g0, g0 + slab_group))
        a_re = [jnp.broadcast_to(pre_ref[sl, 0:1, :], (S5_SEGS, 128)) for sl in slabs]
        a_im = [jnp.broadcast_to(pim_ref[sl, 0:1, :], (S5_SEGS, 128)) for sl in slabs]

        def scan_body(tau, carry, slabs=slabs, a_re=a_re, a_im=a_im):
            xr, xi = carry
            nr, ni = [], []
            for n, sl in enumerate(slabs):
                rows = pl.ds(tau, S5_SEGS, stride=tseg)
                br = bure_ref[sl, rows, :]
                bi = buim_ref[sl, rows, :]
                r = a_re[n] * xr[n] - a_im[n] * xi[n] + br
                i = a_re[n] * xi[n] + a_im[n] * xr[n] + bi
                bure_ref[sl, rows, :] = r
                buim_ref[sl, rows, :] = i
                nr.append(r)
                ni.append(i)
            return tuple(nr), tuple(ni)

        zero = tuple(jnp.zeros((S5_SEGS, 128), F32) for _ in slabs)
        end_re, end_im = lax.fori_loop(0, tseg, scan_body, (zero, zero))

        for n, sl in enumerate(slabs):
            at_re = pre_ref[sl, tseg - 1:tseg, :]
            at_im = pim_ref[sl, tseg - 1:tseg, :]
            xr = cre_ref[sl, 0:1, :]
            xi = cim_ref[sl, 0:1, :]
            rows_re, rows_im = [], []
            for s in range(S5_SEGS):
                rows_re.append(xr)
                rows_im.append(xi)
                er = end_re[n][s:s + 1]
                ei = end_im[n][s:s + 1]
                xr, xi = er + at_re * xr - at_im * xi, ei + at_re * xi + at_im * xr
            cre_ref[sl] = jnp.broadcast_to(xr, (S5_SEGS, 128))
            cim_ref[sl] = jnp.broadcast_to(xi, (S5_SEGS, 128))
            xinre_ref[sl] = jnp.concatenate(rows_re, axis=0)
            xinim_ref[sl] = jnp.concatenate(rows_im, axis=0)

    def fix_body(sl, carry):
        xin_re = xinre_ref[sl]
        xin_im = xinim_ref[sl]
        for tau in range(tseg):
            rows = pl.ds(tau, S5_SEGS, stride=tseg)
            pr = pre_ref[sl, tau:tau + 1, :]
            pi = pim_ref[sl, tau:tau + 1, :]
            bure_ref[sl, rows, :] = bure_ref[sl, rows, :] + (pr * xin_re - pi * xin_im)
            buim_ref[sl, rows, :] = buim_ref[sl, rows, :] + (pr * xin_im + pi * xin_re)
        return carry

    lax.fori_loop(0, S5_SLABS, fix_body, 0)

    ys = []
    for j in range(S5_JB):
        x_re = jnp.concatenate([bure_ref[j * spb + s] for s in range(spb)], axis=1).astype(BF16)
        x_im = jnp.concatenate([buim_ref[j * spb + s] for s in range(spb)], axis=1).astype(BF16)
        ys.append(jnp.dot(x_re, wcre_ref[j], preferred_element_type=F32)
                  - jnp.dot(x_im, wcim_ref[j], preferred_element_type=F32))
    y = jnp.concatenate(ys, axis=1) + d_ref[...] * u_bf.astype(F32)
    y = _gelu_tanh(y)
    gate = jnp.dot(y.astype(BF16), wglu_ref[...], preferred_element_type=F32)
    y = y * (1.0 / (1.0 + jnp.exp(-gate)))
    o_ref[...] = (y * z_ref[...].astype(F32)).astype(o_ref.dtype)


def _s5_time_tile(rows_per_batch):
    for tseg in (66, 44, 132, 12, 4, 6, 2):
        t = S5_SEGS * tseg
        if rows_per_batch % t == 0 and t % 16 == 0:
            return t, tseg
    raise ValueError(f"no S5 time tile for {rows_per_batch}")


def _s5(proj, bsz, wb, wcre, wcim, p_re, p_im, d, wglu, t, tseg):
    m = proj.shape[0]
    nt = m // bsz // t
    ucol = (2 * RET_QK + 2 * RET_W) // S5_W
    const = lambda *shape: pl.BlockSpec(shape, lambda b, i: (0,) * len(shape))
    slabs = lambda p: p.reshape(tseg, S5_SLABS, 128).transpose(1, 0, 2)
    return pl.pallas_call(
        functools.partial(_s5_kernel, tseg=tseg, slab_group=8),
        grid=(bsz, nt),
        in_specs=[pl.BlockSpec((t, S5_W), lambda b, i: (b * nt + i, ucol)),
                  pl.BlockSpec((t, S5_W), lambda b, i: (b * nt + i, ucol + 1)),
                  const(S5_JB, 256, 2048), const(S5_JB, 1024, 256), const(S5_JB, 1024, 256),
                  const(S5_SLABS, tseg, 128), const(S5_SLABS, tseg, 128), const(1, S5_W),
                  const(S5_W, S5_W)],
        out_specs=pl.BlockSpec((t, S5_W), lambda b, i: (b * nt + i, 0)),
        out_shape=jax.ShapeDtypeStruct((m, S5_W), BF16),
        scratch_shapes=[pltpu.VMEM((S5_SLABS, t, 128), F32), pltpu.VMEM((S5_SLABS, t, 128), F32)]
        + [pltpu.VMEM((S5_SLABS, S5_SEGS, 128), F32)] * 4,
        compiler_params=_cparams(("arbitrary", "arbitrary")),
        name="s5",
    )(proj, proj, wb, wcre, wcim, slabs(p_re), slabs(p_im), d.reshape(1, S5_W), wglu)


def _s5_block_weights(bb_re, bb_im, c_re, c_im):
    eye = jnp.eye(16, dtype=F32)

    def wb_part(bb):
        a = bb.reshape(S5_GH, S5_JB, 16, S5_P)
        w = jnp.einsum('hjgp,gk->jghkp', a, eye)
        return w.reshape(S5_JB, 256, 1024)

    wb = jnp.concatenate([wb_part(bb_re), wb_part(bb_im)], axis=2).astype(BF16)

    def wc_part(c):
        a = c.astype(F32).reshape(S5_JB, 16, S5_GH, S5_P)
        w = jnp.einsum('jghp,gk->jgpkh', a, eye)
        return w.reshape(S5_JB, 1024, 256).astype(BF16)

    return wb, wc_part(c_re), wc_part(c_im)


def _out_ab_kernel(lead_ref, xa_ref, xb_ref, xc_ref, oa_ref, ob_ref, w_ref, nw_ref, wg_ref,
                   h1_ref, xn_ref, gl_ref):
    t = pl.program_id(1)
    first = jnp.where(t == 0, lead_ref[...], xa_ref[...])
    h0 = (first, xb_ref[...], xc_ref[...])
    sls = [slice(r * CHUNK, (r + 1) * CHUNK) for r in range(ROW_CHUNKS)]
    accs = [jnp.dot(oa_ref[sl, :], w_ref[:RET_W, :], preferred_element_type=F32)
            + jnp.dot(ob_ref[sl, :], w_ref[RET_W:, :], preferred_element_type=F32) for sl in sls]
    h1s = [h0[r] + accs[r] for r in range(ROW_CHUNKS)]
    xns = [_rms_scale(h1, nw_ref[...]).astype(BF16) for h1 in h1s]
    gls = [jnp.dot(xn, wg_ref[...], preferred_element_type=F32) for xn in xns]
    for r, sl in enumerate(sls):
        h1_ref[sl, :] = h1s[r]
        xn_ref[sl, :] = xns[r]
        gl_ref[sl, :] = gls[r]


def _out_ab(x, lead, oa, ob, w, norm_w, wg, nt):
    bsz, _, d = x.shape
    tile = ROW_CHUNKS * CHUNK
    m = bsz * nt * tile
    row = lambda width: pl.BlockSpec((tile, width), lambda b, t: (b * nt + t, 0))
    const = lambda a: pl.BlockSpec(a.shape, lambda b, t: (0, 0))
    out = jax.ShapeDtypeStruct
    return pl.pallas_call(
        _out_ab_kernel,
        grid=(bsz, nt),
        in_specs=[pl.BlockSpec((CHUNK, d), lambda b, t: (0, 0))] + _chunk_specs(d, nt, 1)
        + [row(RET_W), row(S5_W), const(w), pl.BlockSpec((1, d), lambda b, t: (0, 0)), const(wg)],
        out_specs=[row(d), row(d), row(128)],
        out_shape=(out((m, d), F32), out((m, d), BF16), out((m, 128), F32)),
        compiler_params=_cparams(("parallel", "parallel")),
        name="out_proj_ab",
    )(lead, x, x, x, oa, ob, w, norm_w.reshape(1, d), wg)


LOG2E = 1.4426950408889634


def _split2(x):
    x1 = x.astype(BF16)
    return x1, (x - x1.astype(F32)).astype(BF16)


def _neg_abs(x):
    return -jnp.abs(x)


def _block_ref_rows(b, level):
    rows, width = b.shape
    half = 1 << level
    blk = 2 * half
    if blk >= 8:
        b3 = b.reshape(rows // blk, blk, width)
        ref = jnp.broadcast_to(b3[:, half - 1:half, :], b3.shape)
        return ref.reshape(rows, width)
    b3 = b.reshape(rows // 8, 8, width)
    sub = lax.broadcasted_iota(jnp.int32, b3.shape, 1)
    pick = lambda r: jnp.broadcast_to(b3[:, r:r + 1, :], b3.shape)
    if blk == 4:
        ref = jnp.where(sub < 4, pick(1), pick(5))
    else:
        ref = jnp.where(sub < 2, pick(0), jnp.where(sub < 4, pick(2), jnp.where(sub < 6, pick(4), pick(6))))
    return ref.reshape(rows, width)


def _gla_kernel(q_ref, k_ref, v_ref, z_ref, gl_ref, wg_ref, bg_ref, nw_ref, h_ref, wo_ref, fw_ref,
                y_ref, st_ref, mix_ref, *, nc, nchunks):
    c = pl.program_id(0)
    n = jnp.minimum(c, nchunks - 1) % nc

    @pl.when(c == 0)
    def _():
        mix_ref[...] = jnp.zeros_like(mix_ref)

    @pl.when(n == 0)
    def _():
        st_ref[...] = jnp.zeros_like(st_ref)


    ridx = lax.broadcasted_iota(jnp.int32, (CHUNK, 1), 0)
    ii = lax.broadcasted_iota(jnp.int32, (CHUNK, CHUNK), 0)
    jj = lax.broadcasted_iota(jnp.int32, (CHUNK, CHUNK), 1)
    pair_code = jnp.where(ii > jj, ii ^ jj, 0)
    eye = ii == jj
    tri = (ii >= jj).astype(BF16)

    gl1, gl2 = _split2(gl_ref[:, :GLA_RANK])
    wg1, wg2 = _split2(wg_ref[...])
    x = jnp.dot(jnp.concatenate([gl1, gl2, gl1], axis=1), jnp.concatenate([wg1, wg1, wg2], axis=0),
                preferred_element_type=F32) + bg_ref[...]
    log_a = (jnp.minimum(x, 0.0) - jnp.log(1.0 + jnp.exp(-jnp.abs(x)))) * (LOG2E / GLA_TAU)
    log_a = jnp.where(ridx >= jnp.where(n > 0, 0, PAD), log_a, 0.0)
    g1, g2 = _split2(log_a)
    b_all = jnp.dot(jnp.concatenate([tri, tri], axis=1), jnp.concatenate([g1, g2], axis=0),
                    preferred_element_type=F32)
    odd = (ridx & 1) == 1
    h2 = h_ref[...]
    for h0 in range(0, GLA_HEADS, 2):
        heads = (h0, h0 + 1)
        qs, ks, bs, scores = [], [], [], []
        for h in heads:
            kc = slice(h * GLA_DK, (h + 1) * GLA_DK)
            vc = slice(h * GLA_DV, (h + 1) * GLA_DV)
            h2 = h2 + jnp.dot(mix_ref[:, vc], wo_ref[vc, :], preferred_element_type=F32)
            qs.append(q_ref[:, kc].astype(F32))
            ks.append(k_ref[:, kc].astype(F32))
            bs.append(b_all[:, kc])
            scores.append(jnp.where(eye, jnp.sum(qs[-1] * ks[-1], axis=1, keepdims=True), 0.0))
        for lv in range(GLA_LEVELS):
            half = 1 << lv
            zz = []
            for q, k, b in zip(qs, ks, bs):
                if lv == 0:
                    expo = jnp.where(odd, b - pltpu.roll(b, 1, 0), 0.0)
                else:
                    expo = _neg_abs(b - _block_ref_rows(b, lv))
                if half >= 8:
                    sel = jnp.concatenate([(q if r % 2 else k)[r * half:(r + 1) * half]
                                           for r in range(CHUNK // half)], axis=0)
                else:
                    sel = jnp.where(((ridx >> lv) & 1) == 1, q, k)
                zz.append((sel * jnp.exp2(expo)).astype(BF16))
            z2 = jnp.concatenate(zz, axis=0)
            gram = lax.dot_general(z2, z2, (((1,), (1,)), ((), ())), preferred_element_type=F32)
            mask = (pair_code >> lv) == 1
            for i in range(2):
                blk = gram[i * CHUNK:(i + 1) * CHUNK, i * CHUNK:(i + 1) * CHUNK]
                scores[i] = scores[i] + jnp.where(mask, blk, 0.0)
        for i, h in enumerate(heads):
            vc = slice(h * GLA_DV, (h + 1) * GLA_DV)
            q, k, b = qs[i], ks[i], bs[i]
            v = v_ref[:, vc]
            b_last = b[CHUNK - 1:CHUNK, :]
            o = jnp.dot(scores[i].astype(BF16), v, preferred_element_type=F32)
            st = st_ref[h]
            qe = (q * jnp.exp2(b)).astype(BF16)
            o = o + lax.dot_general(qe, st.astype(BF16), (((1,), (1,)), ((), ())), preferred_element_type=F32)
            ke = (k * jnp.exp2(b_last - b)).astype(BF16)
            kv_t = lax.dot_general(v, ke, (((0,), (0,)), ((), ())), preferred_element_type=F32)
            st_ref[h] = st * jnp.exp2(b_last) + kv_t
            mix_ref[:, vc] = (_rms_scale(o, nw_ref[:, vc]) * z_ref[:, vc].astype(F32)).astype(BF16)

    y_ref[...] = _rms_scale(h2, fw_ref[...])


def _gla(proj, glow, w_gate, b_gate, norm_w, h1, w_out, final_w, bsz, nc):
    m, d = h1.shape
    nchunks = m // CHUNK
    const = lambda *shape: pl.BlockSpec(shape, lambda c: (0,) * len(shape))
    cur = lambda width, col: pl.BlockSpec((CHUNK, width), lambda c: (jnp.minimum(c, nchunks - 1), col))
    prev = lambda c: jnp.maximum(c - 1, 0)
    return pl.pallas_call(
        functools.partial(_gla_kernel, nc=nc, nchunks=nchunks),
        grid=(nchunks + 1,),
        in_specs=[cur(GLA_QK, 0), cur(GLA_QK, 1), cur(GLA_W, 1), cur(GLA_W, 2), cur(128, 0),
                  const(GLA_RANK, GLA_QK), const(1, GLA_QK), const(1, GLA_W),
                  pl.BlockSpec((CHUNK, d), lambda c: (prev(c), 0)),
                  const(GLA_W, d), const(1, d)],
        out_specs=pl.BlockSpec((None, CHUNK, d),
                               lambda c: (prev(c) // nc, jnp.maximum(prev(c) % nc - 1, 0), 0)),
        out_shape=jax.ShapeDtypeStruct((bsz, (nc - 1) * CHUNK, d), F32),
        scratch_shapes=[pltpu.VMEM((GLA_HEADS, GLA_DV, GLA_DK), F32), pltpu.VMEM((CHUNK, GLA_W), BF16)],
        compiler_params=_cparams(("arbitrary",)),
        name="gla_out",
    )(proj, proj, proj, proj, glow, w_gate.astype(F32), b_gate.reshape(1, GLA_QK).astype(F32),
      norm_w.reshape(1, GLA_W).astype(F32), h1, w_out, final_w.reshape(1, d).astype(F32))


def _rope_tables(bsz, rows_per_batch):
    pos = jnp.maximum(jnp.arange(rows_per_batch, dtype=F32) - PAD, 0.0)
    inv_freq = jnp.power(ROPE_BASE, -jnp.arange(0, RET_DK, 2, dtype=F32) / RET_DK)
    ang = pos[:, None] * inv_freq[None, :]
    cos, sin = jnp.cos(ang), jnp.sin(ang)
    cos2 = jnp.concatenate([cos, cos], axis=1)
    sin2 = jnp.concatenate([-sin, sin], axis=1)
    return jnp.tile(cos2, (bsz, 1)), jnp.tile(sin2, (bsz, 1))


def kernel(x, meta, norm_ab_w, w_in_ab, ret_norm_w, s5_lam_re, s5_lam_im, s5_log_dt, s5_b_re, s5_b_im,
           s5_c_re, s5_c_im, s5_d, s5_w_glu, w_out_ab, norm_c_w, w_in_c, gla_w_gate, gla_b_gate,
           gla_norm_w, w_out_c, final_norm_w):
    bsz, seq, d = x.shape
    assert seq % CHUNK == 0 and w_in_ab.shape[0] == 1 and w_in_c.shape[0] == 1
    rpb = seq + CHUNK
    nc = rpb // CHUNK
    assert nc % ROW_CHUNKS == 0
    nt = nc // ROW_CHUNKS
    m = bsz * rpb
    tm = _pick_tile(m, (1408, 768, 384, 256, 128))
    lead = jnp.concatenate([jnp.zeros((PAD, d), x.dtype), meta.astype(x.dtype)], axis=0)

    xn0 = _embed_norm(x, lead, norm_ab_w[0], nt)
    plain, silu = ("plain", 1.0), ("silu", 1.0)
    modes_ab = [("rope", 1.0), ("rope", RET_DK ** -0.5), plain, plain, silu, silu, plain, silu]
    proj0 = _in_proj(xn0, w_in_ab, modes_ab, tm, rope=_rope_tables(bsz, rpb))
    o_a = _retention(proj0, ret_norm_w[0], nc)
    t, tseg = _s5_time_tile(rpb)
    p_re, p_im, bb_re, bb_im = _s5_prep(s5_lam_re[0], s5_lam_im[0], s5_log_dt[0], s5_b_re[0], s5_b_im[0], tseg)
    wb, wcre, wcim = _s5_block_weights(bb_re, bb_im, s5_c_re[0], s5_c_im[0])
    o_b = _s5(proj0, bsz, wb, wcre, wcim, p_re, p_im, s5_d[0], s5_w_glu[0].astype(BF16), t, tseg)

    w_in_c_t = jnp.swapaxes(w_in_c, 1, 2)
    n_main = 2 * GLA_QK + 2 * GLA_W
    wg = jnp.pad(w_in_c_t[0, n_main:, :], ((0, 128 - GLA_RANK), (0, 0))).astype(BF16).T
    h1, xn1, glow = _out_ab(x, lead, o_a, o_b, w_out_ab[0].astype(BF16), norm_c_w[0], wg, nt)

    modes_c = [("plain", GLA_DK ** -0.5), plain, plain, plain, silu, silu]
    proj1 = _in_proj(xn1, w_in_c_t, modes_c, tm, w_is_transposed=True)
    return _gla(proj1, glow, gla_w_gate[0], gla_b_gate[0], gla_norm_w[0], h1, w_out_c[0].astype(BF16),
                final_norm_w, bsz, nc)
```

```python
import functools
import math

import numpy as np
import jax
import jax.numpy as jnp
from jax import lax
from jax.experimental import pallas as pl
from jax.experimental.pallas import tpu as pltpu

F32 = jnp.float32
BF16 = jnp.bfloat16

N_META = 16
CHUNK = 128
PAD = CHUNK - N_META
EPS = 1e-6
ROW_CHUNKS = 3

RET_HEADS = 8
RET_DK = 128
RET_DV = 256
RET_QK = RET_HEADS * RET_DK
RET_W = RET_HEADS * RET_DV
ROPE_BASE = 10000.0

S5_W = 1024
S5_GH = 16
S5_G = 64
S5_P = 64
S5_F = S5_G * S5_P
S5_SLABS = S5_F // 128
S5_JB = 4
S5_SEGS = 8

GLA_HEADS = 4
GLA_DK = 256
GLA_DV = 512
GLA_QK = GLA_HEADS * GLA_DK
GLA_W = GLA_HEADS * GLA_DV
GLA_RANK = 16
GLA_TAU = 16.0
GLA_LEVELS = 7

PROJ_TN = 1024
PROJ_ROW_SPLIT = 4
VMEM_LIMIT = 56 * 1024 * 1024


def _cparams(sem):
    return pltpu.CompilerParams(dimension_semantics=sem, vmem_limit_bytes=VMEM_LIMIT)


def _silu(x):
    return x * (1.0 / (1.0 + jnp.exp(-x)))


def _pick_tile(n, candidates):
    for c in candidates:
        if n % c == 0:
            return c
    raise ValueError(f"no tile for {n}")


def _rms_scale(x, w):
    return x * lax.rsqrt(jnp.mean(x * x, axis=-1, keepdims=True) + EPS) * w


def _chunk_specs(d, nt, n_before):
    del nt
    return [pl.BlockSpec((None, CHUNK, d),
                         lambda b, t, k=k: (b, jnp.maximum(ROW_CHUNKS * t + k - n_before, 0), 0))
            for k in range(ROW_CHUNKS)]


def _embed_norm_kernel(lead_ref, xa_ref, xb_ref, xc_ref, w_ref, o_ref):
    t = pl.program_id(1)
    first = jnp.where(t == 0, lead_ref[...], xa_ref[...])
    for r, rows in enumerate((first, xb_ref[...], xc_ref[...])):
        o_ref[r * CHUNK:(r + 1) * CHUNK, :] = _rms_scale(rows, w_ref[...]).astype(o_ref.dtype)


def _embed_norm(x, lead, w, nt):
    bsz, _, d = x.shape
    tile = ROW_CHUNKS * CHUNK
    return pl.pallas_call(
        _embed_norm_kernel,
        grid=(bsz, nt),
        in_specs=[pl.BlockSpec((CHUNK, d), lambda b, t: (0, 0))] + _chunk_specs(d, nt, 1)
        + [pl.BlockSpec((1, d), lambda b, t: (0, 0))],
        out_specs=pl.BlockSpec((tile, d), lambda b, t: (b * nt + t, 0)),
        out_shape=jax.ShapeDtypeStruct((bsz * nt * tile, d), BF16),
        compiler_params=_cparams(("parallel", "parallel")),
        name="embed_norm",
    )(lead, x, x, x, w.reshape(1, d))


def _rope_rows(a, cos, sin, scale):
    outs = []
    for h in range(a.shape[1] // RET_DK):
        blk = a[:, h * RET_DK:(h + 1) * RET_DK]
        r = blk * cos + pltpu.roll(blk, RET_DK // 2, 1) * sin
        outs.append(r * scale if scale != 1.0 else r)
    return jnp.concatenate(outs, axis=1)


def _in_proj_kernel(*refs, modes, has_rope, w_is_transposed):
    if has_rope:
        x_ref, w_ref, cos_ref, sin_ref, o_ref, wbf_ref = refs
    else:
        x_ref, w_ref, o_ref, wbf_ref = refs
    j = pl.program_id(0)

    @pl.when(pl.program_id(1) == 0)
    def _():
        wbf_ref[...] = w_ref[...].astype(BF16)

    contract = (((1,), (1 if w_is_transposed else 0,)), ((), ()))
    for mode in sorted(set(modes)):
        cond = functools.reduce(jnp.logical_or, [j == jj for jj, mm in enumerate(modes) if mm == mode])

        @pl.when(cond)
        def _(mode=mode):
            kind, scale = mode
            rows = x_ref.shape[0] // PROJ_ROW_SPLIT
            sls = [slice(r * rows, (r + 1) * rows) for r in range(PROJ_ROW_SPLIT)]
            accs = [lax.dot_general(x_ref[sl, :], wbf_ref[...], contract, preferred_element_type=F32)
                    for sl in sls]
            for sl, acc in zip(sls, accs):
                if kind == "rope":
                    acc = _rope_rows(acc, cos_ref[sl, :], sin_ref[sl, :], scale)
                elif kind == "silu":
                    acc = _silu(acc)
                elif scale != 1.0:
                    acc = acc * scale
                o_ref[sl, :] = acc.astype(o_ref.dtype)


def _in_proj(x, w, modes, tm, rope=None, w_is_transposed=False):
    m, k = x.shape
    nt = len(modes)
    if w_is_transposed:
        w_spec = pl.BlockSpec((None, PROJ_TN, k), lambda j, i: (0, j, 0))
        w_scratch = pltpu.VMEM((PROJ_TN, k), BF16)
    else:
        w_spec = pl.BlockSpec((None, k, PROJ_TN), lambda j, i: (0, 0, j))
        w_scratch = pltpu.VMEM((k, PROJ_TN), BF16)
    in_specs = [pl.BlockSpec((tm, k), lambda j, i: (i, 0)), w_spec]
    args = [x, w]
    if rope is not None:
        in_specs += [pl.BlockSpec((tm, RET_DK), lambda j, i: (i, 0))] * 2
        args += list(rope)
    return pl.pallas_call(
        functools.partial(_in_proj_kernel, modes=tuple(modes), has_rope=rope is not None,
                          w_is_transposed=w_is_transposed),
        grid=(nt, m // tm),
        in_specs=in_specs,
        out_specs=pl.BlockSpec((tm, PROJ_TN), lambda j, i: (i, j)),
        out_shape=jax.ShapeDtypeStruct((m, nt * PROJ_TN), BF16),
        scratch_shapes=[w_scratch],
        compiler_params=_cparams(("arbitrary", "arbitrary")),
        name="in_proj",
    )(*args)


def _ret_log_decay(h):
    return math.log1p(-(2.0 ** (-5.0 - h)))


def _retention_kernel(q_ref, k_ref, v_ref, z_ref, nw_ref, o_ref, s_ref, dec_ref, qs_ref, ks_ref):
    first_step = (pl.program_id(0) == 0) & (pl.program_id(1) == 0)

    @pl.when(first_step)
    def _():
        ii = lax.broadcasted_iota(jnp.int32, (CHUNK, CHUNK), 0)
        jj = lax.broadcasted_iota(jnp.int32, (CHUNK, CHUNK), 1)
        diff = (ii - jj).astype(F32)
        row = ii.astype(F32)
        for h in range(RET_HEADS):
            lg = _ret_log_decay(h)
            dec_ref[h] = jnp.where(ii >= jj, jnp.exp(lg * jnp.maximum(diff, 0.0)), 0.0)
            qs_ref[h] = jnp.exp(lg * (row + 1.0))
            ks_ref[h] = jnp.exp(lg * (CHUNK - 1.0 - row))

    @pl.when(pl.program_id(1) == 0)
    def _():
        s_ref[...] = jnp.zeros_like(s_ref)

    heads = range(RET_HEADS)
    vcs = [slice(h * RET_DV, (h + 1) * RET_DV) for h in heads]
    nt_dims = (((1,), (1,)), ((), ()))
    tn_dims = (((0,), (0,)), ((), ()))
    states = [s_ref[h] for h in heads]
    for r in range(q_ref.shape[0] // CHUNK):
        rows = slice(r * CHUNK, (r + 1) * CHUNK)
        qs = [q_ref[rows, h * RET_DK:(h + 1) * RET_DK] for h in heads]
        ks = [k_ref[rows, h * RET_DK:(h + 1) * RET_DK] for h in heads]
        vs = [v_ref[rows, vc] for vc in vcs]
        scores = [lax.dot_general(qs[h], ks[h], nt_dims, preferred_element_type=F32) for h in heads]
        kvs = [lax.dot_general((ks[h].astype(F32) * ks_ref[h]).astype(BF16), vs[h], tn_dims,
                               preferred_element_type=F32) for h in heads]
        lhs = [jnp.concatenate([(scores[h] * dec_ref[h]).astype(BF16),
                                (qs[h].astype(F32) * qs_ref[h]).astype(BF16)], axis=1) for h in heads]
        outs = [jnp.dot(lhs[h], jnp.concatenate([vs[h], states[h].astype(BF16)], axis=0),
                        preferred_element_type=F32) for h in heads]
        states = [states[h] * math.exp(_ret_log_decay(h) * CHUNK) + kvs[h] for h in heads]
        for h in heads:
            o_ref[rows, vcs[h]] = (_rms_scale(outs[h], nw_ref[:, vcs[h]])
                                   * z_ref[rows, vcs[h]].astype(F32)).astype(o_ref.dtype)
    for h in heads:
        s_ref[h] = states[h]


def _retention(proj, norm_w, bsz, nt):
    m = proj.shape[0]
    tile = ROW_CHUNKS * CHUNK
    tab = pltpu.VMEM((RET_HEADS, CHUNK, CHUNK), F32)
    row = lambda width, col: pl.BlockSpec((tile, width), lambda b, t: (b * nt + t, col))
    return pl.pallas_call(
        _retention_kernel,
        grid=(bsz, nt),
        in_specs=[row(RET_QK, 0), row(RET_QK, 1), row(RET_W, 1), row(RET_W, 2),
                  pl.BlockSpec((1, RET_W), lambda b, t: (0, 0))],
        out_specs=row(RET_W, 0),
        out_shape=jax.ShapeDtypeStruct((m, RET_W), BF16),
        scratch_shapes=[pltpu.VMEM((RET_HEADS, RET_DK, RET_DV), F32), tab, tab, tab],
        compiler_params=_cparams(("arbitrary", "arbitrary")),
        name="retention",
    )(proj, proj, proj, proj, norm_w.reshape(1, RET_W))


def _s5_prep_kernel(lre_ref, lim_ref, ldt_ref, bre_ref, bim_ref, pre_ref, pim_ref, bbre_ref, bbim_ref):
    lre = lre_ref[...]
    lim = lim_ref[...]
    dt = jnp.exp(ldt_ref[...])
    tseg = pre_ref.shape[0]
    n = lax.broadcasted_iota(jnp.int32, (tseg, 1), 0).astype(F32) + 1.0
    mag = jnp.exp(n * (lre * dt))
    ang = n * (lim * dt)
    p_re = mag * jnp.cos(ang)
    p_im = mag * jnp.sin(ang)
    pre_ref[...] = p_re
    pim_ref[...] = p_im
    a_re = p_re[0:1]
    a_im = p_im[0:1]
    den = lre * lre + lim * lim
    nr = a_re - 1.0
    f_re = (nr * lre + a_im * lim) / den
    f_im = (a_im * lre - nr * lim) / den
    b_re = bre_ref[...]
    b_im = bim_ref[...]
    bbre_ref[...] = f_re * b_re - f_im * b_im
    bbim_ref[...] = f_re * b_im + f_im * b_re


def _s5_prep(lam_re, lam_im, log_dt, b_re, b_im, tseg):
    flat = lambda a: a.reshape(1, S5_F)
    ldt = jnp.broadcast_to(log_dt[:, None], (S5_G, S5_P))
    bt = lambda a: a.reshape(S5_F, S5_GH).T
    out = jax.ShapeDtypeStruct
    return pl.pallas_call(
        _s5_prep_kernel,
        out_shape=(out((tseg, S5_F), F32), out((tseg, S5_F), F32),
                   out((S5_GH, S5_F), F32), out((S5_GH, S5_F), F32)),
        compiler_params=pltpu.CompilerParams(vmem_limit_bytes=VMEM_LIMIT),
        name="s5_prep",
    )(flat(lam_re), flat(lam_im), flat(ldt), bt(b_re), bt(b_im))


def _gelu_tanh(x):
    return 0.5 * x * (1.0 + jnp.tanh(math.sqrt(2.0 / math.pi) * (x + 0.044715 * (x * x * x))))


def _s5_kernel(u_ref, z_ref, wb_ref, wcre_ref, wcim_ref, pre_ref, pim_ref, d_ref, wglu_ref,
               o_ref, bure_ref, buim_ref, cre_ref, cim_ref, xinre_ref, xinim_ref, *, tseg, slab_group):
    ci = pl.program_id(1)

    @pl.when(ci == 0)
    def _():
        cre_ref[...] = jnp.zeros_like(cre_ref)
        cim_ref[...] = jnp.zeros_like(cim_ref)

    u_bf = u_ref[...]
    spb = S5_SLABS // S5_JB
    for j in range(S5_JB):
        res = jnp.dot(u_bf[:, j * 256:(j + 1) * 256], wb_ref[j], preferred_element_type=F32)
        for s in range(spb):
            bure_ref[j * spb + s] = res[:, s * 128:(s + 1) * 128]
            buim_ref[j * spb + s] = res[:, (spb + s) * 128:(spb + s + 1) * 128]

    for g0 in range(0, S5_SLABS, slab_group):
        slabs = list(range(g0, g0 + slab_group))
        a_re = [jnp.broadcast_to(pre_ref[sl, 0:1, :], (S5_SEGS, 128)) for sl in slabs]
        a_im = [jnp.broadcast_to(pim_ref[sl, 0:1, :], (S5_SEGS, 128)) for sl in slabs]

        def scan_body(tau, carry, slabs=slabs, a_re=a_re, a_im=a_im):
            xr, xi = carry
            nr, ni = [], []
            for n, sl in enumerate(slabs):
                rows = pl.ds(tau, S5_SEGS, stride=tseg)
                br = bure_ref[sl, rows, :]
                bi = buim_ref[sl, rows, :]
                r = a_re[n] * xr[n] - a_im[n] * xi[n] + br
                i = a_re[n] * xi[n] + a_im[n] * xr[n] + bi
                bure_ref[sl, rows, :] = r
                buim_ref[sl, rows, :] = i
                nr.append(r)
                ni.append(i)
            return tuple(nr), tuple(ni)

        zero = tuple(jnp.zeros((S5_SEGS, 128), F32) for _ in slabs)
        end_re, end_im = lax.fori_loop(0, tseg, scan_body, (zero, zero))

        for n, sl in enumerate(slabs):
            at_re = pre_ref[sl, tseg - 1:tseg, :]
            at_im = pim_ref[sl, tseg - 1:tseg, :]
            xr = cre_ref[sl, 0:1, :]
            xi = cim_ref[sl, 0:1, :]
            rows_re, rows_im = [], []
            for s in range(S5_SEGS):
                rows_re.append(xr)
                rows_im.append(xi)
                er = end_re[n][s:s + 1]
                ei = end_im[n][s:s + 1]
                xr, xi = er + at_re * xr - at_im * xi, ei + at_re * xi + at_im * xr
            cre_ref[sl] = jnp.broadcast_to(xr, (S5_SEGS, 128))
            cim_ref[sl] = jnp.broadcast_to(xi, (S5_SEGS, 128))
            xinre_ref[sl] = jnp.concatenate(rows_re, axis=0)
            xinim_ref[sl] = jnp.concatenate(rows_im, axis=0)

    def fix_body(sl, carry):
        xin_re = xinre_ref[sl]
        xin_im = xinim_ref[sl]
        for tau in range(tseg):
            rows = pl.ds(tau, S5_SEGS, stride=tseg)
            pr = pre_ref[sl, tau:tau + 1, :]
            pi = pim_ref[sl, tau:tau + 1, :]
            bure_ref[sl, rows, :] = bure_ref[sl, rows, :] + (pr * xin_re - pi * xin_im)
            buim_ref[sl, rows, :] = buim_ref[sl, rows, :] + (pr * xin_im + pi * xin_re)
        return carry

    lax.fori_loop(0, S5_SLABS, fix_body, 0)

    ys = []
    for j in range(S5_JB):
        x_re = jnp.concatenate([bure_ref[j * spb + s] for s in range(spb)], axis=1).astype(BF16)
        x_im = jnp.concatenate([buim_ref[j * spb + s] for s in range(spb)], axis=1).astype(BF16)
        ys.append(jnp.dot(x_re, wcre_ref[j], preferred_element_type=F32)
                  - jnp.dot(x_im, wcim_ref[j], preferred_element_type=F32))
    y = jnp.concatenate(ys, axis=1) + d_ref[...] * u_bf.astype(F32)
    y = _gelu_tanh(y)
    gate = jnp.dot(y.astype(BF16), wglu_ref[...], preferred_element_type=F32)
    y = y * (1.0 / (1.0 + jnp.exp(-gate)))
    o_ref[...] = (y * z_ref[...].astype(F32)).astype(o_ref.dtype)


def _s5_time_tile(rows_per_batch):
    for tseg in (66, 44, 132, 12, 4, 6, 2):
        t = S5_SEGS * tseg
        if rows_per_batch % t == 0 and t % 16 == 0:
            return t, tseg
    raise ValueError(f"no S5 time tile for {rows_per_batch}")


def _s5(proj, bsz, wb, wcre, wcim, p_re, p_im, d, wglu, t, tseg):
    m = proj.shape[0]
    nt = m // bsz // t
    ucol = (2 * RET_QK + 2 * RET_W) // S5_W
    const = lambda *shape: pl.BlockSpec(shape, lambda b, i: (0,) * len(shape))
    slabs = lambda p: p.reshape(tseg, S5_SLABS, 128).transpose(1, 0, 2)
    return pl.pallas_call(
        functools.partial(_s5_kernel, tseg=tseg, slab_group=8),
        grid=(bsz, nt),
        in_specs=[pl.BlockSpec((t, S5_W), lambda b, i: (b * nt + i, ucol)),
                  pl.BlockSpec((t, S5_W), lambda b, i: (b * nt + i, ucol + 1)),
                  const(S5_JB, 256, 2048), const(S5_JB, 1024, 256), const(S5_JB, 1024, 256),
                  const(S5_SLABS, tseg, 128), const(S5_SLABS, tseg, 128), const(1, S5_W),
                  const(S5_W, S5_W)],
        out_specs=pl.BlockSpec((t, S5_W), lambda b, i: (b * nt + i, 0)),
        out_shape=jax.ShapeDtypeStruct((m, S5_W), BF16),
        scratch_shapes=[pltpu.VMEM((S5_SLABS, t, 128), F32), pltpu.VMEM((S5_SLABS, t, 128), F32)]
        + [pltpu.VMEM((S5_SLABS, S5_SEGS, 128), F32)] * 4,
        compiler_params=_cparams(("arbitrary", "arbitrary")),
        name="s5",
    )(proj, proj, wb, wcre, wcim, slabs(p_re), slabs(p_im), d.reshape(1, S5_W), wglu)


def _s5_block_weights(bb_re, bb_im, c_re, c_im):
    eye = jnp.eye(16, dtype=F32)

    def wb_part(bb):
        a = bb.reshape(S5_GH, S5_JB, 16, S5_P)
        w = jnp.einsum('hjgp,gk->jghkp', a, eye)
        return w.reshape(S5_JB, 256, 1024)

    wb = jnp.concatenate([wb_part(bb_re), wb_part(bb_im)], axis=2).astype(BF16)

    def wc_part(c):
        a = c.astype(F32).reshape(S5_JB, 16, S5_GH, S5_P)
        w = jnp.einsum('jghp,gk->jgpkh', a, eye)
        return w.reshape(S5_JB, 1024, 256).astype(BF16)

    return wb, wc_part(c_re), wc_part(c_im)


def _out_ab_kernel(lead_ref, xa_ref, xb_ref, xc_ref, oa_ref, ob_ref, w_ref, nw_ref, wg_ref,
                   h1_ref, xn_ref, gl_ref):
    t = pl.program_id(1)
    first = jnp.where(t == 0, lead_ref[...], xa_ref[...])
    h0 = (first, xb_ref[...], xc_ref[...])
    sls = [slice(r * CHUNK, (r + 1) * CHUNK) for r in range(ROW_CHUNKS)]
    accs = [jnp.dot(oa_ref[sl, :], w_ref[:RET_W, :], preferred_element_type=F32)
            + jnp.dot(ob_ref[sl, :], w_ref[RET_W:, :], preferred_element_type=F32) for sl in sls]
    h1s = [h0[r] + accs[r] for r in range(ROW_CHUNKS)]
    xns = [_rms_scale(h1, nw_ref[...]).astype(BF16) for h1 in h1s]
    gls = [jnp.dot(xn, wg_ref[...], preferred_element_type=F32) for xn in xns]
    for r, sl in enumerate(sls):
        h1_ref[sl, :] = h1s[r]
        xn_ref[sl, :] = xns[r]
        gl_ref[sl, :] = gls[r]


def _out_ab(x, lead, oa, ob, w, norm_w, wg, nt):
    bsz, _, d = x.shape
    tile = ROW_CHUNKS * CHUNK
    m = bsz * nt * tile
    row = lambda width: pl.BlockSpec((tile, width), lambda b, t: (b * nt + t, 0))
    const = lambda a: pl.BlockSpec(a.shape, lambda b, t: (0, 0))
    out = jax.ShapeDtypeStruct
    return pl.pallas_call(
        _out_ab_kernel,
        grid=(bsz, nt),
        in_specs=[pl.BlockSpec((CHUNK, d), lambda b, t: (0, 0))] + _chunk_specs(d, nt, 1)
        + [row(RET_W), row(S5_W), const(w), pl.BlockSpec((1, d), lambda b, t: (0, 0)), const(wg)],
        out_specs=[row(d), row(d), row(128)],
        out_shape=(out((m, d), F32), out((m, d), BF16), out((m, 128), F32)),
        compiler_params=_cparams(("parallel", "parallel")),
        name="out_proj_ab",
    )(lead, x, x, x, oa, ob, w, norm_w.reshape(1, d), wg)


LOG2E = 1.4426950408889634


def _split2(x):
    x1 = x.astype(BF16)
    return x1, (x - x1.astype(F32)).astype(BF16)


def _neg_abs(x):
    return -jnp.abs(x)


def _block_ref_rows(b, level):
    rows, width = b.shape
    half = 1 << level
    blk = 2 * half
    if blk >= 8:
        b3 = b.reshape(rows // blk, blk, width)
        ref = jnp.broadcast_to(b3[:, half - 1:half, :], b3.shape)
        return ref.reshape(rows, width)
    b3 = b.reshape(rows // 8, 8, width)
    sub = lax.broadcasted_iota(jnp.int32, b3.shape, 1)
    pick = lambda r: jnp.broadcast_to(b3[:, r:r + 1, :], b3.shape)
    if blk == 4:
        ref = jnp.where(sub < 4, pick(1), pick(5))
    else:
        ref = jnp.where(sub < 2, pick(0), jnp.where(sub < 4, pick(2), jnp.where(sub < 6, pick(4), pick(6))))
    return ref.reshape(rows, width)


def _gla_kernel(q_ref, k_ref, v_ref, z_ref, gl_ref, wg_ref, bg_ref, nw_ref, h_ref, wo_ref, fw_ref,
                y_ref, st_ref, mix_ref, *, nc, nchunks):
    c = pl.program_id(0)
    n = jnp.minimum(c, nchunks - 1) % nc

    @pl.when(c == 0)
    def _():
        mix_ref[...] = jnp.zeros_like(mix_ref)

    @pl.when(n == 0)
    def _():
        st_ref[...] = jnp.zeros_like(st_ref)


    ridx = lax.broadcasted_iota(jnp.int32, (CHUNK, 1), 0)
    ii = lax.broadcasted_iota(jnp.int32, (CHUNK, CHUNK), 0)
    jj = lax.broadcasted_iota(jnp.int32, (CHUNK, CHUNK), 1)
    pair_code = jnp.where(ii > jj, ii ^ jj, 0)
    eye = ii == jj
    tri = (ii >= jj).astype(BF16)

    gl1, gl2 = _split2(gl_ref[:, :GLA_RANK])
    wg1, wg2 = _split2(wg_ref[...])
    x = jnp.dot(jnp.concatenate([gl1, gl2, gl1], axis=1), jnp.concatenate([wg1, wg1, wg2], axis=0),
                preferred_element_type=F32) + bg_ref[...]
    log_a = (jnp.minimum(x, 0.0) - jnp.log(1.0 + jnp.exp(-jnp.abs(x)))) * (LOG2E / GLA_TAU)
    log_a = jnp.where(ridx >= jnp.where(n > 0, 0, PAD), log_a, 0.0)
    g1, g2 = _split2(log_a)
    b_all = jnp.dot(jnp.concatenate([tri, tri], axis=1), jnp.concatenate([g1, g2], axis=0),
                    preferred_element_type=F32)
    odd = (ridx & 1) == 1
    h2 = h_ref[...]
    for h0 in range(0, GLA_HEADS, 2):
        heads = (h0, h0 + 1)
        qs, ks, bs, scores = [], [], [], []
        for h in heads:
            kc = slice(h * GLA_DK, (h + 1) * GLA_DK)
            vc = slice(h * GLA_DV, (h + 1) * GLA_DV)
            h2 = h2 + jnp.dot(mix_ref[:, vc], wo_ref[vc, :], preferred_element_type=F32)
            qs.append(q_ref[:, kc].astype(F32))
            ks.append(k_ref[:, kc].astype(F32))
            bs.append(b_all[:, kc])
            scores.append(jnp.where(eye, jnp.sum(qs[-1] * ks[-1], axis=1, keepdims=True), 0.0))
        for lv in range(GLA_LEVELS):
            half = 1 << lv
            zz = []
            for q, k, b in zip(qs, ks, bs):
                if lv == 0:
                    expo = jnp.where(odd, b - pltpu.roll(b, 1, 0), 0.0)
                else:
                    expo = _neg_abs(b - _block_ref_rows(b, lv))
                if half >= 8:
                    sel = jnp.concatenate([(q if r % 2 else k)[r * half:(r + 1) * half]
                                           for r in range(CHUNK // half)], axis=0)
                else:
                    sel = jnp.where(((ridx >> lv) & 1) == 1, q, k)
                zz.append((sel * jnp.exp2(expo)).astype(BF16))
            z2 = jnp.concatenate(zz, axis=0)
            gram = lax.dot_general(z2, z2, (((1,), (1,)), ((), ())), preferred_element_type=F32)
            mask = (pair_code >> lv) == 1
            for i in range(2):
                blk = gram[i * CHUNK:(i + 1) * CHUNK, i * CHUNK:(i + 1) * CHUNK]
                scores[i] = scores[i] + jnp.where(mask, blk, 0.0)
        for i, h in enumerate(heads):
            vc = slice(h * GLA_DV, (h + 1) * GLA_DV)
            q, k, b = qs[i], ks[i], bs[i]
            v = v_ref[:, vc]
            b_last = b[CHUNK - 1:CHUNK, :]
            o = jnp.dot(scores[i].astype(BF16), v, preferred_element_type=F32)
            st = st_ref[h]
            qe = (q * jnp.exp2(b)).astype(BF16)
            o = o + lax.dot_general(qe, st.astype(BF16), (((1,), (1,)), ((), ())), preferred_element_type=F32)
            ke = (k * jnp.exp2(b_last - b)).astype(BF16)
            kv_t = lax.dot_general(v, ke, (((0,), (0,)), ((), ())), preferred_element_type=F32)
            st_ref[h] = st * jnp.exp2(b_last) + kv_t
            mix_ref[:, vc] = (_rms_scale(o, nw_ref[:, vc]) * z_ref[:, vc].astype(F32)).astype(BF16)

    y_ref[...] = _rms_scale(h2, fw_ref[...])


def _gla(proj, glow, w_gate, b_gate, norm_w, h1, w_out, final_w, bsz, nc):
    m, d = h1.shape
    nchunks = m // CHUNK
    const = lambda *shape: pl.BlockSpec(shape, lambda c: (0,) * len(shape))
    cur = lambda width, col: pl.BlockSpec((CHUNK, width), lambda c: (jnp.minimum(c, nchunks - 1), col))
    prev = lambda c: jnp.maximum(c - 1, 0)
    return pl.pallas_call(
        functools.partial(_gla_kernel, nc=nc, nchunks=nchunks),
        grid=(nchunks + 1,),
        in_specs=[cur(GLA_QK, 0), cur(GLA_QK, 1), cur(GLA_W, 1), cur(GLA_W, 2), cur(128, 0),
                  const(GLA_RANK, GLA_QK), const(1, GLA_QK), const(1, GLA_W),
                  pl.BlockSpec((CHUNK, d), lambda c: (prev(c), 0)),
                  const(GLA_W, d), const(1, d)],
        out_specs=pl.BlockSpec((None, CHUNK, d),
                               lambda c: (prev(c) // nc, jnp.maximum(prev(c) % nc - 1, 0), 0)),
        out_shape=jax.ShapeDtypeStruct((bsz, (nc - 1) * CHUNK, d), F32),
        scratch_shapes=[pltpu.VMEM((GLA_HEADS, GLA_DV, GLA_DK), F32), pltpu.VMEM((CHUNK, GLA_W), BF16)],
        compiler_params=_cparams(("arbitrary",)),
        name="gla_out",
    )(proj, proj, proj, proj, glow, w_gate.astype(F32), b_gate.reshape(1, GLA_QK).astype(F32),
      norm_w.reshape(1, GLA_W).astype(F32), h1, w_out, final_w.reshape(1, d).astype(F32))


def _rope_tables(bsz, rows_per_batch):
    pos = np.maximum(np.arange(rows_per_batch, dtype=np.float64) - PAD, 0.0)
    inv_freq = np.power(ROPE_BASE, -np.arange(0, RET_DK, 2, dtype=np.float64) / RET_DK)
    ang = pos[:, None] * inv_freq[None, :]
    cos, sin = np.cos(ang), np.sin(ang)
    cos2 = np.tile(np.concatenate([cos, cos], axis=1), (bsz, 1)).astype(np.float32)
    sin2 = np.tile(np.concatenate([-sin, sin], axis=1), (bsz, 1)).astype(np.float32)
    return jnp.asarray(cos2), jnp.asarray(sin2)


def kernel(x, meta, norm_ab_w, w_in_ab, ret_norm_w, s5_lam_re, s5_lam_im, s5_log_dt, s5_b_re, s5_b_im,
           s5_c_re, s5_c_im, s5_d, s5_w_glu, w_out_ab, norm_c_w, w_in_c, gla_w_gate, gla_b_gate,
           gla_norm_w, w_out_c, final_norm_w):
    bsz, seq, d = x.shape
    assert seq % CHUNK == 0 and w_in_ab.shape[0] == 1 and w_in_c.shape[0] == 1
    rpb = seq + CHUNK
    nc = rpb // CHUNK
    assert nc % ROW_CHUNKS == 0
    nt = nc // ROW_CHUNKS
    m = bsz * rpb
    tm = _pick_tile(m, (1408, 768, 384, 256, 128))
    lead = jnp.concatenate([jnp.zeros((PAD, d), x.dtype), meta.astype(x.dtype)], axis=0)

    xn0 = _embed_norm(x, lead, norm_ab_w[0], nt)
    plain, silu = ("plain", 1.0), ("silu", 1.0)
    modes_ab = [("rope", 1.0), ("rope", RET_DK ** -0.5), plain, plain, silu, silu, plain, silu]
    proj0 = _in_proj(xn0, w_in_ab, modes_ab, tm, rope=_rope_tables(bsz, rpb))
    o_a = _retention(proj0, ret_norm_w[0], bsz, nt)
    t, tseg = _s5_time_tile(rpb)
    p_re, p_im, bb_re, bb_im = _s5_prep(s5_lam_re[0], s5_lam_im[0], s5_log_dt[0], s5_b_re[0], s5_b_im[0], tseg)
    wb, wcre, wcim = _s5_block_weights(bb_re, bb_im, s5_c_re[0], s5_c_im[0])
    o_b = _s5(proj0, bsz, wb, wcre, wcim, p_re, p_im, s5_d[0], s5_w_glu[0].astype(BF16), t, tseg)

    w_in_c_t = jnp.swapaxes(w_in_c, 1, 2)
    n_main = 2 * GLA_QK + 2 * GLA_W
    wg = jnp.pad(w_in_c_t[0, n_main:, :], ((0, 128 - GLA_RANK), (0, 0))).astype(BF16).T
    h1, xn1, glow = _out_ab(x, lead, o_a, o_b, w_out_ab[0].astype(BF16), norm_c_w[0], wg, nt)

    modes_c = [("plain", GLA_DK ** -0.5), plain, plain, plain, silu, silu]
    proj1 = _in_proj(xn1, w_in_c_t, modes_c, tm, w_is_transposed=True)
    return _gla(proj1, glow, gla_w_gate[0], gla_b_gate[0], gla_norm_w[0], h1, w_out_c[0].astype(BF16),
                final_norm_w, bsz, nc)
```

```python
import functools
import math

import numpy as np
import jax
import jax.numpy as jnp
from jax import lax
from jax.experimental import pallas as pl
from jax.experimental.pallas import tpu as pltpu

F32 = jnp.float32
BF16 = jnp.bfloat16

N_META = 16
CHUNK = 128
PAD = CHUNK - N_META
EPS = 1e-6
ROW_CHUNKS = 3

RET_HEADS = 8
RET_DK = 128
RET_DV = 256
RET_QK = RET_HEADS * RET_DK
RET_W = RET_HEADS * RET_DV
ROPE_BASE = 10000.0

S5_W = 1024
S5_GH = 16
S5_G = 64
S5_P = 64
S5_F = S5_G * S5_P
S5_SLABS = S5_F // 128
S5_JB = 4
S5_SEGS = 8
S5_SCAN_UNROLL = 3

GLA_HEADS = 4
GLA_DK = 256
GLA_DV = 512
GLA_QK = GLA_HEADS * GLA_DK
GLA_W = GLA_HEADS * GLA_DV
GLA_RANK = 16
GLA_TAU = 16.0
GLA_LEVELS = 7

PROJ_TN = 1024
PROJ_ROW_SPLIT = 4
VMEM_LIMIT = 56 * 1024 * 1024


def _cparams(sem):
    return pltpu.CompilerParams(dimension_semantics=sem, vmem_limit_bytes=VMEM_LIMIT)


def _silu(x):
    return x * (1.0 / (1.0 + jnp.exp(-x)))


def _pick_tile(n, candidates):
    for c in candidates:
        if n % c == 0:
            return c
    raise ValueError(f"no tile for {n}")


def _rms_scale(x, w):
    return x * lax.rsqrt(jnp.mean(x * x, axis=-1, keepdims=True) + EPS) * w


def _chunk_specs(d, nt, n_before):
    del nt
    return [pl.BlockSpec((None, CHUNK, d),
                         lambda b, t, k=k: (b, jnp.maximum(ROW_CHUNKS * t + k - n_before, 0), 0))
            for k in range(ROW_CHUNKS)]


def _embed_norm_kernel(lead_ref, xa_ref, xb_ref, xc_ref, w_ref, o_ref):
    t = pl.program_id(1)
    first = jnp.where(t == 0, lead_ref[...], xa_ref[...])
    for r, rows in enumerate((first, xb_ref[...], xc_ref[...])):
        o_ref[r * CHUNK:(r + 1) * CHUNK, :] = _rms_scale(rows, w_ref[...]).astype(o_ref.dtype)


def _embed_norm(x, lead, w, nt):
    bsz, _, d = x.shape
    tile = ROW_CHUNKS * CHUNK
    return pl.pallas_call(
        _embed_norm_kernel,
        grid=(bsz, nt),
        in_specs=[pl.BlockSpec((CHUNK, d), lambda b, t: (0, 0))] + _chunk_specs(d, nt, 1)
        + [pl.BlockSpec((1, d), lambda b, t: (0, 0))],
        out_specs=pl.BlockSpec((tile, d), lambda b, t: (b * nt + t, 0)),
        out_shape=jax.ShapeDtypeStruct((bsz * nt * tile, d), BF16),
        compiler_params=_cparams(("parallel", "parallel")),
        name="embed_norm",
    )(lead, x, x, x, w.reshape(1, d))


def _rope_rows(a, cos, sin, scale):
    outs = []
    for h in range(a.shape[1] // RET_DK):
        blk = a[:, h * RET_DK:(h + 1) * RET_DK]
        r = blk * cos + pltpu.roll(blk, RET_DK // 2, 1) * sin
        outs.append(r * scale if scale != 1.0 else r)
    return jnp.concatenate(outs, axis=1)


def _in_proj_kernel(*refs, modes, has_rope, w_is_transposed):
    if has_rope:
        x_ref, w_ref, cos_ref, sin_ref, o_ref, wbf_ref = refs
    else:
        x_ref, w_ref, o_ref, wbf_ref = refs
    j = pl.program_id(0)

    @pl.when(pl.program_id(1) == 0)
    def _():
        wbf_ref[...] = w_ref[...].astype(BF16)

    contract = (((1,), (1 if w_is_transposed else 0,)), ((), ()))
    for mode in sorted(set(modes)):
        cond = functools.reduce(jnp.logical_or, [j == jj for jj, mm in enumerate(modes) if mm == mode])

        @pl.when(cond)
        def _(mode=mode):
            kind, scale = mode
            rows = x_ref.shape[0] // PROJ_ROW_SPLIT
            sls = [slice(r * rows, (r + 1) * rows) for r in range(PROJ_ROW_SPLIT)]
            accs = [lax.dot_general(x_ref[sl, :], wbf_ref[...], contract, preferred_element_type=F32)
                    for sl in sls]
            for sl, acc in zip(sls, accs):
                if kind == "rope":
                    acc = _rope_rows(acc, cos_ref[sl, :], sin_ref[sl, :], scale)
                elif kind == "silu":
                    acc = _silu(acc)
                elif scale != 1.0:
                    acc = acc * scale
                o_ref[sl, :] = acc.astype(o_ref.dtype)


def _in_proj(x, w, modes, tm, rope=None, w_is_transposed=False):
    m, k = x.shape
    nt = len(modes)
    if w_is_transposed:
        w_spec = pl.BlockSpec((None, PROJ_TN, k), lambda j, i: (0, j, 0))
        w_scratch = pltpu.VMEM((PROJ_TN, k), BF16)
    else:
        w_spec = pl.BlockSpec((None, k, PROJ_TN), lambda j, i: (0, 0, j))
        w_scratch = pltpu.VMEM((k, PROJ_TN), BF16)
    in_specs = [pl.BlockSpec((tm, k), lambda j, i: (i, 0)), w_spec]
    args = [x, w]
    if rope is not None:
        in_specs += [pl.BlockSpec((tm, RET_DK), lambda j, i: (i, 0))] * 2
        args += list(rope)
    return pl.pallas_call(
        functools.partial(_in_proj_kernel, modes=tuple(modes), has_rope=rope is not None,
                          w_is_transposed=w_is_transposed),
        grid=(nt, m // tm),
        in_specs=in_specs,
        out_specs=pl.BlockSpec((tm, PROJ_TN), lambda j, i: (i, j)),
        out_shape=jax.ShapeDtypeStruct((m, nt * PROJ_TN), BF16),
        scratch_shapes=[w_scratch],
        compiler_params=_cparams(("arbitrary", "arbitrary")),
        name="in_proj",
    )(*args)


def _ret_log_decay(h):
    return math.log1p(-(2.0 ** (-5.0 - h)))


def _retention_kernel(q_ref, k_ref, v_ref, z_ref, nw_ref, o_ref, s_ref, dec_ref, qs_ref, ks_ref):
    first_step = (pl.program_id(0) == 0) & (pl.program_id(1) == 0)

    @pl.when(first_step)
    def _():
        ii = lax.broadcasted_iota(jnp.int32, (CHUNK, CHUNK), 0)
        jj = lax.broadcasted_iota(jnp.int32, (CHUNK, CHUNK), 1)
        diff = (ii - jj).astype(F32)
        row = ii.astype(F32)
        for h in range(RET_HEADS):
            lg = _ret_log_decay(h)
            dec_ref[h] = jnp.where(ii >= jj, jnp.exp(lg * jnp.maximum(diff, 0.0)), 0.0)
            qs_ref[h] = jnp.exp(lg * (row + 1.0))
            ks_ref[h] = jnp.exp(lg * (CHUNK - 1.0 - row))

    @pl.when(pl.program_id(1) == 0)
    def _():
        s_ref[...] = jnp.zeros_like(s_ref)

    heads = range(RET_HEADS)
    vcs = [slice(h * RET_DV, (h + 1) * RET_DV) for h in heads]
    nt_dims = (((1,), (1,)), ((), ()))
    tn_dims = (((0,), (0,)), ((), ()))
    states = [s_ref[h] for h in heads]
    for r in range(q_ref.shape[0] // CHUNK):
        rows = slice(r * CHUNK, (r + 1) * CHUNK)
        qs = [q_ref[rows, h * RET_DK:(h + 1) * RET_DK] for h in heads]
        ks = [k_ref[rows, h * RET_DK:(h + 1) * RET_DK] for h in heads]
        vs = [v_ref[rows, vc] for vc in vcs]
        scores = [lax.dot_general(qs[h], ks[h], nt_dims, preferred_element_type=F32) for h in heads]
        kvs = [lax.dot_general((ks[h].astype(F32) * ks_ref[h]).astype(BF16), vs[h], tn_dims,
                               preferred_element_type=F32) for h in heads]
        lhs = [jnp.concatenate([(scores[h] * dec_ref[h]).astype(BF16),
                                (qs[h].astype(F32) * qs_ref[h]).astype(BF16)], axis=1) for h in heads]
        outs = [jnp.dot(lhs[h], jnp.concatenate([vs[h], states[h].astype(BF16)], axis=0),
                        preferred_element_type=F32) for h in heads]
        states = [states[h] * math.exp(_ret_log_decay(h) * CHUNK) + kvs[h] for h in heads]
        for h in heads:
            o_ref[rows, vcs[h]] = (_rms_scale(outs[h], nw_ref[:, vcs[h]])
                                   * z_ref[rows, vcs[h]].astype(F32)).astype(o_ref.dtype)
    for h in heads:
        s_ref[h] = states[h]


def _retention(proj, norm_w, bsz, nt):
    m = proj.shape[0]
    tile = ROW_CHUNKS * CHUNK
    tab = pltpu.VMEM((RET_HEADS, CHUNK, CHUNK), F32)
    row = lambda width, col: pl.BlockSpec((tile, width), lambda b, t: (b * nt + t, col))
    return pl.pallas_call(
        _retention_kernel,
        grid=(bsz, nt),
        in_specs=[row(RET_QK, 0), row(RET_QK, 1), row(RET_W, 1), row(RET_W, 2),
                  pl.BlockSpec((1, RET_W), lambda b, t: (0, 0))],
        out_specs=row(RET_W, 0),
        out_shape=jax.ShapeDtypeStruct((m, RET_W), BF16),
        scratch_shapes=[pltpu.VMEM((RET_HEADS, RET_DK, RET_DV), F32), tab, tab, tab],
        compiler_params=_cparams(("arbitrary", "arbitrary")),
        name="retention",
    )(proj, proj, proj, proj, norm_w.reshape(1, RET_W))


def _s5_prep_kernel(lre_ref, lim_ref, ldt_ref, bre_ref, bim_ref, cre_ref, cim_ref,
                    pre_ref, pim_ref, wb_ref, wcre_ref, wcim_ref):
    lre = lre_ref[...]
    lim = lim_ref[...]
    dt = jnp.exp(ldt_ref[...])
    tseg = pre_ref.shape[1]
    n = lax.broadcasted_iota(jnp.int32, (tseg, 1), 0).astype(F32) + 1.0
    mag = jnp.exp(n * (lre * dt))
    ang = n * (lim * dt)
    p_re = mag * jnp.cos(ang)
    p_im = mag * jnp.sin(ang)
    for sl in range(S5_SLABS):
        pre_ref[sl] = p_re[:, sl * 128:(sl + 1) * 128]
        pim_ref[sl] = p_im[:, sl * 128:(sl + 1) * 128]
    a_re = p_re[0:1]
    a_im = p_im[0:1]
    den = lre * lre + lim * lim
    nr = a_re - 1.0
    f_re = (nr * lre + a_im * lim) / den
    f_im = (a_im * lre - nr * lim) / den
    b_re = bre_ref[...]
    b_im = bim_ref[...]
    bb_re = f_re * b_re - f_im * b_im
    bb_im = f_re * b_im + f_im * b_re
    blk = S5_F // S5_JB
    rows = lax.broadcasted_iota(jnp.int32, (16 * S5_GH, blk), 0)
    cols = lax.broadcasted_iota(jnp.int32, (16 * S5_GH, blk), 1)
    same_group = (rows // S5_GH) == (cols // S5_P)

    def block_diag(a, j):
        piece = a[:, j * blk:(j + 1) * blk]
        return jnp.where(same_group, jnp.concatenate([piece] * 16, axis=0), 0.0).astype(BF16)

    for j in range(S5_JB):
        wb_ref[j, :, :blk] = block_diag(bb_re, j)
        wb_ref[j, :, blk:] = block_diag(bb_im, j)
        wcre_ref[j] = block_diag(cre_ref[...], j)
        wcim_ref[j] = block_diag(cim_ref[...], j)


def _s5_prep(lam_re, lam_im, log_dt, b_re, b_im, c_re, c_im, tseg):
    flat = lambda a: a.reshape(1, S5_F)
    ldt = jnp.broadcast_to(log_dt[:, None], (S5_G, S5_P))
    bt = lambda a: a.reshape(S5_F, S5_GH).T
    ct = lambda a: a.transpose(1, 0, 2).reshape(S5_GH, S5_F)
    out = jax.ShapeDtypeStruct
    blk = S5_F // S5_JB
    return pl.pallas_call(
        _s5_prep_kernel,
        out_shape=(out((S5_SLABS, tseg, 128), F32), out((S5_SLABS, tseg, 128), F32),
                   out((S5_JB, 16 * S5_GH, 2 * blk), BF16),
                   out((S5_JB, 16 * S5_GH, blk), BF16), out((S5_JB, 16 * S5_GH, blk), BF16)),
        compiler_params=pltpu.CompilerParams(vmem_limit_bytes=VMEM_LIMIT),
        name="s5_prep",
    )(flat(lam_re), flat(lam_im), flat(ldt), bt(b_re), bt(b_im), ct(c_re.astype(F32)), ct(c_im.astype(F32)))


def _gelu_tanh(x):
    return 0.5 * x * (1.0 + jnp.tanh(math.sqrt(2.0 / math.pi) * (x + 0.044715 * (x * x * x))))


def _s5_kernel(u_ref, z_ref, wb_ref, wcre_ref, wcim_ref, pre_ref, pim_ref, d_ref, wglu_ref,
               o_ref, bure_ref, buim_ref, cre_ref, cim_ref, xinre_ref, xinim_ref, *, tseg, slab_group):
    ci = pl.program_id(1)

    @pl.when(ci == 0)
    def _():
        cre_ref[...] = jnp.zeros_like(cre_ref)
        cim_ref[...] = jnp.zeros_like(cim_ref)

    u_bf = u_ref[...]
    spb = S5_SLABS // S5_JB
    for j in range(S5_JB):
        res = jnp.dot(u_bf[:, j * 256:(j + 1) * 256], wb_ref[j], preferred_element_type=F32)
        for s in range(spb):
            bure_ref[j * spb + s] = res[:, s * 128:(s + 1) * 128]
            buim_ref[j * spb + s] = res[:, (spb + s) * 128:(spb + s + 1) * 128]

    for g0 in range(0, S5_SLABS, slab_group):
        slabs = list(range(g0, g0 + slab_group))
        a_re = [jnp.broadcast_to(pre_ref[sl, 0:1, :], (S5_SEGS, 128)) for sl in slabs]
        a_im = [jnp.broadcast_to(pim_ref[sl, 0:1, :], (S5_SEGS, 128)) for sl in slabs]

        def scan_body(tau, carry, slabs=slabs, a_re=a_re, a_im=a_im):
            xr, xi = carry
            nr, ni = [], []
            for n, sl in enumerate(slabs):
                rows = pl.ds(tau, S5_SEGS, stride=tseg)
                br = bure_ref[sl, rows, :]
                bi = buim_ref[sl, rows, :]
                r = a_re[n] * xr[n] - a_im[n] * xi[n] + br
                i = a_re[n] * xi[n] + a_im[n] * xr[n] + bi
                bure_ref[sl, rows, :] = r
                buim_ref[sl, rows, :] = i
                nr.append(r)
                ni.append(i)
            return tuple(nr), tuple(ni)

        zero = tuple(jnp.zeros((S5_SEGS, 128), F32) for _ in slabs)
        end_re, end_im = lax.fori_loop(0, tseg, scan_body, (zero, zero),
                                       unroll=S5_SCAN_UNROLL if tseg % S5_SCAN_UNROLL == 0 else 1)

        for n, sl in enumerate(slabs):
            at_re = pre_ref[sl, tseg - 1:tseg, :]
            at_im = pim_ref[sl, tseg - 1:tseg, :]
            xr = cre_ref[sl, 0:1, :]
            xi = cim_ref[sl, 0:1, :]
            rows_re, rows_im = [], []
            for s in range(S5_SEGS):
                rows_re.append(xr)
                rows_im.append(xi)
                er = end_re[n][s:s + 1]
                ei = end_im[n][s:s + 1]
                xr, xi = er + at_re * xr - at_im * xi, ei + at_re * xi + at_im * xr
            cre_ref[sl] = jnp.broadcast_to(xr, (S5_SEGS, 128))
            cim_ref[sl] = jnp.broadcast_to(xi, (S5_SEGS, 128))
            xinre_ref[sl] = jnp.concatenate(rows_re, axis=0)
            xinim_ref[sl] = jnp.concatenate(rows_im, axis=0)

    def fix_body(sl, carry):
        xin_re = xinre_ref[sl]
        xin_im = xinim_ref[sl]
        for tau in range(tseg):
            rows = pl.ds(tau, S5_SEGS, stride=tseg)
            pr = pre_ref[sl, tau:tau + 1, :]
            pi = pim_ref[sl, tau:tau + 1, :]
            bure_ref[sl, rows, :] = bure_ref[sl, rows, :] + (pr * xin_re - pi * xin_im)
            buim_ref[sl, rows, :] = buim_ref[sl, rows, :] + (pr * xin_im + pi * xin_re)
        return carry

    lax.fori_loop(0, S5_SLABS, fix_body, 0)

    ys = []
    for j in range(S5_JB):
        x_re = jnp.concatenate([bure_ref[j * spb + s] for s in range(spb)], axis=1).astype(BF16)
        x_im = jnp.concatenate([buim_ref[j * spb + s] for s in range(spb)], axis=1).astype(BF16)
        nt_dims = (((1,), (1,)), ((), ()))
        ys.append(lax.dot_general(x_re, wcre_ref[j], nt_dims, preferred_element_type=F32)
                  - lax.dot_general(x_im, wcim_ref[j], nt_dims, preferred_element_type=F32))
    y = jnp.concatenate(ys, axis=1) + d_ref[...] * u_bf.astype(F32)
    y = _gelu_tanh(y)
    gate = jnp.dot(y.astype(BF16), wglu_ref[...], preferred_element_type=F32)
    y = y * (1.0 / (1.0 + jnp.exp(-gate)))
    o_ref[...] = (y * z_ref[...].astype(F32)).astype(o_ref.dtype)


def _s5_time_tile(rows_per_batch):
    for tseg in (66, 44, 132, 12, 4, 6, 2):
        t = S5_SEGS * tseg
        if rows_per_batch % t == 0 and t % 16 == 0:
            return t, tseg
    raise ValueError(f"no S5 time tile for {rows_per_batch}")


def _s5(proj, bsz, wb, wcre, wcim, p_re, p_im, d, wglu, t, tseg):
    m = proj.shape[0]
    nt = m // bsz // t
    ucol = (2 * RET_QK + 2 * RET_W) // S5_W
    const = lambda *shape: pl.BlockSpec(shape, lambda b, i: (0,) * len(shape))
    return pl.pallas_call(
        functools.partial(_s5_kernel, tseg=tseg, slab_group=8),
        grid=(bsz, nt),
        in_specs=[pl.BlockSpec((t, S5_W), lambda b, i: (b * nt + i, ucol)),
                  pl.BlockSpec((t, S5_W), lambda b, i: (b * nt + i, ucol + 1)),
                  const(*wb.shape), const(*wcre.shape), const(*wcim.shape),
                  const(S5_SLABS, tseg, 128), const(S5_SLABS, tseg, 128), const(1, S5_W),
                  const(S5_W, S5_W)],
        out_specs=pl.BlockSpec((t, S5_W), lambda b, i: (b * nt + i, 0)),
        out_shape=jax.ShapeDtypeStruct((m, S5_W), BF16),
        scratch_shapes=[pltpu.VMEM((S5_SLABS, t, 128), F32), pltpu.VMEM((S5_SLABS, t, 128), F32)]
        + [pltpu.VMEM((S5_SLABS, S5_SEGS, 128), F32)] * 4,
        compiler_params=_cparams(("arbitrary", "arbitrary")),
        name="s5",
    )(proj, proj, wb, wcre, wcim, p_re, p_im, d.reshape(1, S5_W), wglu)


def _out_ab_kernel(lead_ref, xa_ref, xb_ref, xc_ref, oa_ref, ob_ref, w_ref, nw_ref, wg_ref,
                   h1_ref, xn_ref, gl_ref):
    t = pl.program_id(1)
    first = jnp.where(t == 0, lead_ref[...], xa_ref[...])
    h0 = (first, xb_ref[...], xc_ref[...])
    sls = [slice(r * CHUNK, (r + 1) * CHUNK) for r in range(ROW_CHUNKS)]
    accs = [jnp.dot(oa_ref[sl, :], w_ref[:RET_W, :], preferred_element_type=F32)
            + jnp.dot(ob_ref[sl, :], w_ref[RET_W:, :], preferred_element_type=F32) for sl in sls]
    h1s = [h0[r] + accs[r] for r in range(ROW_CHUNKS)]
    xns = [_rms_scale(h1, nw_ref[...]).astype(BF16) for h1 in h1s]
    gls = [jnp.dot(xn, wg_ref[...], preferred_element_type=F32) for xn in xns]
    for r, sl in enumerate(sls):
        h1_ref[sl, :] = h1s[r]
        xn_ref[sl, :] = xns[r]
        gl_ref[sl, :] = gls[r]


def _out_ab(x, lead, oa, ob, w, norm_w, wg, nt):
    bsz, _, d = x.shape
    tile = ROW_CHUNKS * CHUNK
    m = bsz * nt * tile
    row = lambda width: pl.BlockSpec((tile, width), lambda b, t: (b * nt + t, 0))
    const = lambda a: pl.BlockSpec(a.shape, lambda b, t: (0, 0))
    out = jax.ShapeDtypeStruct
    return pl.pallas_call(
        _out_ab_kernel,
        grid=(bsz, nt),
        in_specs=[pl.BlockSpec((CHUNK, d), lambda b, t: (0, 0))] + _chunk_specs(d, nt, 1)
        + [row(RET_W), row(S5_W), const(w), pl.BlockSpec((1, d), lambda b, t: (0, 0)), const(wg)],
        out_specs=[row(d), row(d), row(128)],
        out_shape=(out((m, d), F32), out((m, d), BF16), out((m, 128), F32)),
        compiler_params=_cparams(("parallel", "parallel")),
        name="out_proj_ab",
    )(lead, x, x, x, oa, ob, w, norm_w.reshape(1, d), wg)


LOG2E = 1.4426950408889634


def _split2(x):
    x1 = x.astype(BF16)
    return x1, (x - x1.astype(F32)).astype(BF16)


def _neg_abs(x):
    return -jnp.abs(x)


def _block_ref_rows(b, level):
    rows, width = b.shape
    half = 1 << level
    blk = 2 * half
    if blk >= 8:
        b3 = b.reshape(rows // blk, blk, width)
        ref = jnp.broadcast_to(b3[:, half - 1:half, :], b3.shape)
        return ref.reshape(rows, width)
    b3 = b.reshape(rows // 8, 8, width)
    sub = lax.broadcasted_iota(jnp.int32, b3.shape, 1)
    pick = lambda r: jnp.broadcast_to(b3[:, r:r + 1, :], b3.shape)
    if blk == 4:
        ref = jnp.where(sub < 4, pick(1), pick(5))
    else:
        ref = jnp.where(sub < 2, pick(0), jnp.where(sub < 4, pick(2), jnp.where(sub < 6, pick(4), pick(6))))
    return ref.reshape(rows, width)


def _gla_kernel(q_ref, k_ref, v_ref, z_ref, gl_ref, wg_ref, bg_ref, nw_ref, h_ref, wo_ref, fw_ref,
                y_ref, st_ref, mix_ref, *, nc, nchunks):
    c = pl.program_id(0)
    n = jnp.minimum(c, nchunks - 1) % nc

    @pl.when(c == 0)
    def _():
        mix_ref[...] = jnp.zeros_like(mix_ref)

    @pl.when(n == 0)
    def _():
        st_ref[...] = jnp.zeros_like(st_ref)


    ridx = lax.broadcasted_iota(jnp.int32, (CHUNK, 1), 0)
    ii = lax.broadcasted_iota(jnp.int32, (CHUNK, CHUNK), 0)
    jj = lax.broadcasted_iota(jnp.int32, (CHUNK, CHUNK), 1)
    pair_code = jnp.where(ii > jj, ii ^ jj, 0)
    eye = ii == jj
    tri = (ii >= jj).astype(BF16)

    gl1, gl2 = _split2(gl_ref[:, :GLA_RANK])
    wg1, wg2 = _split2(wg_ref[...])
    x = jnp.dot(jnp.concatenate([gl1, gl2, gl1], axis=1), jnp.concatenate([wg1, wg1, wg2], axis=0),
                preferred_element_type=F32) + bg_ref[...]
    log_a = (jnp.minimum(x, 0.0) - jnp.log(1.0 + jnp.exp(-jnp.abs(x)))) * (LOG2E / GLA_TAU)
    log_a = jnp.where(ridx >= jnp.where(n > 0, 0, PAD), log_a, 0.0)
    g1, g2 = _split2(log_a)
    b_all = jnp.dot(jnp.concatenate([tri, tri], axis=1), jnp.concatenate([g1, g2], axis=0),
                    preferred_element_type=F32)
    odd = (ridx & 1) == 1
    h2 = h_ref[...]
    for h0 in range(0, GLA_HEADS, 2):
        heads = (h0, h0 + 1)
        qs, ks, bs, scores = [], [], [], []
        for h in heads:
            kc = slice(h * GLA_DK, (h + 1) * GLA_DK)
            vc = slice(h * GLA_DV, (h + 1) * GLA_DV)
            h2 = h2 + jnp.dot(mix_ref[:, vc], wo_ref[vc, :], preferred_element_type=F32)
            qs.append(q_ref[:, kc].astype(F32))
            ks.append(k_ref[:, kc].astype(F32))
            bs.append(b_all[:, kc])
            scores.append(jnp.where(eye, jnp.sum(qs[-1] * ks[-1], axis=1, keepdims=True), 0.0))
        for lv in range(GLA_LEVELS):
            half = 1 << lv
            zz = []
            for q, k, b in zip(qs, ks, bs):
                if lv == 0:
                    expo = jnp.where(odd, b - pltpu.roll(b, 1, 0), 0.0)
                else:
                    expo = _neg_abs(b - _block_ref_rows(b, lv))
                if half >= 8:
                    sel = jnp.concatenate([(q if r % 2 else k)[r * half:(r + 1) * half]
                                           for r in range(CHUNK // half)], axis=0)
                else:
                    sel = jnp.where(((ridx >> lv) & 1) == 1, q, k)
                zz.append((sel * jnp.exp2(expo)).astype(BF16))
            z2 = jnp.concatenate(zz, axis=0)
            gram = lax.dot_general(z2, z2, (((1,), (1,)), ((), ())), preferred_element_type=F32)
            mask = (pair_code >> lv) == 1
            for i in range(2):
                blk = gram[i * CHUNK:(i + 1) * CHUNK, i * CHUNK:(i + 1) * CHUNK]
                scores[i] = scores[i] + jnp.where(mask, blk, 0.0)
        for i, h in enumerate(heads):
            vc = slice(h * GLA_DV, (h + 1) * GLA_DV)
            q, k, b = qs[i], ks[i], bs[i]
            v = v_ref[:, vc]
            b_last = b[CHUNK - 1:CHUNK, :]
            o = jnp.dot(scores[i].astype(BF16), v, preferred_element_type=F32)
            st = st_ref[h]
            qe = (q * jnp.exp2(b)).astype(BF16)
            o = o + lax.dot_general(qe, st.astype(BF16), (((1,), (1,)), ((), ())), preferred_element_type=F32)
            ke = (k * jnp.exp2(b_last - b)).astype(BF16)
            kv_t = lax.dot_general(v, ke, (((0,), (0,)), ((), ())), preferred_element_type=F32)
            st_ref[h] = st * jnp.exp2(b_last) + kv_t
            mix_ref[:, vc] = (_rms_scale(o, nw_ref[:, vc]) * z_ref[:, vc].astype(F32)).astype(BF16)

    y_ref[...] = _rms_scale(h2, fw_ref[...])


def _gla(proj, glow, w_gate, b_gate, norm_w, h1, w_out, final_w, bsz, nc):
    m, d = h1.shape
    nchunks = m // CHUNK
    const = lambda *shape: pl.BlockSpec(shape, lambda c: (0,) * len(shape))
    cur = lambda width, col: pl.BlockSpec((CHUNK, width), lambda c: (jnp.minimum(c, nchunks - 1), col))
    prev = lambda c: jnp.maximum(c - 1, 0)
    return pl.pallas_call(
        functools.partial(_gla_kernel, nc=nc, nchunks=nchunks),
        grid=(nchunks + 1,),
        in_specs=[cur(GLA_QK, 0), cur(GLA_QK, 1), cur(GLA_W, 1), cur(GLA_W, 2), cur(128, 0),
                  const(GLA_RANK, GLA_QK), const(1, GLA_QK), const(1, GLA_W),
                  pl.BlockSpec((CHUNK, d), lambda c: (prev(c), 0)),
                  const(GLA_W, d), const(1, d)],
        out_specs=pl.BlockSpec((None, CHUNK, d),
                               lambda c: (prev(c) // nc, jnp.maximum(prev(c) % nc - 1, 0), 0)),
        out_shape=jax.ShapeDtypeStruct((bsz, (nc - 1) * CHUNK, d), F32),
        scratch_shapes=[pltpu.VMEM((GLA_HEADS, GLA_DV, GLA_DK), F32), pltpu.VMEM((CHUNK, GLA_W), BF16)],
        compiler_params=_cparams(("arbitrary",)),
        name="gla_out",
    )(proj, proj, proj, proj, glow, w_gate.astype(F32), b_gate.reshape(1, GLA_QK).astype(F32),
      norm_w.reshape(1, GLA_W).astype(F32), h1, w_out, final_w.reshape(1, d).astype(F32))


def _rope_tables(bsz, rows_per_batch):
    pos = np.maximum(np.arange(rows_per_batch, dtype=np.float64) - PAD, 0.0)
    inv_freq = np.power(ROPE_BASE, -np.arange(0, RET_DK, 2, dtype=np.float64) / RET_DK)
    ang = pos[:, None] * inv_freq[None, :]
    cos, sin = np.cos(ang), np.sin(ang)
    cos2 = np.tile(np.concatenate([cos, cos], axis=1), (bsz, 1)).astype(np.float32)
    sin2 = np.tile(np.concatenate([-sin, sin], axis=1), (bsz, 1)).astype(np.float32)
    return jnp.asarray(cos2), jnp.asarray(sin2)


def kernel(x, meta, norm_ab_w, w_in_ab, ret_norm_w, s5_lam_re, s5_lam_im, s5_log_dt, s5_b_re, s5_b_im,
           s5_c_re, s5_c_im, s5_d, s5_w_glu, w_out_ab, norm_c_w, w_in_c, gla_w_gate, gla_b_gate,
           gla_norm_w, w_out_c, final_norm_w):
    bsz, seq, d = x.shape
    assert seq % CHUNK == 0 and w_in_ab.shape[0] == 1 and w_in_c.shape[0] == 1
    rpb = seq + CHUNK
    nc = rpb // CHUNK
    assert nc % ROW_CHUNKS == 0
    nt = nc // ROW_CHUNKS
    m = bsz * rpb
    tm = _pick_tile(m, (1408, 768, 384, 256, 128))
    lead = jnp.concatenate([jnp.zeros((PAD, d), x.dtype), meta.astype(x.dtype)], axis=0)

    xn0 = _embed_norm(x, lead, norm_ab_w[0], nt)
    plain, silu = ("plain", 1.0), ("silu", 1.0)
    modes_ab = [("rope", 1.0), ("rope", RET_DK ** -0.5), plain, plain, silu, silu, plain, silu]
    proj0 = _in_proj(xn0, w_in_ab, modes_ab, tm, rope=_rope_tables(bsz, rpb))
    o_a = _retention(proj0, ret_norm_w[0], bsz, nt)
    t, tseg = _s5_time_tile(rpb)
    p_re, p_im, wb, wcre, wcim = _s5_prep(s5_lam_re[0], s5_lam_im[0], s5_log_dt[0], s5_b_re[0], s5_b_im[0],
                                          s5_c_re[0], s5_c_im[0], tseg)
    o_b = _s5(proj0, bsz, wb, wcre, wcim, p_re, p_im, s5_d[0], s5_w_glu[0].astype(BF16), t, tseg)

    w_in_c_t = jnp.swapaxes(w_in_c, 1, 2)
    n_main = 2 * GLA_QK + 2 * GLA_W
    wg = jnp.pad(w_in_c_t[0, n_main:, :], ((0, 128 - GLA_RANK), (0, 0))).astype(BF16).T
    h1, xn1, glow = _out_ab(x, lead, o_a, o_b, w_out_ab[0].astype(BF16), norm_c_w[0], wg, nt)

    modes_c = [("plain", GLA_DK ** -0.5), plain, plain, plain, silu, silu]
    proj1 = _in_proj(xn1, w_in_c_t, modes_c, tm, w_is_transposed=True)
    return _gla(proj1, glow, gla_w_gate[0], gla_b_gate[0], gla_norm_w[0], h1, w_out_c[0].astype(BF16),
                final_norm_w, bsz, nc)
```

```python
import functools
import math

import numpy as np
import jax
import jax.numpy as jnp
from jax import lax
from jax.experimental import pallas as pl
from jax.experimental.pallas import tpu as pltpu

F32 = jnp.float32
BF16 = jnp.bfloat16

N_META = 16
CHUNK = 128
PAD = CHUNK - N_META
EPS = 1e-6
ROW_CHUNKS = 3

RET_HEADS = 8
RET_DK = 128
RET_DV = 256
RET_QK = RET_HEADS * RET_DK
RET_W = RET_HEADS * RET_DV
ROPE_BASE = 10000.0

S5_W = 1024
S5_GH = 16
S5_G = 64
S5_P = 64
S5_F = S5_G * S5_P
S5_SLABS = S5_F // 128
S5_JB = 4
S5_SEGS = 8
S5_SCAN_UNROLL = 3

GLA_HEADS = 4
GLA_DK = 256
GLA_DV = 512
GLA_QK = GLA_HEADS * GLA_DK
GLA_W = GLA_HEADS * GLA_DV
GLA_RANK = 16
GLA_TAU = 16.0
GLA_LEVELS = 7

PROJ_TN = 1024
PROJ_ROW_SPLIT = 4
VMEM_LIMIT = 56 * 1024 * 1024


def _cparams(sem):
    return pltpu.CompilerParams(dimension_semantics=sem, vmem_limit_bytes=VMEM_LIMIT)


def _silu(x):
    return x * (1.0 / (1.0 + jnp.exp(-x)))


def _pick_tile(n, candidates):
    for c in candidates:
        if n % c == 0:
            return c
    raise ValueError(f"no tile for {n}")


def _rms_scale(x, w):
    return x * lax.rsqrt(jnp.mean(x * x, axis=-1, keepdims=True) + EPS) * w


def _chunk_specs(d, nt, n_before):
    del nt
    return [pl.BlockSpec((None, CHUNK, d),
                         lambda b, t, k=k: (b, jnp.maximum(ROW_CHUNKS * t + k - n_before, 0), 0))
            for k in range(ROW_CHUNKS)]


def _embed_norm_kernel(lead_ref, xa_ref, xb_ref, xc_ref, w_ref, o_ref):
    t = pl.program_id(1)
    first = jnp.where(t == 0, lead_ref[...], xa_ref[...])
    for r, rows in enumerate((first, xb_ref[...], xc_ref[...])):
        o_ref[r * CHUNK:(r + 1) * CHUNK, :] = _rms_scale(rows, w_ref[...]).astype(o_ref.dtype)


def _embed_norm(x, lead, w, nt):
    bsz, _, d = x.shape
    tile = ROW_CHUNKS * CHUNK
    return pl.pallas_call(
        _embed_norm_kernel,
        grid=(bsz, nt),
        in_specs=[pl.BlockSpec((CHUNK, d), lambda b, t: (0, 0))] + _chunk_specs(d, nt, 1)
        + [pl.BlockSpec((1, d), lambda b, t: (0, 0))],
        out_specs=pl.BlockSpec((tile, d), lambda b, t: (b * nt + t, 0)),
        out_shape=jax.ShapeDtypeStruct((bsz * nt * tile, d), BF16),
        compiler_params=_cparams(("parallel", "parallel")),
        name="embed_norm",
    )(lead, x, x, x, w.reshape(1, d))


def _rope_rows(a, cos, sin, scale):
    outs = []
    for h in range(a.shape[1] // RET_DK):
        blk = a[:, h * RET_DK:(h + 1) * RET_DK]
        r = blk * cos + pltpu.roll(blk, RET_DK // 2, 1) * sin
        outs.append(r * scale if scale != 1.0 else r)
    return jnp.concatenate(outs, axis=1)


def _in_proj_kernel(*refs, modes, has_rope, w_is_transposed):
    if has_rope:
        x_ref, w_ref, cos_ref, sin_ref, o_ref, wbf_ref = refs
    else:
        x_ref, w_ref, o_ref, wbf_ref = refs
    j = pl.program_id(0)

    @pl.when(pl.program_id(1) == 0)
    def _():
        wbf_ref[...] = w_ref[...].astype(BF16)

    contract = (((1,), (1 if w_is_transposed else 0,)), ((), ()))
    for mode in sorted(set(modes)):
        cond = functools.reduce(jnp.logical_or, [j == jj for jj, mm in enumerate(modes) if mm == mode])

        @pl.when(cond)
        def _(mode=mode):
            kind, scale = mode
            rows = x_ref.shape[0] // PROJ_ROW_SPLIT
            sls = [slice(r * rows, (r + 1) * rows) for r in range(PROJ_ROW_SPLIT)]
            accs = [lax.dot_general(x_ref[sl, :], wbf_ref[...], contract, preferred_element_type=F32)
                    for sl in sls]
            for sl, acc in zip(sls, accs):
                if kind == "rope":
                    acc = _rope_rows(acc, cos_ref[sl, :], sin_ref[sl, :], scale)
                elif kind == "silu":
                    acc = _silu(acc)
                elif scale != 1.0:
                    acc = acc * scale
                o_ref[sl, :] = acc.astype(o_ref.dtype)


def _in_proj(x, w, modes, tm, rope=None, w_is_transposed=False):
    m, k = x.shape
    nt = len(modes)
    if w_is_transposed:
        w_spec = pl.BlockSpec((None, PROJ_TN, k), lambda j, i: (0, j, 0))
        w_scratch = pltpu.VMEM((PROJ_TN, k), BF16)
    else:
        w_spec = pl.BlockSpec((None, k, PROJ_TN), lambda j, i: (0, 0, j))
        w_scratch = pltpu.VMEM((k, PROJ_TN), BF16)
    in_specs = [pl.BlockSpec((tm, k), lambda j, i: (i, 0)), w_spec]
    args = [x, w]
    if rope is not None:
        in_specs += [pl.BlockSpec((tm, RET_DK), lambda j, i: (i, 0))] * 2
        args += list(rope)
    return pl.pallas_call(
        functools.partial(_in_proj_kernel, modes=tuple(modes), has_rope=rope is not None,
                          w_is_transposed=w_is_transposed),
        grid=(nt, m // tm),
        in_specs=in_specs,
        out_specs=pl.BlockSpec((tm, PROJ_TN), lambda j, i: (i, j)),
        out_shape=jax.ShapeDtypeStruct((m, nt * PROJ_TN), BF16),
        scratch_shapes=[w_scratch],
        compiler_params=_cparams(("arbitrary", "arbitrary")),
        name="in_proj",
    )(*args)


def _ret_log_decay(h):
    return math.log1p(-(2.0 ** (-5.0 - h)))


def _retention_kernel(q_ref, k_ref, v_ref, z_ref, nw_ref, o_ref, s_ref, dec_ref, qs_ref, ks_ref):
    first_step = (pl.program_id(0) == 0) & (pl.program_id(1) == 0)

    @pl.when(first_step)
    def _():
        ii = lax.broadcasted_iota(jnp.int32, (CHUNK, CHUNK), 0)
        jj = lax.broadcasted_iota(jnp.int32, (CHUNK, CHUNK), 1)
        diff = (ii - jj).astype(F32)
        row = ii.astype(F32)
        for h in range(RET_HEADS):
            lg = _ret_log_decay(h)
            dec_ref[h] = jnp.where(ii >= jj, jnp.exp(lg * jnp.maximum(diff, 0.0)), 0.0)
            qs_ref[h] = jnp.exp(lg * (row + 1.0))
            ks_ref[h] = jnp.exp(lg * (CHUNK - 1.0 - row))

    @pl.when(pl.program_id(1) == 0)
    def _():
        s_ref[...] = jnp.zeros_like(s_ref)

    heads = range(RET_HEADS)
    vcs = [slice(h * RET_DV, (h + 1) * RET_DV) for h in heads]
    nt_dims = (((1,), (1,)), ((), ()))
    tn_dims = (((0,), (0,)), ((), ()))
    states = [s_ref[h] for h in heads]
    for r in range(q_ref.shape[0] // CHUNK):
        rows = slice(r * CHUNK, (r + 1) * CHUNK)
        qs = [q_ref[rows, h * RET_DK:(h + 1) * RET_DK] for h in heads]
        ks = [k_ref[rows, h * RET_DK:(h + 1) * RET_DK] for h in heads]
        vs = [v_ref[rows, vc] for vc in vcs]
        scores = [lax.dot_general(qs[h], ks[h], nt_dims, preferred_element_type=F32) for h in heads]
        kvs = [lax.dot_general((ks[h].astype(F32) * ks_ref[h]).astype(BF16), vs[h], tn_dims,
                               preferred_element_type=F32) for h in heads]
        lhs = [jnp.concatenate([(scores[h] * dec_ref[h]).astype(BF16),
                                (qs[h].astype(F32) * qs_ref[h]).astype(BF16)], axis=1) for h in heads]
        outs = [jnp.dot(lhs[h], jnp.concatenate([vs[h], states[h].astype(BF16)], axis=0),
                        preferred_element_type=F32) for h in heads]
        states = [states[h] * math.exp(_ret_log_decay(h) * CHUNK) + kvs[h] for h in heads]
        for h in heads:
            o_ref[rows, vcs[h]] = (_rms_scale(outs[h], nw_ref[:, vcs[h]])
                                   * z_ref[rows, vcs[h]].astype(F32)).astype(o_ref.dtype)
    for h in heads:
        s_ref[h] = states[h]


def _retention(proj, norm_w, bsz, nt):
    m = proj.shape[0]
    tile = ROW_CHUNKS * CHUNK
    tab = pltpu.VMEM((RET_HEADS, CHUNK, CHUNK), F32)
    row = lambda width, col: pl.BlockSpec((tile, width), lambda b, t: (b * nt + t, col))
    return pl.pallas_call(
        _retention_kernel,
        grid=(bsz, nt),
        in_specs=[row(RET_QK, 0), row(RET_QK, 1), row(RET_W, 1), row(RET_W, 2),
                  pl.BlockSpec((1, RET_W), lambda b, t: (0, 0))],
        out_specs=row(RET_W, 0),
        out_shape=jax.ShapeDtypeStruct((m, RET_W), BF16),
        scratch_shapes=[pltpu.VMEM((RET_HEADS, RET_DK, RET_DV), F32), tab, tab, tab],
        compiler_params=_cparams(("arbitrary", "arbitrary")),
        name="retention",
    )(proj, proj, proj, proj, norm_w.reshape(1, RET_W))


def _s5_prep_kernel(lre_ref, lim_ref, ldt_ref, bre_ref, bim_ref, cre_ref, cim_ref,
                    pre_ref, pim_ref, wb_ref, wcre_ref, wcim_ref):
    lre = lre_ref[...]
    lim = lim_ref[...]
    dt = jnp.exp(ldt_ref[...])
    tseg = pre_ref.shape[1]
    n = lax.broadcasted_iota(jnp.int32, (tseg, 1), 0).astype(F32) + 1.0
    mag = jnp.exp(n * (lre * dt))
    ang = n * (lim * dt)
    p_re = mag * jnp.cos(ang)
    p_im = mag * jnp.sin(ang)
    for sl in range(S5_SLABS):
        pre_ref[sl] = p_re[:, sl * 128:(sl + 1) * 128]
        pim_ref[sl] = p_im[:, sl * 128:(sl + 1) * 128]
    a_re = p_re[0:1]
    a_im = p_im[0:1]
    den = lre * lre + lim * lim
    nr = a_re - 1.0
    f_re = (nr * lre + a_im * lim) / den
    f_im = (a_im * lre - nr * lim) / den
    b_re = bre_ref[...]
    b_im = bim_ref[...]
    bb_re = f_re * b_re - f_im * b_im
    bb_im = f_re * b_im + f_im * b_re
    blk = S5_F // S5_JB
    rows = lax.broadcasted_iota(jnp.int32, (16 * S5_GH, blk), 0)
    cols = lax.broadcasted_iota(jnp.int32, (16 * S5_GH, blk), 1)
    same_group = (rows // S5_GH) == (cols // S5_P)

    def block_diag(a, j):
        piece = a[:, j * blk:(j + 1) * blk]
        return jnp.where(same_group, jnp.concatenate([piece] * 16, axis=0), 0.0).astype(BF16)

    for j in range(S5_JB):
        wb_ref[j, :, :blk] = block_diag(bb_re, j)
        wb_ref[j, :, blk:] = block_diag(bb_im, j)
        wcre_ref[j] = block_diag(cre_ref[...], j)
        wcim_ref[j] = block_diag(cim_ref[...], j)


def _s5_prep(lam_re, lam_im, log_dt, b_re, b_im, c_re, c_im, tseg):
    flat = lambda a: a.reshape(1, S5_F)
    ldt = jnp.broadcast_to(log_dt[:, None], (S5_G, S5_P))
    bt = lambda a: a.reshape(S5_F, S5_GH).T
    ct = lambda a: a.transpose(1, 0, 2).reshape(S5_GH, S5_F)
    out = jax.ShapeDtypeStruct
    blk = S5_F // S5_JB
    return pl.pallas_call(
        _s5_prep_kernel,
        out_shape=(out((S5_SLABS, tseg, 128), F32), out((S5_SLABS, tseg, 128), F32),
                   out((S5_JB, 16 * S5_GH, 2 * blk), BF16),
                   out((S5_JB, 16 * S5_GH, blk), BF16), out((S5_JB, 16 * S5_GH, blk), BF16)),
        compiler_params=pltpu.CompilerParams(vmem_limit_bytes=VMEM_LIMIT),
        name="s5_prep",
    )(flat(lam_re), flat(lam_im), flat(ldt), bt(b_re), bt(b_im), ct(c_re.astype(F32)), ct(c_im.astype(F32)))


def _gelu_tanh(x):
    return 0.5 * x * (1.0 + jnp.tanh(math.sqrt(2.0 / math.pi) * (x + 0.044715 * (x * x * x))))


def _s5_kernel(u_ref, z_ref, wb_ref, wcre_ref, wcim_ref, pre_ref, pim_ref, d_ref, wglu_ref,
               o_ref, bure_ref, buim_ref, cre_ref, cim_ref, xinre_ref, xinim_ref, wglubf_ref,
               *, tseg, slab_group):
    ci = pl.program_id(1)

    @pl.when((pl.program_id(0) == 0) & (ci == 0))
    def _():
        wglubf_ref[...] = wglu_ref[...].astype(BF16)

    @pl.when(ci == 0)
    def _():
        cre_ref[...] = jnp.zeros_like(cre_ref)
        cim_ref[...] = jnp.zeros_like(cim_ref)

    u_bf = u_ref[...]
    spb = S5_SLABS // S5_JB
    for j in range(S5_JB):
        res = jnp.dot(u_bf[:, j * 256:(j + 1) * 256], wb_ref[j], preferred_element_type=F32)
        for s in range(spb):
            bure_ref[j * spb + s] = res[:, s * 128:(s + 1) * 128]
            buim_ref[j * spb + s] = res[:, (spb + s) * 128:(spb + s + 1) * 128]

    for g0 in range(0, S5_SLABS, slab_group):
        slabs = list(range(g0, g0 + slab_group))
        a_re = [jnp.broadcast_to(pre_ref[sl, 0:1, :], (S5_SEGS, 128)) for sl in slabs]
        a_im = [jnp.broadcast_to(pim_ref[sl, 0:1, :], (S5_SEGS, 128)) for sl in slabs]

        def scan_body(tau, carry, slabs=slabs, a_re=a_re, a_im=a_im):
            xr, xi = carry
            nr, ni = [], []
            for n, sl in enumerate(slabs):
                rows = pl.ds(tau, S5_SEGS, stride=tseg)
                br = bure_ref[sl, rows, :]
                bi = buim_ref[sl, rows, :]
                r = a_re[n] * xr[n] - a_im[n] * xi[n] + br
                i = a_re[n] * xi[n] + a_im[n] * xr[n] + bi
                bure_ref[sl, rows, :] = r
                buim_ref[sl, rows, :] = i
                nr.append(r)
                ni.append(i)
            return tuple(nr), tuple(ni)

        zero = tuple(jnp.zeros((S5_SEGS, 128), F32) for _ in slabs)
        end_re, end_im = lax.fori_loop(0, tseg, scan_body, (zero, zero),
                                       unroll=S5_SCAN_UNROLL if tseg % S5_SCAN_UNROLL == 0 else 1)

        for n, sl in enumerate(slabs):
            at_re = pre_ref[sl, tseg - 1:tseg, :]
            at_im = pim_ref[sl, tseg - 1:tseg, :]
            xr = cre_ref[sl, 0:1, :]
            xi = cim_ref[sl, 0:1, :]
            rows_re, rows_im = [], []
            for s in range(S5_SEGS):
                rows_re.append(xr)
                rows_im.append(xi)
                er = end_re[n][s:s + 1]
                ei = end_im[n][s:s + 1]
                xr, xi = er + at_re * xr - at_im * xi, ei + at_re * xi + at_im * xr
            cre_ref[sl] = jnp.broadcast_to(xr, (S5_SEGS, 128))
            cim_ref[sl] = jnp.broadcast_to(xi, (S5_SEGS, 128))
            xinre_ref[sl] = jnp.concatenate(rows_re, axis=0)
            xinim_ref[sl] = jnp.concatenate(rows_im, axis=0)

    def fix_body(sl, carry):
        xin_re = xinre_ref[sl]
        xin_im = xinim_ref[sl]
        for tau in range(tseg):
            rows = pl.ds(tau, S5_SEGS, stride=tseg)
            pr = pre_ref[sl, tau:tau + 1, :]
            pi = pim_ref[sl, tau:tau + 1, :]
            bure_ref[sl, rows, :] = bure_ref[sl, rows, :] + (pr * xin_re - pi * xin_im)
            buim_ref[sl, rows, :] = buim_ref[sl, rows, :] + (pr * xin_im + pi * xin_re)
        return carry

    lax.fori_loop(0, S5_SLABS, fix_body, 0)

    ys = []
    for j in range(S5_JB):
        x_re = jnp.concatenate([bure_ref[j * spb + s] for s in range(spb)], axis=1).astype(BF16)
        x_im = jnp.concatenate([buim_ref[j * spb + s] for s in range(spb)], axis=1).astype(BF16)
        nt_dims = (((1,), (1,)), ((), ()))
        ys.append(lax.dot_general(x_re, wcre_ref[j], nt_dims, preferred_element_type=F32)
                  - lax.dot_general(x_im, wcim_ref[j], nt_dims, preferred_element_type=F32))
    y = jnp.concatenate(ys, axis=1) + d_ref[...] * u_bf.astype(F32)
    y = _gelu_tanh(y)
    gate = jnp.dot(y.astype(BF16), wglubf_ref[...], preferred_element_type=F32)
    y = y * (1.0 / (1.0 + jnp.exp(-gate)))
    o_ref[...] = (y * z_ref[...].astype(F32)).astype(o_ref.dtype)


def _s5_time_tile(rows_per_batch):
    for tseg in (66, 44, 132, 12, 4, 6, 2):
        t = S5_SEGS * tseg
        if rows_per_batch % t == 0 and t % 16 == 0:
            return t, tseg
    raise ValueError(f"no S5 time tile for {rows_per_batch}")


def _s5(proj, bsz, wb, wcre, wcim, p_re, p_im, d, wglu, t, tseg):
    m = proj.shape[0]
    nt = m // bsz // t
    ucol = (2 * RET_QK + 2 * RET_W) // S5_W
    const = lambda *shape: pl.BlockSpec(shape, lambda b, i: (0,) * len(shape), pipeline_mode=pl.Buffered(1))
    return pl.pallas_call(
        functools.partial(_s5_kernel, tseg=tseg, slab_group=8),
        grid=(bsz, nt),
        in_specs=[pl.BlockSpec((t, S5_W), lambda b, i: (b * nt + i, ucol)),
                  pl.BlockSpec((t, S5_W), lambda b, i: (b * nt + i, ucol + 1)),
                  const(*wb.shape), const(*wcre.shape), const(*wcim.shape),
                  const(S5_SLABS, tseg, 128), const(S5_SLABS, tseg, 128), const(1, S5_W),
                  const(S5_W, S5_W)],
        out_specs=pl.BlockSpec((t, S5_W), lambda b, i: (b * nt + i, 0)),
        out_shape=jax.ShapeDtypeStruct((m, S5_W), BF16),
        scratch_shapes=[pltpu.VMEM((S5_SLABS, t, 128), F32), pltpu.VMEM((S5_SLABS, t, 128), F32)]
        + [pltpu.VMEM((S5_SLABS, S5_SEGS, 128), F32)] * 4 + [pltpu.VMEM((S5_W, S5_W), BF16)],
        compiler_params=_cparams(("arbitrary", "arbitrary")),
        name="s5",
    )(proj, proj, wb, wcre, wcim, p_re, p_im, d.reshape(1, S5_W), wglu)


def _out_ab_kernel(lead_ref, xa_ref, xb_ref, xc_ref, oa_ref, ob_ref, w_ref, nw_ref, wg_ref,
                   h1_ref, xn_ref, gl_ref):
    t = pl.program_id(1)
    first = jnp.where(t == 0, lead_ref[...], xa_ref[...])
    h0 = (first, xb_ref[...], xc_ref[...])
    sls = [slice(r * CHUNK, (r + 1) * CHUNK) for r in range(ROW_CHUNKS)]
    accs = [jnp.dot(oa_ref[sl, :], w_ref[:RET_W, :], preferred_element_type=F32)
            + jnp.dot(ob_ref[sl, :], w_ref[RET_W:, :], preferred_element_type=F32) for sl in sls]
    h1s = [h0[r] + accs[r] for r in range(ROW_CHUNKS)]
    xns = [_rms_scale(h1, nw_ref[...]).astype(BF16) for h1 in h1s]
    gls = [jnp.dot(xn, wg_ref[...], preferred_element_type=F32) for xn in xns]
    for r, sl in enumerate(sls):
        h1_ref[sl, :] = h1s[r]
        xn_ref[sl, :] = xns[r]
        gl_ref[sl, :] = gls[r]


def _out_ab(x, lead, oa, ob, w, norm_w, wg, nt):
    bsz, _, d = x.shape
    tile = ROW_CHUNKS * CHUNK
    m = bsz * nt * tile
    row = lambda width: pl.BlockSpec((tile, width), lambda b, t: (b * nt + t, 0))
    const = lambda a: pl.BlockSpec(a.shape, lambda b, t: (0, 0))
    out = jax.ShapeDtypeStruct
    return pl.pallas_call(
        _out_ab_kernel,
        grid=(bsz, nt),
        in_specs=[pl.BlockSpec((CHUNK, d), lambda b, t: (0, 0))] + _chunk_specs(d, nt, 1)
        + [row(RET_W), row(S5_W), const(w), pl.BlockSpec((1, d), lambda b, t: (0, 0)), const(wg)],
        out_specs=[row(d), row(d), row(128)],
        out_shape=(out((m, d), F32), out((m, d), BF16), out((m, 128), F32)),
        compiler_params=_cparams(("parallel", "parallel")),
        name="out_proj_ab",
    )(lead, x, x, x, oa, ob, w, norm_w.reshape(1, d), wg)


LOG2E = 1.4426950408889634


def _split2(x):
    x1 = x.astype(BF16)
    return x1, (x - x1.astype(F32)).astype(BF16)


def _neg_abs(x):
    return -jnp.abs(x)


def _block_ref_rows(b, level):
    rows, width = b.shape
    half = 1 << level
    blk = 2 * half
    if blk >= 8:
        b3 = b.reshape(rows // blk, blk, width)
        ref = jnp.broadcast_to(b3[:, half - 1:half, :], b3.shape)
        return ref.reshape(rows, width)
    b3 = b.reshape(rows // 8, 8, width)
    sub = lax.broadcasted_iota(jnp.int32, b3.shape, 1)
    pick = lambda r: jnp.broadcast_to(b3[:, r:r + 1, :], b3.shape)
    if blk == 4:
        ref = jnp.where(sub < 4, pick(1), pick(5))
    else:
        ref = jnp.where(sub < 2, pick(0), jnp.where(sub < 4, pick(2), jnp.where(sub < 6, pick(4), pick(6))))
    return ref.reshape(rows, width)


def _gla_kernel(q_ref, k_ref, v_ref, z_ref, gl_ref, wg_ref, bg_ref, nw_ref, h_ref, wo_ref, fw_ref,
                y_ref, st_ref, mix_ref, wobf_ref, *, nc, nchunks):
    c = pl.program_id(0)
    n = jnp.minimum(c, nchunks - 1) % nc

    @pl.when(c == 0)
    def _():
        mix_ref[...] = jnp.zeros_like(mix_ref)
        wobf_ref[...] = wo_ref[...].astype(BF16)

    @pl.when(n == 0)
    def _():
        st_ref[...] = jnp.zeros_like(st_ref)

    ridx = lax.broadcasted_iota(jnp.int32, (CHUNK, 1), 0)
    ii = lax.broadcasted_iota(jnp.int32, (CHUNK, CHUNK), 0)
    jj = lax.broadcasted_iota(jnp.int32, (CHUNK, CHUNK), 1)
    pair_code = jnp.where(ii > jj, ii ^ jj, 0)
    eye = ii == jj
    tri = (ii >= jj).astype(BF16)

    gl1, gl2 = _split2(gl_ref[:, :GLA_RANK])
    wg1, wg2 = _split2(wg_ref[...])
    x = jnp.dot(jnp.concatenate([gl1, gl2, gl1], axis=1), jnp.concatenate([wg1, wg1, wg2], axis=0),
                preferred_element_type=F32) + bg_ref[...]
    log_a = (jnp.minimum(x, 0.0) - jnp.log(1.0 + jnp.exp(-jnp.abs(x)))) * (LOG2E / GLA_TAU)
    log_a = jnp.where(ridx >= jnp.where(n > 0, 0, PAD), log_a, 0.0)
    g1, g2 = _split2(log_a)
    b_all = jnp.dot(jnp.concatenate([tri, tri], axis=1), jnp.concatenate([g1, g2], axis=0),
                    preferred_element_type=F32)
    odd = (ridx & 1) == 1
    h2 = h_ref[...]
    for h0 in range(0, GLA_HEADS, 2):
        heads = (h0, h0 + 1)
        qs, ks, bs, scores = [], [], [], []
        for h in heads:
            kc = slice(h * GLA_DK, (h + 1) * GLA_DK)
            vc = slice(h * GLA_DV, (h + 1) * GLA_DV)
            h2 = h2 + jnp.dot(mix_ref[:, vc], wobf_ref[vc, :], preferred_element_type=F32)
            qs.append(q_ref[:, kc].astype(F32))
            ks.append(k_ref[:, kc].astype(F32))
            bs.append(b_all[:, kc])
            scores.append(jnp.where(eye, jnp.sum(qs[-1] * ks[-1], axis=1, keepdims=True), 0.0))
        for lv in range(GLA_LEVELS):
            half = 1 << lv
            zz = []
            for q, k, b in zip(qs, ks, bs):
                if lv == 0:
                    expo = jnp.where(odd, b - pltpu.roll(b, 1, 0), 0.0)
                else:
                    expo = _neg_abs(b - _block_ref_rows(b, lv))
                if half >= 8:
                    sel = jnp.concatenate([(q if r % 2 else k)[r * half:(r + 1) * half]
                                           for r in range(CHUNK // half)], axis=0)
                else:
                    sel = jnp.where(((ridx >> lv) & 1) == 1, q, k)
                zz.append((sel * jnp.exp2(expo)).astype(BF16))
            z2 = jnp.concatenate(zz, axis=0)
            gram = lax.dot_general(z2, z2, (((1,), (1,)), ((), ())), preferred_element_type=F32)
            mask = (pair_code >> lv) == 1
            for i in range(2):
                blk = gram[i * CHUNK:(i + 1) * CHUNK, i * CHUNK:(i + 1) * CHUNK]
                scores[i] = scores[i] + jnp.where(mask, blk, 0.0)
        for i, h in enumerate(heads):
            vc = slice(h * GLA_DV, (h + 1) * GLA_DV)
            q, k, b = qs[i], ks[i], bs[i]
            v = v_ref[:, vc]
            b_last = b[CHUNK - 1:CHUNK, :]
            o = jnp.dot(scores[i].astype(BF16), v, preferred_element_type=F32)
            st = st_ref[h]
            qe = (q * jnp.exp2(b)).astype(BF16)
            o = o + lax.dot_general(qe, st.astype(BF16), (((1,), (1,)), ((), ())), preferred_element_type=F32)
            ke = (k * jnp.exp2(b_last - b)).astype(BF16)
            kv_t = lax.dot_general(v, ke, (((0,), (0,)), ((), ())), preferred_element_type=F32)
            st_ref[h] = st * jnp.exp2(b_last) + kv_t
            mix_ref[:, vc] = (_rms_scale(o, nw_ref[:, vc]) * z_ref[:, vc].astype(F32)).astype(BF16)

    y_ref[...] = _rms_scale(h2, fw_ref[...])


def _gla(proj, glow, w_gate, b_gate, norm_w, h1, w_out, final_w, bsz, nc):
    m, d = h1.shape
    nchunks = m // CHUNK
    const = lambda *shape: pl.BlockSpec(shape, lambda c: (0,) * len(shape))
    cur = lambda width, col: pl.BlockSpec((CHUNK, width), lambda c: (jnp.minimum(c, nchunks - 1), col))
    prev = lambda c: jnp.maximum(c - 1, 0)
    return pl.pallas_call(
        functools.partial(_gla_kernel, nc=nc, nchunks=nchunks),
        grid=(nchunks + 1,),
        in_specs=[cur(GLA_QK, 0), cur(GLA_QK, 1), cur(GLA_W, 1), cur(GLA_W, 2), cur(128, 0),
                  const(GLA_RANK, GLA_QK), const(1, GLA_QK), const(1, GLA_W),
                  pl.BlockSpec((CHUNK, d), lambda c: (prev(c), 0)),
                  pl.BlockSpec((None, GLA_W, d), lambda c: (0, 0, 0), pipeline_mode=pl.Buffered(1)),
                  const(1, d)],
        out_specs=pl.BlockSpec((None, CHUNK, d),
                               lambda c: (prev(c) // nc, jnp.maximum(prev(c) % nc - 1, 0), 0)),
        out_shape=jax.ShapeDtypeStruct((bsz, (nc - 1) * CHUNK, d), F32),
        scratch_shapes=[pltpu.VMEM((GLA_HEADS, GLA_DV, GLA_DK), F32), pltpu.VMEM((CHUNK, GLA_W), BF16),
                        pltpu.VMEM((GLA_W, d), BF16)],
        compiler_params=_cparams(("arbitrary",)),
        name="gla_out",
    )(proj, proj, proj, proj, glow, w_gate.astype(F32), b_gate.reshape(1, GLA_QK).astype(F32),
      norm_w.reshape(1, GLA_W).astype(F32), h1, w_out, final_w.reshape(1, d).astype(F32))


def _rope_tables(bsz, rows_per_batch):
    pos = np.maximum(np.arange(rows_per_batch, dtype=np.float64) - PAD, 0.0)
    inv_freq = np.power(ROPE_BASE, -np.arange(0, RET_DK, 2, dtype=np.float64) / RET_DK)
    ang = pos[:, None] * inv_freq[None, :]
    cos, sin = np.cos(ang), np.sin(ang)
    cos2 = np.tile(np.concatenate([cos, cos], axis=1), (bsz, 1)).astype(np.float32)
    sin2 = np.tile(np.concatenate([-sin, sin], axis=1), (bsz, 1)).astype(np.float32)
    return jnp.asarray(cos2), jnp.asarray(sin2)


def kernel(x, meta, norm_ab_w, w_in_ab, ret_norm_w, s5_lam_re, s5_lam_im, s5_log_dt, s5_b_re, s5_b_im,
           s5_c_re, s5_c_im, s5_d, s5_w_glu, w_out_ab, norm_c_w, w_in_c, gla_w_gate, gla_b_gate,
           gla_norm_w, w_out_c, final_norm_w):
    bsz, seq, d = x.shape
    assert seq % CHUNK == 0 and w_in_ab.shape[0] == 1 and w_in_c.shape[0] == 1
    rpb = seq + CHUNK
    nc = rpb // CHUNK
    assert nc % ROW_CHUNKS == 0
    nt = nc // ROW_CHUNKS
    m = bsz * rpb
    tm = _pick_tile(m, (1408, 768, 384, 256, 128))
    lead = jnp.concatenate([jnp.zeros((PAD, d), x.dtype), meta.astype(x.dtype)], axis=0)

    xn0 = _embed_norm(x, lead, norm_ab_w[0], nt)
    plain, silu = ("plain", 1.0), ("silu", 1.0)
    modes_ab = [("rope", 1.0), ("rope", RET_DK ** -0.5), plain, plain, silu, silu, plain, silu]
    proj0 = _in_proj(xn0, w_in_ab, modes_ab, tm, rope=_rope_tables(bsz, rpb))
    o_a = _retention(proj0, ret_norm_w[0], bsz, nt)
    t, tseg = _s5_time_tile(rpb)
    p_re, p_im, wb, wcre, wcim = _s5_prep(s5_lam_re[0], s5_lam_im[0], s5_log_dt[0], s5_b_re[0], s5_b_im[0],
                                          s5_c_re[0], s5_c_im[0], tseg)
    o_b = _s5(proj0, bsz, wb, wcre, wcim, p_re, p_im, s5_d[0], s5_w_glu[0], t, tseg)

    w_in_c_t = jnp.swapaxes(w_in_c, 1, 2)
    n_main = 2 * GLA_QK + 2 * GLA_W
    wg = jnp.pad(w_in_c_t[0, n_main:, :], ((0, 128 - GLA_RANK), (0, 0))).astype(BF16).T
    h1, xn1, glow = _out_ab(x, lead, o_a, o_b, w_out_ab[0].astype(BF16), norm_c_w[0], wg, nt)

    modes_c = [("plain", GLA_DK ** -0.5), plain, plain, plain, silu, silu]
    proj1 = _in_proj(xn1, w_in_c_t, modes_c, tm, w_is_transposed=True)
    return _gla(proj1, glow, gla_w_gate[0], gla_b_gate[0], gla_norm_w[0], h1, w_out_c,
                final_norm_w, bsz, nc)
```

```python
import functools
import math

import numpy as np
import jax
import jax.numpy as jnp
from jax import lax
from jax.experimental import pallas as pl
from jax.experimental.pallas import tpu as pltpu

F32 = jnp.float32
BF16 = jnp.bfloat16

N_META = 16
LANES = 128
SUBLANES = 8
CHUNK = 128
PAD = CHUNK - N_META
EPS = 1e-6
ROW_CHUNKS = 3

RET_HEADS = 8
RET_DK = 128
RET_DV = 256
RET_QK = RET_HEADS * RET_DK
RET_W = RET_HEADS * RET_DV
ROPE_BASE = 10000.0

S5_W = 1024
S5_GH = 16
S5_G = 64
S5_P = 64
S5_F = S5_G * S5_P
S5_SLABS = S5_F // LANES
S5_JB = 4
S5_BLK_IN = S5_W // S5_JB
S5_SEGS = SUBLANES
S5_SCAN_UNROLL = 6

GLA_HEADS = 4
GLA_DK = 256
GLA_DV = 512
GLA_QK = GLA_HEADS * GLA_DK
GLA_W = GLA_HEADS * GLA_DV
GLA_RANK = 16
GLA_TAU = 16.0
GLA_LEVELS = 7
GATE_PAD = LANES

PROJ_TN = 1024
PROJ_ROW_SPLIT = 4
VMEM_LIMIT = 56 * 1024 * 1024


def _cparams(sem):
    return pltpu.CompilerParams(dimension_semantics=sem, vmem_limit_bytes=VMEM_LIMIT)


def _silu(x):
    return x * (1.0 / (1.0 + jnp.exp(-x)))


def _pick_tile(n, candidates):
    for c in candidates:
        if n % c == 0:
            return c
    raise ValueError(f"no tile for {n}")


def _rms_scale(x, w):
    return x * lax.rsqrt(jnp.mean(x * x, axis=-1, keepdims=True) + EPS) * w


def _chunk_specs(d, n_before):
    return [pl.BlockSpec((None, CHUNK, d),
                         lambda b, t, k=k: (b, jnp.maximum(ROW_CHUNKS * t + k - n_before, 0), 0))
            for k in range(ROW_CHUNKS)]


def _embed_norm_kernel(lead_ref, xa_ref, xb_ref, xc_ref, w_ref, o_ref):
    t = pl.program_id(1)
    first = jnp.where(t == 0, lead_ref[...], xa_ref[...])
    for r, rows in enumerate((first, xb_ref[...], xc_ref[...])):
        o_ref[r * CHUNK:(r + 1) * CHUNK, :] = _rms_scale(rows, w_ref[...]).astype(o_ref.dtype)


def _embed_norm(x, lead, w, nt):
    bsz, _, d = x.shape
    tile = ROW_CHUNKS * CHUNK
    return pl.pallas_call(
        _embed_norm_kernel,
        grid=(bsz, nt),
        in_specs=[pl.BlockSpec((CHUNK, d), lambda b, t: (0, 0))] + _chunk_specs(d, 1)
        + [pl.BlockSpec((1, d), lambda b, t: (0, 0))],
        out_specs=pl.BlockSpec((tile, d), lambda b, t: (b * nt + t, 0)),
        out_shape=jax.ShapeDtypeStruct((bsz * nt * tile, d), BF16),
        compiler_params=_cparams(("parallel", "parallel")),
        name="embed_norm",
    )(lead, x, x, x, w.reshape(1, d))


def _rope_rows(a, cos, sin, scale):
    outs = []
    for h in range(a.shape[1] // RET_DK):
        blk = a[:, h * RET_DK:(h + 1) * RET_DK]
        r = blk * cos + pltpu.roll(blk, RET_DK // 2, 1) * sin
        outs.append(r * scale if scale != 1.0 else r)
    return jnp.concatenate(outs, axis=1)


def _in_proj_kernel(*refs, modes, has_rope, w_is_transposed):
    if has_rope:
        x_ref, w_ref, cos_ref, sin_ref, o_ref, wbf_ref = refs
    else:
        x_ref, w_ref, o_ref, wbf_ref = refs
    j = pl.program_id(0)

    @pl.when(pl.program_id(1) == 0)
    def _():
        wbf_ref[...] = w_ref[...].astype(BF16)

    contract = (((1,), (1 if w_is_transposed else 0,)), ((), ()))
    for mode in sorted(set(modes)):
        cond = functools.reduce(jnp.logical_or, [j == jj for jj, mm in enumerate(modes) if mm == mode])

        @pl.when(cond)
        def _(mode=mode):
            kind, scale = mode
            rows = x_ref.shape[0] // PROJ_ROW_SPLIT
            sls = [slice(r * rows, (r + 1) * rows) for r in range(PROJ_ROW_SPLIT)]
            accs = [lax.dot_general(x_ref[sl, :], wbf_ref[...], contract, preferred_element_type=F32)
                    for sl in sls]
            for sl, acc in zip(sls, accs):
                if kind == "rope":
                    acc = _rope_rows(acc, cos_ref[sl, :], sin_ref[sl, :], scale)
                elif kind == "silu":
                    acc = _silu(acc)
                elif scale != 1.0:
                    acc = acc * scale
                o_ref[sl, :] = acc.astype(o_ref.dtype)


def _in_proj(x, w, modes, tm, rope=None, w_is_transposed=False):
    m, k = x.shape
    nt = len(modes)
    assert m % tm == 0 and tm % (PROJ_ROW_SPLIT * 16) == 0
    if w_is_transposed:
        w_spec = pl.BlockSpec((None, PROJ_TN, k), lambda j, i: (0, j, 0))
        w_scratch = pltpu.VMEM((PROJ_TN, k), BF16)
    else:
        w_spec = pl.BlockSpec((None, k, PROJ_TN), lambda j, i: (0, 0, j))
        w_scratch = pltpu.VMEM((k, PROJ_TN), BF16)
    in_specs = [pl.BlockSpec((tm, k), lambda j, i: (i, 0)), w_spec]
    args = [x, w]
    if rope is not None:
        in_specs += [pl.BlockSpec((tm, RET_DK), lambda j, i: (i, 0))] * 2
        args += list(rope)
    return pl.pallas_call(
        functools.partial(_in_proj_kernel, modes=tuple(modes), has_rope=rope is not None,
                          w_is_transposed=w_is_transposed),
        grid=(nt, m // tm),
        in_specs=in_specs,
        out_specs=pl.BlockSpec((tm, PROJ_TN), lambda j, i: (i, j)),
        out_shape=jax.ShapeDtypeStruct((m, nt * PROJ_TN), BF16),
        scratch_shapes=[w_scratch],
        compiler_params=_cparams(("arbitrary", "arbitrary")),
        name="in_proj",
    )(*args)


def _ret_log_decay(h):
    return math.log1p(-(2.0 ** (-5.0 - h)))


def _retention_kernel(q_ref, k_ref, v_ref, z_ref, nw_ref, o_ref, s_ref, dec_ref, qs_ref, ks_ref):
    first_step = (pl.program_id(0) == 0) & (pl.program_id(1) == 0)

    @pl.when(first_step)
    def _():
        ii = lax.broadcasted_iota(jnp.int32, (CHUNK, CHUNK), 0)
        jj = lax.broadcasted_iota(jnp.int32, (CHUNK, CHUNK), 1)
        diff = (ii - jj).astype(F32)
        row = ii.astype(F32)
        for h in range(RET_HEADS):
            lg = _ret_log_decay(h)
            dec_ref[h] = jnp.where(ii >= jj, jnp.exp(lg * jnp.maximum(diff, 0.0)), 0.0)
            qs_ref[h] = jnp.exp(lg * (row + 1.0))
            ks_ref[h] = jnp.exp(lg * (CHUNK - 1.0 - row))

    @pl.when(pl.program_id(1) == 0)
    def _():
        s_ref[...] = jnp.zeros_like(s_ref)

    heads = range(RET_HEADS)
    vcs = [slice(h * RET_DV, (h + 1) * RET_DV) for h in heads]
    nt_dims = (((1,), (1,)), ((), ()))
    tn_dims = (((0,), (0,)), ((), ()))
    states = [s_ref[h] for h in heads]
    for r in range(q_ref.shape[0] // CHUNK):
        rows = slice(r * CHUNK, (r + 1) * CHUNK)
        qs = [q_ref[rows, h * RET_DK:(h + 1) * RET_DK] for h in heads]
        ks = [k_ref[rows, h * RET_DK:(h + 1) * RET_DK] for h in heads]
        vs = [v_ref[rows, vc] for vc in vcs]
        scores = [lax.dot_general(qs[h], ks[h], nt_dims, preferred_element_type=F32) for h in heads]
        kvs = [lax.dot_general((ks[h].astype(F32) * ks_ref[h]).astype(BF16), vs[h], tn_dims,
                               preferred_element_type=F32) for h in heads]
        lhs = [jnp.concatenate([(scores[h] * dec_ref[h]).astype(BF16),
                                (qs[h].astype(F32) * qs_ref[h]).astype(BF16)], axis=1) for h in heads]
        outs = [jnp.dot(lhs[h], jnp.concatenate([vs[h], states[h].astype(BF16)], axis=0),
                        preferred_element_type=F32) for h in heads]
        states = [states[h] * math.exp(_ret_log_decay(h) * CHUNK) + kvs[h] for h in heads]
        for h in heads:
            o_ref[rows, vcs[h]] = (_rms_scale(outs[h], nw_ref[:, vcs[h]])
                                   * z_ref[rows, vcs[h]].astype(F32)).astype(o_ref.dtype)
    for h in heads:
        s_ref[h] = states[h]


def _retention(proj, norm_w, bsz, nt):
    m = proj.shape[0]
    tile = ROW_CHUNKS * CHUNK
    tab = pltpu.VMEM((RET_HEADS, CHUNK, CHUNK), F32)
    row = lambda width, col: pl.BlockSpec((tile, width), lambda b, t: (b * nt + t, col))
    return pl.pallas_call(
        _retention_kernel,
        grid=(bsz, nt),
        in_specs=[row(RET_QK, 0), row(RET_QK, 1), row(RET_W, 1), row(RET_W, 2),
                  pl.BlockSpec((1, RET_W), lambda b, t: (0, 0))],
        out_specs=row(RET_W, 0),
        out_shape=jax.ShapeDtypeStruct((m, RET_W), BF16),
        scratch_shapes=[pltpu.VMEM((RET_HEADS, RET_DK, RET_DV), F32), tab, tab, tab],
        compiler_params=_cparams(("arbitrary", "arbitrary")),
        name="retention",
    )(proj, proj, proj, proj, norm_w.reshape(1, RET_W))


def _s5_prep_kernel(lre_ref, lim_ref, ldt_ref, bre_ref, bim_ref, cre_ref, cim_ref,
                    pre_ref, pim_ref, wb_ref, wcre_ref, wcim_ref):
    lre = lre_ref[...]
    lim = lim_ref[...]
    dt = jnp.exp(ldt_ref[...])
    tseg = pre_ref.shape[1]
    n = lax.broadcasted_iota(jnp.int32, (tseg, 1), 0).astype(F32) + 1.0
    mag = jnp.exp(n * (lre * dt))
    ang = n * (lim * dt)
    p_re = mag * jnp.cos(ang)
    p_im = mag * jnp.sin(ang)
    for sl in range(S5_SLABS):
        pre_ref[sl] = p_re[:, sl * LANES:(sl + 1) * LANES]
        pim_ref[sl] = p_im[:, sl * LANES:(sl + 1) * LANES]
    a_re = p_re[0:1]
    a_im = p_im[0:1]
    den = lre * lre + lim * lim
    nr = a_re - 1.0
    f_re = (nr * lre + a_im * lim) / den
    f_im = (a_im * lre - nr * lim) / den
    b_re = bre_ref[...]
    b_im = bim_ref[...]
    bb_re = f_re * b_re - f_im * b_im
    bb_im = f_re * b_im + f_im * b_re
    blk = S5_F // S5_JB
    rows = lax.broadcasted_iota(jnp.int32, (16 * S5_GH, blk), 0)
    cols = lax.broadcasted_iota(jnp.int32, (16 * S5_GH, blk), 1)
    same_group = (rows // S5_GH) == (cols // S5_P)

    def block_diag(a, j):
        piece = a[:, j * blk:(j + 1) * blk]
        return jnp.where(same_group, jnp.concatenate([piece] * 16, axis=0), 0.0).astype(BF16)

    for j in range(S5_JB):
        wb_ref[j, :, :blk] = block_diag(bb_re, j)
        wb_ref[j, :, blk:] = block_diag(bb_im, j)
        wcre_ref[j] = block_diag(cre_ref[...], j)
        wcim_ref[j] = block_diag(cim_ref[...], j)


def _s5_prep(lam_re, lam_im, log_dt, b_re, b_im, c_re, c_im, tseg):
    flat = lambda a: a.reshape(1, S5_F)
    ldt = jnp.broadcast_to(log_dt[:, None], (S5_G, S5_P))
    bt = lambda a: a.reshape(S5_F, S5_GH).T
    ct = lambda a: a.transpose(1, 0, 2).reshape(S5_GH, S5_F)
    out = jax.ShapeDtypeStruct
    blk = S5_F // S5_JB
    return pl.pallas_call(
        _s5_prep_kernel,
        out_shape=(out((S5_SLABS, tseg, LANES), F32), out((S5_SLABS, tseg, LANES), F32),
                   out((S5_JB, 16 * S5_GH, 2 * blk), BF16),
                   out((S5_JB, 16 * S5_GH, blk), BF16), out((S5_JB, 16 * S5_GH, blk), BF16)),
        compiler_params=pltpu.CompilerParams(vmem_limit_bytes=VMEM_LIMIT),
        name="s5_prep",
    )(flat(lam_re), flat(lam_im), flat(ldt), bt(b_re), bt(b_im), ct(c_re.astype(F32)), ct(c_im.astype(F32)))


def _gelu_tanh(x):
    return 0.5 * x * (1.0 + jnp.tanh(math.sqrt(2.0 / math.pi) * (x + 0.044715 * (x * x * x))))


def _s5_kernel(u_ref, z_ref, wb_ref, wcre_ref, wcim_ref, pre_ref, pim_ref, d_ref, wglu_ref,
               o_ref, bure_ref, buim_ref, cre_ref, cim_ref, xinre_ref, xinim_ref, wglubf_ref,
               *, tseg, slab_group):
    ci = pl.program_id(1)

    @pl.when((pl.program_id(0) == 0) & (ci == 0))
    def _():
        wglubf_ref[...] = wglu_ref[...].astype(BF16)

    @pl.when(ci == 0)
    def _():
        cre_ref[...] = jnp.zeros_like(cre_ref)
        cim_ref[...] = jnp.zeros_like(cim_ref)

    u_bf = u_ref[...]
    spb = S5_SLABS // S5_JB
    for j in range(S5_JB):
        res = jnp.dot(u_bf[:, j * S5_BLK_IN:(j + 1) * S5_BLK_IN], wb_ref[j], preferred_element_type=F32)
        for s in range(spb):
            bure_ref[j * spb + s] = res[:, s * LANES:(s + 1) * LANES]
            buim_ref[j * spb + s] = res[:, (spb + s) * LANES:(spb + s + 1) * LANES]

    for g0 in range(0, S5_SLABS, slab_group):
        slabs = list(range(g0, g0 + slab_group))
        a_re = [jnp.broadcast_to(pre_ref[sl, 0:1, :], (S5_SEGS, LANES)) for sl in slabs]
        a_im = [jnp.broadcast_to(pim_ref[sl, 0:1, :], (S5_SEGS, LANES)) for sl in slabs]

        def scan_body(tau, carry, slabs=slabs, a_re=a_re, a_im=a_im):
            xr, xi = carry
            nr, ni = [], []
            for n, sl in enumerate(slabs):
                rows = pl.ds(tau, S5_SEGS, stride=tseg)
                br = bure_ref[sl, rows, :]
                bi = buim_ref[sl, rows, :]
                r = a_re[n] * xr[n] - a_im[n] * xi[n] + br
                i = a_re[n] * xi[n] + a_im[n] * xr[n] + bi
                bure_ref[sl, rows, :] = r
                buim_ref[sl, rows, :] = i
                nr.append(r)
                ni.append(i)
            return tuple(nr), tuple(ni)

        zero = tuple(jnp.zeros((S5_SEGS, LANES), F32) for _ in slabs)
        end_re, end_im = lax.fori_loop(0, tseg, scan_body, (zero, zero),
                                       unroll=S5_SCAN_UNROLL if tseg % S5_SCAN_UNROLL == 0 else 1)

        for n, sl in enumerate(slabs):
            at_re = pre_ref[sl, tseg - 1:tseg, :]
            at_im = pim_ref[sl, tseg - 1:tseg, :]
            xr = cre_ref[sl, 0:1, :]
            xi = cim_ref[sl, 0:1, :]
            rows_re, rows_im = [], []
            for s in range(S5_SEGS):
                rows_re.append(xr)
                rows_im.append(xi)
                er = end_re[n][s:s + 1]
                ei = end_im[n][s:s + 1]
                xr, xi = er + at_re * xr - at_im * xi, ei + at_re * xi + at_im * xr
            cre_ref[sl] = jnp.broadcast_to(xr, (S5_SEGS, LANES))
            cim_ref[sl] = jnp.broadcast_to(xi, (S5_SEGS, LANES))
            xinre_ref[sl] = jnp.concatenate(rows_re, axis=0)
            xinim_ref[sl] = jnp.concatenate(rows_im, axis=0)

    def fix_body(sl, carry):
        xin_re = xinre_ref[sl]
        xin_im = xinim_ref[sl]
        for tau in range(tseg):
            rows = pl.ds(tau, S5_SEGS, stride=tseg)
            pr = pre_ref[sl, tau:tau + 1, :]
            pi = pim_ref[sl, tau:tau + 1, :]
            bure_ref[sl, rows, :] = bure_ref[sl, rows, :] + (pr * xin_re - pi * xin_im)
            buim_ref[sl, rows, :] = buim_ref[sl, rows, :] + (pr * xin_im + pi * xin_re)
        return carry

    lax.fori_loop(0, S5_SLABS, fix_body, 0)

    ys = []
    for j in range(S5_JB):
        x_re = jnp.concatenate([bure_ref[j * spb + s] for s in range(spb)], axis=1).astype(BF16)
        x_im = jnp.concatenate([buim_ref[j * spb + s] for s in range(spb)], axis=1).astype(BF16)
        nt_dims = (((1,), (1,)), ((), ()))
        ys.append(lax.dot_general(x_re, wcre_ref[j], nt_dims, preferred_element_type=F32)
                  - lax.dot_general(x_im, wcim_ref[j], nt_dims, preferred_element_type=F32))
    y = jnp.concatenate(ys, axis=1) + d_ref[...] * u_bf.astype(F32)
    y = _gelu_tanh(y)
    gate = jnp.dot(y.astype(BF16), wglubf_ref[...], preferred_element_type=F32)
    y = y * (1.0 / (1.0 + jnp.exp(-gate)))
    o_ref[...] = (y * z_ref[...].astype(F32)).astype(o_ref.dtype)


def _s5_time_tile(rows_per_batch):
    for tseg in (66, 44, 132, 12, 4, 6, 2):
        t = S5_SEGS * tseg
        if rows_per_batch % t == 0 and t % 16 == 0:
            return t, tseg
    raise ValueError(f"no S5 time tile for {rows_per_batch}")


def _s5(proj, bsz, wb, wcre, wcim, p_re, p_im, d, wglu, t, tseg):
    m = proj.shape[0]
    nt = m // bsz // t
    ucol = (2 * RET_QK + 2 * RET_W) // S5_W
    const = lambda *shape: pl.BlockSpec(shape, lambda b, i: (0,) * len(shape), pipeline_mode=pl.Buffered(1))
    return pl.pallas_call(
        functools.partial(_s5_kernel, tseg=tseg, slab_group=8),
        grid=(bsz, nt),
        in_specs=[pl.BlockSpec((t, S5_W), lambda b, i: (b * nt + i, ucol)),
                  pl.BlockSpec((t, S5_W), lambda b, i: (b * nt + i, ucol + 1)),
                  const(*wb.shape), const(*wcre.shape), const(*wcim.shape),
                  const(S5_SLABS, tseg, LANES), const(S5_SLABS, tseg, LANES), const(1, S5_W),
                  const(S5_W, S5_W)],
        out_specs=pl.BlockSpec((t, S5_W), lambda b, i: (b * nt + i, 0)),
        out_shape=jax.ShapeDtypeStruct((m, S5_W), BF16),
        scratch_shapes=[pltpu.VMEM((S5_SLABS, t, LANES), F32), pltpu.VMEM((S5_SLABS, t, LANES), F32)]
        + [pltpu.VMEM((S5_SLABS, S5_SEGS, LANES), F32)] * 4 + [pltpu.VMEM((S5_W, S5_W), BF16)],
        compiler_params=_cparams(("arbitrary", "arbitrary")),
        name="s5",
    )(proj, proj, wb, wcre, wcim, p_re, p_im, d.reshape(1, S5_W), wglu)


def _out_ab_kernel(lead_ref, xa_ref, xb_ref, xc_ref, oa_ref, ob_ref, w_ref, nw_ref, wg_ref,
                   h1_ref, xn_ref, gl_ref):
    t = pl.program_id(1)
    first = jnp.where(t == 0, lead_ref[...], xa_ref[...])
    h0 = (first, xb_ref[...], xc_ref[...])
    sls = [slice(r * CHUNK, (r + 1) * CHUNK) for r in range(ROW_CHUNKS)]
    accs = [jnp.dot(oa_ref[sl, :], w_ref[:RET_W, :], preferred_element_type=F32)
            + jnp.dot(ob_ref[sl, :], w_ref[RET_W:, :], preferred_element_type=F32) for sl in sls]
    h1s = [h0[r] + accs[r] for r in range(ROW_CHUNKS)]
    xns = [_rms_scale(h1, nw_ref[...]).astype(BF16) for h1 in h1s]
    gls = [jnp.dot(xn, wg_ref[...], preferred_element_type=F32) for xn in xns]
    for r, sl in enumerate(sls):
        h1_ref[sl, :] = h1s[r]
        xn_ref[sl, :] = xns[r]
        gl_ref[sl, :] = gls[r]


def _out_ab(x, lead, oa, ob, w, norm_w, wg, nt):
    bsz, _, d = x.shape
    tile = ROW_CHUNKS * CHUNK
    m = bsz * nt * tile
    row = lambda width: pl.BlockSpec((tile, width), lambda b, t: (b * nt + t, 0))
    const = lambda a: pl.BlockSpec(a.shape, lambda b, t: (0, 0))
    out = jax.ShapeDtypeStruct
    return pl.pallas_call(
        _out_ab_kernel,
        grid=(bsz, nt),
        in_specs=[pl.BlockSpec((CHUNK, d), lambda b, t: (0, 0))] + _chunk_specs(d, 1)
        + [row(RET_W), row(S5_W), const(w), pl.BlockSpec((1, d), lambda b, t: (0, 0)), const(wg)],
        out_specs=[row(d), row(d), row(GATE_PAD)],
        out_shape=(out((m, d), F32), out((m, d), BF16), out((m, GATE_PAD), F32)),
        compiler_params=_cparams(("parallel", "parallel")),
        name="out_proj_ab",
    )(lead, x, x, x, oa, ob, w, norm_w.reshape(1, d), wg)


LOG2E = 1.4426950408889634


def _split2(x):
    x1 = x.astype(BF16)
    return x1, (x - x1.astype(F32)).astype(BF16)


def _neg_abs(x):
    return -jnp.abs(x)


def _block_ref_rows(b, level):
    rows, width = b.shape
    half = 1 << level
    blk = 2 * half
    if blk >= 8:
        b3 = b.reshape(rows // blk, blk, width)
        ref = jnp.broadcast_to(b3[:, half - 1:half, :], b3.shape)
        return ref.reshape(rows, width)
    b3 = b.reshape(rows // 8, 8, width)
    sub = lax.broadcasted_iota(jnp.int32, b3.shape, 1)
    pick = lambda r: jnp.broadcast_to(b3[:, r:r + 1, :], b3.shape)
    if blk == 4:
        ref = jnp.where(sub < 4, pick(1), pick(5))
    else:
        ref = jnp.where(sub < 2, pick(0), jnp.where(sub < 4, pick(2), jnp.where(sub < 6, pick(4), pick(6))))
    return ref.reshape(rows, width)


def _gla_kernel(q_ref, k_ref, v_ref, z_ref, gl_ref, wg_ref, bg_ref, nw_ref, h_ref, wo_ref, fw_ref,
                y_ref, st_ref, mix_ref, wobf_ref, *, nc, nchunks):
    c = pl.program_id(0)
    n = jnp.minimum(c, nchunks - 1) % nc

    @pl.when(c == 0)
    def _():
        mix_ref[...] = jnp.zeros_like(mix_ref)
        wobf_ref[...] = wo_ref[...].astype(BF16)

    @pl.when(n == 0)
    def _():
        st_ref[...] = jnp.zeros_like(st_ref)

    ridx = lax.broadcasted_iota(jnp.int32, (CHUNK, 1), 0)
    ii = lax.broadcasted_iota(jnp.int32, (CHUNK, CHUNK), 0)
    jj = lax.broadcasted_iota(jnp.int32, (CHUNK, CHUNK), 1)
    pair_code = jnp.where(ii > jj, ii ^ jj, 0)
    eye = ii == jj
    tri = (ii >= jj).astype(BF16)

    gl1, gl2 = _split2(gl_ref[:, :GLA_RANK])
    wg1, wg2 = _split2(wg_ref[...])
    x = jnp.dot(jnp.concatenate([gl1, gl2, gl1], axis=1), jnp.concatenate([wg1, wg1, wg2], axis=0),
                preferred_element_type=F32) + bg_ref[...]
    log_a = (jnp.minimum(x, 0.0) - jnp.log(1.0 + jnp.exp(-jnp.abs(x)))) * (LOG2E / GLA_TAU)
    log_a = jnp.where(ridx >= jnp.where(n > 0, 0, PAD), log_a, 0.0)
    g1, g2 = _split2(log_a)
    b_all = jnp.dot(jnp.concatenate([tri, tri], axis=1), jnp.concatenate([g1, g2], axis=0),
                    preferred_element_type=F32)
    odd = (ridx & 1) == 1
    h2 = h_ref[...]
    for h0 in range(0, GLA_HEADS, 2):
        heads = (h0, h0 + 1)
        qs, ks, bs, scores = [], [], [], []
        for h in heads:
            kc = slice(h * GLA_DK, (h + 1) * GLA_DK)
            vc = slice(h * GLA_DV, (h + 1) * GLA_DV)
            h2 = h2 + jnp.dot(mix_ref[:, vc], wobf_ref[vc, :], preferred_element_type=F32)
            qs.append(q_ref[:, kc].astype(F32))
            ks.append(k_ref[:, kc].astype(F32))
            bs.append(b_all[:, kc])
            scores.append(jnp.where(eye, jnp.sum(qs[-1] * ks[-1], axis=1, keepdims=True), 0.0))
        for lv in range(GLA_LEVELS):
            half = 1 << lv
            zz = []
            for q, k, b in zip(qs, ks, bs):
                if lv == 0:
                    expo = jnp.where(odd, b - pltpu.roll(b, 1, 0), 0.0)
                else:
                    expo = _neg_abs(b - _block_ref_rows(b, lv))
                if half >= 8:
                    sel = jnp.concatenate([(q if r % 2 else k)[r * half:(r + 1) * half]
                                           for r in range(CHUNK // half)], axis=0)
                else:
                    sel = jnp.where(((ridx >> lv) & 1) == 1, q, k)
                zz.append((sel * jnp.exp2(expo)).astype(BF16))
            z2 = jnp.concatenate(zz, axis=0)
            gram = lax.dot_general(z2, z2, (((1,), (1,)), ((), ())), preferred_element_type=F32)
            mask = (pair_code >> lv) == 1
            for i in range(2):
                blk = gram[i * CHUNK:(i + 1) * CHUNK, i * CHUNK:(i + 1) * CHUNK]
                scores[i] = scores[i] + jnp.where(mask, blk, 0.0)
        for i, h in enumerate(heads):
            vc = slice(h * GLA_DV, (h + 1) * GLA_DV)
            q, k, b = qs[i], ks[i], bs[i]
            v = v_ref[:, vc]
            b_last = b[CHUNK - 1:CHUNK, :]
            o = jnp.dot(scores[i].astype(BF16), v, preferred_element_type=F32)
            st = st_ref[h]
            qe = (q * jnp.exp2(b)).astype(BF16)
            o = o + lax.dot_general(qe, st.astype(BF16), (((1,), (1,)), ((), ())), preferred_element_type=F32)
            ke = (k * jnp.exp2(b_last - b)).astype(BF16)
            kv_t = lax.dot_general(v, ke, (((0,), (0,)), ((), ())), preferred_element_type=F32)
            st_ref[h] = st * jnp.exp2(b_last) + kv_t
            mix_ref[:, vc] = (_rms_scale(o, nw_ref[:, vc]) * z_ref[:, vc].astype(F32)).astype(BF16)

    y_ref[...] = _rms_scale(h2, fw_ref[...])


def _gla(proj, glow, w_gate, b_gate, norm_w, h1, w_out, final_w, bsz, nc):
    m, d = h1.shape
    nchunks = m // CHUNK
    const = lambda *shape: pl.BlockSpec(shape, lambda c: (0,) * len(shape))
    cur = lambda width, col: pl.BlockSpec((CHUNK, width), lambda c: (jnp.minimum(c, nchunks - 1), col))
    prev = lambda c: jnp.maximum(c - 1, 0)
    return pl.pallas_call(
        functools.partial(_gla_kernel, nc=nc, nchunks=nchunks),
        grid=(nchunks + 1,),
        in_specs=[cur(GLA_QK, 0), cur(GLA_QK, 1), cur(GLA_W, 1), cur(GLA_W, 2), cur(GATE_PAD, 0),
                  const(GLA_RANK, GLA_QK), const(1, GLA_QK), const(1, GLA_W),
                  pl.BlockSpec((CHUNK, d), lambda c: (prev(c), 0)),
                  pl.BlockSpec((None, GLA_W, d), lambda c: (0, 0, 0), pipeline_mode=pl.Buffered(1)),
                  const(1, d)],
        out_specs=pl.BlockSpec((None, CHUNK, d),
                               lambda c: (prev(c) // nc, jnp.maximum(prev(c) % nc - 1, 0), 0)),
        out_shape=jax.ShapeDtypeStruct((bsz, (nc - 1) * CHUNK, d), F32),
        scratch_shapes=[pltpu.VMEM((GLA_HEADS, GLA_DV, GLA_DK), F32), pltpu.VMEM((CHUNK, GLA_W), BF16),
                        pltpu.VMEM((GLA_W, d), BF16)],
        compiler_params=_cparams(("arbitrary",)),
        name="gla_out",
    )(proj, proj, proj, proj, glow, w_gate.astype(F32), b_gate.reshape(1, GLA_QK).astype(F32),
      norm_w.reshape(1, GLA_W).astype(F32), h1, w_out, final_w.reshape(1, d).astype(F32))


def _rope_tables(bsz, rows_per_batch):
    pos = np.maximum(np.arange(rows_per_batch, dtype=np.float64) - PAD, 0.0)
    inv_freq = np.power(ROPE_BASE, -np.arange(0, RET_DK, 2, dtype=np.float64) / RET_DK)
    ang = pos[:, None] * inv_freq[None, :]
    cos, sin = np.cos(ang), np.sin(ang)
    cos2 = np.tile(np.concatenate([cos, cos], axis=1), (bsz, 1)).astype(np.float32)
    sin2 = np.tile(np.concatenate([-sin, sin], axis=1), (bsz, 1)).astype(np.float32)
    return jnp.asarray(cos2), jnp.asarray(sin2)


def kernel(x, meta, norm_ab_w, w_in_ab, ret_norm_w, s5_lam_re, s5_lam_im, s5_log_dt, s5_b_re, s5_b_im,
           s5_c_re, s5_c_im, s5_d, s5_w_glu, w_out_ab, norm_c_w, w_in_c, gla_w_gate, gla_b_gate,
           gla_norm_w, w_out_c, final_norm_w):
    bsz, seq, d = x.shape
    assert seq % CHUNK == 0 and w_in_ab.shape[0] == 1 and w_in_c.shape[0] == 1
    rpb = seq + CHUNK
    nc = rpb // CHUNK
    assert nc % ROW_CHUNKS == 0
    nt = nc // ROW_CHUNKS
    m = bsz * rpb
    tm = _pick_tile(m, (1408, 768, 384, 256, 128))
    lead = jnp.concatenate([jnp.zeros((PAD, d), x.dtype), meta.astype(x.dtype)], axis=0)

    xn0 = _embed_norm(x, lead, norm_ab_w[0], nt)
    plain, silu = ("plain", 1.0), ("silu", 1.0)
    modes_ab = [("rope", 1.0), ("rope", RET_DK ** -0.5), plain, plain, silu, silu, plain, silu]
    proj0 = _in_proj(xn0, w_in_ab, modes_ab, tm, rope=_rope_tables(bsz, rpb))
    o_a = _retention(proj0, ret_norm_w[0], bsz, nt)
    t, tseg = _s5_time_tile(rpb)
    p_re, p_im, wb, wcre, wcim = _s5_prep(s5_lam_re[0], s5_lam_im[0], s5_log_dt[0], s5_b_re[0], s5_b_im[0],
                                          s5_c_re[0], s5_c_im[0], tseg)
    o_b = _s5(proj0, bsz, wb, wcre, wcim, p_re, p_im, s5_d[0], s5_w_glu[0], t, tseg)

    w_in_c_t = jnp.swapaxes(w_in_c, 1, 2)
    n_main = 2 * GLA_QK + 2 * GLA_W
    wg = jnp.pad(w_in_c_t[0, n_main:, :], ((0, GATE_PAD - GLA_RANK), (0, 0))).astype(BF16).T
    h1, xn1, glow = _out_ab(x, lead, o_a, o_b, w_out_ab[0].astype(BF16), norm_c_w[0], wg, nt)

    modes_c = [("plain", GLA_DK ** -0.5), plain, plain, plain, silu, silu]
    proj1 = _in_proj(xn1, w_in_c_t, modes_c, tm, w_is_transposed=True)
    return _gla(proj1, glow, gla_w_gate[0], gla_b_gate[0], gla_norm_w[0], h1, w_out_c,
                final_norm_w, bsz, nc)
```

```python
import functools
import math

import numpy as np
import jax
import jax.numpy as jnp
from jax import lax
from jax.experimental import pallas as pl
from jax.experimental.pallas import tpu as pltpu

F32 = jnp.float32
BF16 = jnp.bfloat16

N_META = 16
LANES = 128
SUBLANES = 8
CHUNK = 128
PAD = CHUNK - N_META
EPS = 1e-6
ROW_CHUNKS = 3

RET_HEADS = 8
RET_DK = 128
RET_DV = 256
RET_QK = RET_HEADS * RET_DK
RET_W = RET_HEADS * RET_DV
ROPE_BASE = 10000.0

S5_W = 1024
S5_GH = 16
S5_G = 64
S5_P = 64
S5_F = S5_G * S5_P
S5_SLABS = S5_F // LANES
S5_JB = 4
S5_BLK_IN = S5_W // S5_JB
S5_SEGS = SUBLANES
S5_SCAN_UNROLL = 6

GLA_HEADS = 4
GLA_DK = 256
GLA_DV = 512
GLA_QK = GLA_HEADS * GLA_DK
GLA_W = GLA_HEADS * GLA_DV
GLA_RANK = 16
GLA_TAU = 16.0
GLA_LEVELS = 7
GATE_PAD = LANES

PROJ_TN = 1024
PROJ_ROW_SPLIT = 4
VMEM_LIMIT = 56 * 1024 * 1024


def _cparams(sem):
    return pltpu.CompilerParams(dimension_semantics=sem, vmem_limit_bytes=VMEM_LIMIT)


def _silu(x):
    return x * (1.0 / (1.0 + jnp.exp(-x)))


def _pick_tile(n, candidates):
    for c in candidates:
        if n % c == 0:
            return c
    raise ValueError(f"no tile for {n}")


def _rms_scale(x, w):
    return x * lax.rsqrt(jnp.mean(x * x, axis=-1, keepdims=True) + EPS) * w


def _chunk_specs(d, n_before):
    return [pl.BlockSpec((None, CHUNK, d),
                         lambda b, t, k=k: (b, jnp.maximum(ROW_CHUNKS * t + k - n_before, 0), 0))
            for k in range(ROW_CHUNKS)]


def _embed_norm_kernel(lead_ref, xa_ref, xb_ref, xc_ref, w_ref, o_ref):
    t = pl.program_id(1)
    first = jnp.where(t == 0, lead_ref[...], xa_ref[...])
    for r, rows in enumerate((first, xb_ref[...], xc_ref[...])):
        o_ref[r * CHUNK:(r + 1) * CHUNK, :] = _rms_scale(rows, w_ref[...]).astype(o_ref.dtype)


def _embed_norm(x, lead, w, nt):
    bsz, _, d = x.shape
    tile = ROW_CHUNKS * CHUNK
    return pl.pallas_call(
        _embed_norm_kernel,
        grid=(bsz, nt),
        in_specs=[pl.BlockSpec((CHUNK, d), lambda b, t: (0, 0))] + _chunk_specs(d, 1)
        + [pl.BlockSpec((1, d), lambda b, t: (0, 0))],
        out_specs=pl.BlockSpec((tile, d), lambda b, t: (b * nt + t, 0)),
        out_shape=jax.ShapeDtypeStruct((bsz * nt * tile, d), BF16),
        compiler_params=_cparams(("parallel", "parallel")),
        name="embed_norm",
    )(lead, x, x, x, w.reshape(1, d))


def _rope_rows(a, cos, sin, scale):
    outs = []
    for h in range(a.shape[1] // RET_DK):
        blk = a[:, h * RET_DK:(h + 1) * RET_DK]
        r = blk * cos + pltpu.roll(blk, RET_DK // 2, 1) * sin
        outs.append(r * scale if scale != 1.0 else r)
    return jnp.concatenate(outs, axis=1)


def _in_proj_kernel(*refs, modes, has_rope, w_is_transposed):
    if has_rope:
        x_ref, w_ref, cos_ref, sin_ref, o_ref, wbf_ref = refs
    else:
        x_ref, w_ref, o_ref, wbf_ref = refs
    j = pl.program_id(0)

    @pl.when(pl.program_id(1) == 0)
    def _():
        wbf_ref[...] = w_ref[...].astype(BF16)

    contract = (((1,), (1 if w_is_transposed else 0,)), ((), ()))
    for mode in sorted(set(modes)):
        cond = functools.reduce(jnp.logical_or, [j == jj for jj, mm in enumerate(modes) if mm == mode])

        @pl.when(cond)
        def _(mode=mode):
            kind, scale = mode
            rows = x_ref.shape[0] // PROJ_ROW_SPLIT
            sls = [slice(r * rows, (r + 1) * rows) for r in range(PROJ_ROW_SPLIT)]
            accs = [lax.dot_general(x_ref[sl, :], wbf_ref[...], contract, preferred_element_type=F32)
                    for sl in sls]
            for sl, acc in zip(sls, accs):
                if kind == "rope":
                    acc = _rope_rows(acc, cos_ref[sl, :], sin_ref[sl, :], scale)
                elif kind == "silu":
                    acc = _silu(acc)
                elif scale != 1.0:
                    acc = acc * scale
                o_ref[sl, :] = acc.astype(o_ref.dtype)


def _in_proj(x, w, modes, tm, rope=None, w_is_transposed=False):
    m, k = x.shape
    nt = len(modes)
    assert m % tm == 0 and tm % (PROJ_ROW_SPLIT * 16) == 0
    if w_is_transposed:
        w_spec = pl.BlockSpec((None, PROJ_TN, k), lambda j, i: (0, j, 0))
        w_scratch = pltpu.VMEM((PROJ_TN, k), BF16)
    else:
        w_spec = pl.BlockSpec((None, k, PROJ_TN), lambda j, i: (0, 0, j))
        w_scratch = pltpu.VMEM((k, PROJ_TN), BF16)
    in_specs = [pl.BlockSpec((tm, k), lambda j, i: (i, 0)), w_spec]
    args = [x, w]
    if rope is not None:
        in_specs += [pl.BlockSpec((tm, RET_DK), lambda j, i: (i, 0))] * 2
        args += list(rope)
    return pl.pallas_call(
        functools.partial(_in_proj_kernel, modes=tuple(modes), has_rope=rope is not None,
                          w_is_transposed=w_is_transposed),
        grid=(nt, m // tm),
        in_specs=in_specs,
        out_specs=pl.BlockSpec((tm, PROJ_TN), lambda j, i: (i, j)),
        out_shape=jax.ShapeDtypeStruct((m, nt * PROJ_TN), BF16),
        scratch_shapes=[w_scratch],
        compiler_params=_cparams(("arbitrary", "arbitrary")),
        name="in_proj",
    )(*args)


def _ret_log_decay(h):
    return math.log1p(-(2.0 ** (-5.0 - h)))


def _retention_kernel(q_ref, k_ref, v_ref, z_ref, nw_ref, o_ref, s_ref, dec_ref, qs_ref, ks_ref):
    first_step = (pl.program_id(0) == 0) & (pl.program_id(1) == 0)

    @pl.when(first_step)
    def _():
        ii = lax.broadcasted_iota(jnp.int32, (CHUNK, CHUNK), 0)
        jj = lax.broadcasted_iota(jnp.int32, (CHUNK, CHUNK), 1)
        diff = (ii - jj).astype(F32)
        row = ii.astype(F32)
        for h in range(RET_HEADS):
            lg = _ret_log_decay(h)
            dec_ref[h] = jnp.where(ii >= jj, jnp.exp(lg * jnp.maximum(diff, 0.0)), 0.0)
            qs_ref[h] = jnp.exp(lg * (row + 1.0))
            ks_ref[h] = jnp.exp(lg * (CHUNK - 1.0 - row))

    @pl.when(pl.program_id(1) == 0)
    def _():
        s_ref[...] = jnp.zeros_like(s_ref)

    heads = range(RET_HEADS)
    vcs = [slice(h * RET_DV, (h + 1) * RET_DV) for h in heads]
    nt_dims = (((1,), (1,)), ((), ()))
    tn_dims = (((0,), (0,)), ((), ()))
    states = [s_ref[h] for h in heads]
    for r in range(q_ref.shape[0] // CHUNK):
        rows = slice(r * CHUNK, (r + 1) * CHUNK)
        qs = [q_ref[rows, h * RET_DK:(h + 1) * RET_DK] for h in heads]
        ks = [k_ref[rows, h * RET_DK:(h + 1) * RET_DK] for h in heads]
        vs = [v_ref[rows, vc] for vc in vcs]
        scores = [lax.dot_general(qs[h], ks[h], nt_dims, preferred_element_type=F32) for h in heads]
        kvs = [lax.dot_general((ks[h].astype(F32) * ks_ref[h]).astype(BF16), vs[h], tn_dims,
                               preferred_element_type=F32) for h in heads]
        lhs = [jnp.concatenate([(scores[h] * dec_ref[h]).astype(BF16),
                                (qs[h].astype(F32) * qs_ref[h]).astype(BF16)], axis=1) for h in heads]
        outs = [jnp.dot(lhs[h], jnp.concatenate([vs[h], states[h].astype(BF16)], axis=0),
                        preferred_element_type=F32) for h in heads]
        states = [states[h] * math.exp(_ret_log_decay(h) * CHUNK) + kvs[h] for h in heads]
        for h in heads:
            o_ref[rows, vcs[h]] = (_rms_scale(outs[h], nw_ref[:, vcs[h]])
                                   * z_ref[rows, vcs[h]].astype(F32)).astype(o_ref.dtype)
    for h in heads:
        s_ref[h] = states[h]


def _retention(proj, norm_w, bsz, nt):
    m = proj.shape[0]
    tile = ROW_CHUNKS * CHUNK
    tab = pltpu.VMEM((RET_HEADS, CHUNK, CHUNK), F32)
    row = lambda width, col: pl.BlockSpec((tile, width), lambda b, t: (b * nt + t, col))
    return pl.pallas_call(
        _retention_kernel,
        grid=(bsz, nt),
        in_specs=[row(RET_QK, 0), row(RET_QK, 1), row(RET_W, 1), row(RET_W, 2),
                  pl.BlockSpec((1, RET_W), lambda b, t: (0, 0))],
        out_specs=row(RET_W, 0),
        out_shape=jax.ShapeDtypeStruct((m, RET_W), BF16),
        scratch_shapes=[pltpu.VMEM((RET_HEADS, RET_DK, RET_DV), F32), tab, tab, tab],
        compiler_params=_cparams(("arbitrary", "arbitrary")),
        name="retention",
    )(proj, proj, proj, proj, norm_w.reshape(1, RET_W))


def _s5_prep_kernel(lre_ref, lim_ref, ldt_ref, bre_ref, bim_ref, cre_ref, cim_ref,
                    pre_ref, pim_ref, wb_ref, wcre_ref, wcim_ref):
    lre = lre_ref[...]
    lim = lim_ref[...]
    dt = jnp.exp(ldt_ref[...])
    tseg = pre_ref.shape[1]
    n = lax.broadcasted_iota(jnp.int32, (tseg, 1), 0).astype(F32) + 1.0
    mag = jnp.exp(n * (lre * dt))
    ang = n * (lim * dt)
    p_re = mag * jnp.cos(ang)
    p_im = mag * jnp.sin(ang)
    for sl in range(S5_SLABS):
        pre_ref[sl] = p_re[:, sl * LANES:(sl + 1) * LANES]
        pim_ref[sl] = p_im[:, sl * LANES:(sl + 1) * LANES]
    a_re = p_re[0:1]
    a_im = p_im[0:1]
    den = lre * lre + lim * lim
    nr = a_re - 1.0
    f_re = (nr * lre + a_im * lim) / den
    f_im = (a_im * lre - nr * lim) / den
    b_re = bre_ref[...]
    b_im = bim_ref[...]
    bb_re = f_re * b_re - f_im * b_im
    bb_im = f_re * b_im + f_im * b_re
    blk = S5_F // S5_JB
    rows = lax.broadcasted_iota(jnp.int32, (16 * S5_GH, blk), 0)
    cols = lax.broadcasted_iota(jnp.int32, (16 * S5_GH, blk), 1)
    same_group = (rows // S5_GH) == (cols // S5_P)

    def block_diag(a, j):
        piece = a[:, j * blk:(j + 1) * blk]
        return jnp.where(same_group, jnp.concatenate([piece] * 16, axis=0), 0.0).astype(BF16)

    for j in range(S5_JB):
        wb_ref[j, :, :blk] = block_diag(bb_re, j)
        wb_ref[j, :, blk:] = block_diag(bb_im, j)
        wcre_ref[j] = block_diag(cre_ref[...], j)
        wcim_ref[j] = block_diag(cim_ref[...], j)


def _s5_prep(lam_re, lam_im, log_dt, b_re, b_im, c_re, c_im, tseg):
    flat = lambda a: a.reshape(1, S5_F)
    ldt = jnp.broadcast_to(log_dt[:, None], (S5_G, S5_P))
    bt = lambda a: a.reshape(S5_F, S5_GH).T
    ct = lambda a: a.transpose(1, 0, 2).reshape(S5_GH, S5_F)
    out = jax.ShapeDtypeStruct
    blk = S5_F // S5_JB
    return pl.pallas_call(
        _s5_prep_kernel,
        out_shape=(out((S5_SLABS, tseg, LANES), F32), out((S5_SLABS, tseg, LANES), F32),
                   out((S5_JB, 16 * S5_GH, 2 * blk), BF16),
                   out((S5_JB, 16 * S5_GH, blk), BF16), out((S5_JB, 16 * S5_GH, blk), BF16)),
        compiler_params=pltpu.CompilerParams(vmem_limit_bytes=VMEM_LIMIT),
        name="s5_prep",
    )(flat(lam_re), flat(lam_im), flat(ldt), bt(b_re), bt(b_im), ct(c_re.astype(F32)), ct(c_im.astype(F32)))


def _gelu_tanh(x):
    return 0.5 * x * (1.0 + jnp.tanh(math.sqrt(2.0 / math.pi) * (x + 0.044715 * (x * x * x))))


def _s5_kernel(u_ref, z_ref, wb_ref, wcre_ref, wcim_ref, pre_ref, pim_ref, d_ref, wglu_ref,
               o_ref, bure_ref, buim_ref, cre_ref, cim_ref, xinre_ref, xinim_ref, wglubf_ref,
               *, tseg, slab_group):
    ci = pl.program_id(1)

    @pl.when((pl.program_id(0) == 0) & (ci == 0))
    def _():
        wglubf_ref[...] = wglu_ref[...].astype(BF16)

    @pl.when(ci == 0)
    def _():
        cre_ref[...] = jnp.zeros_like(cre_ref)
        cim_ref[...] = jnp.zeros_like(cim_ref)

    u_bf = u_ref[...]
    spb = S5_SLABS // S5_JB
    for j in range(S5_JB):
        res = jnp.dot(u_bf[:, j * S5_BLK_IN:(j + 1) * S5_BLK_IN], wb_ref[j], preferred_element_type=F32)
        for s in range(spb):
            bure_ref[j * spb + s] = res[:, s * LANES:(s + 1) * LANES]
            buim_ref[j * spb + s] = res[:, (spb + s) * LANES:(spb + s + 1) * LANES]

    for g0 in range(0, S5_SLABS, slab_group):
        slabs = list(range(g0, g0 + slab_group))
        a_re = [jnp.broadcast_to(pre_ref[sl, 0:1, :], (S5_SEGS, LANES)) for sl in slabs]
        a_im = [jnp.broadcast_to(pim_ref[sl, 0:1, :], (S5_SEGS, LANES)) for sl in slabs]

        def scan_body(tau, carry, slabs=slabs, a_re=a_re, a_im=a_im):
            xr, xi = carry
            nr, ni = [], []
            for n, sl in enumerate(slabs):
                rows = pl.ds(tau, S5_SEGS, stride=tseg)
                br = bure_ref[sl, rows, :]
                bi = buim_ref[sl, rows, :]
                r = a_re[n] * xr[n] - a_im[n] * xi[n] + br
                i = a_re[n] * xi[n] + a_im[n] * xr[n] + bi
                bure_ref[sl, rows, :] = r
                buim_ref[sl, rows, :] = i
                nr.append(r)
                ni.append(i)
            return tuple(nr), tuple(ni)

        zero = tuple(jnp.zeros((S5_SEGS, LANES), F32) for _ in slabs)
        end_re, end_im = lax.fori_loop(0, tseg, scan_body, (zero, zero),
                                       unroll=S5_SCAN_UNROLL if tseg % S5_SCAN_UNROLL == 0 else 1)

        for n, sl in enumerate(slabs):
            at_re = pre_ref[sl, tseg - 1:tseg, :]
            at_im = pim_ref[sl, tseg - 1:tseg, :]
            xr = cre_ref[sl, 0:1, :]
            xi = cim_ref[sl, 0:1, :]
            rows_re, rows_im = [], []
            for s in range(S5_SEGS):
                rows_re.append(xr)
                rows_im.append(xi)
                er = end_re[n][s:s + 1]
                ei = end_im[n][s:s + 1]
                xr, xi = er + at_re * xr - at_im * xi, ei + at_re * xi + at_im * xr
            cre_ref[sl] = jnp.broadcast_to(xr, (S5_SEGS, LANES))
            cim_ref[sl] = jnp.broadcast_to(xi, (S5_SEGS, LANES))
            xinre_ref[sl] = jnp.concatenate(rows_re, axis=0)
            xinim_ref[sl] = jnp.concatenate(rows_im, axis=0)

    def fix_body(sl, carry):
        xin_re = xinre_ref[sl]
        xin_im = xinim_ref[sl]
        for tau in range(tseg):
            rows = pl.ds(tau, S5_SEGS, stride=tseg)
            pr = pre_ref[sl, tau:tau + 1, :]
            pi = pim_ref[sl, tau:tau + 1, :]
            bure_ref[sl, rows, :] = bure_ref[sl, rows, :] + (pr * xin_re - pi * xin_im)
            buim_ref[sl, rows, :] = buim_ref[sl, rows, :] + (pr * xin_im + pi * xin_re)
        return carry

    lax.fori_loop(0, S5_SLABS, fix_body, 0)

    ys = []
    for j in range(S5_JB):
        x_re = jnp.concatenate([bure_ref[j * spb + s] for s in range(spb)], axis=1).astype(BF16)
        x_im = jnp.concatenate([buim_ref[j * spb + s] for s in range(spb)], axis=1).astype(BF16)
        nt_dims = (((1,), (1,)), ((), ()))
        ys.append(lax.dot_general(x_re, wcre_ref[j], nt_dims, preferred_element_type=F32)
                  - lax.dot_general(x_im, wcim_ref[j], nt_dims, preferred_element_type=F32))
    y = jnp.concatenate(ys, axis=1) + d_ref[...] * u_bf.astype(F32)
    y = _gelu_tanh(y)
    gate = jnp.dot(y.astype(BF16), wglubf_ref[...], preferred_element_type=F32)
    y = y * (1.0 / (1.0 + jnp.exp(-gate)))
    o_ref[...] = (y * z_ref[...].astype(F32)).astype(o_ref.dtype)


def _s5_time_tile(rows_per_batch):
    for tseg in (66, 44, 132, 12, 4, 6, 2):
        t = S5_SEGS * tseg
        if rows_per_batch % t == 0 and t % 16 == 0:
            return t, tseg
    raise ValueError(f"no S5 time tile for {rows_per_batch}")


def _s5(proj, bsz, wb, wcre, wcim, p_re, p_im, d, wglu, t, tseg):
    m = proj.shape[0]
    nt = m // bsz // t
    ucol = (2 * RET_QK + 2 * RET_W) // S5_W
    const = lambda *shape: pl.BlockSpec(shape, lambda b, i: (0,) * len(shape), pipeline_mode=pl.Buffered(1))
    return pl.pallas_call(
        functools.partial(_s5_kernel, tseg=tseg, slab_group=8),
        grid=(bsz, nt),
        in_specs=[pl.BlockSpec((t, S5_W), lambda b, i: (b * nt + i, ucol)),
                  pl.BlockSpec((t, S5_W), lambda b, i: (b * nt + i, ucol + 1)),
                  const(*wb.shape), const(*wcre.shape), const(*wcim.shape),
                  const(S5_SLABS, tseg, LANES), const(S5_SLABS, tseg, LANES), const(1, S5_W),
                  const(S5_W, S5_W)],
        out_specs=pl.BlockSpec((t, S5_W), lambda b, i: (b * nt + i, 0)),
        out_shape=jax.ShapeDtypeStruct((m, S5_W), BF16),
        scratch_shapes=[pltpu.VMEM((S5_SLABS, t, LANES), F32), pltpu.VMEM((S5_SLABS, t, LANES), F32)]
        + [pltpu.VMEM((S5_SLABS, S5_SEGS, LANES), F32)] * 4 + [pltpu.VMEM((S5_W, S5_W), BF16)],
        compiler_params=_cparams(("arbitrary", "arbitrary")),
        name="s5",
    )(proj, proj, wb, wcre, wcim, p_re, p_im, d.reshape(1, S5_W), wglu)


def _out_ab_kernel(lead_ref, xa_ref, xb_ref, xc_ref, oa_ref, ob_ref, w_ref, nw_ref, wg_ref,
                   h1_ref, xn_ref, gl_ref):
    t = pl.program_id(1)
    first = jnp.where(t == 0, lead_ref[...], xa_ref[...])
    h0 = (first, xb_ref[...], xc_ref[...])
    sls = [slice(r * CHUNK, (r + 1) * CHUNK) for r in range(ROW_CHUNKS)]
    accs = [jnp.dot(oa_ref[sl, :], w_ref[:RET_W, :], preferred_element_type=F32)
            + jnp.dot(ob_ref[sl, :], w_ref[RET_W:, :], preferred_element_type=F32) for sl in sls]
    h1s = [h0[r] + accs[r] for r in range(ROW_CHUNKS)]
    xns = [_rms_scale(h1, nw_ref[...]).astype(BF16) for h1 in h1s]
    gls = [jnp.dot(xn, wg_ref[...], preferred_element_type=F32) for xn in xns]
    for r, sl in enumerate(sls):
        h1_ref[sl, :] = h1s[r]
        xn_ref[sl, :] = xns[r]
        gl_ref[sl, :] = gls[r]


def _out_ab(x, lead, oa, ob, w, norm_w, wg, nt):
    bsz, _, d = x.shape
    tile = ROW_CHUNKS * CHUNK
    m = bsz * nt * tile
    row = lambda width: pl.BlockSpec((tile, width), lambda b, t: (b * nt + t, 0))
    const = lambda a: pl.BlockSpec(a.shape, lambda b, t: (0, 0))
    out = jax.ShapeDtypeStruct
    return pl.pallas_call(
        _out_ab_kernel,
        grid=(bsz, nt),
        in_specs=[pl.BlockSpec((CHUNK, d), lambda b, t: (0, 0))] + _chunk_specs(d, 1)
        + [row(RET_W), row(S5_W), const(w), pl.BlockSpec((1, d), lambda b, t: (0, 0)), const(wg)],
        out_specs=[row(d), row(d), row(GATE_PAD)],
        out_shape=(out((m, d), F32), out((m, d), BF16), out((m, GATE_PAD), F32)),
        compiler_params=_cparams(("parallel", "parallel")),
        name="out_proj_ab",
    )(lead, x, x, x, oa, ob, w, norm_w.reshape(1, d), wg)


LOG2E = 1.4426950408889634


def _split2(x):
    x1 = x.astype(BF16)
    return x1, (x - x1.astype(F32)).astype(BF16)


def _neg_abs(x):
    return -jnp.abs(x)


def _block_ref_rows(b, level):
    rows, width = b.shape
    half = 1 << level
    blk = 2 * half
    if blk >= 8:
        b3 = b.reshape(rows // blk, blk, width)
        ref = jnp.broadcast_to(b3[:, half - 1:half, :], b3.shape)
        return ref.reshape(rows, width)
    b3 = b.reshape(rows // 8, 8, width)
    sub = lax.broadcasted_iota(jnp.int32, b3.shape, 1)
    pick = lambda r: jnp.broadcast_to(b3[:, r:r + 1, :], b3.shape)
    if blk == 4:
        ref = jnp.where(sub < 4, pick(1), pick(5))
    else:
        ref = jnp.where(sub < 2, pick(0), jnp.where(sub < 4, pick(2), jnp.where(sub < 6, pick(4), pick(6))))
    return ref.reshape(rows, width)


def _gla_kernel(q_ref, k_ref, v_ref, z_ref, gl_ref, wg_ref, bg_ref, nw_ref, h_ref, wo_ref, fw_ref,
                y_ref, st_ref, mix_ref, wobf_ref, *, nc, nchunks):
    c = pl.program_id(0)
    n = jnp.minimum(c, nchunks - 1) % nc

    @pl.when(c == 0)
    def _():
        mix_ref[...] = jnp.zeros_like(mix_ref)
        wobf_ref[...] = wo_ref[...].astype(BF16)

    @pl.when(n == 0)
    def _():
        st_ref[...] = jnp.zeros_like(st_ref)

    ridx = lax.broadcasted_iota(jnp.int32, (CHUNK, 1), 0)
    ii = lax.broadcasted_iota(jnp.int32, (CHUNK, CHUNK), 0)
    jj = lax.broadcasted_iota(jnp.int32, (CHUNK, CHUNK), 1)
    pair_code = jnp.where(ii > jj, ii ^ jj, 0)
    eye = ii == jj
    tri = (ii >= jj).astype(BF16)

    gl1, gl2 = _split2(gl_ref[:, :GLA_RANK])
    wg1, wg2 = _split2(wg_ref[...])
    x = jnp.dot(jnp.concatenate([gl1, gl2, gl1], axis=1), jnp.concatenate([wg1, wg1, wg2], axis=0),
                preferred_element_type=F32) + bg_ref[...]
    log_a = (jnp.minimum(x, 0.0) - jnp.log(1.0 + jnp.exp(-jnp.abs(x)))) * (LOG2E / GLA_TAU)
    log_a = jnp.where(ridx >= jnp.where(n > 0, 0, PAD), log_a, 0.0)
    g1, g2 = _split2(log_a)
    b_all = jnp.dot(jnp.concatenate([tri, tri], axis=1), jnp.concatenate([g1, g2], axis=0),
                    preferred_element_type=F32)
    odd = (ridx & 1) == 1
    h2 = h_ref[...]

    def finish(heads, qs, ks, bs, scores):
        for i, h in enumerate(heads):
            vc = slice(h * GLA_DV, (h + 1) * GLA_DV)
            q, k, b = qs[i], ks[i], bs[i]
            v = v_ref[:, vc]
            b_last = b[CHUNK - 1:CHUNK, :]
            o = jnp.dot(scores[i].astype(BF16), v, preferred_element_type=F32)
            st = st_ref[h]
            qe = (q * jnp.exp2(b)).astype(BF16)
            o = o + lax.dot_general(qe, st.astype(BF16), (((1,), (1,)), ((), ())), preferred_element_type=F32)
            ke = (k * jnp.exp2(b_last - b)).astype(BF16)
            kv_t = lax.dot_general(v, ke, (((0,), (0,)), ((), ())), preferred_element_type=F32)
            st_ref[h] = st * jnp.exp2(b_last) + kv_t
            mix_ref[:, vc] = (_rms_scale(o, nw_ref[:, vc]) * z_ref[:, vc].astype(F32)).astype(BF16)

    pending = []
    for h0 in range(0, GLA_HEADS, 2):
        heads = (h0, h0 + 1)
        qs, ks, bs, scores = [], [], [], []
        for h in heads:
            kc = slice(h * GLA_DK, (h + 1) * GLA_DK)
            vc = slice(h * GLA_DV, (h + 1) * GLA_DV)
            h2 = h2 + jnp.dot(mix_ref[:, vc], wobf_ref[vc, :], preferred_element_type=F32)
            qs.append(q_ref[:, kc].astype(F32))
            ks.append(k_ref[:, kc].astype(F32))
            bs.append(b_all[:, kc])
            scores.append(jnp.where(eye, jnp.sum(qs[-1] * ks[-1], axis=1, keepdims=True), 0.0))
        for lv in range(GLA_LEVELS):
            half = 1 << lv
            if lv == GLA_LEVELS // 2 and pending:
                finish(*pending.pop())
            zz = []
            for q, k, b in zip(qs, ks, bs):
                if lv == 0:
                    expo = jnp.where(odd, b - pltpu.roll(b, 1, 0), 0.0)
                else:
                    expo = _neg_abs(b - _block_ref_rows(b, lv))
                if half >= 8:
                    sel = jnp.concatenate([(q if r % 2 else k)[r * half:(r + 1) * half]
                                           for r in range(CHUNK // half)], axis=0)
                else:
                    sel = jnp.where(((ridx >> lv) & 1) == 1, q, k)
                zz.append((sel * jnp.exp2(expo)).astype(BF16))
            z2 = jnp.concatenate(zz, axis=0)
            gram = lax.dot_general(z2, z2, (((1,), (1,)), ((), ())), preferred_element_type=F32)
            mask = (pair_code >> lv) == 1
            for i in range(2):
                blk = gram[i * CHUNK:(i + 1) * CHUNK, i * CHUNK:(i + 1) * CHUNK]
                scores[i] = scores[i] + jnp.where(mask, blk, 0.0)
        pending.append((heads, qs, ks, bs, scores))
    finish(*pending.pop())

    y_ref[...] = _rms_scale(h2, fw_ref[...])


def _gla(proj, glow, w_gate, b_gate, norm_w, h1, w_out, final_w, bsz, nc):
    m, d = h1.shape
    nchunks = m // CHUNK
    const = lambda *shape: pl.BlockSpec(shape, lambda c: (0,) * len(shape))
    cur = lambda width, col: pl.BlockSpec((CHUNK, width), lambda c: (jnp.minimum(c, nchunks - 1), col))
    prev = lambda c: jnp.maximum(c - 1, 0)
    return pl.pallas_call(
        functools.partial(_gla_kernel, nc=nc, nchunks=nchunks),
        grid=(nchunks + 1,),
        in_specs=[cur(GLA_QK, 0), cur(GLA_QK, 1), cur(GLA_W, 1), cur(GLA_W, 2), cur(GATE_PAD, 0),
                  const(GLA_RANK, GLA_QK), const(1, GLA_QK), const(1, GLA_W),
                  pl.BlockSpec((CHUNK, d), lambda c: (prev(c), 0)),
                  pl.BlockSpec((None, GLA_W, d), lambda c: (0, 0, 0), pipeline_mode=pl.Buffered(1)),
                  const(1, d)],
        out_specs=pl.BlockSpec((None, CHUNK, d),
                               lambda c: (prev(c) // nc, jnp.maximum(prev(c) % nc - 1, 0), 0)),
        out_shape=jax.ShapeDtypeStruct((bsz, (nc - 1) * CHUNK, d), F32),
        scratch_shapes=[pltpu.VMEM((GLA_HEADS, GLA_DV, GLA_DK), F32), pltpu.VMEM((CHUNK, GLA_W), BF16),
                        pltpu.VMEM((GLA_W, d), BF16)],
        compiler_params=_cparams(("arbitrary",)),
        name="gla_out",
    )(proj, proj, proj, proj, glow, w_gate.astype(F32), b_gate.reshape(1, GLA_QK).astype(F32),
      norm_w.reshape(1, GLA_W).astype(F32), h1, w_out, final_w.reshape(1, d).astype(F32))


def _rope_tables(bsz, rows_per_batch):
    pos = np.maximum(np.arange(rows_per_batch, dtype=np.float64) - PAD, 0.0)
    inv_freq = np.power(ROPE_BASE, -np.arange(0, RET_DK, 2, dtype=np.float64) / RET_DK)
    ang = pos[:, None] * inv_freq[None, :]
    cos, sin = np.cos(ang), np.sin(ang)
    cos2 = np.tile(np.concatenate([cos, cos], axis=1), (bsz, 1)).astype(np.float32)
    sin2 = np.tile(np.concatenate([-sin, sin], axis=1), (bsz, 1)).astype(np.float32)
    return jnp.asarray(cos2), jnp.asarray(sin2)


def kernel(x, meta, norm_ab_w, w_in_ab, ret_norm_w, s5_lam_re, s5_lam_im, s5_log_dt, s5_b_re, s5_b_im,
           s5_c_re, s5_c_im, s5_d, s5_w_glu, w_out_ab, norm_c_w, w_in_c, gla_w_gate, gla_b_gate,
           gla_norm_w, w_out_c, final_norm_w):
    bsz, seq, d = x.shape
    assert seq % CHUNK == 0 and w_in_ab.shape[0] == 1 and w_in_c.shape[0] == 1
    rpb = seq + CHUNK
    nc = rpb // CHUNK
    assert nc % ROW_CHUNKS == 0
    nt = nc // ROW_CHUNKS
    m = bsz * rpb
    tm = _pick_tile(m, (1408, 768, 384, 256, 128))
    lead = jnp.concatenate([jnp.zeros((PAD, d), x.dtype), meta.astype(x.dtype)], axis=0)

    xn0 = _embed_norm(x, lead, norm_ab_w[0], nt)
    plain, silu = ("plain", 1.0), ("silu", 1.0)
    modes_ab = [("rope", 1.0), ("rope", RET_DK ** -0.5), plain, plain, silu, silu, plain, silu]
    proj0 = _in_proj(xn0, w_in_ab, modes_ab, tm, rope=_rope_tables(bsz, rpb))
    o_a = _retention(proj0, ret_norm_w[0], bsz, nt)
    t, tseg = _s5_time_tile(rpb)
    p_re, p_im, wb, wcre, wcim = _s5_prep(s5_lam_re[0], s5_lam_im[0], s5_log_dt[0], s5_b_re[0], s5_b_im[0],
                                          s5_c_re[0], s5_c_im[0], tseg)
    o_b = _s5(proj0, bsz, wb, wcre, wcim, p_re, p_im, s5_d[0], s5_w_glu[0], t, tseg)

    w_in_c_t = jnp.swapaxes(w_in_c, 1, 2)
    n_main = 2 * GLA_QK + 2 * GLA_W
    wg = jnp.pad(w_in_c_t[0, n_main:, :], ((0, GATE_PAD - GLA_RANK), (0, 0))).astype(BF16).T
    h1, xn1, glow = _out_ab(x, lead, o_a, o_b, w_out_ab[0].astype(BF16), norm_c_w[0], wg, nt)

    modes_c = [("plain", GLA_DK ** -0.5), plain, plain, plain, silu, silu]
    proj1 = _in_proj(xn1, w_in_c_t, modes_c, tm, w_is_transposed=True)
    return _gla(proj1, glow, gla_w_gate[0], gla_b_gate[0], gla_norm_w[0], h1, w_out_c,
                final_norm_w, bsz, nc)
```

```python
import functools
import math

import numpy as np
import jax
import jax.numpy as jnp
from jax import lax
from jax.experimental import pallas as pl
from jax.experimental.pallas import tpu as pltpu

F32 = jnp.float32
BF16 = jnp.bfloat16

N_META = 16
LANES = 128
SUBLANES = 8
CHUNK = 128
PAD = CHUNK - N_META
EPS = 1e-6
ROW_CHUNKS = 3

RET_HEADS = 8
RET_DK = 128
RET_DV = 256
RET_QK = RET_HEADS * RET_DK
RET_W = RET_HEADS * RET_DV
ROPE_BASE = 10000.0

S5_W = 1024
S5_GH = 16
S5_G = 64
S5_P = 64
S5_F = S5_G * S5_P
S5_SLABS = S5_F // LANES
S5_JB = 4
S5_BLK_IN = S5_W // S5_JB
S5_SEGS = SUBLANES
S5_SCAN_UNROLL = 4

GLA_HEADS = 4
GLA_DK = 256
GLA_DV = 512
GLA_QK = GLA_HEADS * GLA_DK
GLA_W = GLA_HEADS * GLA_DV
GLA_RANK = 16
GLA_TAU = 16.0
GLA_LEVELS = 7
GATE_PAD = LANES

PROJ_TN = 1024
PROJ_ROW_SPLIT = 4
VMEM_LIMIT = 56 * 1024 * 1024


def _cparams(sem):
    return pltpu.CompilerParams(dimension_semantics=sem, vmem_limit_bytes=VMEM_LIMIT)


def _silu(x):
    return x * (1.0 / (1.0 + jnp.exp(-x)))


def _pick_tile(n, candidates):
    for c in candidates:
        if n % c == 0:
            return c
    raise ValueError(f"no tile for {n}")


def _rms_scale(x, w):
    return x * lax.rsqrt(jnp.mean(x * x, axis=-1, keepdims=True) + EPS) * w


def _chunk_specs(d, n_before):
    return [pl.BlockSpec((None, CHUNK, d),
                         lambda b, t, k=k: (b, jnp.maximum(ROW_CHUNKS * t + k - n_before, 0), 0))
            for k in range(ROW_CHUNKS)]


def _embed_norm_kernel(lead_ref, xa_ref, xb_ref, xc_ref, w_ref, o_ref):
    t = pl.program_id(1)
    first = jnp.where(t == 0, lead_ref[...], xa_ref[...])
    for r, rows in enumerate((first, xb_ref[...], xc_ref[...])):
        o_ref[r * CHUNK:(r + 1) * CHUNK, :] = _rms_scale(rows, w_ref[...]).astype(o_ref.dtype)


def _embed_norm(x, lead, w, nt):
    bsz, _, d = x.shape
    tile = ROW_CHUNKS * CHUNK
    return pl.pallas_call(
        _embed_norm_kernel,
        grid=(bsz, nt),
        in_specs=[pl.BlockSpec((CHUNK, d), lambda b, t: (0, 0))] + _chunk_specs(d, 1)
        + [pl.BlockSpec((1, d), lambda b, t: (0, 0))],
        out_specs=pl.BlockSpec((tile, d), lambda b, t: (b * nt + t, 0)),
        out_shape=jax.ShapeDtypeStruct((bsz * nt * tile, d), BF16),
        compiler_params=_cparams(("parallel", "parallel")),
        name="embed_norm",
    )(lead, x, x, x, w.reshape(1, d))


def _rope_rows(a, cos, sin, scale):
    outs = []
    for h in range(a.shape[1] // RET_DK):
        blk = a[:, h * RET_DK:(h + 1) * RET_DK]
        r = blk * cos + pltpu.roll(blk, RET_DK // 2, 1) * sin
        outs.append(r * scale if scale != 1.0 else r)
    return jnp.concatenate(outs, axis=1)


def _in_proj_kernel(*refs, modes, has_rope, w_is_transposed):
    if has_rope:
        x_ref, w_ref, cos_ref, sin_ref, o_ref, wbf_ref = refs
    else:
        x_ref, w_ref, o_ref, wbf_ref = refs
    j = pl.program_id(0)

    @pl.when(pl.program_id(1) == 0)
    def _():
        wbf_ref[...] = w_ref[...].astype(BF16)

    contract = (((1,), (1 if w_is_transposed else 0,)), ((), ()))
    for mode in sorted(set(modes)):
        cond = functools.reduce(jnp.logical_or, [j == jj for jj, mm in enumerate(modes) if mm == mode])

        @pl.when(cond)
        def _(mode=mode):
            kind, scale = mode
            rows = x_ref.shape[0] // PROJ_ROW_SPLIT
            sls = [slice(r * rows, (r + 1) * rows) for r in range(PROJ_ROW_SPLIT)]
            accs = [lax.dot_general(x_ref[sl, :], wbf_ref[...], contract, preferred_element_type=F32)
                    for sl in sls]
            for sl, acc in zip(sls, accs):
                if kind == "rope":
                    acc = _rope_rows(acc, cos_ref[sl, :], sin_ref[sl, :], scale)
                elif kind == "silu":
                    acc = _silu(acc)
                elif scale != 1.0:
                    acc = acc * scale
                o_ref[sl, :] = acc.astype(o_ref.dtype)


def _in_proj(x, w, modes, tm, rope=None, w_is_transposed=False):
    m, k = x.shape
    nt = len(modes)
    assert m % tm == 0 and tm % (PROJ_ROW_SPLIT * 16) == 0
    if w_is_transposed:
        w_spec = pl.BlockSpec((None, PROJ_TN, k), lambda j, i: (0, j, 0))
        w_scratch = pltpu.VMEM((PROJ_TN, k), BF16)
    else:
        w_spec = pl.BlockSpec((None, k, PROJ_TN), lambda j, i: (0, 0, j))
        w_scratch = pltpu.VMEM((k, PROJ_TN), BF16)
    in_specs = [pl.BlockSpec((tm, k), lambda j, i: (i, 0)), w_spec]
    args = [x, w]
    if rope is not None:
        in_specs += [pl.BlockSpec((tm, RET_DK), lambda j, i: (i, 0))] * 2
        args += list(rope)
    return pl.pallas_call(
        functools.partial(_in_proj_kernel, modes=tuple(modes), has_rope=rope is not None,
                          w_is_transposed=w_is_transposed),
        grid=(nt, m // tm),
        in_specs=in_specs,
        out_specs=pl.BlockSpec((tm, PROJ_TN), lambda j, i: (i, j)),
        out_shape=jax.ShapeDtypeStruct((m, nt * PROJ_TN), BF16),
        scratch_shapes=[w_scratch],
        compiler_params=_cparams(("arbitrary", "arbitrary")),
        name="in_proj",
    )(*args)


def _ret_log_decay(h):
    return math.log1p(-(2.0 ** (-5.0 - h)))


def _retention_kernel(q_ref, k_ref, v_ref, z_ref, nw_ref, o_ref, s_ref, dec_ref, qs_ref, ks_ref):
    first_step = (pl.program_id(0) == 0) & (pl.program_id(1) == 0)

    @pl.when(first_step)
    def _():
        ii = lax.broadcasted_iota(jnp.int32, (CHUNK, CHUNK), 0)
        jj = lax.broadcasted_iota(jnp.int32, (CHUNK, CHUNK), 1)
        diff = (ii - jj).astype(F32)
        row = ii.astype(F32)
        for h in range(RET_HEADS):
            lg = _ret_log_decay(h)
            dec_ref[h] = jnp.where(ii >= jj, jnp.exp(lg * jnp.maximum(diff, 0.0)), 0.0)
            qs_ref[h] = jnp.exp(lg * (row + 1.0))
            ks_ref[h] = jnp.exp(lg * (CHUNK - 1.0 - row))

    @pl.when(pl.program_id(1) == 0)
    def _():
        s_ref[...] = jnp.zeros_like(s_ref)

    heads = range(RET_HEADS)
    vcs = [slice(h * RET_DV, (h + 1) * RET_DV) for h in heads]
    nt_dims = (((1,), (1,)), ((), ()))
    tn_dims = (((0,), (0,)), ((), ()))
    states = [s_ref[h] for h in heads]
    for r in range(q_ref.shape[0] // CHUNK):
        rows = slice(r * CHUNK, (r + 1) * CHUNK)
        qs = [q_ref[rows, h * RET_DK:(h + 1) * RET_DK] for h in heads]
        ks = [k_ref[rows, h * RET_DK:(h + 1) * RET_DK] for h in heads]
        vs = [v_ref[rows, vc] for vc in vcs]
        scores = [lax.dot_general(qs[h], ks[h], nt_dims, preferred_element_type=F32) for h in heads]
        kvs = [lax.dot_general((ks[h].astype(F32) * ks_ref[h]).astype(BF16), vs[h], tn_dims,
                               preferred_element_type=F32) for h in heads]
        lhs = [jnp.concatenate([(scores[h] * dec_ref[h]).astype(BF16),
                                (qs[h].astype(F32) * qs_ref[h]).astype(BF16)], axis=1) for h in heads]
        outs = [jnp.dot(lhs[h], jnp.concatenate([vs[h], states[h].astype(BF16)], axis=0),
                        preferred_element_type=F32) for h in heads]
        states = [states[h] * math.exp(_ret_log_decay(h) * CHUNK) + kvs[h] for h in heads]
        for h in heads:
            o_ref[rows, vcs[h]] = (_rms_scale(outs[h], nw_ref[:, vcs[h]])
                                   * z_ref[rows, vcs[h]].astype(F32)).astype(o_ref.dtype)
    for h in heads:
        s_ref[h] = states[h]


def _retention(proj, norm_w, bsz, nt):
    m = proj.shape[0]
    tile = ROW_CHUNKS * CHUNK
    tab = pltpu.VMEM((RET_HEADS, CHUNK, CHUNK), F32)
    row = lambda width, col: pl.BlockSpec((tile, width), lambda b, t: (b * nt + t, col))
    return pl.pallas_call(
        _retention_kernel,
        grid=(bsz, nt),
        in_specs=[row(RET_QK, 0), row(RET_QK, 1), row(RET_W, 1), row(RET_W, 2),
                  pl.BlockSpec((1, RET_W), lambda b, t: (0, 0))],
        out_specs=row(RET_W, 0),
        out_shape=jax.ShapeDtypeStruct((m, RET_W), BF16),
        scratch_shapes=[pltpu.VMEM((RET_HEADS, RET_DK, RET_DV), F32), tab, tab, tab],
        compiler_params=_cparams(("arbitrary", "arbitrary")),
        name="retention",
    )(proj, proj, proj, proj, norm_w.reshape(1, RET_W))


def _s5_prep_kernel(lre_ref, lim_ref, ldt_ref, bre_ref, bim_ref, cre_ref, cim_ref,
                    pre_ref, pim_ref, w2_ref, wcre_ref, wcim_ref, wcare_ref, wcaim_ref, k0_ref):
    lre = lre_ref[...]
    lim = lim_ref[...]
    dt = jnp.exp(ldt_ref[...])
    tseg = pre_ref.shape[1]
    n = 2.0 * (lax.broadcasted_iota(jnp.int32, (tseg, 1), 0).astype(F32) + 1.0)
    mag = jnp.exp(n * (lre * dt))
    ang = n * (lim * dt)
    p_re = mag * jnp.cos(ang)
    p_im = mag * jnp.sin(ang)
    for sl in range(S5_SLABS):
        pre_ref[sl] = p_re[:, sl * LANES:(sl + 1) * LANES]
        pim_ref[sl] = p_im[:, sl * LANES:(sl + 1) * LANES]
    mag1 = jnp.exp(lre * dt)
    a_re = mag1 * jnp.cos(lim * dt)
    a_im = mag1 * jnp.sin(lim * dt)
    den = lre * lre + lim * lim
    nr = a_re - 1.0
    f_re = (nr * lre + a_im * lim) / den
    f_im = (a_im * lre - nr * lim) / den
    b_re = bre_ref[...]
    b_im = bim_ref[...]
    bb_re = f_re * b_re - f_im * b_im
    bb_im = f_re * b_im + f_im * b_re
    bba_re = a_re * bb_re - a_im * bb_im
    bba_im = a_re * bb_im + a_im * bb_re
    c_re = cre_ref[...]
    c_im = cim_ref[...]
    ca_re = c_re * a_re - c_im * a_im
    ca_im = c_re * a_im + c_im * a_re
    blk = S5_F // S5_JB
    rows = lax.broadcasted_iota(jnp.int32, (S5_BLK_IN, blk), 0)
    cols = lax.broadcasted_iota(jnp.int32, (S5_BLK_IN, blk), 1)
    same_group = (rows // S5_GH) == (cols // S5_P)

    def block_diag(a, j):
        piece = a[:, j * blk:(j + 1) * blk]
        return jnp.where(same_group, jnp.concatenate([piece] * 16, axis=0), 0.0)

    for j in range(S5_JB):
        bd_bre, bd_bim = block_diag(bb_re, j), block_diag(bb_im, j)
        bd_cre, bd_cim = block_diag(c_re, j), block_diag(c_im, j)
        w2_ref[j, :S5_BLK_IN, :blk] = block_diag(bba_re, j).astype(BF16)
        w2_ref[j, :S5_BLK_IN, blk:] = block_diag(bba_im, j).astype(BF16)
        w2_ref[j, S5_BLK_IN:, :blk] = bd_bre.astype(BF16)
        w2_ref[j, S5_BLK_IN:, blk:] = bd_bim.astype(BF16)
        wcre_ref[j] = bd_cre.astype(BF16)
        wcim_ref[j] = bd_cim.astype(BF16)
        wcare_ref[j] = block_diag(ca_re, j).astype(BF16)
        wcaim_ref[j] = block_diag(ca_im, j).astype(BF16)
        nt_dims = (((1,), (1,)), ((), ()))
        k0 = (lax.dot_general(bd_bre, bd_cre, nt_dims, preferred_element_type=F32)
              - lax.dot_general(bd_bim, bd_cim, nt_dims, preferred_element_type=F32))
        k0_ref[j] = k0.astype(BF16)


def _s5_prep(lam_re, lam_im, log_dt, b_re, b_im, c_re, c_im, tseg):
    flat = lambda a: a.reshape(1, S5_F)
    ldt = jnp.broadcast_to(log_dt[:, None], (S5_G, S5_P))
    bt = lambda a: a.reshape(S5_F, S5_GH).T
    ct = lambda a: a.transpose(1, 0, 2).reshape(S5_GH, S5_F)
    out = jax.ShapeDtypeStruct
    blk = S5_F // S5_JB
    wc = out((S5_JB, S5_BLK_IN, blk), BF16)
    return pl.pallas_call(
        _s5_prep_kernel,
        out_shape=(out((S5_SLABS, tseg, LANES), F32), out((S5_SLABS, tseg, LANES), F32),
                   out((S5_JB, 2 * S5_BLK_IN, 2 * blk), BF16), wc, wc, wc, wc,
                   out((S5_JB, S5_BLK_IN, S5_BLK_IN), BF16)),
        compiler_params=pltpu.CompilerParams(vmem_limit_bytes=VMEM_LIMIT),
        name="s5_prep",
    )(flat(lam_re), flat(lam_im), flat(ldt), bt(b_re), bt(b_im), ct(c_re.astype(F32)), ct(c_im.astype(F32)))


def _gelu_tanh(x):
    return 0.5 * x * (1.0 + jnp.tanh(math.sqrt(2.0 / math.pi) * (x + 0.044715 * (x * x * x))))


def _s5_kernel(ue_ref, uo_ref, ze_ref, zo_ref, w2_ref, wcre_ref, wcim_ref, wcare_ref, wcaim_ref, k0_ref,
               pre_ref, pim_ref, d_ref, wglu_ref,
               o_ref, bure_ref, buim_ref, cre_ref, cim_ref, xinre_ref, xinim_ref, yhat_ref,
               *, tseg, slab_group):
    ci = pl.program_id(1)

    @pl.when(ci == 0)
    def _():
        cre_ref[...] = jnp.zeros_like(cre_ref)
        cim_ref[...] = jnp.zeros_like(cim_ref)
        yhat_ref[...] = jnp.zeros_like(yhat_ref)

    u_e = ue_ref[...]
    u_o = uo_ref[...]
    spb = S5_SLABS // S5_JB
    for j in range(S5_JB):
        blk = slice(j * S5_BLK_IN, (j + 1) * S5_BLK_IN)
        res = jnp.dot(jnp.concatenate([u_e[:, blk], u_o[:, blk]], axis=1), w2_ref[j], preferred_element_type=F32)
        for s in range(spb):
            bure_ref[j * spb + s] = res[:, s * LANES:(s + 1) * LANES]
            buim_ref[j * spb + s] = res[:, (spb + s) * LANES:(spb + s + 1) * LANES]

    for g0 in range(0, S5_SLABS, slab_group):
        slabs = list(range(g0, g0 + slab_group))
        a_re = [jnp.broadcast_to(pre_ref[sl, 0:1, :], (S5_SEGS, LANES)) for sl in slabs]
        a_im = [jnp.broadcast_to(pim_ref[sl, 0:1, :], (S5_SEGS, LANES)) for sl in slabs]

        def scan_body(tau, carry, slabs=slabs, a_re=a_re, a_im=a_im):
            xr, xi = carry
            nr, ni = [], []
            for n, sl in enumerate(slabs):
                rows = pl.ds(tau, S5_SEGS, stride=tseg)
                br = bure_ref[sl, rows, :]
                bi = buim_ref[sl, rows, :]
                r = a_re[n] * xr[n] - a_im[n] * xi[n] + br
                i = a_re[n] * xi[n] + a_im[n] * xr[n] + bi
                bure_ref[sl, rows, :] = r
                buim_ref[sl, rows, :] = i
                nr.append(r)
                ni.append(i)
            return tuple(nr), tuple(ni)

        zero = tuple(jnp.zeros((S5_SEGS, LANES), F32) for _ in slabs)
        end_re, end_im = lax.fori_loop(0, tseg, scan_body, (zero, zero),
                                       unroll=S5_SCAN_UNROLL if tseg % S5_SCAN_UNROLL == 0 else 1)

        for n, sl in enumerate(slabs):
            at_re = pre_ref[sl, tseg - 1:tseg, :]
            at_im = pim_ref[sl, tseg - 1:tseg, :]
            xr = cre_ref[sl, 0:1, :]
            xi = cim_ref[sl, 0:1, :]
            rows_re, rows_im = [], []
            for s in range(S5_SEGS):
                rows_re.append(xr)
                rows_im.append(xi)
                er = end_re[n][s:s + 1]
                ei = end_im[n][s:s + 1]
                xr, xi = er + at_re * xr - at_im * xi, ei + at_re * xi + at_im * xr
            cre_ref[sl] = jnp.broadcast_to(xr, (S5_SEGS, LANES))
            cim_ref[sl] = jnp.broadcast_to(xi, (S5_SEGS, LANES))
            xinre_ref[sl] = jnp.concatenate(rows_re, axis=0)
            xinim_ref[sl] = jnp.concatenate(rows_im, axis=0)

    def fix_body(sl, carry):
        xin_re = xinre_ref[sl]
        xin_im = xinim_ref[sl]
        for tau in range(tseg):
            rows = pl.ds(tau, S5_SEGS, stride=tseg)
            pr = pre_ref[sl, tau:tau + 1, :]
            pi = pim_ref[sl, tau:tau + 1, :]
            bure_ref[sl, rows, :] = bure_ref[sl, rows, :] + (pr * xin_re - pi * xin_im)
            buim_ref[sl, rows, :] = buim_ref[sl, rows, :] + (pr * xin_im + pi * xin_re)
        return carry

    lax.fori_loop(0, S5_SLABS, fix_body, 0)

    nt_dims = (((1,), (1,)), ((), ()))
    y_odd, y_hat, y_dir = [], [], []
    for j in range(S5_JB):
        x_re = jnp.concatenate([bure_ref[j * spb + s] for s in range(spb)], axis=1).astype(BF16)
        x_im = jnp.concatenate([buim_ref[j * spb + s] for s in range(spb)], axis=1).astype(BF16)
        y_odd.append(lax.dot_general(x_re, wcre_ref[j], nt_dims, preferred_element_type=F32)
                     - lax.dot_general(x_im, wcim_ref[j], nt_dims, preferred_element_type=F32))
        y_hat.append(lax.dot_general(x_re, wcare_ref[j], nt_dims, preferred_element_type=F32)
                     - lax.dot_general(x_im, wcaim_ref[j], nt_dims, preferred_element_type=F32))
        y_dir.append(jnp.dot(u_e[:, j * S5_BLK_IN:(j + 1) * S5_BLK_IN], k0_ref[j], preferred_element_type=F32))
    y_hat = jnp.concatenate(y_hat, axis=1)
    ridx = lax.broadcasted_iota(jnp.int32, (y_hat.shape[0], 1), 0)
    y_even = jnp.where(ridx == 0, yhat_ref[...], pltpu.roll(y_hat, 1, 0)) + jnp.concatenate(y_dir, axis=1)
    yhat_ref[...] = y_hat[y_hat.shape[0] - 1:, :]
    halves = ((y_even, u_e, ze_ref), (jnp.concatenate(y_odd, axis=1), u_o, zo_ref))
    for hi, (y, u, z_ref) in enumerate(halves):
        y = _gelu_tanh(y + d_ref[...] * u.astype(F32))
        gate = jnp.dot(y.astype(BF16), wglu_ref[...], preferred_element_type=F32)
        y = y * (1.0 / (1.0 + jnp.exp(-gate)))
        o_ref[:, hi * S5_W:(hi + 1) * S5_W] = (y * z_ref[...].astype(F32)).astype(o_ref.dtype)


def _s5_time_tile(rows_per_batch):
    for tseg in (44, 66, 22, 6, 2):
        t = 2 * S5_SEGS * tseg
        if rows_per_batch % t == 0 and (S5_SEGS * tseg) % 16 == 0:
            return t, tseg
    raise ValueError(f"no S5 time tile for {rows_per_batch}")


def _s5(proj, bsz, prep, d, wglu, t, tseg):
    m, width = proj.shape
    p_re, p_im, w2, wcre, wcim, wcare, wcaim, k0 = prep
    rows = t // 2
    nt = m // bsz // t
    pairs = proj.reshape(m // 2, 2 * width)
    ucol = (2 * RET_QK + 2 * RET_W) // S5_W
    odd = width // S5_W
    const = lambda a: pl.BlockSpec(a.shape, lambda b, i: (0,) * a.ndim, pipeline_mode=pl.Buffered(1))
    tile = lambda col: pl.BlockSpec((rows, S5_W), lambda b, i: (b * nt + i, col))
    d2 = d.reshape(1, S5_W)
    out = pl.pallas_call(
        functools.partial(_s5_kernel, tseg=tseg, slab_group=8),
        grid=(bsz, nt),
        in_specs=[tile(ucol), tile(odd + ucol), tile(ucol + 1), tile(odd + ucol + 1),
                  const(w2), const(wcre), const(wcim), const(wcare), const(wcaim), const(k0),
                  const(p_re), const(p_im), const(d2), const(wglu)],
        out_specs=pl.BlockSpec((rows, 2 * S5_W), lambda b, i: (b * nt + i, 0)),
        out_shape=jax.ShapeDtypeStruct((m // 2, 2 * S5_W), BF16),
        scratch_shapes=[pltpu.VMEM((S5_SLABS, rows, LANES), F32), pltpu.VMEM((S5_SLABS, rows, LANES), F32)]
        + [pltpu.VMEM((S5_SLABS, S5_SEGS, LANES), F32)] * 4 + [pltpu.VMEM((1, S5_W), F32)],
        compiler_params=_cparams(("arbitrary", "arbitrary")),
        name="s5",
    )(pairs, pairs, pairs, pairs, w2, wcre, wcim, wcare, wcaim, k0, p_re, p_im, d2, wglu)
    return out.reshape(m, S5_W)


def _out_ab_kernel(lead_ref, xa_ref, xb_ref, xc_ref, oa_ref, ob_ref, w_ref, nw_ref, wg_ref,
                   h1_ref, xn_ref, gl_ref):
    t = pl.program_id(1)
    first = jnp.where(t == 0, lead_ref[...], xa_ref[...])
    h0 = (first, xb_ref[...], xc_ref[...])
    sls = [slice(r * CHUNK, (r + 1) * CHUNK) for r in range(ROW_CHUNKS)]
    accs = [jnp.dot(oa_ref[sl, :], w_ref[:RET_W, :], preferred_element_type=F32)
            + jnp.dot(ob_ref[sl, :], w_ref[RET_W:, :], preferred_element_type=F32) for sl in sls]
    h1s = [h0[r] + accs[r] for r in range(ROW_CHUNKS)]
    xns = [_rms_scale(h1, nw_ref[...]).astype(BF16) for h1 in h1s]
    gls = [jnp.dot(xn, wg_ref[...], preferred_element_type=F32) for xn in xns]
    for r, sl in enumerate(sls):
        h1_ref[sl, :] = h1s[r]
        xn_ref[sl, :] = xns[r]
        gl_ref[sl, :] = gls[r]


def _out_ab(x, lead, oa, ob, w, norm_w, wg, nt):
    bsz, _, d = x.shape
    tile = ROW_CHUNKS * CHUNK
    m = bsz * nt * tile
    row = lambda width: pl.BlockSpec((tile, width), lambda b, t: (b * nt + t, 0))
    const = lambda a: pl.BlockSpec(a.shape, lambda b, t: (0, 0))
    out = jax.ShapeDtypeStruct
    return pl.pallas_call(
        _out_ab_kernel,
        grid=(bsz, nt),
        in_specs=[pl.BlockSpec((CHUNK, d), lambda b, t: (0, 0))] + _chunk_specs(d, 1)
        + [row(RET_W), row(S5_W), const(w), pl.BlockSpec((1, d), lambda b, t: (0, 0)), const(wg)],
        out_specs=[row(d), row(d), row(GATE_PAD)],
        out_shape=(out((m, d), F32), out((m, d), BF16), out((m, GATE_PAD), F32)),
        compiler_params=_cparams(("parallel", "parallel")),
        name="out_proj_ab",
    )(lead, x, x, x, oa, ob, w, norm_w.reshape(1, d), wg)


LOG2E = 1.4426950408889634


def _split2(x):
    x1 = x.astype(BF16)
    return x1, (x - x1.astype(F32)).astype(BF16)


def _neg_abs(x):
    return -jnp.abs(x)


def _block_ref_rows(b, level):
    rows, width = b.shape
    half = 1 << level
    blk = 2 * half
    if blk >= 8:
        b3 = b.reshape(rows // blk, blk, width)
        ref = jnp.broadcast_to(b3[:, half - 1:half, :], b3.shape)
        return ref.reshape(rows, width)
    b3 = b.reshape(rows // 8, 8, width)
    sub = lax.broadcasted_iota(jnp.int32, b3.shape, 1)
    pick = lambda r: jnp.broadcast_to(b3[:, r:r + 1, :], b3.shape)
    if blk == 4:
        ref = jnp.where(sub < 4, pick(1), pick(5))
    else:
        ref = jnp.where(sub < 2, pick(0), jnp.where(sub < 4, pick(2), jnp.where(sub < 6, pick(4), pick(6))))
    return ref.reshape(rows, width)


def _gla_kernel(q_ref, k_ref, v_ref, z_ref, gl_ref, wg_ref, bg_ref, nw_ref, h_ref, wo_ref, fw_ref,
                y_ref, st_ref, mix_ref, wobf_ref, *, nc, nchunks):
    c = pl.program_id(0)
    n = jnp.minimum(c, nchunks - 1) % nc

    @pl.when(c == 0)
    def _():
        mix_ref[...] = jnp.zeros_like(mix_ref)
        wobf_ref[...] = wo_ref[...].astype(BF16)

    @pl.when(n == 0)
    def _():
        st_ref[...] = jnp.zeros_like(st_ref)

    ridx = lax.broadcasted_iota(jnp.int32, (CHUNK, 1), 0)
    ii = lax.broadcasted_iota(jnp.int32, (CHUNK, CHUNK), 0)
    jj = lax.broadcasted_iota(jnp.int32, (CHUNK, CHUNK), 1)
    pair_code = jnp.where(ii > jj, ii ^ jj, 0)
    eye = ii == jj
    tri = (ii >= jj).astype(BF16)

    gl1, gl2 = _split2(gl_ref[:, :GLA_RANK])
    wg1, wg2 = _split2(wg_ref[...])
    x = jnp.dot(jnp.concatenate([gl1, gl2, gl1], axis=1), jnp.concatenate([wg1, wg1, wg2], axis=0),
                preferred_element_type=F32) + bg_ref[...]
    log_a = (jnp.minimum(x, 0.0) - jnp.log(1.0 + jnp.exp(-jnp.abs(x)))) * (LOG2E / GLA_TAU)
    log_a = jnp.where(ridx >= jnp.where(n > 0, 0, PAD), log_a, 0.0)
    g1, g2 = _split2(log_a)
    b_all = jnp.dot(jnp.concatenate([tri, tri], axis=1), jnp.concatenate([g1, g2], axis=0),
                    preferred_element_type=F32)
    odd = (ridx & 1) == 1
    h2 = h_ref[...]

    def finish(heads, qs, ks, bs, scores):
        for i, h in enumerate(heads):
            vc = slice(h * GLA_DV, (h + 1) * GLA_DV)
            q, k, b = qs[i], ks[i], bs[i]
            v = v_ref[:, vc]
            b_last = b[CHUNK - 1:CHUNK, :]
            o = jnp.dot(scores[i].astype(BF16), v, preferred_element_type=F32)
            st = st_ref[h]
            qe = (q * jnp.exp2(b)).astype(BF16)
            o = o + lax.dot_general(qe, st.astype(BF16), (((1,), (1,)), ((), ())), preferred_element_type=F32)
            ke = (k * jnp.exp2(b_last - b)).astype(BF16)
            kv_t = lax.dot_general(v, ke, (((0,), (0,)), ((), ())), preferred_element_type=F32)
            st_ref[h] = st * jnp.exp2(b_last) + kv_t
            mix_ref[:, vc] = (_rms_scale(o, nw_ref[:, vc]) * z_ref[:, vc].astype(F32)).astype(BF16)

    pending = []
    for h0 in range(0, GLA_HEADS, 2):
        heads = (h0, h0 + 1)
        qs, ks, bs, scores = [], [], [], []
        for h in heads:
            kc = slice(h * GLA_DK, (h + 1) * GLA_DK)
            vc = slice(h * GLA_DV, (h + 1) * GLA_DV)
            h2 = h2 + jnp.dot(mix_ref[:, vc], wobf_ref[vc, :], preferred_element_type=F32)
            qs.append(q_ref[:, kc].astype(F32))
            ks.append(k_ref[:, kc].astype(F32))
            bs.append(b_all[:, kc])
            scores.append(jnp.where(eye, jnp.sum(qs[-1] * ks[-1], axis=1, keepdims=True), 0.0))
        for lv in range(GLA_LEVELS):
            half = 1 << lv
            if lv == GLA_LEVELS // 2 and pending:
                finish(*pending.pop())
            zz = []
            for q, k, b in zip(qs, ks, bs):
                if lv == 0:
                    expo = jnp.where(odd, b - pltpu.roll(b, 1, 0), 0.0)
                else:
                    expo = _neg_abs(b - _block_ref_rows(b, lv))
                if half >= 8:
                    sel = jnp.concatenate([(q if r % 2 else k)[r * half:(r + 1) * half]
                                           for r in range(CHUNK // half)], axis=0)
                else:
                    sel = jnp.where(((ridx >> lv) & 1) == 1, q, k)
                zz.append((sel * jnp.exp2(expo)).astype(BF16))
            z2 = jnp.concatenate(zz, axis=0)
            gram = lax.dot_general(z2, z2, (((1,), (1,)), ((), ())), preferred_element_type=F32)
            mask = (pair_code >> lv) == 1
            for i in range(2):
                blk = gram[i * CHUNK:(i + 1) * CHUNK, i * CHUNK:(i + 1) * CHUNK]
                scores[i] = scores[i] + jnp.where(mask, blk, 0.0)
        pending.append((heads, qs, ks, bs, scores))
    finish(*pending.pop())

    y_ref[...] = _rms_scale(h2, fw_ref[...])


def _gla(proj, glow, w_gate, b_gate, norm_w, h1, w_out, final_w, bsz, nc):
    m, d = h1.shape
    nchunks = m // CHUNK
    const = lambda *shape: pl.BlockSpec(shape, lambda c: (0,) * len(shape))
    cur = lambda width, col: pl.BlockSpec((CHUNK, width), lambda c: (jnp.minimum(c, nchunks - 1), col))
    prev = lambda c: jnp.maximum(c - 1, 0)
    return pl.pallas_call(
        functools.partial(_gla_kernel, nc=nc, nchunks=nchunks),
        grid=(nchunks + 1,),
        in_specs=[cur(GLA_QK, 0), cur(GLA_QK, 1), cur(GLA_W, 1), cur(GLA_W, 2), cur(GATE_PAD, 0),
                  const(GLA_RANK, GLA_QK), const(1, GLA_QK), const(1, GLA_W),
                  pl.BlockSpec((CHUNK, d), lambda c: (prev(c), 0)),
                  pl.BlockSpec((None, GLA_W, d), lambda c: (0, 0, 0), pipeline_mode=pl.Buffered(1)),
                  const(1, d)],
        out_specs=pl.BlockSpec((None, CHUNK, d),
                               lambda c: (prev(c) // nc, jnp.maximum(prev(c) % nc - 1, 0), 0)),
        out_shape=jax.ShapeDtypeStruct((bsz, (nc - 1) * CHUNK, d), F32),
        scratch_shapes=[pltpu.VMEM((GLA_HEADS, GLA_DV, GLA_DK), F32), pltpu.VMEM((CHUNK, GLA_W), BF16),
                        pltpu.VMEM((GLA_W, d), BF16)],
        compiler_params=_cparams(("arbitrary",)),
        name="gla_out",
    )(proj, proj, proj, proj, glow, w_gate.astype(F32), b_gate.reshape(1, GLA_QK).astype(F32),
      norm_w.reshape(1, GLA_W).astype(F32), h1, w_out, final_w.reshape(1, d).astype(F32))


def _rope_tables(bsz, rows_per_batch):
    pos = np.maximum(np.arange(rows_per_batch, dtype=np.float64) - PAD, 0.0)
    inv_freq = np.power(ROPE_BASE, -np.arange(0, RET_DK, 2, dtype=np.float64) / RET_DK)
    ang = pos[:, None] * inv_freq[None, :]
    cos, sin = np.cos(ang), np.sin(ang)
    cos2 = np.tile(np.concatenate([cos, cos], axis=1), (bsz, 1)).astype(np.float32)
    sin2 = np.tile(np.concatenate([-sin, sin], axis=1), (bsz, 1)).astype(np.float32)
    return jnp.asarray(cos2), jnp.asarray(sin2)


def kernel(x, meta, norm_ab_w, w_in_ab, ret_norm_w, s5_lam_re, s5_lam_im, s5_log_dt, s5_b_re, s5_b_im,
           s5_c_re, s5_c_im, s5_d, s5_w_glu, w_out_ab, norm_c_w, w_in_c, gla_w_gate, gla_b_gate,
           gla_norm_w, w_out_c, final_norm_w):
    bsz, seq, d = x.shape
    assert seq % CHUNK == 0 and w_in_ab.shape[0] == 1 and w_in_c.shape[0] == 1
    rpb = seq + CHUNK
    nc = rpb // CHUNK
    assert nc % ROW_CHUNKS == 0
    nt = nc // ROW_CHUNKS
    m = bsz * rpb
    tm = _pick_tile(m, (1408, 768, 384, 256, 128))
    lead = jnp.concatenate([jnp.zeros((PAD, d), x.dtype), meta.astype(x.dtype)], axis=0)

    xn0 = _embed_norm(x, lead, norm_ab_w[0], nt)
    plain, silu = ("plain", 1.0), ("silu", 1.0)
    modes_ab = [("rope", 1.0), ("rope", RET_DK ** -0.5), plain, plain, silu, silu, plain, silu]
    proj0 = _in_proj(xn0, w_in_ab, modes_ab, tm, rope=_rope_tables(bsz, rpb))
    o_a = _retention(proj0, ret_norm_w[0], bsz, nt)
    t, tseg = _s5_time_tile(rpb)
    prep = _s5_prep(s5_lam_re[0], s5_lam_im[0], s5_log_dt[0], s5_b_re[0], s5_b_im[0],
                    s5_c_re[0], s5_c_im[0], tseg)
    o_b = _s5(proj0, bsz, prep, s5_d[0], s5_w_glu[0].astype(BF16), t, tseg)

    w_in_c_t = jnp.swapaxes(w_in_c, 1, 2)
    n_main = 2 * GLA_QK + 2 * GLA_W
    wg = jnp.pad(w_in_c_t[0, n_main:, :], ((0, GATE_PAD - GLA_RANK), (0, 0))).astype(BF16).T
    h1, xn1, glow = _out_ab(x, lead, o_a, o_b, w_out_ab[0].astype(BF16), norm_c_w[0], wg, nt)

    modes_c = [("plain", GLA_DK ** -0.5), plain, plain, plain, silu, silu]
    proj1 = _in_proj(xn1, w_in_c_t, modes_c, tm, w_is_transposed=True)
    return _gla(proj1, glow, gla_w_gate[0], gla_b_gate[0], gla_norm_w[0], h1, w_out_c,
                final_norm_w, bsz, nc)
```

```python
import functools
import math

import numpy as np
import jax
import jax.numpy as jnp
from jax import lax
from jax.experimental import pallas as pl
from jax.experimental.pallas import tpu as pltpu

F32 = jnp.float32
BF16 = jnp.bfloat16

N_META = 16
LANES = 128
SUBLANES = 8
CHUNK = 128
PAD = CHUNK - N_META
EPS = 1e-6
ROW_CHUNKS = 3

RET_HEADS = 8
RET_DK = 128
RET_DV = 256
RET_QK = RET_HEADS * RET_DK
RET_W = RET_HEADS * RET_DV
ROPE_BASE = 10000.0

S5_W = 1024
S5_GH = 16
S5_G = 64
S5_P = 64
S5_F = S5_G * S5_P
S5_SLABS = S5_F // LANES
S5_JB = 4
S5_BLK_IN = S5_W // S5_JB
S5_SEGS = SUBLANES
S5_SCAN_UNROLL = 4

GLA_HEADS = 4
GLA_DK = 256
GLA_DV = 512
GLA_QK = GLA_HEADS * GLA_DK
GLA_W = GLA_HEADS * GLA_DV
GLA_RANK = 16
GLA_TAU = 16.0
GLA_LEVELS = 7
GATE_PAD = LANES

PROJ_TN = 1024
PROJ_ROW_SPLIT = 4
VMEM_LIMIT = 56 * 1024 * 1024


def _cparams(sem):
    return pltpu.CompilerParams(dimension_semantics=sem, vmem_limit_bytes=VMEM_LIMIT)


def _silu(x):
    return x * (1.0 / (1.0 + jnp.exp(-x)))


def _pick_tile(n, candidates):
    for c in candidates:
        if n % c == 0:
            return c
    raise ValueError(f"no tile for {n}")


def _rms_scale(x, w):
    return x * lax.rsqrt(jnp.mean(x * x, axis=-1, keepdims=True) + EPS) * w


def _chunk_specs(d, n_before):
    return [pl.BlockSpec((None, CHUNK, d),
                         lambda b, t, k=k: (b, jnp.maximum(ROW_CHUNKS * t + k - n_before, 0), 0))
            for k in range(ROW_CHUNKS)]


def _embed_norm_kernel(lead_ref, xa_ref, xb_ref, xc_ref, w_ref, o_ref):
    t = pl.program_id(1)
    first = jnp.where(t == 0, lead_ref[...], xa_ref[...])
    for r, rows in enumerate((first, xb_ref[...], xc_ref[...])):
        o_ref[r * CHUNK:(r + 1) * CHUNK, :] = _rms_scale(rows, w_ref[...]).astype(o_ref.dtype)


def _embed_norm(x, lead, w, nt):
    bsz, _, d = x.shape
    tile = ROW_CHUNKS * CHUNK
    return pl.pallas_call(
        _embed_norm_kernel,
        grid=(bsz, nt),
        in_specs=[pl.BlockSpec((CHUNK, d), lambda b, t: (0, 0))] + _chunk_specs(d, 1)
        + [pl.BlockSpec((1, d), lambda b, t: (0, 0))],
        out_specs=pl.BlockSpec((tile, d), lambda b, t: (b * nt + t, 0)),
        out_shape=jax.ShapeDtypeStruct((bsz * nt * tile, d), BF16),
        compiler_params=_cparams(("parallel", "parallel")),
        name="embed_norm",
    )(lead, x, x, x, w.reshape(1, d))


def _rope_rows(a, cos, sin, scale):
    outs = []
    for h in range(a.shape[1] // RET_DK):
        blk = a[:, h * RET_DK:(h + 1) * RET_DK]
        r = blk * cos + pltpu.roll(blk, RET_DK // 2, 1) * sin
        outs.append(r * scale if scale != 1.0 else r)
    return jnp.concatenate(outs, axis=1)


def _in_proj_kernel(*refs, modes, has_rope, w_is_transposed):
    if has_rope:
        x_ref, w_ref, cos_ref, sin_ref, o_ref, wbf_ref = refs
    else:
        x_ref, w_ref, o_ref, wbf_ref = refs
    j = pl.program_id(0)

    @pl.when(pl.program_id(1) == 0)
    def _():
        wbf_ref[...] = w_ref[...].astype(BF16)

    contract = (((1,), (1 if w_is_transposed else 0,)), ((), ()))
    for mode in sorted(set(modes)):
        cond = functools.reduce(jnp.logical_or, [j == jj for jj, mm in enumerate(modes) if mm == mode])

        @pl.when(cond)
        def _(mode=mode):
            kind, scale = mode
            rows = x_ref.shape[0] // PROJ_ROW_SPLIT
            sls = [slice(r * rows, (r + 1) * rows) for r in range(PROJ_ROW_SPLIT)]
            accs = [lax.dot_general(x_ref[sl, :], wbf_ref[...], contract, preferred_element_type=F32)
                    for sl in sls]
            for sl, acc in zip(sls, accs):
                if kind == "rope":
                    acc = _rope_rows(acc, cos_ref[sl, :], sin_ref[sl, :], scale)
                elif kind == "silu":
                    acc = _silu(acc)
                elif scale != 1.0:
                    acc = acc * scale
                o_ref[sl, :] = acc.astype(o_ref.dtype)


def _in_proj(x, w, modes, tm, rope=None, w_is_transposed=False):
    m, k = x.shape
    nt = len(modes)
    assert m % tm == 0 and tm % (PROJ_ROW_SPLIT * 16) == 0
    if w_is_transposed:
        w_spec = pl.BlockSpec((None, PROJ_TN, k), lambda j, i: (0, j, 0))
        w_scratch = pltpu.VMEM((PROJ_TN, k), BF16)
    else:
        w_spec = pl.BlockSpec((None, k, PROJ_TN), lambda j, i: (0, 0, j))
        w_scratch = pltpu.VMEM((k, PROJ_TN), BF16)
    in_specs = [pl.BlockSpec((tm, k), lambda j, i: (i, 0)), w_spec]
    args = [x, w]
    if rope is not None:
        in_specs += [pl.BlockSpec((tm, RET_DK), lambda j, i: (i, 0))] * 2
        args += list(rope)
    return pl.pallas_call(
        functools.partial(_in_proj_kernel, modes=tuple(modes), has_rope=rope is not None,
                          w_is_transposed=w_is_transposed),
        grid=(nt, m // tm),
        in_specs=in_specs,
        out_specs=pl.BlockSpec((tm, PROJ_TN), lambda j, i: (i, j)),
        out_shape=jax.ShapeDtypeStruct((m, nt * PROJ_TN), BF16),
        scratch_shapes=[w_scratch],
        compiler_params=_cparams(("arbitrary", "arbitrary")),
        name="in_proj",
    )(*args)


def _ret_log_decay(h):
    return math.log1p(-(2.0 ** (-5.0 - h)))


def _retention_kernel(q_ref, k_ref, v_ref, z_ref, nw_ref, o_ref, s_ref, dec_ref, qs_ref, ks_ref):
    first_step = (pl.program_id(0) == 0) & (pl.program_id(1) == 0)

    @pl.when(first_step)
    def _():
        ii = lax.broadcasted_iota(jnp.int32, (CHUNK, CHUNK), 0)
        jj = lax.broadcasted_iota(jnp.int32, (CHUNK, CHUNK), 1)
        diff = (ii - jj).astype(F32)
        row = ii.astype(F32)
        for h in range(RET_HEADS):
            lg = _ret_log_decay(h)
            dec_ref[h] = jnp.where(ii >= jj, jnp.exp(lg * jnp.maximum(diff, 0.0)), 0.0)
            qs_ref[h] = jnp.exp(lg * (row + 1.0))
            ks_ref[h] = jnp.exp(lg * (CHUNK - 1.0 - row))

    @pl.when(pl.program_id(1) == 0)
    def _():
        s_ref[...] = jnp.zeros_like(s_ref)

    heads = range(RET_HEADS)
    vcs = [slice(h * RET_DV, (h + 1) * RET_DV) for h in heads]
    nt_dims = (((1,), (1,)), ((), ()))
    tn_dims = (((0,), (0,)), ((), ()))
    states = [s_ref[h] for h in heads]
    for r in range(q_ref.shape[0] // CHUNK):
        rows = slice(r * CHUNK, (r + 1) * CHUNK)
        qs = [q_ref[rows, h * RET_DK:(h + 1) * RET_DK] for h in heads]
        ks = [k_ref[rows, h * RET_DK:(h + 1) * RET_DK] for h in heads]
        vs = [v_ref[rows, vc] for vc in vcs]
        scores = [lax.dot_general(qs[h], ks[h], nt_dims, preferred_element_type=F32) for h in heads]
        kvs = [lax.dot_general((ks[h].astype(F32) * ks_ref[h]).astype(BF16), vs[h], tn_dims,
                               preferred_element_type=F32) for h in heads]
        lhs = [jnp.concatenate([(scores[h] * dec_ref[h]).astype(BF16),
                                (qs[h].astype(F32) * qs_ref[h]).astype(BF16)], axis=1) for h in heads]
        outs = [jnp.dot(lhs[h], jnp.concatenate([vs[h], states[h].astype(BF16)], axis=0),
                        preferred_element_type=F32) for h in heads]
        states = [states[h] * math.exp(_ret_log_decay(h) * CHUNK) + kvs[h] for h in heads]
        for h in heads:
            o_ref[rows, vcs[h]] = (_rms_scale(outs[h], nw_ref[:, vcs[h]])
                                   * z_ref[rows, vcs[h]].astype(F32)).astype(o_ref.dtype)
    for h in heads:
        s_ref[h] = states[h]


def _retention(proj, norm_w, bsz, nt):
    m = proj.shape[0]
    tile = ROW_CHUNKS * CHUNK
    tab = pltpu.VMEM((RET_HEADS, CHUNK, CHUNK), F32)
    row = lambda width, col: pl.BlockSpec((tile, width), lambda b, t: (b * nt + t, col))
    return pl.pallas_call(
        _retention_kernel,
        grid=(bsz, nt),
        in_specs=[row(RET_QK, 0), row(RET_QK, 1), row(RET_W, 1), row(RET_W, 2),
                  pl.BlockSpec((1, RET_W), lambda b, t: (0, 0))],
        out_specs=row(RET_W, 0),
        out_shape=jax.ShapeDtypeStruct((m, RET_W), BF16),
        scratch_shapes=[pltpu.VMEM((RET_HEADS, RET_DK, RET_DV), F32), tab, tab, tab],
        compiler_params=_cparams(("arbitrary", "arbitrary")),
        name="retention",
    )(proj, proj, proj, proj, norm_w.reshape(1, RET_W))


def _s5_prep_kernel(lre_ref, lim_ref, ldt_ref, bre_ref, bim_ref, cre_ref, cim_ref,
                    pre_ref, pim_ref, w2_ref, wcre_ref, wcim_ref, wcare_ref, wcaim_ref, k0_ref):
    lre = lre_ref[...]
    lim = lim_ref[...]
    dt = jnp.exp(ldt_ref[...])
    tseg = pre_ref.shape[1]
    n = 2.0 * (lax.broadcasted_iota(jnp.int32, (tseg, 1), 0).astype(F32) + 1.0)
    mag = jnp.exp(n * (lre * dt))
    ang = n * (lim * dt)
    p_re = mag * jnp.cos(ang)
    p_im = mag * jnp.sin(ang)
    for sl in range(S5_SLABS):
        pre_ref[sl] = p_re[:, sl * LANES:(sl + 1) * LANES]
        pim_ref[sl] = p_im[:, sl * LANES:(sl + 1) * LANES]
    mag1 = jnp.exp(lre * dt)
    a_re = mag1 * jnp.cos(lim * dt)
    a_im = mag1 * jnp.sin(lim * dt)
    den = lre * lre + lim * lim
    nr = a_re - 1.0
    f_re = (nr * lre + a_im * lim) / den
    f_im = (a_im * lre - nr * lim) / den
    b_re = bre_ref[...]
    b_im = bim_ref[...]
    bb_re = f_re * b_re - f_im * b_im
    bb_im = f_re * b_im + f_im * b_re
    bba_re = a_re * bb_re - a_im * bb_im
    bba_im = a_re * bb_im + a_im * bb_re
    c_re = cre_ref[...]
    c_im = cim_ref[...]
    ca_re = c_re * a_re - c_im * a_im
    ca_im = c_re * a_im + c_im * a_re
    blk = S5_F // S5_JB
    rows = lax.broadcasted_iota(jnp.int32, (S5_BLK_IN, blk), 0)
    cols = lax.broadcasted_iota(jnp.int32, (S5_BLK_IN, blk), 1)
    same_group = (rows // S5_GH) == (cols // S5_P)

    def block_diag(a, j):
        piece = a[:, j * blk:(j + 1) * blk]
        return jnp.where(same_group, jnp.concatenate([piece] * 16, axis=0), 0.0)

    for j in range(S5_JB):
        bd_bre, bd_bim = block_diag(bb_re, j), block_diag(bb_im, j)
        bd_cre, bd_cim = block_diag(c_re, j), block_diag(c_im, j)
        w2_ref[j, :S5_BLK_IN, :blk] = block_diag(bba_re, j).astype(BF16)
        w2_ref[j, :S5_BLK_IN, blk:] = block_diag(bba_im, j).astype(BF16)
        w2_ref[j, S5_BLK_IN:, :blk] = bd_bre.astype(BF16)
        w2_ref[j, S5_BLK_IN:, blk:] = bd_bim.astype(BF16)
        wcre_ref[j] = bd_cre.astype(BF16)
        wcim_ref[j] = bd_cim.astype(BF16)
        wcare_ref[j] = block_diag(ca_re, j).astype(BF16)
        wcaim_ref[j] = block_diag(ca_im, j).astype(BF16)
        nt_dims = (((1,), (1,)), ((), ()))
        k0 = (lax.dot_general(bd_bre, bd_cre, nt_dims, preferred_element_type=F32)
              - lax.dot_general(bd_bim, bd_cim, nt_dims, preferred_element_type=F32))
        k0_ref[j] = k0.astype(BF16)


def _s5_prep(lam_re, lam_im, log_dt, b_re, b_im, c_re, c_im, tseg):
    flat = lambda a: a.reshape(1, S5_F)
    ldt = jnp.broadcast_to(log_dt[:, None], (S5_G, S5_P))
    bt = lambda a: a.reshape(S5_F, S5_GH).T
    ct = lambda a: a.transpose(1, 0, 2).reshape(S5_GH, S5_F)
    out = jax.ShapeDtypeStruct
    blk = S5_F // S5_JB
    wc = out((S5_JB, S5_BLK_IN, blk), BF16)
    return pl.pallas_call(
        _s5_prep_kernel,
        out_shape=(out((S5_SLABS, tseg, LANES), F32), out((S5_SLABS, tseg, LANES), F32),
                   out((S5_JB, 2 * S5_BLK_IN, 2 * blk), BF16), wc, wc, wc, wc,
                   out((S5_JB, S5_BLK_IN, S5_BLK_IN), BF16)),
        compiler_params=pltpu.CompilerParams(vmem_limit_bytes=VMEM_LIMIT),
        name="s5_prep",
    )(flat(lam_re), flat(lam_im), flat(ldt), bt(b_re), bt(b_im), ct(c_re.astype(F32)), ct(c_im.astype(F32)))


def _gelu_tanh(x):
    return 0.5 * x * (1.0 + jnp.tanh(math.sqrt(2.0 / math.pi) * (x + 0.044715 * (x * x * x))))


def _s5_kernel(u_ref, z_ref, w2_ref, wcre_ref, wcim_ref, wcare_ref, wcaim_ref, k0_ref,
               pre_ref, pim_ref, d_ref, wglu_ref,
               o_ref, bure_ref, buim_ref, cre_ref, cim_ref, xinre_ref, xinim_ref, yhat_ref, par_ref,
               *, tseg, slab_group):
    ci = pl.program_id(1)

    @pl.when(ci == 0)
    def _():
        cre_ref[...] = jnp.zeros_like(cre_ref)
        cim_ref[...] = jnp.zeros_like(cim_ref)
        yhat_ref[...] = jnp.zeros_like(yhat_ref)

    half = u_ref.shape[0] // 2
    lane_slabs = S5_W // LANES
    u_all = u_ref[...].astype(F32)
    for s in range(lane_slabs):
        par_ref[s] = u_all[:, s * LANES:(s + 1) * LANES]
    parity = lambda p: jnp.concatenate([par_ref[s, pl.ds(p, half, stride=2), :] for s in range(lane_slabs)], axis=1)
    u_e32, u_o32 = parity(0), parity(1)
    u_e, u_o = u_e32.astype(BF16), u_o32.astype(BF16)
    spb = S5_SLABS // S5_JB
    for j in range(S5_JB):
        blk = slice(j * S5_BLK_IN, (j + 1) * S5_BLK_IN)
        res = jnp.dot(jnp.concatenate([u_e[:, blk], u_o[:, blk]], axis=1), w2_ref[j], preferred_element_type=F32)
        for s in range(spb):
            bure_ref[j * spb + s] = res[:, s * LANES:(s + 1) * LANES]
            buim_ref[j * spb + s] = res[:, (spb + s) * LANES:(spb + s + 1) * LANES]

    for g0 in range(0, S5_SLABS, slab_group):
        slabs = list(range(g0, g0 + slab_group))
        a_re = [jnp.broadcast_to(pre_ref[sl, 0:1, :], (S5_SEGS, LANES)) for sl in slabs]
        a_im = [jnp.broadcast_to(pim_ref[sl, 0:1, :], (S5_SEGS, LANES)) for sl in slabs]

        def scan_body(tau, carry, slabs=slabs, a_re=a_re, a_im=a_im):
            xr, xi = carry
            nr, ni = [], []
            for n, sl in enumerate(slabs):
                rows = pl.ds(tau, S5_SEGS, stride=tseg)
                br = bure_ref[sl, rows, :]
                bi = buim_ref[sl, rows, :]
                r = a_re[n] * xr[n] - a_im[n] * xi[n] + br
                i = a_re[n] * xi[n] + a_im[n] * xr[n] + bi
                bure_ref[sl, rows, :] = r
                buim_ref[sl, rows, :] = i
                nr.append(r)
                ni.append(i)
            return tuple(nr), tuple(ni)

        zero = tuple(jnp.zeros((S5_SEGS, LANES), F32) for _ in slabs)
        end_re, end_im = lax.fori_loop(0, tseg, scan_body, (zero, zero),
                                       unroll=S5_SCAN_UNROLL if tseg % S5_SCAN_UNROLL == 0 else 1)

        for n, sl in enumerate(slabs):
            at_re = pre_ref[sl, tseg - 1:tseg, :]
            at_im = pim_ref[sl, tseg - 1:tseg, :]
            xr = cre_ref[sl, 0:1, :]
            xi = cim_ref[sl, 0:1, :]
            rows_re, rows_im = [], []
            for s in range(S5_SEGS):
                rows_re.append(xr)
                rows_im.append(xi)
                er = end_re[n][s:s + 1]
                ei = end_im[n][s:s + 1]
                xr, xi = er + at_re * xr - at_im * xi, ei + at_re * xi + at_im * xr
            cre_ref[sl] = jnp.broadcast_to(xr, (S5_SEGS, LANES))
            cim_ref[sl] = jnp.broadcast_to(xi, (S5_SEGS, LANES))
            xinre_ref[sl] = jnp.concatenate(rows_re, axis=0)
            xinim_ref[sl] = jnp.concatenate(rows_im, axis=0)

    def fix_body(sl, carry):
        xin_re = xinre_ref[sl]
        xin_im = xinim_ref[sl]
        for tau in range(tseg):
            rows = pl.ds(tau, S5_SEGS, stride=tseg)
            pr = pre_ref[sl, tau:tau + 1, :]
            pi = pim_ref[sl, tau:tau + 1, :]
            bure_ref[sl, rows, :] = bure_ref[sl, rows, :] + (pr * xin_re - pi * xin_im)
            buim_ref[sl, rows, :] = buim_ref[sl, rows, :] + (pr * xin_im + pi * xin_re)
        return carry

    lax.fori_loop(0, S5_SLABS, fix_body, 0)

    nt_dims = (((1,), (1,)), ((), ()))
    y_odd, y_hat, y_dir = [], [], []
    for j in range(S5_JB):
        x_re = jnp.concatenate([bure_ref[j * spb + s] for s in range(spb)], axis=1).astype(BF16)
        x_im = jnp.concatenate([buim_ref[j * spb + s] for s in range(spb)], axis=1).astype(BF16)
        y_odd.append(lax.dot_general(x_re, wcre_ref[j], nt_dims, preferred_element_type=F32)
                     - lax.dot_general(x_im, wcim_ref[j], nt_dims, preferred_element_type=F32))
        y_hat.append(lax.dot_general(x_re, wcare_ref[j], nt_dims, preferred_element_type=F32)
                     - lax.dot_general(x_im, wcaim_ref[j], nt_dims, preferred_element_type=F32))
        y_dir.append(jnp.dot(u_e[:, j * S5_BLK_IN:(j + 1) * S5_BLK_IN], k0_ref[j], preferred_element_type=F32))
    y_hat = jnp.concatenate(y_hat, axis=1)
    ridx = lax.broadcasted_iota(jnp.int32, (y_hat.shape[0], 1), 0)
    y_even = jnp.where(ridx == 0, yhat_ref[...], pltpu.roll(y_hat, 1, 0)) + jnp.concatenate(y_dir, axis=1)
    yhat_ref[...] = y_hat[y_hat.shape[0] - 1:, :]
    halves = ((y_even, u_e32), (jnp.concatenate(y_odd, axis=1), u_o32))
    for p, (y, u) in enumerate(halves):
        y = _gelu_tanh(y + d_ref[...] * u)
        gate = jnp.dot(y.astype(BF16), wglu_ref[...], preferred_element_type=F32)
        y = y * (1.0 / (1.0 + jnp.exp(-gate)))
        for s in range(lane_slabs):
            par_ref[s, pl.ds(p, half, stride=2), :] = y[:, s * LANES:(s + 1) * LANES]
    y_all = jnp.concatenate([par_ref[s] for s in range(lane_slabs)], axis=1)
    o_ref[...] = (y_all * z_ref[...].astype(F32)).astype(o_ref.dtype)


def _s5_time_tile(rows_per_batch):
    for tseg in (44, 66, 22, 6, 2):
        t = 2 * S5_SEGS * tseg
        if rows_per_batch % t == 0 and (S5_SEGS * tseg) % 16 == 0:
            return t, tseg
    raise ValueError(f"no S5 time tile for {rows_per_batch}")


def _s5(proj, bsz, prep, d, wglu, t, tseg):
    m = proj.shape[0]
    p_re, p_im, w2, wcre, wcim, wcare, wcaim, k0 = prep
    rows = t // 2
    nt = m // bsz // t
    ucol = (2 * RET_QK + 2 * RET_W) // S5_W
    const = lambda a: pl.BlockSpec(a.shape, lambda b, i: (0,) * a.ndim, pipeline_mode=pl.Buffered(1))
    tile = lambda col: pl.BlockSpec((t, S5_W), lambda b, i: (b * nt + i, col))
    d2 = d.reshape(1, S5_W)
    return pl.pallas_call(
        functools.partial(_s5_kernel, tseg=tseg, slab_group=8),
        grid=(bsz, nt),
        in_specs=[tile(ucol), tile(ucol + 1),
                  const(w2), const(wcre), const(wcim), const(wcare), const(wcaim), const(k0),
                  const(p_re), const(p_im), const(d2), const(wglu)],
        out_specs=tile(0),
        out_shape=jax.ShapeDtypeStruct((m, S5_W), BF16),
        scratch_shapes=[pltpu.VMEM((S5_SLABS, rows, LANES), F32), pltpu.VMEM((S5_SLABS, rows, LANES), F32)]
        + [pltpu.VMEM((S5_SLABS, S5_SEGS, LANES), F32)] * 4
        + [pltpu.VMEM((1, S5_W), F32), pltpu.VMEM((S5_W // LANES, t, LANES), F32)],
        compiler_params=_cparams(("arbitrary", "arbitrary")),
        name="s5",
    )(proj, proj, w2, wcre, wcim, wcare, wcaim, k0, p_re, p_im, d2, wglu)


def _out_ab_kernel(lead_ref, xa_ref, xb_ref, xc_ref, oa_ref, ob_ref, w_ref, nw_ref, wg_ref,
                   h1_ref, xn_ref, gl_ref):
    t = pl.program_id(1)
    first = jnp.where(t == 0, lead_ref[...], xa_ref[...])
    h0 = (first, xb_ref[...], xc_ref[...])
    sls = [slice(r * CHUNK, (r + 1) * CHUNK) for r in range(ROW_CHUNKS)]
    accs = [jnp.dot(oa_ref[sl, :], w_ref[:RET_W, :], preferred_element_type=F32)
            + jnp.dot(ob_ref[sl, :], w_ref[RET_W:, :], preferred_element_type=F32) for sl in sls]
    h1s = [h0[r] + accs[r] for r in range(ROW_CHUNKS)]
    xns = [_rms_scale(h1, nw_ref[...]).astype(BF16) for h1 in h1s]
    gls = [jnp.dot(xn, wg_ref[...], preferred_element_type=F32) for xn in xns]
    for r, sl in enumerate(sls):
        h1_ref[sl, :] = h1s[r]
        xn_ref[sl, :] = xns[r]
        gl_ref[sl, :] = gls[r]


def _out_ab(x, lead, oa, ob, w, norm_w, wg, nt):
    bsz, _, d = x.shape
    tile = ROW_CHUNKS * CHUNK
    m = bsz * nt * tile
    row = lambda width: pl.BlockSpec((tile, width), lambda b, t: (b * nt + t, 0))
    const = lambda a: pl.BlockSpec(a.shape, lambda b, t: (0, 0))
    out = jax.ShapeDtypeStruct
    return pl.pallas_call(
        _out_ab_kernel,
        grid=(bsz, nt),
        in_specs=[pl.BlockSpec((CHUNK, d), lambda b, t: (0, 0))] + _chunk_specs(d, 1)
        + [row(RET_W), row(S5_W), const(w), pl.BlockSpec((1, d), lambda b, t: (0, 0)), const(wg)],
        out_specs=[row(d), row(d), row(GATE_PAD)],
        out_shape=(out((m, d), F32), out((m, d), BF16), out((m, GATE_PAD), F32)),
        compiler_params=_cparams(("parallel", "parallel")),
        name="out_proj_ab",
    )(lead, x, x, x, oa, ob, w, norm_w.reshape(1, d), wg)


LOG2E = 1.4426950408889634


def _split2(x):
    x1 = x.astype(BF16)
    return x1, (x - x1.astype(F32)).astype(BF16)


def _neg_abs(x):
    return -jnp.abs(x)


def _block_ref_rows(b, level):
    rows, width = b.shape
    half = 1 << level
    blk = 2 * half
    if blk >= 8:
        b3 = b.reshape(rows // blk, blk, width)
        ref = jnp.broadcast_to(b3[:, half - 1:half, :], b3.shape)
        return ref.reshape(rows, width)
    b3 = b.reshape(rows // 8, 8, width)
    sub = lax.broadcasted_iota(jnp.int32, b3.shape, 1)
    pick = lambda r: jnp.broadcast_to(b3[:, r:r + 1, :], b3.shape)
    if blk == 4:
        ref = jnp.where(sub < 4, pick(1), pick(5))
    else:
        ref = jnp.where(sub < 2, pick(0), jnp.where(sub < 4, pick(2), jnp.where(sub < 6, pick(4), pick(6))))
    return ref.reshape(rows, width)


def _gla_kernel(q_ref, k_ref, v_ref, z_ref, gl_ref, wg_ref, bg_ref, nw_ref, h_ref, wo_ref, fw_ref,
                y_ref, st_ref, mix_ref, wobf_ref, *, nc, nchunks):
    c = pl.program_id(0)
    n = jnp.minimum(c, nchunks - 1) % nc

    @pl.when(c == 0)
    def _():
        mix_ref[...] = jnp.zeros_like(mix_ref)
        wobf_ref[...] = wo_ref[...].astype(BF16)

    @pl.when(n == 0)
    def _():
        st_ref[...] = jnp.zeros_like(st_ref)

    ridx = lax.broadcasted_iota(jnp.int32, (CHUNK, 1), 0)
    ii = lax.broadcasted_iota(jnp.int32, (CHUNK, CHUNK), 0)
    jj = lax.broadcasted_iota(jnp.int32, (CHUNK, CHUNK), 1)
    pair_code = jnp.where(ii > jj, ii ^ jj, 0)
    eye = ii == jj
    tri = (ii >= jj).astype(BF16)

    gl1, gl2 = _split2(gl_ref[:, :GLA_RANK])
    wg1, wg2 = _split2(wg_ref[...])
    x = jnp.dot(jnp.concatenate([gl1, gl2, gl1], axis=1), jnp.concatenate([wg1, wg1, wg2], axis=0),
                preferred_element_type=F32) + bg_ref[...]
    log_a = (jnp.minimum(x, 0.0) - jnp.log(1.0 + jnp.exp(-jnp.abs(x)))) * (LOG2E / GLA_TAU)
    log_a = jnp.where(ridx >= jnp.where(n > 0, 0, PAD), log_a, 0.0)
    g1, g2 = _split2(log_a)
    b_all = jnp.dot(jnp.concatenate([tri, tri], axis=1), jnp.concatenate([g1, g2], axis=0),
                    preferred_element_type=F32)
    odd = (ridx & 1) == 1
    h2 = h_ref[...]

    def finish(heads, qs, ks, bs, scores):
        for i, h in enumerate(heads):
            vc = slice(h * GLA_DV, (h + 1) * GLA_DV)
            q, k, b = qs[i], ks[i], bs[i]
            v = v_ref[:, vc]
            b_last = b[CHUNK - 1:CHUNK, :]
            o = jnp.dot(scores[i].astype(BF16), v, preferred_element_type=F32)
            st = st_ref[h]
            qe = (q * jnp.exp2(b)).astype(BF16)
            o = o + lax.dot_general(qe, st.astype(BF16), (((1,), (1,)), ((), ())), preferred_element_type=F32)
            ke = (k * jnp.exp2(b_last - b)).astype(BF16)
            kv_t = lax.dot_general(v, ke, (((0,), (0,)), ((), ())), preferred_element_type=F32)
            st_ref[h] = st * jnp.exp2(b_last) + kv_t
            mix_ref[:, vc] = (_rms_scale(o, nw_ref[:, vc]) * z_ref[:, vc].astype(F32)).astype(BF16)

    pending = []
    for h0 in range(0, GLA_HEADS, 2):
        heads = (h0, h0 + 1)
        qs, ks, bs, scores = [], [], [], []
        for h in heads:
            kc = slice(h * GLA_DK, (h + 1) * GLA_DK)
            vc = slice(h * GLA_DV, (h + 1) * GLA_DV)
            h2 = h2 + jnp.dot(mix_ref[:, vc], wobf_ref[vc, :], preferred_element_type=F32)
            qs.append(q_ref[:, kc].astype(F32))
            ks.append(k_ref[:, kc].astype(F32))
            bs.append(b_all[:, kc])
            scores.append(jnp.where(eye, jnp.sum(qs[-1] * ks[-1], axis=1, keepdims=True), 0.0))
        for lv in range(GLA_LEVELS):
            half = 1 << lv
            if lv == GLA_LEVELS // 2 and pending:
                finish(*pending.pop())
            zz = []
            for q, k, b in zip(qs, ks, bs):
                if lv == 0:
                    expo = jnp.where(odd, b - pltpu.roll(b, 1, 0), 0.0)
                else:
                    expo = _neg_abs(b - _block_ref_rows(b, lv))
                if half >= 8:
                    sel = jnp.concatenate([(q if r % 2 else k)[r * half:(r + 1) * half]
                                           for r in range(CHUNK // half)], axis=0)
                else:
                    sel = jnp.where(((ridx >> lv) & 1) == 1, q, k)
                zz.append((sel * jnp.exp2(expo)).astype(BF16))
            z2 = jnp.concatenate(zz, axis=0)
            gram = lax.dot_general(z2, z2, (((1,), (1,)), ((), ())), preferred_element_type=F32)
            mask = (pair_code >> lv) == 1
            for i in range(2):
                blk = gram[i * CHUNK:(i + 1) * CHUNK, i * CHUNK:(i + 1) * CHUNK]
                scores[i] = scores[i] + jnp.where(mask, blk, 0.0)
        pending.append((heads, qs, ks, bs, scores))
    finish(*pending.pop())

    y_ref[...] = _rms_scale(h2, fw_ref[...])


def _gla(proj, glow, w_gate, b_gate, norm_w, h1, w_out, final_w, bsz, nc):
    m, d = h1.shape
    nchunks = m // CHUNK
    const = lambda *shape: pl.BlockSpec(shape, lambda c: (0,) * len(shape))
    cur = lambda width, col: pl.BlockSpec((CHUNK, width), lambda c: (jnp.minimum(c, nchunks - 1), col))
    prev = lambda c: jnp.maximum(c - 1, 0)
    return pl.pallas_call(
        functools.partial(_gla_kernel, nc=nc, nchunks=nchunks),
        grid=(nchunks + 1,),
        in_specs=[cur(GLA_QK, 0), cur(GLA_QK, 1), cur(GLA_W, 1), cur(GLA_W, 2), cur(GATE_PAD, 0),
                  const(GLA_RANK, GLA_QK), const(1, GLA_QK), const(1, GLA_W),
                  pl.BlockSpec((CHUNK, d), lambda c: (prev(c), 0)),
                  pl.BlockSpec((None, GLA_W, d), lambda c: (0, 0, 0), pipeline_mode=pl.Buffered(1)),
                  const(1, d)],
        out_specs=pl.BlockSpec((None, CHUNK, d),
                               lambda c: (prev(c) // nc, jnp.maximum(prev(c) % nc - 1, 0), 0)),
        out_shape=jax.ShapeDtypeStruct((bsz, (nc - 1) * CHUNK, d), F32),
        scratch_shapes=[pltpu.VMEM((GLA_HEADS, GLA_DV, GLA_DK), F32), pltpu.VMEM((CHUNK, GLA_W), BF16),
                        pltpu.VMEM((GLA_W, d), BF16)],
        compiler_params=_cparams(("arbitrary",)),
        name="gla_out",
    )(proj, proj, proj, proj, glow, w_gate.astype(F32), b_gate.reshape(1, GLA_QK).astype(F32),
      norm_w.reshape(1, GLA_W).astype(F32), h1, w_out, final_w.reshape(1, d).astype(F32))


def _rope_tables(bsz, rows_per_batch):
    pos = np.maximum(np.arange(rows_per_batch, dtype=np.float64) - PAD, 0.0)
    inv_freq = np.power(ROPE_BASE, -np.arange(0, RET_DK, 2, dtype=np.float64) / RET_DK)
    ang = pos[:, None] * inv_freq[None, :]
    cos, sin = np.cos(ang), np.sin(ang)
    cos2 = np.tile(np.concatenate([cos, cos], axis=1), (bsz, 1)).astype(np.float32)
    sin2 = np.tile(np.concatenate([-sin, sin], axis=1), (bsz, 1)).astype(np.float32)
    return jnp.asarray(cos2), jnp.asarray(sin2)


def kernel(x, meta, norm_ab_w, w_in_ab, ret_norm_w, s5_lam_re, s5_lam_im, s5_log_dt, s5_b_re, s5_b_im,
           s5_c_re, s5_c_im, s5_d, s5_w_glu, w_out_ab, norm_c_w, w_in_c, gla_w_gate, gla_b_gate,
           gla_norm_w, w_out_c, final_norm_w):
    bsz, seq, d = x.shape
    assert seq % CHUNK == 0 and w_in_ab.shape[0] == 1 and w_in_c.shape[0] == 1
    rpb = seq + CHUNK
    nc = rpb // CHUNK
    assert nc % ROW_CHUNKS == 0
    nt = nc // ROW_CHUNKS
    m = bsz * rpb
    tm = _pick_tile(m, (1408, 768, 384, 256, 128))
    lead = jnp.concatenate([jnp.zeros((PAD, d), x.dtype), meta.astype(x.dtype)], axis=0)

    xn0 = _embed_norm(x, lead, norm_ab_w[0], nt)
    plain, silu = ("plain", 1.0), ("silu", 1.0)
    modes_ab = [("rope", 1.0), ("rope", RET_DK ** -0.5), plain, plain, silu, silu, plain, silu]
    proj0 = _in_proj(xn0, w_in_ab, modes_ab, tm, rope=_rope_tables(bsz, rpb))
    o_a = _retention(proj0, ret_norm_w[0], bsz, nt)
    t, tseg = _s5_time_tile(rpb)
    prep = _s5_prep(s5_lam_re[0], s5_lam_im[0], s5_log_dt[0], s5_b_re[0], s5_b_im[0],
                    s5_c_re[0], s5_c_im[0], tseg)
    o_b = _s5(proj0, bsz, prep, s5_d[0], s5_w_glu[0].astype(BF16), t, tseg)

    w_in_c_t = jnp.swapaxes(w_in_c, 1, 2)
    n_main = 2 * GLA_QK + 2 * GLA_W
    wg = jnp.pad(w_in_c_t[0, n_main:, :], ((0, GATE_PAD - GLA_RANK), (0, 0))).astype(BF16).T
    h1, xn1, glow = _out_ab(x, lead, o_a, o_b, w_out_ab[0].astype(BF16), norm_c_w[0], wg, nt)

    modes_c = [("plain", GLA_DK ** -0.5), plain, plain, plain, silu, silu]
    proj1 = _in_proj(xn1, w_in_c_t, modes_c, tm, w_is_transposed=True)
    return _gla(proj1, glow, gla_w_gate[0], gla_b_gate[0], gla_norm_w[0], h1, w_out_c,
                final_norm_w, bsz, nc)
```

```python
import functools
import math

import numpy as np
import jax
import jax.numpy as jnp
from jax import lax
from jax.experimental import pallas as pl
from jax.experimental.pallas import tpu as pltpu

F32 = jnp.float32
BF16 = jnp.bfloat16

N_META = 16
LANES = 128
SUBLANES = 8
CHUNK = 128
PAD = CHUNK - N_META
EPS = 1e-6
ROW_CHUNKS = 3

RET_HEADS = 8
RET_DK = 128
RET_DV = 256
RET_QK = RET_HEADS * RET_DK
RET_W = RET_HEADS * RET_DV
ROPE_BASE = 10000.0

S5_W = 1024
S5_GH = 16
S5_G = 64
S5_P = 64
S5_F = S5_G * S5_P
S5_SLABS = S5_F // LANES
MXU_TILE = 256
S5_JB = 8
S5_BLK_IN = S5_W // S5_JB
S5_K0_IN = MXU_TILE
S5_SEGS = SUBLANES
S5_SCAN_UNROLL = 4

GLA_HEADS = 4
GLA_DK = 256
GLA_DV = 512
GLA_QK = GLA_HEADS * GLA_DK
GLA_W = GLA_HEADS * GLA_DV
GLA_RANK = 16
GLA_TAU = 16.0
GLA_LEVELS = 7
GATE_PAD = LANES

PROJ_TN = 1024
PROJ_ROW_SPLIT = 4
VMEM_LIMIT = 56 * 1024 * 1024


def _cparams(sem):
    return pltpu.CompilerParams(dimension_semantics=sem, vmem_limit_bytes=VMEM_LIMIT)


def _silu(x):
    return x * (1.0 / (1.0 + jnp.exp(-x)))


def _pick_tile(n, candidates):
    for c in candidates:
        if n % c == 0:
            return c
    raise ValueError(f"no tile for {n}")


def _rms_scale(x, w):
    return x * lax.rsqrt(jnp.mean(x * x, axis=-1, keepdims=True) + EPS) * w


def _chunk_specs(d, n_before):
    return [pl.BlockSpec((None, CHUNK, d),
                         lambda b, t, k=k: (b, jnp.maximum(ROW_CHUNKS * t + k - n_before, 0), 0))
            for k in range(ROW_CHUNKS)]


def _embed_norm_kernel(lead_ref, xa_ref, xb_ref, xc_ref, w_ref, o_ref):
    t = pl.program_id(1)
    first = jnp.where(t == 0, lead_ref[...], xa_ref[...])
    for r, rows in enumerate((first, xb_ref[...], xc_ref[...])):
        o_ref[r * CHUNK:(r + 1) * CHUNK, :] = _rms_scale(rows, w_ref[...]).astype(o_ref.dtype)


def _embed_norm(x, lead, w, nt):
    bsz, _, d = x.shape
    tile = ROW_CHUNKS * CHUNK
    return pl.pallas_call(
        _embed_norm_kernel,
        grid=(bsz, nt),
        in_specs=[pl.BlockSpec((CHUNK, d), lambda b, t: (0, 0))] + _chunk_specs(d, 1)
        + [pl.BlockSpec((1, d), lambda b, t: (0, 0))],
        out_specs=pl.BlockSpec((tile, d), lambda b, t: (b * nt + t, 0)),
        out_shape=jax.ShapeDtypeStruct((bsz * nt * tile, d), BF16),
        compiler_params=_cparams(("parallel", "parallel")),
        name="embed_norm",
    )(lead, x, x, x, w.reshape(1, d))


def _rope_rows(a, cos, sin, scale):
    outs = []
    for h in range(a.shape[1] // RET_DK):
        blk = a[:, h * RET_DK:(h + 1) * RET_DK]
        r = blk * cos + pltpu.roll(blk, RET_DK // 2, 1) * sin
        outs.append(r * scale if scale != 1.0 else r)
    return jnp.concatenate(outs, axis=1)


def _in_proj_kernel(*refs, modes, has_rope, w_is_transposed):
    if has_rope:
        x_ref, w_ref, cos_ref, sin_ref, o_ref, wbf_ref = refs
    else:
        x_ref, w_ref, o_ref, wbf_ref = refs
    j = pl.program_id(0)

    @pl.when(pl.program_id(1) == 0)
    def _():
        wbf_ref[...] = w_ref[...].astype(BF16)

    contract = (((1,), (1 if w_is_transposed else 0,)), ((), ()))
    for mode in sorted(set(modes)):
        cond = functools.reduce(jnp.logical_or, [j == jj for jj, mm in enumerate(modes) if mm == mode])

        @pl.when(cond)
        def _(mode=mode):
            kind, scale = mode
            rows = x_ref.shape[0] // PROJ_ROW_SPLIT
            sls = [slice(r * rows, (r + 1) * rows) for r in range(PROJ_ROW_SPLIT)]
            accs = [lax.dot_general(x_ref[sl, :], wbf_ref[...], contract, preferred_element_type=F32)
                    for sl in sls]
            for sl, acc in zip(sls, accs):
                if kind == "rope":
                    acc = _rope_rows(acc, cos_ref[sl, :], sin_ref[sl, :], scale)
                elif kind == "silu":
                    acc = _silu(acc)
                elif scale != 1.0:
                    acc = acc * scale
                o_ref[sl, :] = acc.astype(o_ref.dtype)


def _in_proj(x, w, modes, tm, rope=None, w_is_transposed=False):
    m, k = x.shape
    nt = len(modes)
    assert m % tm == 0 and tm % (PROJ_ROW_SPLIT * 16) == 0
    if w_is_transposed:
        w_spec = pl.BlockSpec((None, PROJ_TN, k), lambda j, i: (0, j, 0))
        w_scratch = pltpu.VMEM((PROJ_TN, k), BF16)
    else:
        w_spec = pl.BlockSpec((None, k, PROJ_TN), lambda j, i: (0, 0, j))
        w_scratch = pltpu.VMEM((k, PROJ_TN), BF16)
    in_specs = [pl.BlockSpec((tm, k), lambda j, i: (i, 0)), w_spec]
    args = [x, w]
    if rope is not None:
        in_specs += [pl.BlockSpec((tm, RET_DK), lambda j, i: (i, 0))] * 2
        args += list(rope)
    return pl.pallas_call(
        functools.partial(_in_proj_kernel, modes=tuple(modes), has_rope=rope is not None,
                          w_is_transposed=w_is_transposed),
        grid=(nt, m // tm),
        in_specs=in_specs,
        out_specs=pl.BlockSpec((tm, PROJ_TN), lambda j, i: (i, j)),
        out_shape=jax.ShapeDtypeStruct((m, nt * PROJ_TN), BF16),
        scratch_shapes=[w_scratch],
        compiler_params=_cparams(("arbitrary", "arbitrary")),
        name="in_proj",
    )(*args)


def _ret_log_decay(h):
    return math.log1p(-(2.0 ** (-5.0 - h)))


def _retention_kernel(q_ref, k_ref, v_ref, z_ref, nw_ref, o_ref, s_ref, dec_ref, qs_ref, ks_ref):
    first_step = (pl.program_id(0) == 0) & (pl.program_id(1) == 0)

    @pl.when(first_step)
    def _():
        ii = lax.broadcasted_iota(jnp.int32, (CHUNK, CHUNK), 0)
        jj = lax.broadcasted_iota(jnp.int32, (CHUNK, CHUNK), 1)
        diff = (ii - jj).astype(F32)
        row = ii.astype(F32)
        for h in range(RET_HEADS):
            lg = _ret_log_decay(h)
            dec_ref[h] = jnp.where(ii >= jj, jnp.exp(lg * jnp.maximum(diff, 0.0)), 0.0)
            qs_ref[h] = jnp.exp(lg * (row + 1.0))
            ks_ref[h] = jnp.exp(lg * (CHUNK - 1.0 - row))

    @pl.when(pl.program_id(1) == 0)
    def _():
        s_ref[...] = jnp.zeros_like(s_ref)

    heads = range(RET_HEADS)
    vcs = [slice(h * RET_DV, (h + 1) * RET_DV) for h in heads]
    nt_dims = (((1,), (1,)), ((), ()))
    tn_dims = (((0,), (0,)), ((), ()))
    states = [s_ref[h] for h in heads]
    for r in range(q_ref.shape[0] // CHUNK):
        rows = slice(r * CHUNK, (r + 1) * CHUNK)
        qs = [q_ref[rows, h * RET_DK:(h + 1) * RET_DK] for h in heads]
        ks = [k_ref[rows, h * RET_DK:(h + 1) * RET_DK] for h in heads]
        vs = [v_ref[rows, vc] for vc in vcs]
        scores = [lax.dot_general(qs[h], ks[h], nt_dims, preferred_element_type=F32) for h in heads]
        kvs = [lax.dot_general((ks[h].astype(F32) * ks_ref[h]).astype(BF16), vs[h], tn_dims,
                               preferred_element_type=F32) for h in heads]
        lhs = [jnp.concatenate([(scores[h] * dec_ref[h]).astype(BF16),
                                (qs[h].astype(F32) * qs_ref[h]).astype(BF16)], axis=1) for h in heads]
        outs = [jnp.dot(lhs[h], jnp.concatenate([vs[h], states[h].astype(BF16)], axis=0),
                        preferred_element_type=F32) for h in heads]
        states = [states[h] * math.exp(_ret_log_decay(h) * CHUNK) + kvs[h] for h in heads]
        for h in heads:
            o_ref[rows, vcs[h]] = (_rms_scale(outs[h], nw_ref[:, vcs[h]])
                                   * z_ref[rows, vcs[h]].astype(F32)).astype(o_ref.dtype)
    for h in heads:
        s_ref[h] = states[h]


def _retention(proj, norm_w, bsz, nt):
    m = proj.shape[0]
    tile = ROW_CHUNKS * CHUNK
    tab = pltpu.VMEM((RET_HEADS, CHUNK, CHUNK), F32)
    row = lambda width, col: pl.BlockSpec((tile, width), lambda b, t: (b * nt + t, col))
    return pl.pallas_call(
        _retention_kernel,
        grid=(bsz, nt),
        in_specs=[row(RET_QK, 0), row(RET_QK, 1), row(RET_W, 1), row(RET_W, 2),
                  pl.BlockSpec((1, RET_W), lambda b, t: (0, 0))],
        out_specs=row(RET_W, 0),
        out_shape=jax.ShapeDtypeStruct((m, RET_W), BF16),
        scratch_shapes=[pltpu.VMEM((RET_HEADS, RET_DK, RET_DV), F32), tab, tab, tab],
        compiler_params=_cparams(("arbitrary", "arbitrary")),
        name="retention",
    )(proj, proj, proj, proj, norm_w.reshape(1, RET_W))


def _s5_prep_kernel(lre_ref, lim_ref, ldt_ref, bre_ref, bim_ref, cre_ref, cim_ref,
                    pre_ref, pim_ref, w2_ref, wcre_ref, wcim_ref, k0_ref):
    lre = lre_ref[...]
    lim = lim_ref[...]
    dt = jnp.exp(ldt_ref[...])
    tseg = pre_ref.shape[1]
    n = 2.0 * (lax.broadcasted_iota(jnp.int32, (tseg, 1), 0).astype(F32) + 1.0)
    mag = jnp.exp(n * (lre * dt))
    ang = n * (lim * dt)
    p_re = mag * jnp.cos(ang)
    p_im = mag * jnp.sin(ang)
    for sl in range(S5_SLABS):
        pre_ref[sl] = p_re[:, sl * LANES:(sl + 1) * LANES]
        pim_ref[sl] = p_im[:, sl * LANES:(sl + 1) * LANES]
    mag1 = jnp.exp(lre * dt)
    a_re = mag1 * jnp.cos(lim * dt)
    a_im = mag1 * jnp.sin(lim * dt)
    den = lre * lre + lim * lim
    nr = a_re - 1.0
    f_re = (nr * lre + a_im * lim) / den
    f_im = (a_im * lre - nr * lim) / den
    b_re = bre_ref[...]
    b_im = bim_ref[...]
    bb_re = f_re * b_re - f_im * b_im
    bb_im = f_re * b_im + f_im * b_re
    bba_re = a_re * bb_re - a_im * bb_im
    bba_im = a_re * bb_im + a_im * bb_re
    c_re = cre_ref[...]
    c_im = cim_ref[...]
    ca_re = c_re * a_re - c_im * a_im
    ca_im = c_re * a_im + c_im * a_re
    blk = S5_F // S5_JB
    rows = lax.broadcasted_iota(jnp.int32, (S5_BLK_IN, blk), 0)
    cols = lax.broadcasted_iota(jnp.int32, (S5_BLK_IN, blk), 1)
    same_group = (rows // S5_GH) == (cols // S5_P)

    def block_diag(a, j):
        piece = a[:, j * blk:(j + 1) * blk]
        return jnp.where(same_group, jnp.concatenate([piece] * (S5_G // S5_JB), axis=0), 0.0)

    k0_ref[...] = jnp.zeros_like(k0_ref)
    per_k0 = S5_K0_IN // S5_BLK_IN
    for j in range(S5_JB):
        bd_bre, bd_bim = block_diag(bb_re, j), block_diag(bb_im, j)
        bd_cre, bd_cim = block_diag(c_re, j), block_diag(c_im, j)
        w2_ref[j, :S5_BLK_IN, :blk] = block_diag(bba_re, j).astype(BF16)
        w2_ref[j, :S5_BLK_IN, blk:] = block_diag(bba_im, j).astype(BF16)
        w2_ref[j, S5_BLK_IN:, :blk] = bd_bre.astype(BF16)
        w2_ref[j, S5_BLK_IN:, blk:] = bd_bim.astype(BF16)
        wcre_ref[j, :S5_BLK_IN, :] = bd_cre.astype(BF16)
        wcim_ref[j, :S5_BLK_IN, :] = bd_cim.astype(BF16)
        wcre_ref[j, S5_BLK_IN:, :] = block_diag(ca_re, j).astype(BF16)
        wcim_ref[j, S5_BLK_IN:, :] = block_diag(ca_im, j).astype(BF16)
        nt_dims = (((1,), (1,)), ((), ()))
        k0 = (lax.dot_general(bd_bre, bd_cre, nt_dims, preferred_element_type=F32)
              - lax.dot_general(bd_bim, bd_cim, nt_dims, preferred_element_type=F32))
        r0 = (j % per_k0) * S5_BLK_IN
        k0_ref[j // per_k0, r0:r0 + S5_BLK_IN, r0:r0 + S5_BLK_IN] = k0.astype(BF16)


def _s5_prep(lam_re, lam_im, log_dt, b_re, b_im, c_re, c_im, tseg):
    flat = lambda a: a.reshape(1, S5_F)
    ldt = jnp.broadcast_to(log_dt[:, None], (S5_G, S5_P))
    bt = lambda a: a.reshape(S5_F, S5_GH).T
    ct = lambda a: a.transpose(1, 0, 2).reshape(S5_GH, S5_F)
    out = jax.ShapeDtypeStruct
    blk = S5_F // S5_JB
    wc = out((S5_JB, 2 * S5_BLK_IN, blk), BF16)
    return pl.pallas_call(
        _s5_prep_kernel,
        out_shape=(out((S5_SLABS, tseg, LANES), F32), out((S5_SLABS, tseg, LANES), F32),
                   out((S5_JB, 2 * S5_BLK_IN, 2 * blk), BF16), wc, wc,
                   out((S5_W // S5_K0_IN, S5_K0_IN, S5_K0_IN), BF16)),
        compiler_params=pltpu.CompilerParams(vmem_limit_bytes=VMEM_LIMIT),
        name="s5_prep",
    )(flat(lam_re), flat(lam_im), flat(ldt), bt(b_re), bt(b_im), ct(c_re.astype(F32)), ct(c_im.astype(F32)))


def _gelu_tanh(x):
    return 0.5 * x * (1.0 + jnp.tanh(math.sqrt(2.0 / math.pi) * (x + 0.044715 * (x * x * x))))


def _s5_kernel(u_ref, z_ref, w2_ref, wcre_ref, wcim_ref, k0_ref,
               pre_ref, pim_ref, d_ref, wglu_ref,
               o_ref, bure_ref, buim_ref, cre_ref, cim_ref, xinre_ref, xinim_ref, yhat_ref, par_ref,
               *, tseg, slab_group):
    ci = pl.program_id(1)

    @pl.when(ci == 0)
    def _():
        cre_ref[...] = jnp.zeros_like(cre_ref)
        cim_ref[...] = jnp.zeros_like(cim_ref)
        yhat_ref[...] = jnp.zeros_like(yhat_ref)

    half = u_ref.shape[0] // 2
    lane_slabs = S5_W // LANES
    u_all = u_ref[...].astype(F32)
    for s in range(lane_slabs):
        par_ref[s] = u_all[:, s * LANES:(s + 1) * LANES]
    parity = lambda p: jnp.concatenate([par_ref[s, pl.ds(p, half, stride=2), :] for s in range(lane_slabs)], axis=1)
    u_e32, u_o32 = parity(0), parity(1)
    u_e, u_o = u_e32.astype(BF16), u_o32.astype(BF16)
    spb = S5_SLABS // S5_JB
    for j in range(S5_JB):
        blk = slice(j * S5_BLK_IN, (j + 1) * S5_BLK_IN)
        res = jnp.dot(jnp.concatenate([u_e[:, blk], u_o[:, blk]], axis=1), w2_ref[j], preferred_element_type=F32)
        for s in range(spb):
            bure_ref[j * spb + s] = res[:, s * LANES:(s + 1) * LANES]
            buim_ref[j * spb + s] = res[:, (spb + s) * LANES:(spb + s + 1) * LANES]

    for g0 in range(0, S5_SLABS, slab_group):
        slabs = list(range(g0, g0 + slab_group))
        a_re = [jnp.broadcast_to(pre_ref[sl, 0:1, :], (S5_SEGS, LANES)) for sl in slabs]
        a_im = [jnp.broadcast_to(pim_ref[sl, 0:1, :], (S5_SEGS, LANES)) for sl in slabs]

        def scan_body(tau, carry, slabs=slabs, a_re=a_re, a_im=a_im):
            xr, xi = carry
            nr, ni = [], []
            for n, sl in enumerate(slabs):
                rows = pl.ds(tau, S5_SEGS, stride=tseg)
                br = bure_ref[sl, rows, :]
                bi = buim_ref[sl, rows, :]
                r = a_re[n] * xr[n] - a_im[n] * xi[n] + br
                i = a_re[n] * xi[n] + a_im[n] * xr[n] + bi
                bure_ref[sl, rows, :] = r
                buim_ref[sl, rows, :] = i
                nr.append(r)
                ni.append(i)
            return tuple(nr), tuple(ni)

        zero = tuple(jnp.zeros((S5_SEGS, LANES), F32) for _ in slabs)
        end_re, end_im = lax.fori_loop(0, tseg, scan_body, (zero, zero),
                                       unroll=S5_SCAN_UNROLL if tseg % S5_SCAN_UNROLL == 0 else 1)

        for n, sl in enumerate(slabs):
            at_re = pre_ref[sl, tseg - 1:tseg, :]
            at_im = pim_ref[sl, tseg - 1:tseg, :]
            xr = cre_ref[sl, 0:1, :]
            xi = cim_ref[sl, 0:1, :]
            rows_re, rows_im = [], []
            for s in range(S5_SEGS):
                rows_re.append(xr)
                rows_im.append(xi)
                er = end_re[n][s:s + 1]
                ei = end_im[n][s:s + 1]
                xr, xi = er + at_re * xr - at_im * xi, ei + at_re * xi + at_im * xr
            cre_ref[sl] = jnp.broadcast_to(xr, (S5_SEGS, LANES))
            cim_ref[sl] = jnp.broadcast_to(xi, (S5_SEGS, LANES))
            xinre_ref[sl] = jnp.concatenate(rows_re, axis=0)
            xinim_ref[sl] = jnp.concatenate(rows_im, axis=0)

    def fix_body(sl, carry):
        xin_re = xinre_ref[sl]
        xin_im = xinim_ref[sl]
        for tau in range(tseg):
            rows = pl.ds(tau, S5_SEGS, stride=tseg)
            pr = pre_ref[sl, tau:tau + 1, :]
            pi = pim_ref[sl, tau:tau + 1, :]
            bure_ref[sl, rows, :] = bure_ref[sl, rows, :] + (pr * xin_re - pi * xin_im)
            buim_ref[sl, rows, :] = buim_ref[sl, rows, :] + (pr * xin_im + pi * xin_re)
        return carry

    lax.fori_loop(0, S5_SLABS, fix_body, 0)

    nt_dims = (((1,), (1,)), ((), ()))
    y_odd, y_hat = [], []
    for j in range(S5_JB):
        x_re = jnp.concatenate([bure_ref[j * spb + s] for s in range(spb)], axis=1).astype(BF16)
        x_im = jnp.concatenate([buim_ref[j * spb + s] for s in range(spb)], axis=1).astype(BF16)
        both = (lax.dot_general(x_re, wcre_ref[j], nt_dims, preferred_element_type=F32)
                - lax.dot_general(x_im, wcim_ref[j], nt_dims, preferred_element_type=F32))
        y_odd.append(both[:, :S5_BLK_IN])
        y_hat.append(both[:, S5_BLK_IN:])
    y_dir = [jnp.dot(u_e[:, j * S5_K0_IN:(j + 1) * S5_K0_IN], k0_ref[j], preferred_element_type=F32)
             for j in range(S5_W // S5_K0_IN)]
    y_hat = jnp.concatenate(y_hat, axis=1)
    ridx = lax.broadcasted_iota(jnp.int32, (y_hat.shape[0], 1), 0)
    y_even = jnp.where(ridx == 0, yhat_ref[...], pltpu.roll(y_hat, 1, 0)) + jnp.concatenate(y_dir, axis=1)
    yhat_ref[...] = y_hat[y_hat.shape[0] - 1:, :]
    halves = ((y_even, u_e32), (jnp.concatenate(y_odd, axis=1), u_o32))
    for p, (y, u) in enumerate(halves):
        y = _gelu_tanh(y + d_ref[...] * u)
        gate = jnp.dot(y.astype(BF16), wglu_ref[...], preferred_element_type=F32)
        y = y * (1.0 / (1.0 + jnp.exp(-gate)))
        for s in range(lane_slabs):
            par_ref[s, pl.ds(p, half, stride=2), :] = y[:, s * LANES:(s + 1) * LANES]
    y_all = jnp.concatenate([par_ref[s] for s in range(lane_slabs)], axis=1)
    o_ref[...] = (y_all * z_ref[...].astype(F32)).astype(o_ref.dtype)


def _s5_time_tile(rows_per_batch):
    for tseg in (44, 66, 22, 6, 2):
        t = 2 * S5_SEGS * tseg
        if rows_per_batch % t == 0 and (S5_SEGS * tseg) % 16 == 0:
            return t, tseg
    raise ValueError(f"no S5 time tile for {rows_per_batch}")


def _s5(proj, bsz, prep, d, wglu, t, tseg):
    m = proj.shape[0]
    p_re, p_im, w2, wcre, wcim, k0 = prep
    rows = t // 2
    nt = m // bsz // t
    ucol = (2 * RET_QK + 2 * RET_W) // S5_W
    const = lambda a: pl.BlockSpec(a.shape, lambda b, i: (0,) * a.ndim, pipeline_mode=pl.Buffered(1))
    tile = lambda col: pl.BlockSpec((t, S5_W), lambda b, i: (b * nt + i, col))
    d2 = d.reshape(1, S5_W)
    return pl.pallas_call(
        functools.partial(_s5_kernel, tseg=tseg, slab_group=8),
        grid=(bsz, nt),
        in_specs=[tile(ucol), tile(ucol + 1),
                  const(w2), const(wcre), const(wcim), const(k0),
                  const(p_re), const(p_im), const(d2), const(wglu)],
        out_specs=tile(0),
        out_shape=jax.ShapeDtypeStruct((m, S5_W), BF16),
        scratch_shapes=[pltpu.VMEM((S5_SLABS, rows, LANES), F32), pltpu.VMEM((S5_SLABS, rows, LANES), F32)]
        + [pltpu.VMEM((S5_SLABS, S5_SEGS, LANES), F32)] * 4
        + [pltpu.VMEM((1, S5_W), F32), pltpu.VMEM((S5_W // LANES, t, LANES), F32)],
        compiler_params=_cparams(("arbitrary", "arbitrary")),
        name="s5",
    )(proj, proj, w2, wcre, wcim, k0, p_re, p_im, d2, wglu)


def _out_ab_kernel(lead_ref, xa_ref, xb_ref, xc_ref, oa_ref, ob_ref, w_ref, nw_ref, wg_ref,
                   h1_ref, xn_ref, gl_ref):
    t = pl.program_id(1)
    first = jnp.where(t == 0, lead_ref[...], xa_ref[...])
    h0 = (first, xb_ref[...], xc_ref[...])
    sls = [slice(r * CHUNK, (r + 1) * CHUNK) for r in range(ROW_CHUNKS)]
    accs = [jnp.dot(oa_ref[sl, :], w_ref[:RET_W, :], preferred_element_type=F32)
            + jnp.dot(ob_ref[sl, :], w_ref[RET_W:, :], preferred_element_type=F32) for sl in sls]
    h1s = [h0[r] + accs[r] for r in range(ROW_CHUNKS)]
    xns = [_rms_scale(h1, nw_ref[...]).astype(BF16) for h1 in h1s]
    gls = [jnp.dot(xn, wg_ref[...], preferred_element_type=F32) for xn in xns]
    for r, sl in enumerate(sls):
        h1_ref[sl, :] = h1s[r]
        xn_ref[sl, :] = xns[r]
        gl_ref[sl, :] = gls[r]


def _out_ab(x, lead, oa, ob, w, norm_w, wg, nt):
    bsz, _, d = x.shape
    tile = ROW_CHUNKS * CHUNK
    m = bsz * nt * tile
    row = lambda width: pl.BlockSpec((tile, width), lambda b, t: (b * nt + t, 0))
    const = lambda a: pl.BlockSpec(a.shape, lambda b, t: (0, 0))
    out = jax.ShapeDtypeStruct
    return pl.pallas_call(
        _out_ab_kernel,
        grid=(bsz, nt),
        in_specs=[pl.BlockSpec((CHUNK, d), lambda b, t: (0, 0))] + _chunk_specs(d, 1)
        + [row(RET_W), row(S5_W), const(w), pl.BlockSpec((1, d), lambda b, t: (0, 0)), const(wg)],
        out_specs=[row(d), row(d), row(GATE_PAD)],
        out_shape=(out((m, d), F32), out((m, d), BF16), out((m, GATE_PAD), F32)),
        compiler_params=_cparams(("parallel", "parallel")),
        name="out_proj_ab",
    )(lead, x, x, x, oa, ob, w, norm_w.reshape(1, d), wg)


LOG2E = 1.4426950408889634


def _split2(x):
    x1 = x.astype(BF16)
    return x1, (x - x1.astype(F32)).astype(BF16)


def _neg_abs(x):
    return -jnp.abs(x)


def _block_ref_rows(b, level):
    rows, width = b.shape
    half = 1 << level
    blk = 2 * half
    if blk >= 8:
        b3 = b.reshape(rows // blk, blk, width)
        ref = jnp.broadcast_to(b3[:, half - 1:half, :], b3.shape)
        return ref.reshape(rows, width)
    b3 = b.reshape(rows // 8, 8, width)
    sub = lax.broadcasted_iota(jnp.int32, b3.shape, 1)
    pick = lambda r: jnp.broadcast_to(b3[:, r:r + 1, :], b3.shape)
    if blk == 4:
        ref = jnp.where(sub < 4, pick(1), pick(5))
    else:
        ref = jnp.where(sub < 2, pick(0), jnp.where(sub < 4, pick(2), jnp.where(sub < 6, pick(4), pick(6))))
    return ref.reshape(rows, width)


def _gla_kernel(q_ref, k_ref, v_ref, z_ref, gl_ref, wg_ref, bg_ref, nw_ref, h_ref, wo_ref, fw_ref,
                y_ref, st_ref, mix_ref, wobf_ref, *, nc, nchunks):
    c = pl.program_id(0)
    n = jnp.minimum(c, nchunks - 1) % nc

    @pl.when(c == 0)
    def _():
        mix_ref[...] = jnp.zeros_like(mix_ref)
        wobf_ref[...] = wo_ref[...].astype(BF16)

    @pl.when(n == 0)
    def _():
        st_ref[...] = jnp.zeros_like(st_ref)

    ridx = lax.broadcasted_iota(jnp.int32, (CHUNK, 1), 0)
    ii = lax.broadcasted_iota(jnp.int32, (CHUNK, CHUNK), 0)
    jj = lax.broadcasted_iota(jnp.int32, (CHUNK, CHUNK), 1)
    pair_code = jnp.where(ii > jj, ii ^ jj, 0)
    eye = ii == jj
    tri = (ii >= jj).astype(BF16)

    gl1, gl2 = _split2(gl_ref[:, :GLA_RANK])
    wg1, wg2 = _split2(wg_ref[...])
    x = jnp.dot(jnp.concatenate([gl1, gl2, gl1], axis=1), jnp.concatenate([wg1, wg1, wg2], axis=0),
                preferred_element_type=F32) + bg_ref[...]
    log_a = (jnp.minimum(x, 0.0) - jnp.log(1.0 + jnp.exp(-jnp.abs(x)))) * (LOG2E / GLA_TAU)
    log_a = jnp.where(ridx >= jnp.where(n > 0, 0, PAD), log_a, 0.0)
    g1, g2 = _split2(log_a)
    b_all = jnp.dot(jnp.concatenate([tri, tri], axis=1), jnp.concatenate([g1, g2], axis=0),
                    preferred_element_type=F32)
    odd = (ridx & 1) == 1
    h2 = h_ref[...]

    def finish(heads, qs, ks, bs, scores):
        for i, h in enumerate(heads):
            vc = slice(h * GLA_DV, (h + 1) * GLA_DV)
            q, k, b = qs[i], ks[i], bs[i]
            v = v_ref[:, vc]
            b_last = b[CHUNK - 1:CHUNK, :]
            o = jnp.dot(scores[i].astype(BF16), v, preferred_element_type=F32)
            st = st_ref[h]
            qe = (q * jnp.exp2(b)).astype(BF16)
            o = o + lax.dot_general(qe, st.astype(BF16), (((1,), (1,)), ((), ())), preferred_element_type=F32)
            ke = (k * jnp.exp2(b_last - b)).astype(BF16)
            kv_t = lax.dot_general(v, ke, (((0,), (0,)), ((), ())), preferred_element_type=F32)
            st_ref[h] = st * jnp.exp2(b_last) + kv_t
            mix_ref[:, vc] = (_rms_scale(o, nw_ref[:, vc]) * z_ref[:, vc].astype(F32)).astype(BF16)

    pending = []
    for h0 in range(0, GLA_HEADS, 2):
        heads = (h0, h0 + 1)
        qs, ks, bs, scores = [], [], [], []
        for h in heads:
            kc = slice(h * GLA_DK, (h + 1) * GLA_DK)
            vc = slice(h * GLA_DV, (h + 1) * GLA_DV)
            h2 = h2 + jnp.dot(mix_ref[:, vc], wobf_ref[vc, :], preferred_element_type=F32)
            qs.append(q_ref[:, kc].astype(F32))
            ks.append(k_ref[:, kc].astype(F32))
            bs.append(b_all[:, kc])
            scores.append(jnp.where(eye, jnp.sum(qs[-1] * ks[-1], axis=1, keepdims=True), 0.0))
        for lv in range(GLA_LEVELS):
            half = 1 << lv
            if lv == GLA_LEVELS // 2 and pending:
                finish(*pending.pop())
            zz = []
            for q, k, b in zip(qs, ks, bs):
                if lv == 0:
                    expo = jnp.where(odd, b - pltpu.roll(b, 1, 0), 0.0)
                else:
                    expo = _neg_abs(b - _block_ref_rows(b, lv))
                if half >= 8:
                    sel = jnp.concatenate([(q if r % 2 else k)[r * half:(r + 1) * half]
                                           for r in range(CHUNK // half)], axis=0)
                else:
                    sel = jnp.where(((ridx >> lv) & 1) == 1, q, k)
                zz.append((sel * jnp.exp2(expo)).astype(BF16))
            z2 = jnp.concatenate(zz, axis=0)
            gram = lax.dot_general(z2, z2, (((1,), (1,)), ((), ())), preferred_element_type=F32)
            mask = (pair_code >> lv) == 1
            for i in range(2):
                blk = gram[i * CHUNK:(i + 1) * CHUNK, i * CHUNK:(i + 1) * CHUNK]
                scores[i] = scores[i] + jnp.where(mask, blk, 0.0)
        pending.append((heads, qs, ks, bs, scores))
    finish(*pending.pop())

    y_ref[...] = _rms_scale(h2, fw_ref[...])


def _gla(proj, glow, w_gate, b_gate, norm_w, h1, w_out, final_w, bsz, nc):
    m, d = h1.shape
    nchunks = m // CHUNK
    const = lambda *shape: pl.BlockSpec(shape, lambda c: (0,) * len(shape))
    cur = lambda width, col: pl.BlockSpec((CHUNK, width), lambda c: (jnp.minimum(c, nchunks - 1), col))
    prev = lambda c: jnp.maximum(c - 1, 0)
    return pl.pallas_call(
        functools.partial(_gla_kernel, nc=nc, nchunks=nchunks),
        grid=(nchunks + 1,),
        in_specs=[cur(GLA_QK, 0), cur(GLA_QK, 1), cur(GLA_W, 1), cur(GLA_W, 2), cur(GATE_PAD, 0),
                  const(GLA_RANK, GLA_QK), const(1, GLA_QK), const(1, GLA_W),
                  pl.BlockSpec((CHUNK, d), lambda c: (prev(c), 0)),
                  pl.BlockSpec((None, GLA_W, d), lambda c: (0, 0, 0), pipeline_mode=pl.Buffered(1)),
                  const(1, d)],
        out_specs=pl.BlockSpec((None, CHUNK, d),
                               lambda c: (prev(c) // nc, jnp.maximum(prev(c) % nc - 1, 0), 0)),
        out_shape=jax.ShapeDtypeStruct((bsz, (nc - 1) * CHUNK, d), F32),
        scratch_shapes=[pltpu.VMEM((GLA_HEADS, GLA_DV, GLA_DK), F32), pltpu.VMEM((CHUNK, GLA_W), BF16),
                        pltpu.VMEM((GLA_W, d), BF16)],
        compiler_params=_cparams(("arbitrary",)),
        name="gla_out",
    )(proj, proj, proj, proj, glow, w_gate.astype(F32), b_gate.reshape(1, GLA_QK).astype(F32),
      norm_w.reshape(1, GLA_W).astype(F32), h1, w_out, final_w.reshape(1, d).astype(F32))


def _rope_tables(bsz, rows_per_batch):
    pos = np.maximum(np.arange(rows_per_batch, dtype=np.float64) - PAD, 0.0)
    inv_freq = np.power(ROPE_BASE, -np.arange(0, RET_DK, 2, dtype=np.float64) / RET_DK)
    ang = pos[:, None] * inv_freq[None, :]
    cos, sin = np.cos(ang), np.sin(ang)
    cos2 = np.tile(np.concatenate([cos, cos], axis=1), (bsz, 1)).astype(np.float32)
    sin2 = np.tile(np.concatenate([-sin, sin], axis=1), (bsz, 1)).astype(np.float32)
    return jnp.asarray(cos2), jnp.asarray(sin2)


def kernel(x, meta, norm_ab_w, w_in_ab, ret_norm_w, s5_lam_re, s5_lam_im, s5_log_dt, s5_b_re, s5_b_im,
           s5_c_re, s5_c_im, s5_d, s5_w_glu, w_out_ab, norm_c_w, w_in_c, gla_w_gate, gla_b_gate,
           gla_norm_w, w_out_c, final_norm_w):
    bsz, seq, d = x.shape
    assert seq % CHUNK == 0 and w_in_ab.shape[0] == 1 and w_in_c.shape[0] == 1
    rpb = seq + CHUNK
    nc = rpb // CHUNK
    assert nc % ROW_CHUNKS == 0
    nt = nc // ROW_CHUNKS
    m = bsz * rpb
    tm = _pick_tile(m, (1408, 768, 384, 256, 128))
    lead = jnp.concatenate([jnp.zeros((PAD, d), x.dtype), meta.astype(x.dtype)], axis=0)

    xn0 = _embed_norm(x, lead, norm_ab_w[0], nt)
    plain, silu = ("plain", 1.0), ("silu", 1.0)
    modes_ab = [("rope", 1.0), ("rope", RET_DK ** -0.5), plain, plain, silu, silu, plain, silu]
    proj0 = _in_proj(xn0, w_in_ab, modes_ab, tm, rope=_rope_tables(bsz, rpb))
    o_a = _retention(proj0, ret_norm_w[0], bsz, nt)
    t, tseg = _s5_time_tile(rpb)
    prep = _s5_prep(s5_lam_re[0], s5_lam_im[0], s5_log_dt[0], s5_b_re[0], s5_b_im[0],
                    s5_c_re[0], s5_c_im[0], tseg)
    o_b = _s5(proj0, bsz, prep, s5_d[0], s5_w_glu[0].astype(BF16), t, tseg)

    w_in_c_t = jnp.swapaxes(w_in_c, 1, 2)
    n_main = 2 * GLA_QK + 2 * GLA_W
    wg = jnp.pad(w_in_c_t[0, n_main:, :], ((0, GATE_PAD - GLA_RANK), (0, 0))).astype(BF16).T
    h1, xn1, glow = _out_ab(x, lead, o_a, o_b, w_out_ab[0].astype(BF16), norm_c_w[0], wg, nt)

    modes_c = [("plain", GLA_DK ** -0.5), plain, plain, plain, silu, silu]
    proj1 = _in_proj(xn1, w_in_c_t, modes_c, tm, w_is_transposed=True)
    return _gla(proj1, glow, gla_w_gate[0], gla_b_gate[0], gla_norm_w[0], h1, w_out_c,
                final_norm_w, bsz, nc)
```

```python
import functools
import math

import numpy as np
import jax
import jax.numpy as jnp
from jax import lax
from jax.experimental import pallas as pl
from jax.experimental.pallas import tpu as pltpu

F32 = jnp.float32
BF16 = jnp.bfloat16

N_META = 16
LANES = 128
SUBLANES = 8
CHUNK = 128
PAD = CHUNK - N_META
EPS = 1e-6
ROW_CHUNKS = 3

RET_HEADS = 8
RET_DK = 128
RET_DV = 256
RET_QK = RET_HEADS * RET_DK
RET_W = RET_HEADS * RET_DV
ROPE_BASE = 10000.0

S5_W = 1024
S5_GH = 16
S5_G = 64
S5_P = 64
S5_F = S5_G * S5_P
S5_SLABS = S5_F // LANES
MXU_TILE = 256
S5_JB = 8
S5_BLK_IN = S5_W // S5_JB
S5_K0_IN = MXU_TILE
S5_SEGS = SUBLANES
S5_STEPS = 4
S5_SCAN_UNROLL = 2

GLA_HEADS = 4
GLA_DK = 256
GLA_DV = 512
GLA_QK = GLA_HEADS * GLA_DK
GLA_W = GLA_HEADS * GLA_DV
GLA_RANK = 16
GLA_TAU = 16.0
GLA_LEVELS = 7
GATE_PAD = LANES

PROJ_TN = 1024
PROJ_ROW_SPLIT = 4
VMEM_LIMIT = 56 * 1024 * 1024


def _cparams(sem):
    return pltpu.CompilerParams(dimension_semantics=sem, vmem_limit_bytes=VMEM_LIMIT)


def _silu(x):
    return x * (1.0 / (1.0 + jnp.exp(-x)))


def _pick_tile(n, candidates):
    for c in candidates:
        if n % c == 0:
            return c
    raise ValueError(f"no tile for {n}")


def _rms_scale(x, w):
    return x * lax.rsqrt(jnp.mean(x * x, axis=-1, keepdims=True) + EPS) * w


def _chunk_specs(d, n_before):
    return [pl.BlockSpec((None, CHUNK, d),
                         lambda b, t, k=k: (b, jnp.maximum(ROW_CHUNKS * t + k - n_before, 0), 0))
            for k in range(ROW_CHUNKS)]


def _embed_norm_kernel(lead_ref, xa_ref, xb_ref, xc_ref, w_ref, o_ref):
    t = pl.program_id(1)
    first = jnp.where(t == 0, lead_ref[...], xa_ref[...])
    for r, rows in enumerate((first, xb_ref[...], xc_ref[...])):
        o_ref[r * CHUNK:(r + 1) * CHUNK, :] = _rms_scale(rows, w_ref[...]).astype(o_ref.dtype)


def _embed_norm(x, lead, w, nt):
    bsz, _, d = x.shape
    tile = ROW_CHUNKS * CHUNK
    return pl.pallas_call(
        _embed_norm_kernel,
        grid=(bsz, nt),
        in_specs=[pl.BlockSpec((CHUNK, d), lambda b, t: (0, 0))] + _chunk_specs(d, 1)
        + [pl.BlockSpec((1, d), lambda b, t: (0, 0))],
        out_specs=pl.BlockSpec((tile, d), lambda b, t: (b * nt + t, 0)),
        out_shape=jax.ShapeDtypeStruct((bsz * nt * tile, d), BF16),
        compiler_params=_cparams(("parallel", "parallel")),
        name="embed_norm",
    )(lead, x, x, x, w.reshape(1, d))


def _rope_rows(a, cos, sin, scale):
    outs = []
    for h in range(a.shape[1] // RET_DK):
        blk = a[:, h * RET_DK:(h + 1) * RET_DK]
        r = blk * cos + pltpu.roll(blk, RET_DK // 2, 1) * sin
        outs.append(r * scale if scale != 1.0 else r)
    return jnp.concatenate(outs, axis=1)


def _in_proj_kernel(*refs, modes, has_rope, w_is_transposed):
    if has_rope:
        x_ref, w_ref, cos_ref, sin_ref, o_ref, wbf_ref = refs
    else:
        x_ref, w_ref, o_ref, wbf_ref = refs
    j = pl.program_id(0)

    @pl.when(pl.program_id(1) == 0)
    def _():
        wbf_ref[...] = w_ref[...].astype(BF16)

    contract = (((1,), (1 if w_is_transposed else 0,)), ((), ()))
    for mode in sorted(set(modes)):
        cond = functools.reduce(jnp.logical_or, [j == jj for jj, mm in enumerate(modes) if mm == mode])

        @pl.when(cond)
        def _(mode=mode):
            kind, scale = mode
            rows = x_ref.shape[0] // PROJ_ROW_SPLIT
            sls = [slice(r * rows, (r + 1) * rows) for r in range(PROJ_ROW_SPLIT)]
            accs = [lax.dot_general(x_ref[sl, :], wbf_ref[...], contract, preferred_element_type=F32)
                    for sl in sls]
            for sl, acc in zip(sls, accs):
                if kind == "rope":
                    acc = _rope_rows(acc, cos_ref[sl, :], sin_ref[sl, :], scale)
                elif kind == "silu":
                    acc = _silu(acc)
                elif scale != 1.0:
                    acc = acc * scale
                o_ref[sl, :] = acc.astype(o_ref.dtype)


def _in_proj(x, w, modes, tm, rope=None, w_is_transposed=False):
    m, k = x.shape
    nt = len(modes)
    assert m % tm == 0 and tm % (PROJ_ROW_SPLIT * 16) == 0
    if w_is_transposed:
        w_spec = pl.BlockSpec((None, PROJ_TN, k), lambda j, i: (0, j, 0))
        w_scratch = pltpu.VMEM((PROJ_TN, k), BF16)
    else:
        w_spec = pl.BlockSpec((None, k, PROJ_TN), lambda j, i: (0, 0, j))
        w_scratch = pltpu.VMEM((k, PROJ_TN), BF16)
    in_specs = [pl.BlockSpec((tm, k), lambda j, i: (i, 0)), w_spec]
    args = [x, w]
    if rope is not None:
        in_specs += [pl.BlockSpec((tm, RET_DK), lambda j, i: (i, 0))] * 2
        args += list(rope)
    return pl.pallas_call(
        functools.partial(_in_proj_kernel, modes=tuple(modes), has_rope=rope is not None,
                          w_is_transposed=w_is_transposed),
        grid=(nt, m // tm),
        in_specs=in_specs,
        out_specs=pl.BlockSpec((tm, PROJ_TN), lambda j, i: (i, j)),
        out_shape=jax.ShapeDtypeStruct((m, nt * PROJ_TN), BF16),
        scratch_shapes=[w_scratch],
        compiler_params=_cparams(("arbitrary", "arbitrary")),
        name="in_proj",
    )(*args)


def _ret_log_decay(h):
    return math.log1p(-(2.0 ** (-5.0 - h)))


def _retention_kernel(q_ref, k_ref, v_ref, z_ref, nw_ref, o_ref, s_ref, dec_ref, qs_ref, ks_ref):
    first_step = (pl.program_id(0) == 0) & (pl.program_id(1) == 0)

    @pl.when(first_step)
    def _():
        ii = lax.broadcasted_iota(jnp.int32, (CHUNK, CHUNK), 0)
        jj = lax.broadcasted_iota(jnp.int32, (CHUNK, CHUNK), 1)
        diff = (ii - jj).astype(F32)
        row = ii.astype(F32)
        for h in range(RET_HEADS):
            lg = _ret_log_decay(h)
            dec_ref[h] = jnp.where(ii >= jj, jnp.exp(lg * jnp.maximum(diff, 0.0)), 0.0)
            qs_ref[h] = jnp.exp(lg * (row + 1.0))
            ks_ref[h] = jnp.exp(lg * (CHUNK - 1.0 - row))

    @pl.when(pl.program_id(1) == 0)
    def _():
        s_ref[...] = jnp.zeros_like(s_ref)

    heads = range(RET_HEADS)
    vcs = [slice(h * RET_DV, (h + 1) * RET_DV) for h in heads]
    nt_dims = (((1,), (1,)), ((), ()))
    tn_dims = (((0,), (0,)), ((), ()))
    states = [s_ref[h] for h in heads]
    for r in range(q_ref.shape[0] // CHUNK):
        rows = slice(r * CHUNK, (r + 1) * CHUNK)
        qs = [q_ref[rows, h * RET_DK:(h + 1) * RET_DK] for h in heads]
        ks = [k_ref[rows, h * RET_DK:(h + 1) * RET_DK] for h in heads]
        vs = [v_ref[rows, vc] for vc in vcs]
        scores = [lax.dot_general(qs[h], ks[h], nt_dims, preferred_element_type=F32) for h in heads]
        kvs = [lax.dot_general((ks[h].astype(F32) * ks_ref[h]).astype(BF16), vs[h], tn_dims,
                               preferred_element_type=F32) for h in heads]
        lhs = [jnp.concatenate([(scores[h] * dec_ref[h]).astype(BF16),
                                (qs[h].astype(F32) * qs_ref[h]).astype(BF16)], axis=1) for h in heads]
        outs = [jnp.dot(lhs[h], jnp.concatenate([vs[h], states[h].astype(BF16)], axis=0),
                        preferred_element_type=F32) for h in heads]
        states = [states[h] * math.exp(_ret_log_decay(h) * CHUNK) + kvs[h] for h in heads]
        for h in heads:
            o_ref[rows, vcs[h]] = (_rms_scale(outs[h], nw_ref[:, vcs[h]])
                                   * z_ref[rows, vcs[h]].astype(F32)).astype(o_ref.dtype)
    for h in heads:
        s_ref[h] = states[h]


def _retention(proj, norm_w, bsz, nt):
    m = proj.shape[0]
    tile = ROW_CHUNKS * CHUNK
    tab = pltpu.VMEM((RET_HEADS, CHUNK, CHUNK), F32)
    row = lambda width, col: pl.BlockSpec((tile, width), lambda b, t: (b * nt + t, col))
    return pl.pallas_call(
        _retention_kernel,
        grid=(bsz, nt),
        in_specs=[row(RET_QK, 0), row(RET_QK, 1), row(RET_W, 1), row(RET_W, 2),
                  pl.BlockSpec((1, RET_W), lambda b, t: (0, 0))],
        out_specs=row(RET_W, 0),
        out_shape=jax.ShapeDtypeStruct((m, RET_W), BF16),
        scratch_shapes=[pltpu.VMEM((RET_HEADS, RET_DK, RET_DV), F32), tab, tab, tab],
        compiler_params=_cparams(("arbitrary", "arbitrary")),
        name="retention",
    )(proj, proj, proj, proj, norm_w.reshape(1, RET_W))


def _s5_prep_kernel(lre_ref, lim_ref, ldt_ref, bre_ref, bim_ref, cre_ref, cim_ref,
                    pre_ref, pim_ref, win_ref, wcre_ref, wcim_ref, kd_ref):
    lre = lre_ref[...]
    lim = lim_ref[...]
    dt = jnp.exp(ldt_ref[...])
    tseg = pre_ref.shape[1]
    n = float(S5_STEPS) * (lax.broadcasted_iota(jnp.int32, (tseg, 1), 0).astype(F32) + 1.0)
    mag = jnp.exp(n * (lre * dt))
    ang = n * (lim * dt)
    p_re = mag * jnp.cos(ang)
    p_im = mag * jnp.sin(ang)
    for sl in range(S5_SLABS):
        pre_ref[sl] = p_re[:, sl * LANES:(sl + 1) * LANES]
        pim_ref[sl] = p_im[:, sl * LANES:(sl + 1) * LANES]
    apow = [(jnp.ones_like(lre), jnp.zeros_like(lre))]
    for mm in range(1, S5_STEPS):
        mg = jnp.exp(float(mm) * (lre * dt))
        apow.append((mg * jnp.cos(float(mm) * (lim * dt)), mg * jnp.sin(float(mm) * (lim * dt))))
    a_re, a_im = apow[1]
    den = lre * lre + lim * lim
    nr = a_re - 1.0
    f_re = (nr * lre + a_im * lim) / den
    f_im = (a_im * lre - nr * lim) / den
    b_re = bre_ref[...]
    b_im = bim_ref[...]
    bb_re = f_re * b_re - f_im * b_im
    bb_im = f_re * b_im + f_im * b_re
    c_re = cre_ref[...]
    c_im = cim_ref[...]
    cmul = lambda xr, xi, yr, yi: (xr * yr - xi * yi, xr * yi + xi * yr)
    abb = [cmul(pr, pi, bb_re, bb_im) for pr, pi in apow]
    ca = [cmul(c_re, c_im, pr, pi) for pr, pi in apow]
    blk = S5_F // S5_JB
    rows = lax.broadcasted_iota(jnp.int32, (S5_BLK_IN, blk), 0)
    cols = lax.broadcasted_iota(jnp.int32, (S5_BLK_IN, blk), 1)
    same_group = (rows // S5_GH) == (cols // S5_P)

    def block_diag(a, j):
        piece = a[:, j * blk:(j + 1) * blk]
        return jnp.where(same_group, jnp.concatenate([piece] * (S5_G // S5_JB), axis=0), 0.0)

    kd_ref[...] = jnp.zeros_like(kd_ref)
    per_kd = S5_K0_IN // S5_BLK_IN
    nt_dims = (((1,), (1,)), ((), ()))
    for j in range(S5_JB):
        bd_cre, bd_cim = block_diag(c_re, j), block_diag(c_im, j)
        for st in range(S5_STEPS):
            rs = slice(st * S5_BLK_IN, (st + 1) * S5_BLK_IN)
            w_re, w_im = abb[S5_STEPS - 1 - st]
            win_ref[j, rs, :blk] = block_diag(w_re, j).astype(BF16)
            win_ref[j, rs, blk:] = block_diag(w_im, j).astype(BF16)
            wcre_ref[j, rs, :] = block_diag(ca[st][0], j).astype(BF16)
            wcim_ref[j, rs, :] = block_diag(ca[st][1], j).astype(BF16)
        r0 = (j % per_kd) * S5_BLK_IN
        for mm in range(S5_STEPS - 1):
            kd = (lax.dot_general(block_diag(abb[mm][0], j), bd_cre, nt_dims, preferred_element_type=F32)
                  - lax.dot_general(block_diag(abb[mm][1], j), bd_cim, nt_dims, preferred_element_type=F32))
            kd_ref[mm, j // per_kd, r0:r0 + S5_BLK_IN, r0:r0 + S5_BLK_IN] = kd.astype(BF16)


def _s5_prep(lam_re, lam_im, log_dt, b_re, b_im, c_re, c_im, tseg):
    flat = lambda a: a.reshape(1, S5_F)
    ldt = jnp.broadcast_to(log_dt[:, None], (S5_G, S5_P))
    bt = lambda a: a.reshape(S5_F, S5_GH).T
    ct = lambda a: a.transpose(1, 0, 2).reshape(S5_GH, S5_F)
    out = jax.ShapeDtypeStruct
    blk = S5_F // S5_JB
    wc = out((S5_JB, S5_STEPS * S5_BLK_IN, blk), BF16)
    return pl.pallas_call(
        _s5_prep_kernel,
        out_shape=(out((S5_SLABS, tseg, LANES), F32), out((S5_SLABS, tseg, LANES), F32),
                   out((S5_JB, S5_STEPS * S5_BLK_IN, 2 * blk), BF16), wc, wc,
                   out((S5_STEPS - 1, S5_W // S5_K0_IN, S5_K0_IN, S5_K0_IN), BF16)),
        compiler_params=pltpu.CompilerParams(vmem_limit_bytes=VMEM_LIMIT),
        name="s5_prep",
    )(flat(lam_re), flat(lam_im), flat(ldt), bt(b_re), bt(b_im), ct(c_re.astype(F32)), ct(c_im.astype(F32)))


def _gelu_tanh(x):
    return 0.5 * x * (1.0 + jnp.tanh(math.sqrt(2.0 / math.pi) * (x + 0.044715 * (x * x * x))))


def _s5_kernel(u_ref, z_ref, win_ref, wcre_ref, wcim_ref, kd_ref,
               pre_ref, pim_ref, d_ref, wglu_ref,
               o_ref, bure_ref, buim_ref, cre_ref, cim_ref, xinre_ref, xinim_ref, yhat_ref, par_ref,
               *, tseg, slab_group):
    ci = pl.program_id(1)

    @pl.when(ci == 0)
    def _():
        cre_ref[...] = jnp.zeros_like(cre_ref)
        cim_ref[...] = jnp.zeros_like(cim_ref)
        yhat_ref[...] = jnp.zeros_like(yhat_ref)

    nrow = u_ref.shape[0] // S5_STEPS
    lane_slabs = S5_W // LANES
    u_all = u_ref[...].astype(F32)
    for s in range(lane_slabs):
        par_ref[s] = u_all[:, s * LANES:(s + 1) * LANES]
    phase = lambda p: jnp.concatenate([par_ref[s, pl.ds(p, nrow, stride=S5_STEPS), :]
                                       for s in range(lane_slabs)], axis=1)
    u32 = [phase(p) for p in range(S5_STEPS)]
    ubf = [u.astype(BF16) for u in u32]
    spb = S5_SLABS // S5_JB
    for j in range(S5_JB):
        blk = slice(j * S5_BLK_IN, (j + 1) * S5_BLK_IN)
        res = jnp.dot(jnp.concatenate([u[:, blk] for u in ubf], axis=1), win_ref[j], preferred_element_type=F32)
        for s in range(spb):
            bure_ref[j * spb + s] = res[:, s * LANES:(s + 1) * LANES]
            buim_ref[j * spb + s] = res[:, (spb + s) * LANES:(spb + s + 1) * LANES]

    for g0 in range(0, S5_SLABS, slab_group):
        slabs = list(range(g0, g0 + slab_group))
        a_re = [jnp.broadcast_to(pre_ref[sl, 0:1, :], (S5_SEGS, LANES)) for sl in slabs]
        a_im = [jnp.broadcast_to(pim_ref[sl, 0:1, :], (S5_SEGS, LANES)) for sl in slabs]

        def scan_body(tau, carry, slabs=slabs, a_re=a_re, a_im=a_im):
            xr, xi = carry
            nr, ni = [], []
            for n, sl in enumerate(slabs):
                rows = pl.ds(tau, S5_SEGS, stride=tseg)
                br = bure_ref[sl, rows, :]
                bi = buim_ref[sl, rows, :]
                r = a_re[n] * xr[n] - a_im[n] * xi[n] + br
                i = a_re[n] * xi[n] + a_im[n] * xr[n] + bi
                bure_ref[sl, rows, :] = r
                buim_ref[sl, rows, :] = i
                nr.append(r)
                ni.append(i)
            return tuple(nr), tuple(ni)

        zero = tuple(jnp.zeros((S5_SEGS, LANES), F32) for _ in slabs)
        end_re, end_im = lax.fori_loop(0, tseg, scan_body, (zero, zero),
                                       unroll=S5_SCAN_UNROLL if tseg % S5_SCAN_UNROLL == 0 else 1)

        for n, sl in enumerate(slabs):
            at_re = pre_ref[sl, tseg - 1:tseg, :]
            at_im = pim_ref[sl, tseg - 1:tseg, :]
            xr = cre_ref[sl, 0:1, :]
            xi = cim_ref[sl, 0:1, :]
            rows_re, rows_im = [], []
            for s in range(S5_SEGS):
                rows_re.append(xr)
                rows_im.append(xi)
                er = end_re[n][s:s + 1]
                ei = end_im[n][s:s + 1]
                xr, xi = er + at_re * xr - at_im * xi, ei + at_re * xi + at_im * xr
            cre_ref[sl] = jnp.broadcast_to(xr, (S5_SEGS, LANES))
            cim_ref[sl] = jnp.broadcast_to(xi, (S5_SEGS, LANES))
            xinre_ref[sl] = jnp.concatenate(rows_re, axis=0)
            xinim_ref[sl] = jnp.concatenate(rows_im, axis=0)

    def fix_body(sl, carry):
        xin_re = xinre_ref[sl]
        xin_im = xinim_ref[sl]
        for tau in range(tseg):
            rows = pl.ds(tau, S5_SEGS, stride=tseg)
            pr = pre_ref[sl, tau:tau + 1, :]
            pi = pim_ref[sl, tau:tau + 1, :]
            bure_ref[sl, rows, :] = bure_ref[sl, rows, :] + (pr * xin_re - pi * xin_im)
            buim_ref[sl, rows, :] = buim_ref[sl, rows, :] + (pr * xin_im + pi * xin_re)
        return carry

    lax.fori_loop(0, S5_SLABS, fix_body, 0)

    nt_dims = (((1,), (1,)), ((), ()))
    proj = [[] for _ in range(S5_STEPS)]
    for j in range(S5_JB):
        x_re = jnp.concatenate([bure_ref[j * spb + s] for s in range(spb)], axis=1).astype(BF16)
        x_im = jnp.concatenate([buim_ref[j * spb + s] for s in range(spb)], axis=1).astype(BF16)
        allm = (lax.dot_general(x_re, wcre_ref[j], nt_dims, preferred_element_type=F32)
                - lax.dot_general(x_im, wcim_ref[j], nt_dims, preferred_element_type=F32))
        for mm in range(S5_STEPS):
            proj[mm].append(allm[:, mm * S5_BLK_IN:(mm + 1) * S5_BLK_IN])
    proj = [jnp.concatenate(p, axis=1) for p in proj]
    ridx = lax.broadcasted_iota(jnp.int32, (nrow, 1), 0)
    ys = []
    for st in range(S5_STEPS - 1):
        y = jnp.where(ridx == 0, yhat_ref[st:st + 1, :], pltpu.roll(proj[st + 1], 1, 0))
        direct = []
        for jj in range(S5_W // S5_K0_IN):
            cs = slice(jj * S5_K0_IN, (jj + 1) * S5_K0_IN)
            acc = jnp.dot(ubf[0][:, cs], kd_ref[st, jj], preferred_element_type=F32)
            for l in range(1, st + 1):
                acc = acc + jnp.dot(ubf[l][:, cs], kd_ref[st - l, jj], preferred_element_type=F32)
            direct.append(acc)
        ys.append(y + jnp.concatenate(direct, axis=1))
    ys.append(proj[0])
    for st in range(S5_STEPS - 1):
        yhat_ref[st:st + 1, :] = proj[st + 1][nrow - 1:, :]
    for p, (y, u) in enumerate(zip(ys, u32)):
        y = _gelu_tanh(y + d_ref[...] * u)
        gate = jnp.dot(y.astype(BF16), wglu_ref[...], preferred_element_type=F32)
        y = y * (1.0 / (1.0 + jnp.exp(-gate)))
        for s in range(lane_slabs):
            par_ref[s, pl.ds(p, nrow, stride=S5_STEPS), :] = y[:, s * LANES:(s + 1) * LANES]
    y_all = jnp.concatenate([par_ref[s] for s in range(lane_slabs)], axis=1)
    o_ref[...] = (y_all * z_ref[...].astype(F32)).astype(o_ref.dtype)


def _s5_time_tile(rows_per_batch):
    for tseg in (22, 44, 6, 2, 1):
        t = S5_STEPS * S5_SEGS * tseg
        if rows_per_batch % t == 0 and t % 16 == 0:
            return t, tseg
    raise ValueError(f"no S5 time tile for {rows_per_batch}")


def _s5(proj, bsz, prep, d, wglu, t, tseg):
    m = proj.shape[0]
    p_re, p_im, win, wcre, wcim, kd = prep
    rows = t // S5_STEPS
    nt = m // bsz // t
    ucol = (2 * RET_QK + 2 * RET_W) // S5_W
    const = lambda a: pl.BlockSpec(a.shape, lambda b, i: (0,) * a.ndim, pipeline_mode=pl.Buffered(1))
    tile = lambda col: pl.BlockSpec((t, S5_W), lambda b, i: (b * nt + i, col))
    d2 = d.reshape(1, S5_W)
    return pl.pallas_call(
        functools.partial(_s5_kernel, tseg=tseg, slab_group=8),
        grid=(bsz, nt),
        in_specs=[tile(ucol), tile(ucol + 1),
                  const(win), const(wcre), const(wcim), const(kd),
                  const(p_re), const(p_im), const(d2), const(wglu)],
        out_specs=tile(0),
        out_shape=jax.ShapeDtypeStruct((m, S5_W), BF16),
        scratch_shapes=[pltpu.VMEM((S5_SLABS, rows, LANES), F32), pltpu.VMEM((S5_SLABS, rows, LANES), F32)]
        + [pltpu.VMEM((S5_SLABS, S5_SEGS, LANES), F32)] * 4
        + [pltpu.VMEM((SUBLANES, S5_W), F32), pltpu.VMEM((S5_W // LANES, t, LANES), F32)],
        compiler_params=_cparams(("arbitrary", "arbitrary")),
        name="s5",
    )(proj, proj, win, wcre, wcim, kd, p_re, p_im, d2, wglu)


def _out_ab_kernel(lead_ref, xa_ref, xb_ref, xc_ref, oa_ref, ob_ref, w_ref, nw_ref, wg_ref,
                   h1_ref, xn_ref, gl_ref):
    t = pl.program_id(1)
    first = jnp.where(t == 0, lead_ref[...], xa_ref[...])
    h0 = (first, xb_ref[...], xc_ref[...])
    sls = [slice(r * CHUNK, (r + 1) * CHUNK) for r in range(ROW_CHUNKS)]
    accs = [jnp.dot(oa_ref[sl, :], w_ref[:RET_W, :], preferred_element_type=F32)
            + jnp.dot(ob_ref[sl, :], w_ref[RET_W:, :], preferred_element_type=F32) for sl in sls]
    h1s = [h0[r] + accs[r] for r in range(ROW_CHUNKS)]
    xns = [_rms_scale(h1, nw_ref[...]).astype(BF16) for h1 in h1s]
    gls = [jnp.dot(xn, wg_ref[...], preferred_element_type=F32) for xn in xns]
    for r, sl in enumerate(sls):
        h1_ref[sl, :] = h1s[r]
        xn_ref[sl, :] = xns[r]
        gl_ref[sl, :] = gls[r]


def _out_ab(x, lead, oa, ob, w, norm_w, wg, nt):
    bsz, _, d = x.shape
    tile = ROW_CHUNKS * CHUNK
    m = bsz * nt * tile
    row = lambda width: pl.BlockSpec((tile, width), lambda b, t: (b * nt + t, 0))
    const = lambda a: pl.BlockSpec(a.shape, lambda b, t: (0, 0))
    out = jax.ShapeDtypeStruct
    return pl.pallas_call(
        _out_ab_kernel,
        grid=(bsz, nt),
        in_specs=[pl.BlockSpec((CHUNK, d), lambda b, t: (0, 0))] + _chunk_specs(d, 1)
        + [row(RET_W), row(S5_W), const(w), pl.BlockSpec((1, d), lambda b, t: (0, 0)), const(wg)],
        out_specs=[row(d), row(d), row(GATE_PAD)],
        out_shape=(out((m, d), F32), out((m, d), BF16), out((m, GATE_PAD), F32)),
        compiler_params=_cparams(("parallel", "parallel")),
        name="out_proj_ab",
    )(lead, x, x, x, oa, ob, w, norm_w.reshape(1, d), wg)


LOG2E = 1.4426950408889634


def _split2(x):
    x1 = x.astype(BF16)
    return x1, (x - x1.astype(F32)).astype(BF16)


def _neg_abs(x):
    return -jnp.abs(x)


def _block_ref_rows(b, level):
    rows, width = b.shape
    half = 1 << level
    blk = 2 * half
    if blk >= 8:
        b3 = b.reshape(rows // blk, blk, width)
        ref = jnp.broadcast_to(b3[:, half - 1:half, :], b3.shape)
        return ref.reshape(rows, width)
    b3 = b.reshape(rows // 8, 8, width)
    sub = lax.broadcasted_iota(jnp.int32, b3.shape, 1)
    pick = lambda r: jnp.broadcast_to(b3[:, r:r + 1, :], b3.shape)
    if blk == 4:
        ref = jnp.where(sub < 4, pick(1), pick(5))
    else:
        ref = jnp.where(sub < 2, pick(0), jnp.where(sub < 4, pick(2), jnp.where(sub < 6, pick(4), pick(6))))
    return ref.reshape(rows, width)


def _gla_kernel(q_ref, k_ref, v_ref, z_ref, gl_ref, wg_ref, bg_ref, nw_ref, h_ref, wo_ref, fw_ref,
                y_ref, st_ref, mix_ref, wobf_ref, *, nc, nchunks):
    c = pl.program_id(0)
    n = jnp.minimum(c, nchunks - 1) % nc

    @pl.when(c == 0)
    def _():
        mix_ref[...] = jnp.zeros_like(mix_ref)
        wobf_ref[...] = wo_ref[...].astype(BF16)

    @pl.when(n == 0)
    def _():
        st_ref[...] = jnp.zeros_like(st_ref)

    ridx = lax.broadcasted_iota(jnp.int32, (CHUNK, 1), 0)
    ii = lax.broadcasted_iota(jnp.int32, (CHUNK, CHUNK), 0)
    jj = lax.broadcasted_iota(jnp.int32, (CHUNK, CHUNK), 1)
    pair_code = jnp.where(ii > jj, ii ^ jj, 0)
    eye = ii == jj
    tri = (ii >= jj).astype(BF16)

    gl1, gl2 = _split2(gl_ref[:, :GLA_RANK])
    wg1, wg2 = _split2(wg_ref[...])
    x = jnp.dot(jnp.concatenate([gl1, gl2, gl1], axis=1), jnp.concatenate([wg1, wg1, wg2], axis=0),
                preferred_element_type=F32) + bg_ref[...]
    log_a = (jnp.minimum(x, 0.0) - jnp.log(1.0 + jnp.exp(-jnp.abs(x)))) * (LOG2E / GLA_TAU)
    log_a = jnp.where(ridx >= jnp.where(n > 0, 0, PAD), log_a, 0.0)
    g1, g2 = _split2(log_a)
    b_all = jnp.dot(jnp.concatenate([tri, tri], axis=1), jnp.concatenate([g1, g2], axis=0),
                    preferred_element_type=F32)
    odd = (ridx & 1) == 1
    h2 = h_ref[...]

    def finish(heads, qs, ks, bs, scores):
        for i, h in enumerate(heads):
            vc = slice(h * GLA_DV, (h + 1) * GLA_DV)
            q, k, b = qs[i], ks[i], bs[i]
            v = v_ref[:, vc]
            b_last = b[CHUNK - 1:CHUNK, :]
            o = jnp.dot(scores[i].astype(BF16), v, preferred_element_type=F32)
            st = st_ref[h]
            qe = (q * jnp.exp2(b)).astype(BF16)
            o = o + lax.dot_general(qe, st.astype(BF16), (((1,), (1,)), ((), ())), preferred_element_type=F32)
            ke = (k * jnp.exp2(b_last - b)).astype(BF16)
            kv_t = lax.dot_general(v, ke, (((0,), (0,)), ((), ())), preferred_element_type=F32)
            st_ref[h] = st * jnp.exp2(b_last) + kv_t
            mix_ref[:, vc] = (_rms_scale(o, nw_ref[:, vc]) * z_ref[:, vc].astype(F32)).astype(BF16)

    pending = []
    for h0 in range(0, GLA_HEADS, 2):
        heads = (h0, h0 + 1)
        qs, ks, bs, scores = [], [], [], []
        for h in heads:
            kc = slice(h * GLA_DK, (h + 1) * GLA_DK)
            vc = slice(h * GLA_DV, (h + 1) * GLA_DV)
            h2 = h2 + jnp.dot(mix_ref[:, vc], wobf_ref[vc, :], preferred_element_type=F32)
            qs.append(q_ref[:, kc].astype(F32))
            ks.append(k_ref[:, kc].astype(F32))
            bs.append(b_all[:, kc])
            scores.append(jnp.where(eye, jnp.sum(qs[-1] * ks[-1], axis=1, keepdims=True), 0.0))
        for lv in range(GLA_LEVELS):
            half = 1 << lv
            if lv == GLA_LEVELS // 2 and pending:
                finish(*pending.pop())
            zz = []
            for q, k, b in zip(qs, ks, bs):
                if lv == 0:
                    expo = jnp.where(odd, b - pltpu.roll(b, 1, 0), 0.0)
                else:
                    expo = _neg_abs(b - _block_ref_rows(b, lv))
                if half >= 8:
                    sel = jnp.concatenate([(q if r % 2 else k)[r * half:(r + 1) * half]
                                           for r in range(CHUNK // half)], axis=0)
                else:
                    sel = jnp.where(((ridx >> lv) & 1) == 1, q, k)
                zz.append((sel * jnp.exp2(expo)).astype(BF16))
            z2 = jnp.concatenate(zz, axis=0)
            gram = lax.dot_general(z2, z2, (((1,), (1,)), ((), ())), preferred_element_type=F32)
            mask = (pair_code >> lv) == 1
            for i in range(2):
                blk = gram[i * CHUNK:(i + 1) * CHUNK, i * CHUNK:(i + 1) * CHUNK]
                scores[i] = scores[i] + jnp.where(mask, blk, 0.0)
        pending.append((heads, qs, ks, bs, scores))
    finish(*pending.pop())

    y_ref[...] = _rms_scale(h2, fw_ref[...])


def _gla(proj, glow, w_gate, b_gate, norm_w, h1, w_out, final_w, bsz, nc):
    m, d = h1.shape
    nchunks = m // CHUNK
    const = lambda *shape: pl.BlockSpec(shape, lambda c: (0,) * len(shape))
    cur = lambda width, col: pl.BlockSpec((CHUNK, width), lambda c: (jnp.minimum(c, nchunks - 1), col))
    prev = lambda c: jnp.maximum(c - 1, 0)
    return pl.pallas_call(
        functools.partial(_gla_kernel, nc=nc, nchunks=nchunks),
        grid=(nchunks + 1,),
        in_specs=[cur(GLA_QK, 0), cur(GLA_QK, 1), cur(GLA_W, 1), cur(GLA_W, 2), cur(GATE_PAD, 0),
                  const(GLA_RANK, GLA_QK), const(1, GLA_QK), const(1, GLA_W),
                  pl.BlockSpec((CHUNK, d), lambda c: (prev(c), 0)),
                  pl.BlockSpec((None, GLA_W, d), lambda c: (0, 0, 0), pipeline_mode=pl.Buffered(1)),
                  const(1, d)],
        out_specs=pl.BlockSpec((None, CHUNK, d),
                               lambda c: (prev(c) // nc, jnp.maximum(prev(c) % nc - 1, 0), 0)),
        out_shape=jax.ShapeDtypeStruct((bsz, (nc - 1) * CHUNK, d), F32),
        scratch_shapes=[pltpu.VMEM((GLA_HEADS, GLA_DV, GLA_DK), F32), pltpu.VMEM((CHUNK, GLA_W), BF16),
                        pltpu.VMEM((GLA_W, d), BF16)],
        compiler_params=_cparams(("arbitrary",)),
        name="gla_out",
    )(proj, proj, proj, proj, glow, w_gate.astype(F32), b_gate.reshape(1, GLA_QK).astype(F32),
      norm_w.reshape(1, GLA_W).astype(F32), h1, w_out, final_w.reshape(1, d).astype(F32))


def _rope_tables(bsz, rows_per_batch):
    pos = np.maximum(np.arange(rows_per_batch, dtype=np.float64) - PAD, 0.0)
    inv_freq = np.power(ROPE_BASE, -np.arange(0, RET_DK, 2, dtype=np.float64) / RET_DK)
    ang = pos[:, None] * inv_freq[None, :]
    cos, sin = np.cos(ang), np.sin(ang)
    cos2 = np.tile(np.concatenate([cos, cos], axis=1), (bsz, 1)).astype(np.float32)
    sin2 = np.tile(np.concatenate([-sin, sin], axis=1), (bsz, 1)).astype(np.float32)
    return jnp.asarray(cos2), jnp.asarray(sin2)


def kernel(x, meta, norm_ab_w, w_in_ab, ret_norm_w, s5_lam_re, s5_lam_im, s5_log_dt, s5_b_re, s5_b_im,
           s5_c_re, s5_c_im, s5_d, s5_w_glu, w_out_ab, norm_c_w, w_in_c, gla_w_gate, gla_b_gate,
           gla_norm_w, w_out_c, final_norm_w):
    bsz, seq, d = x.shape
    assert seq % CHUNK == 0 and w_in_ab.shape[0] == 1 and w_in_c.shape[0] == 1
    rpb = seq + CHUNK
    nc = rpb // CHUNK
    assert nc % ROW_CHUNKS == 0
    nt = nc // ROW_CHUNKS
    m = bsz * rpb
    tm = _pick_tile(m, (1408, 768, 384, 256, 128))
    lead = jnp.concatenate([jnp.zeros((PAD, d), x.dtype), meta.astype(x.dtype)], axis=0)

    xn0 = _embed_norm(x, lead, norm_ab_w[0], nt)
    plain, silu = ("plain", 1.0), ("silu", 1.0)
    modes_ab = [("rope", 1.0), ("rope", RET_DK ** -0.5), plain, plain, silu, silu, plain, silu]
    proj0 = _in_proj(xn0, w_in_ab, modes_ab, tm, rope=_rope_tables(bsz, rpb))
    o_a = _retention(proj0, ret_norm_w[0], bsz, nt)
    t, tseg = _s5_time_tile(rpb)
    prep = _s5_prep(s5_lam_re[0], s5_lam_im[0], s5_log_dt[0], s5_b_re[0], s5_b_im[0],
                    s5_c_re[0], s5_c_im[0], tseg)
    o_b = _s5(proj0, bsz, prep, s5_d[0], s5_w_glu[0].astype(BF16), t, tseg)

    w_in_c_t = jnp.swapaxes(w_in_c, 1, 2)
    n_main = 2 * GLA_QK + 2 * GLA_W
    wg = jnp.pad(w_in_c_t[0, n_main:, :], ((0, GATE_PAD - GLA_RANK), (0, 0))).astype(BF16).T
    h1, xn1, glow = _out_ab(x, lead, o_a, o_b, w_out_ab[0].astype(BF16), norm_c_w[0], wg, nt)

    modes_c = [("plain", GLA_DK ** -0.5), plain, plain, plain, silu, silu]
    proj1 = _in_proj(xn1, w_in_c_t, modes_c, tm, w_is_transposed=True)
    return _gla(proj1, glow, gla_w_gate[0], gla_b_gate[0], gla_norm_w[0], h1, w_out_c,
                final_norm_w, bsz, nc)
```

```python
import functools
import math

import numpy as np
import jax
import jax.numpy as jnp
from jax import lax
from jax.experimental import pallas as pl
from jax.experimental.pallas import tpu as pltpu

F32 = jnp.float32
BF16 = jnp.bfloat16

N_META = 16
LANES = 128
SUBLANES = 8
CHUNK = 128
PAD = CHUNK - N_META
EPS = 1e-6
ROW_CHUNKS = 3

RET_HEADS = 8
RET_DK = 128
RET_DV = 256
RET_QK = RET_HEADS * RET_DK
RET_W = RET_HEADS * RET_DV
ROPE_BASE = 10000.0

S5_W = 1024
S5_GH = 16
S5_G = 64
S5_P = 64
S5_F = S5_G * S5_P
S5_SLABS = S5_F // LANES
MXU_TILE = 256
S5_JB = 8
S5_BLK_IN = S5_W // S5_JB
S5_K0_IN = MXU_TILE
S5_SEGS = SUBLANES
S5_SCAN_UNROLL = 4

GLA_HEADS = 4
GLA_DK = 256
GLA_DV = 512
GLA_QK = GLA_HEADS * GLA_DK
GLA_W = GLA_HEADS * GLA_DV
GLA_RANK = 16
GLA_TAU = 16.0
GLA_LEVELS = 7
GATE_PAD = LANES

PROJ_TN = 1024
PROJ_ROW_SPLIT = 4
VMEM_LIMIT = 56 * 1024 * 1024


def _cparams(sem):
    return pltpu.CompilerParams(dimension_semantics=sem, vmem_limit_bytes=VMEM_LIMIT)


def _silu(x):
    return x * (1.0 / (1.0 + jnp.exp(-x)))


def _pick_tile(n, candidates):
    for c in candidates:
        if n % c == 0:
            return c
    raise ValueError(f"no tile for {n}")


def _rms_scale(x, w):
    return x * lax.rsqrt(jnp.mean(x * x, axis=-1, keepdims=True) + EPS) * w


def _chunk_specs(d, n_before):
    return [pl.BlockSpec((None, CHUNK, d),
                         lambda b, t, k=k: (b, jnp.maximum(ROW_CHUNKS * t + k - n_before, 0), 0))
            for k in range(ROW_CHUNKS)]


def _embed_norm_kernel(lead_ref, xa_ref, xb_ref, xc_ref, w_ref, o_ref):
    t = pl.program_id(1)
    first = jnp.where(t == 0, lead_ref[...], xa_ref[...])
    for r, rows in enumerate((first, xb_ref[...], xc_ref[...])):
        o_ref[r * CHUNK:(r + 1) * CHUNK, :] = _rms_scale(rows, w_ref[...]).astype(o_ref.dtype)


def _embed_norm(x, lead, w, nt):
    bsz, _, d = x.shape
    tile = ROW_CHUNKS * CHUNK
    return pl.pallas_call(
        _embed_norm_kernel,
        grid=(bsz, nt),
        in_specs=[pl.BlockSpec((CHUNK, d), lambda b, t: (0, 0))] + _chunk_specs(d, 1)
        + [pl.BlockSpec((1, d), lambda b, t: (0, 0))],
        out_specs=pl.BlockSpec((tile, d), lambda b, t: (b * nt + t, 0)),
        out_shape=jax.ShapeDtypeStruct((bsz * nt * tile, d), BF16),
        compiler_params=_cparams(("parallel", "parallel")),
        name="embed_norm",
    )(lead, x, x, x, w.reshape(1, d))


def _rope_rows(a, cos, sin, scale):
    outs = []
    for h in range(a.shape[1] // RET_DK):
        blk = a[:, h * RET_DK:(h + 1) * RET_DK]
        r = blk * cos + pltpu.roll(blk, RET_DK // 2, 1) * sin
        outs.append(r * scale if scale != 1.0 else r)
    return jnp.concatenate(outs, axis=1)


def _in_proj_kernel(*refs, modes, has_rope, w_is_transposed):
    if has_rope:
        x_ref, w_ref, cos_ref, sin_ref, o_ref, wbf_ref = refs
    else:
        x_ref, w_ref, o_ref, wbf_ref = refs
    j = pl.program_id(0)

    @pl.when(pl.program_id(1) == 0)
    def _():
        wbf_ref[...] = w_ref[...].astype(BF16)

    contract = (((1,), (1 if w_is_transposed else 0,)), ((), ()))
    for mode in sorted(set(modes)):
        cond = functools.reduce(jnp.logical_or, [j == jj for jj, mm in enumerate(modes) if mm == mode])

        @pl.when(cond)
        def _(mode=mode):
            kind, scale = mode
            rows = x_ref.shape[0] // PROJ_ROW_SPLIT
            sls = [slice(r * rows, (r + 1) * rows) for r in range(PROJ_ROW_SPLIT)]
            accs = [lax.dot_general(x_ref[sl, :], wbf_ref[...], contract, preferred_element_type=F32)
                    for sl in sls]
            for sl, acc in zip(sls, accs):
                if kind == "rope":
                    acc = _rope_rows(acc, cos_ref[sl, :], sin_ref[sl, :], scale)
                elif kind == "silu":
                    acc = _silu(acc)
                elif scale != 1.0:
                    acc = acc * scale
                o_ref[sl, :] = acc.astype(o_ref.dtype)


def _in_proj(x, w, modes, tm, rope=None, w_is_transposed=False):
    m, k = x.shape
    nt = len(modes)
    assert m % tm == 0 and tm % (PROJ_ROW_SPLIT * 16) == 0
    if w_is_transposed:
        w_spec = pl.BlockSpec((None, PROJ_TN, k), lambda j, i: (0, j, 0))
        w_scratch = pltpu.VMEM((PROJ_TN, k), BF16)
    else:
        w_spec = pl.BlockSpec((None, k, PROJ_TN), lambda j, i: (0, 0, j))
        w_scratch = pltpu.VMEM((k, PROJ_TN), BF16)
    in_specs = [pl.BlockSpec((tm, k), lambda j, i: (i, 0)), w_spec]
    args = [x, w]
    if rope is not None:
        in_specs += [pl.BlockSpec((tm, RET_DK), lambda j, i: (i, 0))] * 2
        args += list(rope)
    return pl.pallas_call(
        functools.partial(_in_proj_kernel, modes=tuple(modes), has_rope=rope is not None,
                          w_is_transposed=w_is_transposed),
        grid=(nt, m // tm),
        in_specs=in_specs,
        out_specs=pl.BlockSpec((tm, PROJ_TN), lambda j, i: (i, j)),
        out_shape=jax.ShapeDtypeStruct((m, nt * PROJ_TN), BF16),
        scratch_shapes=[w_scratch],
        compiler_params=_cparams(("arbitrary", "arbitrary")),
        name="in_proj",
    )(*args)


def _ret_log_decay(h):
    return math.log1p(-(2.0 ** (-5.0 - h)))


def _retention_kernel(q_ref, k_ref, v_ref, z_ref, nw_ref, o_ref, s_ref, dec_ref, qs_ref, ks_ref):
    first_step = (pl.program_id(0) == 0) & (pl.program_id(1) == 0)

    @pl.when(first_step)
    def _():
        ii = lax.broadcasted_iota(jnp.int32, (CHUNK, CHUNK), 0)
        jj = lax.broadcasted_iota(jnp.int32, (CHUNK, CHUNK), 1)
        diff = (ii - jj).astype(F32)
        row = ii.astype(F32)
        for h in range(RET_HEADS):
            lg = _ret_log_decay(h)
            dec_ref[h] = jnp.where(ii >= jj, jnp.exp(lg * jnp.maximum(diff, 0.0)), 0.0)
            qs_ref[h] = jnp.exp(lg * (row + 1.0))
            ks_ref[h] = jnp.exp(lg * (CHUNK - 1.0 - row))

    @pl.when(pl.program_id(1) == 0)
    def _():
        s_ref[...] = jnp.zeros_like(s_ref)

    heads = range(RET_HEADS)
    vcs = [slice(h * RET_DV, (h + 1) * RET_DV) for h in heads]
    nt_dims = (((1,), (1,)), ((), ()))
    tn_dims = (((0,), (0,)), ((), ()))
    states = [s_ref[h] for h in heads]
    for r in range(q_ref.shape[0] // CHUNK):
        rows = slice(r * CHUNK, (r + 1) * CHUNK)
        qs = [q_ref[rows, h * RET_DK:(h + 1) * RET_DK] for h in heads]
        ks = [k_ref[rows, h * RET_DK:(h + 1) * RET_DK] for h in heads]
        vs = [v_ref[rows, vc] for vc in vcs]
        scores = [lax.dot_general(qs[h], ks[h], nt_dims, preferred_element_type=F32) for h in heads]
        kvs = [lax.dot_general((ks[h].astype(F32) * ks_ref[h]).astype(BF16), vs[h], tn_dims,
                               preferred_element_type=F32) for h in heads]
        lhs = [jnp.concatenate([(scores[h] * dec_ref[h]).astype(BF16),
                                (qs[h].astype(F32) * qs_ref[h]).astype(BF16)], axis=1) for h in heads]
        outs = [jnp.dot(lhs[h], jnp.concatenate([vs[h], states[h].astype(BF16)], axis=0),
                        preferred_element_type=F32) for h in heads]
        states = [states[h] * math.exp(_ret_log_decay(h) * CHUNK) + kvs[h] for h in heads]
        for h in heads:
            o_ref[rows, vcs[h]] = (_rms_scale(outs[h], nw_ref[:, vcs[h]])
                                   * z_ref[rows, vcs[h]].astype(F32)).astype(o_ref.dtype)
    for h in heads:
        s_ref[h] = states[h]


def _retention(proj, norm_w, bsz, nt):
    m = proj.shape[0]
    tile = ROW_CHUNKS * CHUNK
    tab = pltpu.VMEM((RET_HEADS, CHUNK, CHUNK), F32)
    row = lambda width, col: pl.BlockSpec((tile, width), lambda b, t: (b * nt + t, col))
    return pl.pallas_call(
        _retention_kernel,
        grid=(bsz, nt),
        in_specs=[row(RET_QK, 0), row(RET_QK, 1), row(RET_W, 1), row(RET_W, 2),
                  pl.BlockSpec((1, RET_W), lambda b, t: (0, 0))],
        out_specs=row(RET_W, 0),
        out_shape=jax.ShapeDtypeStruct((m, RET_W), BF16),
        scratch_shapes=[pltpu.VMEM((RET_HEADS, RET_DK, RET_DV), F32), tab, tab, tab],
        compiler_params=_cparams(("arbitrary", "arbitrary")),
        name="retention",
    )(proj, proj, proj, proj, norm_w.reshape(1, RET_W))


def _s5_prep_kernel(lre_ref, lim_ref, ldt_ref, bre_ref, bim_ref, cre_ref, cim_ref,
                    pre_ref, pim_ref, w2_ref, wcre_ref, wcim_ref, k0_ref):
    lre = lre_ref[...]
    lim = lim_ref[...]
    dt = jnp.exp(ldt_ref[...])
    tseg = pre_ref.shape[1]
    n = 2.0 * (lax.broadcasted_iota(jnp.int32, (tseg, 1), 0).astype(F32) + 1.0)
    mag = jnp.exp(n * (lre * dt))
    ang = n * (lim * dt)
    p_re = mag * jnp.cos(ang)
    p_im = mag * jnp.sin(ang)
    for sl in range(S5_SLABS):
        pre_ref[sl] = p_re[:, sl * LANES:(sl + 1) * LANES]
        pim_ref[sl] = p_im[:, sl * LANES:(sl + 1) * LANES]
    mag1 = jnp.exp(lre * dt)
    a_re = mag1 * jnp.cos(lim * dt)
    a_im = mag1 * jnp.sin(lim * dt)
    den = lre * lre + lim * lim
    nr = a_re - 1.0
    f_re = (nr * lre + a_im * lim) / den
    f_im = (a_im * lre - nr * lim) / den
    b_re = bre_ref[...]
    b_im = bim_ref[...]
    bb_re = f_re * b_re - f_im * b_im
    bb_im = f_re * b_im + f_im * b_re
    bba_re = a_re * bb_re - a_im * bb_im
    bba_im = a_re * bb_im + a_im * bb_re
    c_re = cre_ref[...]
    c_im = cim_ref[...]
    ca_re = c_re * a_re - c_im * a_im
    ca_im = c_re * a_im + c_im * a_re
    blk = S5_F // S5_JB
    rows = lax.broadcasted_iota(jnp.int32, (S5_BLK_IN, blk), 0)
    cols = lax.broadcasted_iota(jnp.int32, (S5_BLK_IN, blk), 1)
    same_group = (rows // S5_GH) == (cols // S5_P)

    def block_diag(a, j):
        piece = a[:, j * blk:(j + 1) * blk]
        return jnp.where(same_group, jnp.concatenate([piece] * (S5_G // S5_JB), axis=0), 0.0)

    k0_ref[...] = jnp.zeros_like(k0_ref)
    per_k0 = S5_K0_IN // S5_BLK_IN
    for j in range(S5_JB):
        bd_bre, bd_bim = block_diag(bb_re, j), block_diag(bb_im, j)
        bd_cre, bd_cim = block_diag(c_re, j), block_diag(c_im, j)
        w2_ref[j, :S5_BLK_IN, :blk] = block_diag(bba_re, j).astype(BF16)
        w2_ref[j, :S5_BLK_IN, blk:] = block_diag(bba_im, j).astype(BF16)
        w2_ref[j, S5_BLK_IN:, :blk] = bd_bre.astype(BF16)
        w2_ref[j, S5_BLK_IN:, blk:] = bd_bim.astype(BF16)
        wcre_ref[j, :S5_BLK_IN, :] = bd_cre.astype(BF16)
        wcim_ref[j, :S5_BLK_IN, :] = bd_cim.astype(BF16)
        wcre_ref[j, S5_BLK_IN:, :] = block_diag(ca_re, j).astype(BF16)
        wcim_ref[j, S5_BLK_IN:, :] = block_diag(ca_im, j).astype(BF16)
        nt_dims = (((1,), (1,)), ((), ()))
        k0 = (lax.dot_general(bd_bre, bd_cre, nt_dims, preferred_element_type=F32)
              - lax.dot_general(bd_bim, bd_cim, nt_dims, preferred_element_type=F32))
        r0 = (j % per_k0) * S5_BLK_IN
        k0_ref[j // per_k0, r0:r0 + S5_BLK_IN, r0:r0 + S5_BLK_IN] = k0.astype(BF16)


def _s5_prep(lam_re, lam_im, log_dt, b_re, b_im, c_re, c_im, tseg):
    flat = lambda a: a.reshape(1, S5_F)
    ldt = jnp.broadcast_to(log_dt[:, None], (S5_G, S5_P))
    bt = lambda a: a.reshape(S5_F, S5_GH).T
    ct = lambda a: a.transpose(1, 0, 2).reshape(S5_GH, S5_F)
    out = jax.ShapeDtypeStruct
    blk = S5_F // S5_JB
    wc = out((S5_JB, 2 * S5_BLK_IN, blk), BF16)
    return pl.pallas_call(
        _s5_prep_kernel,
        out_shape=(out((S5_SLABS, tseg, LANES), F32), out((S5_SLABS, tseg, LANES), F32),
                   out((S5_JB, 2 * S5_BLK_IN, 2 * blk), BF16), wc, wc,
                   out((S5_W // S5_K0_IN, S5_K0_IN, S5_K0_IN), BF16)),
        compiler_params=pltpu.CompilerParams(vmem_limit_bytes=VMEM_LIMIT),
        name="s5_prep",
    )(flat(lam_re), flat(lam_im), flat(ldt), bt(b_re), bt(b_im), ct(c_re.astype(F32)), ct(c_im.astype(F32)))


def _gelu_tanh(x):
    return 0.5 * x * (1.0 + jnp.tanh(math.sqrt(2.0 / math.pi) * (x + 0.044715 * (x * x * x))))


def _s5_kernel(u_ref, z_ref, w2_ref, wcre_ref, wcim_ref, k0_ref,
               pre_ref, pim_ref, d_ref, wglu_ref,
               o_ref, bure_ref, buim_ref, cre_ref, cim_ref, xinre_ref, xinim_ref, yhat_ref, par_ref,
               *, tseg, slab_group):
    ci = pl.program_id(1)

    @pl.when(ci == 0)
    def _():
        cre_ref[...] = jnp.zeros_like(cre_ref)
        cim_ref[...] = jnp.zeros_like(cim_ref)
        yhat_ref[...] = jnp.zeros_like(yhat_ref)

    half = u_ref.shape[0] // 2
    lane_slabs = S5_W // LANES
    u_all = u_ref[...].astype(F32)
    for s in range(lane_slabs):
        par_ref[s] = u_all[:, s * LANES:(s + 1) * LANES]
    parity = lambda p: jnp.concatenate([par_ref[s, pl.ds(p, half, stride=2), :] for s in range(lane_slabs)], axis=1)
    u_e32, u_o32 = parity(0), parity(1)
    u_e, u_o = u_e32.astype(BF16), u_o32.astype(BF16)
    spb = S5_SLABS // S5_JB
    for j in range(S5_JB):
        blk = slice(j * S5_BLK_IN, (j + 1) * S5_BLK_IN)
        res = jnp.dot(jnp.concatenate([u_e[:, blk], u_o[:, blk]], axis=1), w2_ref[j], preferred_element_type=F32)
        for s in range(spb):
            bure_ref[j * spb + s] = res[:, s * LANES:(s + 1) * LANES]
            buim_ref[j * spb + s] = res[:, (spb + s) * LANES:(spb + s + 1) * LANES]

    for g0 in range(0, S5_SLABS, slab_group):
        slabs = list(range(g0, g0 + slab_group))
        a_re = [jnp.broadcast_to(pre_ref[sl, 0:1, :], (S5_SEGS, LANES)) for sl in slabs]
        a_im = [jnp.broadcast_to(pim_ref[sl, 0:1, :], (S5_SEGS, LANES)) for sl in slabs]

        def scan_body(tau, carry, slabs=slabs, a_re=a_re, a_im=a_im):
            xr, xi = carry
            nr, ni = [], []
            for n, sl in enumerate(slabs):
                rows = pl.ds(tau, S5_SEGS, stride=tseg)
                br = bure_ref[sl, rows, :]
                bi = buim_ref[sl, rows, :]
                r = a_re[n] * xr[n] - a_im[n] * xi[n] + br
                i = a_re[n] * xi[n] + a_im[n] * xr[n] + bi
                bure_ref[sl, rows, :] = r
                buim_ref[sl, rows, :] = i
                nr.append(r)
                ni.append(i)
            return tuple(nr), tuple(ni)

        zero = tuple(jnp.zeros((S5_SEGS, LANES), F32) for _ in slabs)
        end_re, end_im = lax.fori_loop(0, tseg, scan_body, (zero, zero),
                                       unroll=S5_SCAN_UNROLL if tseg % S5_SCAN_UNROLL == 0 else 1)

        for n, sl in enumerate(slabs):
            at_re = pre_ref[sl, tseg - 1:tseg, :]
            at_im = pim_ref[sl, tseg - 1:tseg, :]
            xr = cre_ref[sl, 0:1, :]
            xi = cim_ref[sl, 0:1, :]
            rows_re, rows_im = [], []
            for s in range(S5_SEGS):
                rows_re.append(xr)
                rows_im.append(xi)
                er = end_re[n][s:s + 1]
                ei = end_im[n][s:s + 1]
                xr, xi = er + at_re * xr - at_im * xi, ei + at_re * xi + at_im * xr
            cre_ref[sl] = jnp.broadcast_to(xr, (S5_SEGS, LANES))
            cim_ref[sl] = jnp.broadcast_to(xi, (S5_SEGS, LANES))
            xinre_ref[sl] = jnp.concatenate(rows_re, axis=0)
            xinim_ref[sl] = jnp.concatenate(rows_im, axis=0)

    def fix_body(sl, carry):
        xin_re = xinre_ref[sl]
        xin_im = xinim_ref[sl]
        for tau in range(tseg):
            rows = pl.ds(tau, S5_SEGS, stride=tseg)
            pr = pre_ref[sl, tau:tau + 1, :]
            pi = pim_ref[sl, tau:tau + 1, :]
            bure_ref[sl, rows, :] = bure_ref[sl, rows, :] + (pr * xin_re - pi * xin_im)
            buim_ref[sl, rows, :] = buim_ref[sl, rows, :] + (pr * xin_im + pi * xin_re)
        return carry

    lax.fori_loop(0, S5_SLABS, fix_body, 0)

    nt_dims = (((1,), (1,)), ((), ()))
    y_odd, y_hat = [], []
    for j in range(S5_JB):
        x_re = jnp.concatenate([bure_ref[j * spb + s] for s in range(spb)], axis=1).astype(BF16)
        x_im = jnp.concatenate([buim_ref[j * spb + s] for s in range(spb)], axis=1).astype(BF16)
        both = (lax.dot_general(x_re, wcre_ref[j], nt_dims, preferred_element_type=F32)
                - lax.dot_general(x_im, wcim_ref[j], nt_dims, preferred_element_type=F32))
        y_odd.append(both[:, :S5_BLK_IN])
        y_hat.append(both[:, S5_BLK_IN:])
    y_dir = [jnp.dot(u_e[:, j * S5_K0_IN:(j + 1) * S5_K0_IN], k0_ref[j], preferred_element_type=F32)
             for j in range(S5_W // S5_K0_IN)]
    y_hat = jnp.concatenate(y_hat, axis=1)
    ridx = lax.broadcasted_iota(jnp.int32, (y_hat.shape[0], 1), 0)
    y_even = jnp.where(ridx == 0, yhat_ref[...], pltpu.roll(y_hat, 1, 0)) + jnp.concatenate(y_dir, axis=1)
    yhat_ref[...] = y_hat[y_hat.shape[0] - 1:, :]
    halves = ((y_even, u_e32), (jnp.concatenate(y_odd, axis=1), u_o32))
    for p, (y, u) in enumerate(halves):
        y = _gelu_tanh(y + d_ref[...] * u)
        gate = jnp.dot(y.astype(BF16), wglu_ref[...], preferred_element_type=F32)
        y = y * (1.0 / (1.0 + jnp.exp(-gate)))
        for s in range(lane_slabs):
            par_ref[s, pl.ds(p, half, stride=2), :] = y[:, s * LANES:(s + 1) * LANES]
    y_all = jnp.concatenate([par_ref[s] for s in range(lane_slabs)], axis=1)
    o_ref[...] = (y_all * z_ref[...].astype(F32)).astype(o_ref.dtype)


def _s5_time_tile(rows_per_batch):
    for tseg in (44, 66, 22, 6, 2):
        t = 2 * S5_SEGS * tseg
        if rows_per_batch % t == 0 and (S5_SEGS * tseg) % 16 == 0:
            return t, tseg
    raise ValueError(f"no S5 time tile for {rows_per_batch}")


def _s5(proj, bsz, prep, d, wglu, t, tseg):
    m = proj.shape[0]
    p_re, p_im, w2, wcre, wcim, k0 = prep
    rows = t // 2
    nt = m // bsz // t
    ucol = (2 * RET_QK + 2 * RET_W) // S5_W
    const = lambda a: pl.BlockSpec(a.shape, lambda b, i: (0,) * a.ndim, pipeline_mode=pl.Buffered(1))
    tile = lambda col: pl.BlockSpec((t, S5_W), lambda b, i: (b * nt + i, col))
    d2 = d.reshape(1, S5_W)
    return pl.pallas_call(
        functools.partial(_s5_kernel, tseg=tseg, slab_group=8),
        grid=(bsz, nt),
        in_specs=[tile(ucol), tile(ucol + 1),
                  const(w2), const(wcre), const(wcim), const(k0),
                  const(p_re), const(p_im), const(d2), const(wglu)],
        out_specs=tile(0),
        out_shape=jax.ShapeDtypeStruct((m, S5_W), BF16),
        scratch_shapes=[pltpu.VMEM((S5_SLABS, rows, LANES), F32), pltpu.VMEM((S5_SLABS, rows, LANES), F32)]
        + [pltpu.VMEM((S5_SLABS, S5_SEGS, LANES), F32)] * 4
        + [pltpu.VMEM((1, S5_W), F32), pltpu.VMEM((S5_W // LANES, t, LANES), F32)],
        compiler_params=_cparams(("arbitrary", "arbitrary")),
        name="s5",
    )(proj, proj, w2, wcre, wcim, k0, p_re, p_im, d2, wglu)


def _out_ab_kernel(lead_ref, xa_ref, xb_ref, xc_ref, oa_ref, ob_ref, w_ref, nw_ref, wg_ref,
                   h1_ref, xn_ref, gl_ref):
    t = pl.program_id(1)
    first = jnp.where(t == 0, lead_ref[...], xa_ref[...])
    h0 = (first, xb_ref[...], xc_ref[...])
    sls = [slice(r * CHUNK, (r + 1) * CHUNK) for r in range(ROW_CHUNKS)]
    accs = [jnp.dot(oa_ref[sl, :], w_ref[:RET_W, :], preferred_element_type=F32)
            + jnp.dot(ob_ref[sl, :], w_ref[RET_W:, :], preferred_element_type=F32) for sl in sls]
    h1s = [h0[r] + accs[r] for r in range(ROW_CHUNKS)]
    xns = [_rms_scale(h1, nw_ref[...]).astype(BF16) for h1 in h1s]
    gls = [jnp.dot(xn, wg_ref[...], preferred_element_type=F32) for xn in xns]
    for r, sl in enumerate(sls):
        h1_ref[sl, :] = h1s[r]
        xn_ref[sl, :] = xns[r]
        gl_ref[sl, :] = gls[r]


def _out_ab(x, lead, oa, ob, w, norm_w, wg, nt):
    bsz, _, d = x.shape
    tile = ROW_CHUNKS * CHUNK
    m = bsz * nt * tile
    row = lambda width: pl.BlockSpec((tile, width), lambda b, t: (b * nt + t, 0))
    const = lambda a: pl.BlockSpec(a.shape, lambda b, t: (0, 0))
    out = jax.ShapeDtypeStruct
    return pl.pallas_call(
        _out_ab_kernel,
        grid=(bsz, nt),
        in_specs=[pl.BlockSpec((CHUNK, d), lambda b, t: (0, 0))] + _chunk_specs(d, 1)
        + [row(RET_W), row(S5_W), const(w), pl.BlockSpec((1, d), lambda b, t: (0, 0)), const(wg)],
        out_specs=[row(d), row(d), row(GATE_PAD)],
        out_shape=(out((m, d), F32), out((m, d), BF16), out((m, GATE_PAD), F32)),
        compiler_params=_cparams(("parallel", "parallel")),
        name="out_proj_ab",
    )(lead, x, x, x, oa, ob, w, norm_w.reshape(1, d), wg)


LOG2E = 1.4426950408889634


def _split2(x):
    x1 = x.astype(BF16)
    return x1, (x - x1.astype(F32)).astype(BF16)


def _neg_abs(x):
    return -jnp.abs(x)


def _block_ref_rows(b, level):
    rows, width = b.shape
    half = 1 << level
    blk = 2 * half
    if blk >= 8:
        b3 = b.reshape(rows // blk, blk, width)
        ref = jnp.broadcast_to(b3[:, half - 1:half, :], b3.shape)
        return ref.reshape(rows, width)
    b3 = b.reshape(rows // 8, 8, width)
    sub = lax.broadcasted_iota(jnp.int32, b3.shape, 1)
    pick = lambda r: jnp.broadcast_to(b3[:, r:r + 1, :], b3.shape)
    if blk == 4:
        ref = jnp.where(sub < 4, pick(1), pick(5))
    else:
        ref = jnp.where(sub < 2, pick(0), jnp.where(sub < 4, pick(2), jnp.where(sub < 6, pick(4), pick(6))))
    return ref.reshape(rows, width)


def _gla_kernel(q_ref, k_ref, v_ref, z_ref, gl_ref, wg_ref, bg_ref, nw_ref, h_ref, wo_ref, fw_ref,
                y_ref, st_ref, mix_ref, wobf_ref, *, nc, nchunks):
    c = pl.program_id(0)
    n = jnp.minimum(c, nchunks - 1) % nc

    @pl.when(c == 0)
    def _():
        mix_ref[...] = jnp.zeros_like(mix_ref)
        wobf_ref[...] = wo_ref[...].astype(BF16)

    @pl.when(n == 0)
    def _():
        st_ref[...] = jnp.zeros_like(st_ref)

    ridx = lax.broadcasted_iota(jnp.int32, (CHUNK, 1), 0)
    ii = lax.broadcasted_iota(jnp.int32, (CHUNK, CHUNK), 0)
    jj = lax.broadcasted_iota(jnp.int32, (CHUNK, CHUNK), 1)
    pair_code = jnp.where(ii > jj, ii ^ jj, 0)
    eye = ii == jj
    tri = (ii >= jj).astype(BF16)

    gl1, gl2 = _split2(gl_ref[:, :GLA_RANK])
    wg1, wg2 = _split2(wg_ref[...])
    x = jnp.dot(jnp.concatenate([gl1, gl2, gl1], axis=1), jnp.concatenate([wg1, wg1, wg2], axis=0),
                preferred_element_type=F32) + bg_ref[...]
    log_a = (jnp.minimum(x, 0.0) - jnp.log(1.0 + jnp.exp(-jnp.abs(x)))) * (LOG2E / GLA_TAU)
    log_a = jnp.where(ridx >= jnp.where(n > 0, 0, PAD), log_a, 0.0)
    g1, g2 = _split2(log_a)
    b_all = jnp.dot(jnp.concatenate([tri, tri], axis=1), jnp.concatenate([g1, g2], axis=0),
                    preferred_element_type=F32)
    odd = (ridx & 1) == 1
    h2 = h_ref[...]

    def finish(heads, qs, ks, bs, scores):
        for i, h in enumerate(heads):
            vc = slice(h * GLA_DV, (h + 1) * GLA_DV)
            q, k, b = qs[i], ks[i], bs[i]
            v = v_ref[:, vc]
            b_last = b[CHUNK - 1:CHUNK, :]
            o = jnp.dot(scores[i].astype(BF16), v, preferred_element_type=F32)
            st = st_ref[h]
            qe = (q * jnp.exp2(b)).astype(BF16)
            o = o + lax.dot_general(qe, st.astype(BF16), (((1,), (1,)), ((), ())), preferred_element_type=F32)
            ke = (k * jnp.exp2(b_last - b)).astype(BF16)
            kv_t = lax.dot_general(v, ke, (((0,), (0,)), ((), ())), preferred_element_type=F32)
            st_ref[h] = st * jnp.exp2(b_last) + kv_t
            mix_ref[:, vc] = (_rms_scale(o, nw_ref[:, vc]) * z_ref[:, vc].astype(F32)).astype(BF16)

    pending = []
    for h0 in range(0, GLA_HEADS, 2):
        heads = (h0, h0 + 1)
        qs, ks, bs, scores = [], [], [], []
        for h in heads:
            kc = slice(h * GLA_DK, (h + 1) * GLA_DK)
            vc = slice(h * GLA_DV, (h + 1) * GLA_DV)
            h2 = h2 + jnp.dot(mix_ref[:, vc], wobf_ref[vc, :], preferred_element_type=F32)
            qs.append(q_ref[:, kc].astype(F32))
            ks.append(k_ref[:, kc].astype(F32))
            bs.append(b_all[:, kc])
            scores.append(jnp.where(eye, jnp.sum(qs[-1] * ks[-1], axis=1, keepdims=True), 0.0))
        for lv in range(GLA_LEVELS):
            half = 1 << lv
            if lv == GLA_LEVELS // 2 and pending:
                finish(*pending.pop())
            zz = []
            for q, k, b in zip(qs, ks, bs):
                if lv == 0:
                    expo = jnp.where(odd, b - pltpu.roll(b, 1, 0), 0.0)
                else:
                    expo = _neg_abs(b - _block_ref_rows(b, lv))
                if half >= 8:
                    sel = jnp.concatenate([(q if r % 2 else k)[r * half:(r + 1) * half]
                                           for r in range(CHUNK // half)], axis=0)
                else:
                    sel = jnp.where(((ridx >> lv) & 1) == 1, q, k)
                zz.append((sel * jnp.exp2(expo)).astype(BF16))
            z2 = jnp.concatenate(zz, axis=0)
            gram = lax.dot_general(z2, z2, (((1,), (1,)), ((), ())), preferred_element_type=F32)
            mask = (pair_code >> lv) == 1
            for i in range(2):
                blk = gram[i * CHUNK:(i + 1) * CHUNK, i * CHUNK:(i + 1) * CHUNK]
                scores[i] = scores[i] + jnp.where(mask, blk, 0.0)
        pending.append((heads, qs, ks, bs, scores))
    finish(*pending.pop())

    y_ref[...] = _rms_scale(h2, fw_ref[...])


def _gla(proj, glow, w_gate, b_gate, norm_w, h1, w_out, final_w, bsz, nc):
    m, d = h1.shape
    nchunks = m // CHUNK
    const = lambda *shape: pl.BlockSpec(shape, lambda c: (0,) * len(shape))
    cur = lambda width, col: pl.BlockSpec((CHUNK, width), lambda c: (jnp.minimum(c, nchunks - 1), col))
    prev = lambda c: jnp.maximum(c - 1, 0)
    return pl.pallas_call(
        functools.partial(_gla_kernel, nc=nc, nchunks=nchunks),
        grid=(nchunks + 1,),
        in_specs=[cur(GLA_QK, 0), cur(GLA_QK, 1), cur(GLA_W, 1), cur(GLA_W, 2), cur(GATE_PAD, 0),
                  const(GLA_RANK, GLA_QK), const(1, GLA_QK), const(1, GLA_W),
                  pl.BlockSpec((CHUNK, d), lambda c: (prev(c), 0)),
                  pl.BlockSpec((None, GLA_W, d), lambda c: (0, 0, 0), pipeline_mode=pl.Buffered(1)),
                  const(1, d)],
        out_specs=pl.BlockSpec((None, CHUNK, d),
                               lambda c: (prev(c) // nc, jnp.maximum(prev(c) % nc - 1, 0), 0)),
        out_shape=jax.ShapeDtypeStruct((bsz, (nc - 1) * CHUNK, d), F32),
        scratch_shapes=[pltpu.VMEM((GLA_HEADS, GLA_DV, GLA_DK), F32), pltpu.VMEM((CHUNK, GLA_W), BF16),
                        pltpu.VMEM((GLA_W, d), BF16)],
        compiler_params=_cparams(("arbitrary",)),
        name="gla_out",
    )(proj, proj, proj, proj, glow, w_gate.astype(F32), b_gate.reshape(1, GLA_QK).astype(F32),
      norm_w.reshape(1, GLA_W).astype(F32), h1, w_out, final_w.reshape(1, d).astype(F32))


def _rope_tables(bsz, rows_per_batch):
    pos = np.maximum(np.arange(rows_per_batch, dtype=np.float64) - PAD, 0.0)
    inv_freq = np.power(ROPE_BASE, -np.arange(0, RET_DK, 2, dtype=np.float64) / RET_DK)
    ang = pos[:, None] * inv_freq[None, :]
    cos, sin = np.cos(ang), np.sin(ang)
    cos2 = np.tile(np.concatenate([cos, cos], axis=1), (bsz, 1)).astype(np.float32)
    sin2 = np.tile(np.concatenate([-sin, sin], axis=1), (bsz, 1)).astype(np.float32)
    return jnp.asarray(cos2), jnp.asarray(sin2)


def kernel(x, meta, norm_ab_w, w_in_ab, ret_norm_w, s5_lam_re, s5_lam_im, s5_log_dt, s5_b_re, s5_b_im,
           s5_c_re, s5_c_im, s5_d, s5_w_glu, w_out_ab, norm_c_w, w_in_c, gla_w_gate, gla_b_gate,
           gla_norm_w, w_out_c, final_norm_w):
    bsz, seq, d = x.shape
    assert seq % CHUNK == 0 and w_in_ab.shape[0] == 1 and w_in_c.shape[0] == 1
    rpb = seq + CHUNK
    nc = rpb // CHUNK
    assert nc % ROW_CHUNKS == 0
    nt = nc // ROW_CHUNKS
    m = bsz * rpb
    tm = _pick_tile(m, (2112, 1408, 768, 384, 256, 128))
    lead = jnp.concatenate([jnp.zeros((PAD, d), x.dtype), meta.astype(x.dtype)], axis=0)

    xn0 = _embed_norm(x, lead, norm_ab_w[0], nt)
    plain, silu = ("plain", 1.0), ("silu", 1.0)
    modes_ab = [("rope", 1.0), ("rope", RET_DK ** -0.5), plain, plain, silu, silu, plain, silu]
    proj0 = _in_proj(xn0, w_in_ab, modes_ab, tm, rope=_rope_tables(bsz, rpb))
    o_a = _retention(proj0, ret_norm_w[0], bsz, nt)
    t, tseg = _s5_time_tile(rpb)
    prep = _s5_prep(s5_lam_re[0], s5_lam_im[0], s5_log_dt[0], s5_b_re[0], s5_b_im[0],
                    s5_c_re[0], s5_c_im[0], tseg)
    o_b = _s5(proj0, bsz, prep, s5_d[0], s5_w_glu[0].astype(BF16), t, tseg)

    w_in_c_t = jnp.swapaxes(w_in_c, 1, 2)
    n_main = 2 * GLA_QK + 2 * GLA_W
    wg = jnp.pad(w_in_c_t[0, n_main:, :], ((0, GATE_PAD - GLA_RANK), (0, 0))).astype(BF16).T
    h1, xn1, glow = _out_ab(x, lead, o_a, o_b, w_out_ab[0].astype(BF16), norm_c_w[0], wg, nt)

    modes_c = [("plain", GLA_DK ** -0.5), plain, plain, plain, silu, silu]
    proj1 = _in_proj(xn1, w_in_c_t, modes_c, tm, w_is_transposed=True)
    return _gla(proj1, glow, gla_w_gate[0], gla_b_gate[0], gla_norm_w[0], h1, w_out_c,
                final_norm_w, bsz, nc)
```

```python
import functools
import math

import numpy as np
import jax
import jax.numpy as jnp
from jax import lax
from jax.experimental import pallas as pl
from jax.experimental.pallas import tpu as pltpu

F32 = jnp.float32
BF16 = jnp.bfloat16

N_META = 16
LANES = 128
SUBLANES = 8
CHUNK = 128
PAD = CHUNK - N_META
EPS = 1e-6
ROW_CHUNKS = 3

RET_HEADS = 8
RET_DK = 128
RET_DV = 256
RET_QK = RET_HEADS * RET_DK
RET_W = RET_HEADS * RET_DV
ROPE_BASE = 10000.0

S5_W = 1024
S5_GH = 16
S5_G = 64
S5_P = 64
S5_F = S5_G * S5_P
S5_SLABS = S5_F // LANES
MXU_TILE = 256
S5_JB = 8
S5_BLK_IN = S5_W // S5_JB
S5_K0_IN = MXU_TILE
S5_SEGS = SUBLANES
S5_SCAN_UNROLL = 4

GLA_HEADS = 4
GLA_DK = 256
GLA_DV = 512
GLA_QK = GLA_HEADS * GLA_DK
GLA_W = GLA_HEADS * GLA_DV
GLA_RANK = 16
GLA_TAU = 16.0
GLA_LEVELS = 7
GATE_PAD = LANES

PROJ_TN = 1024
PROJ_ROW_SPLIT = 4
VMEM_LIMIT = 56 * 1024 * 1024


def _cparams(sem):
    return pltpu.CompilerParams(dimension_semantics=sem, vmem_limit_bytes=VMEM_LIMIT)


def _silu(x):
    return x * (1.0 / (1.0 + jnp.exp(-x)))


def _pick_tile(n, candidates):
    for c in candidates:
        if n % c == 0:
            return c
    raise ValueError(f"no tile for {n}")


def _rms_scale(x, w):
    return x * lax.rsqrt(jnp.mean(x * x, axis=-1, keepdims=True) + EPS) * w


def _chunk_specs(d, n_before):
    return [pl.BlockSpec((None, CHUNK, d),
                         lambda b, t, k=k: (b, jnp.maximum(ROW_CHUNKS * t + k - n_before, 0), 0))
            for k in range(ROW_CHUNKS)]


def _embed_norm_kernel(lead_ref, xa_ref, xb_ref, xc_ref, w_ref, o_ref):
    t = pl.program_id(1)
    first = jnp.where(t == 0, lead_ref[...], xa_ref[...])
    for r, rows in enumerate((first, xb_ref[...], xc_ref[...])):
        o_ref[r * CHUNK:(r + 1) * CHUNK, :] = _rms_scale(rows, w_ref[...]).astype(o_ref.dtype)


def _embed_norm(x, lead, w, nt):
    bsz, _, d = x.shape
    tile = ROW_CHUNKS * CHUNK
    return pl.pallas_call(
        _embed_norm_kernel,
        grid=(bsz, nt),
        in_specs=[pl.BlockSpec((CHUNK, d), lambda b, t: (0, 0))] + _chunk_specs(d, 1)
        + [pl.BlockSpec((1, d), lambda b, t: (0, 0))],
        out_specs=pl.BlockSpec((tile, d), lambda b, t: (b * nt + t, 0)),
        out_shape=jax.ShapeDtypeStruct((bsz * nt * tile, d), BF16),
        compiler_params=_cparams(("parallel", "parallel")),
        name="embed_norm",
    )(lead, x, x, x, w.reshape(1, d))


def _rope_rows(a, cos, sin, scale):
    outs = []
    for h in range(a.shape[1] // RET_DK):
        blk = a[:, h * RET_DK:(h + 1) * RET_DK]
        r = blk * cos + pltpu.roll(blk, RET_DK // 2, 1) * sin
        outs.append(r * scale if scale != 1.0 else r)
    return jnp.concatenate(outs, axis=1)


def _in_proj_kernel(*refs, modes, has_rope, w_is_transposed):
    if has_rope:
        x_ref, w_ref, cos_ref, sin_ref, o_ref, wbf_ref = refs
    else:
        x_ref, w_ref, o_ref, wbf_ref = refs
    j = pl.program_id(0)

    @pl.when(pl.program_id(1) == 0)
    def _():
        wbf_ref[...] = w_ref[...].astype(BF16)

    contract = (((1,), (1 if w_is_transposed else 0,)), ((), ()))
    for mode in sorted(set(modes)):
        cond = functools.reduce(jnp.logical_or, [j == jj for jj, mm in enumerate(modes) if mm == mode])

        @pl.when(cond)
        def _(mode=mode):
            kind, scale = mode
            rows = x_ref.shape[0] // PROJ_ROW_SPLIT
            sls = [slice(r * rows, (r + 1) * rows) for r in range(PROJ_ROW_SPLIT)]
            accs = [lax.dot_general(x_ref[sl, :], wbf_ref[...], contract, preferred_element_type=F32)
                    for sl in sls]
            for sl, acc in zip(sls, accs):
                if kind == "rope":
                    acc = _rope_rows(acc, cos_ref[sl, :], sin_ref[sl, :], scale)
                elif kind == "silu":
                    acc = _silu(acc)
                elif scale != 1.0:
                    acc = acc * scale
                o_ref[sl, :] = acc.astype(o_ref.dtype)


def _in_proj(x, w, modes, tm, rope=None, w_is_transposed=False):
    m, k = x.shape
    nt = len(modes)
    assert m % tm == 0 and tm % (PROJ_ROW_SPLIT * 16) == 0
    if w_is_transposed:
        w_spec = pl.BlockSpec((None, PROJ_TN, k), lambda j, i: (0, j, 0))
        w_scratch = pltpu.VMEM((PROJ_TN, k), BF16)
    else:
        w_spec = pl.BlockSpec((None, k, PROJ_TN), lambda j, i: (0, 0, j))
        w_scratch = pltpu.VMEM((k, PROJ_TN), BF16)
    in_specs = [pl.BlockSpec((tm, k), lambda j, i: (i, 0)), w_spec]
    args = [x, w]
    if rope is not None:
        in_specs += [pl.BlockSpec((tm, RET_DK), lambda j, i: (i, 0))] * 2
        args += list(rope)
    return pl.pallas_call(
        functools.partial(_in_proj_kernel, modes=tuple(modes), has_rope=rope is not None,
                          w_is_transposed=w_is_transposed),
        grid=(nt, m // tm),
        in_specs=in_specs,
        out_specs=pl.BlockSpec((tm, PROJ_TN), lambda j, i: (i, j)),
        out_shape=jax.ShapeDtypeStruct((m, nt * PROJ_TN), BF16),
        scratch_shapes=[w_scratch],
        compiler_params=_cparams(("arbitrary", "arbitrary")),
        name="in_proj",
    )(*args)


def _ret_log_decay(h):
    return math.log1p(-(2.0 ** (-5.0 - h)))


def _retention_kernel(q_ref, k_ref, v_ref, z_ref, nw_ref, o_ref, s_ref, dec_ref, qs_ref, ks_ref):
    first_step = (pl.program_id(0) == 0) & (pl.program_id(1) == 0)

    @pl.when(first_step)
    def _():
        ii = lax.broadcasted_iota(jnp.int32, (CHUNK, CHUNK), 0)
        jj = lax.broadcasted_iota(jnp.int32, (CHUNK, CHUNK), 1)
        diff = (ii - jj).astype(F32)
        row = ii.astype(F32)
        for h in range(RET_HEADS):
            lg = _ret_log_decay(h)
            dec_ref[h] = jnp.where(ii >= jj, jnp.exp(lg * jnp.maximum(diff, 0.0)), 0.0)
            qs_ref[h] = jnp.exp(lg * (row + 1.0))
            ks_ref[h] = jnp.exp(lg * (CHUNK - 1.0 - row))

    @pl.when(pl.program_id(1) == 0)
    def _():
        s_ref[...] = jnp.zeros_like(s_ref)

    heads = range(RET_HEADS)
    vcs = [slice(h * RET_DV, (h + 1) * RET_DV) for h in heads]
    nt_dims = (((1,), (1,)), ((), ()))
    tn_dims = (((0,), (0,)), ((), ()))
    states = [s_ref[h] for h in heads]
    for r in range(q_ref.shape[0] // CHUNK):
        rows = slice(r * CHUNK, (r + 1) * CHUNK)
        qs = [q_ref[rows, h * RET_DK:(h + 1) * RET_DK] for h in heads]
        ks = [k_ref[rows, h * RET_DK:(h + 1) * RET_DK] for h in heads]
        vs = [v_ref[rows, vc] for vc in vcs]
        scores = [lax.dot_general(qs[h], ks[h], nt_dims, preferred_element_type=F32) for h in heads]
        kvs = [lax.dot_general((ks[h].astype(F32) * ks_ref[h]).astype(BF16), vs[h], tn_dims,
                               preferred_element_type=F32) for h in heads]
        lhs = [jnp.concatenate([(scores[h] * dec_ref[h]).astype(BF16),
                                (qs[h].astype(F32) * qs_ref[h]).astype(BF16)], axis=1) for h in heads]
        outs = [jnp.dot(lhs[h], jnp.concatenate([vs[h], states[h].astype(BF16)], axis=0),
                        preferred_element_type=F32) for h in heads]
        states = [states[h] * math.exp(_ret_log_decay(h) * CHUNK) + kvs[h] for h in heads]
        for h in heads:
            o_ref[rows, vcs[h]] = (_rms_scale(outs[h], nw_ref[:, vcs[h]])
                                   * z_ref[rows, vcs[h]].astype(F32)).astype(o_ref.dtype)
    for h in heads:
        s_ref[h] = states[h]


def _retention(proj, norm_w, bsz, nt):
    m = proj.shape[0]
    tile = ROW_CHUNKS * CHUNK
    tab = pltpu.VMEM((RET_HEADS, CHUNK, CHUNK), F32)
    row = lambda width, col: pl.BlockSpec((tile, width), lambda b, t: (b * nt + t, col))
    return pl.pallas_call(
        _retention_kernel,
        grid=(bsz, nt),
        in_specs=[row(RET_QK, 0), row(RET_QK, 1), row(RET_W, 1), row(RET_W, 2),
                  pl.BlockSpec((1, RET_W), lambda b, t: (0, 0))],
        out_specs=row(RET_W, 0),
        out_shape=jax.ShapeDtypeStruct((m, RET_W), BF16),
        scratch_shapes=[pltpu.VMEM((RET_HEADS, RET_DK, RET_DV), F32), tab, tab, tab],
        compiler_params=_cparams(("arbitrary", "arbitrary")),
        name="retention",
    )(proj, proj, proj, proj, norm_w.reshape(1, RET_W))


def _s5_prep(lre_ref, lim_ref, ldt_ref, bre_ref, bim_ref, cre_ref, cim_ref,
             pre_ref, pim_ref, w2_ref, wcre_ref, wcim_ref, k0_ref):
    lre = lre_ref[...]
    lim = lim_ref[...]
    dt = jnp.exp(ldt_ref[...])
    tseg = pre_ref.shape[1]
    n = 2.0 * (lax.broadcasted_iota(jnp.int32, (tseg, 1), 0).astype(F32) + 1.0)
    mag = jnp.exp(n * (lre * dt))
    ang = n * (lim * dt)
    p_re = mag * jnp.cos(ang)
    p_im = mag * jnp.sin(ang)
    for sl in range(S5_SLABS):
        pre_ref[sl] = p_re[:, sl * LANES:(sl + 1) * LANES]
        pim_ref[sl] = p_im[:, sl * LANES:(sl + 1) * LANES]
    mag1 = jnp.exp(lre * dt)
    a_re = mag1 * jnp.cos(lim * dt)
    a_im = mag1 * jnp.sin(lim * dt)
    den = lre * lre + lim * lim
    nr = a_re - 1.0
    f_re = (nr * lre + a_im * lim) / den
    f_im = (a_im * lre - nr * lim) / den
    b_re = bre_ref[...]
    b_im = bim_ref[...]
    bb_re = f_re * b_re - f_im * b_im
    bb_im = f_re * b_im + f_im * b_re
    bba_re = a_re * bb_re - a_im * bb_im
    bba_im = a_re * bb_im + a_im * bb_re
    c_re = cre_ref[...]
    c_im = cim_ref[...]
    ca_re = c_re * a_re - c_im * a_im
    ca_im = c_re * a_im + c_im * a_re
    blk = S5_F // S5_JB
    rows = lax.broadcasted_iota(jnp.int32, (S5_BLK_IN, blk), 0)
    cols = lax.broadcasted_iota(jnp.int32, (S5_BLK_IN, blk), 1)
    same_group = (rows // S5_GH) == (cols // S5_P)

    def block_diag(a, j):
        piece = a[:, j * blk:(j + 1) * blk]
        return jnp.where(same_group, jnp.concatenate([piece] * (S5_G // S5_JB), axis=0), 0.0)

    k0_ref[...] = jnp.zeros_like(k0_ref)
    per_k0 = S5_K0_IN // S5_BLK_IN
    for j in range(S5_JB):
        bd_bre, bd_bim = block_diag(bb_re, j), block_diag(bb_im, j)
        bd_cre, bd_cim = block_diag(c_re, j), block_diag(c_im, j)
        w2_ref[j, :S5_BLK_IN, :blk] = block_diag(bba_re, j).astype(BF16)
        w2_ref[j, :S5_BLK_IN, blk:] = block_diag(bba_im, j).astype(BF16)
        w2_ref[j, S5_BLK_IN:, :blk] = bd_bre.astype(BF16)
        w2_ref[j, S5_BLK_IN:, blk:] = bd_bim.astype(BF16)
        wcre_ref[j, :S5_BLK_IN, :] = bd_cre.astype(BF16)
        wcim_ref[j, :S5_BLK_IN, :] = bd_cim.astype(BF16)
        wcre_ref[j, S5_BLK_IN:, :] = block_diag(ca_re, j).astype(BF16)
        wcim_ref[j, S5_BLK_IN:, :] = block_diag(ca_im, j).astype(BF16)
        nt_dims = (((1,), (1,)), ((), ()))
        k0 = (lax.dot_general(bd_bre, bd_cre, nt_dims, preferred_element_type=F32)
              - lax.dot_general(bd_bim, bd_cim, nt_dims, preferred_element_type=F32))
        r0 = (j % per_k0) * S5_BLK_IN
        k0_ref[j // per_k0, r0:r0 + S5_BLK_IN, r0:r0 + S5_BLK_IN] = k0.astype(BF16)


def _s5_params(lam_re, lam_im, log_dt, b_re, b_im, c_re, c_im):
    flat = lambda a: a.reshape(1, S5_F)
    ldt = jnp.broadcast_to(log_dt[:, None], (S5_G, S5_P))
    bt = lambda a: a.reshape(S5_F, S5_GH).T
    ct = lambda a: a.transpose(1, 0, 2).reshape(S5_GH, S5_F)
    return (flat(lam_re), flat(lam_im), flat(ldt), bt(b_re), bt(b_im), ct(c_re.astype(F32)), ct(c_im.astype(F32)))


def _gelu_tanh(x):
    return 0.5 * x * (1.0 + jnp.tanh(math.sqrt(2.0 / math.pi) * (x + 0.044715 * (x * x * x))))


def _s5_kernel(u_ref, z_ref, lre_ref, lim_ref, ldt_ref, bpre_ref, bpim_ref, cpre_ref, cpim_ref, d_ref, wglu_ref,
               o_ref, w2_ref, wcre_ref, wcim_ref, k0_ref, pre_ref, pim_ref,
               bure_ref, buim_ref, cre_ref, cim_ref, xinre_ref, xinim_ref, yhat_ref, par_ref,
               *, tseg, slab_group):
    ci = pl.program_id(1)

    @pl.when((pl.program_id(0) == 0) & (ci == 0))
    def _():
        _s5_prep(lre_ref, lim_ref, ldt_ref, bpre_ref, bpim_ref, cpre_ref, cpim_ref,
                 pre_ref, pim_ref, w2_ref, wcre_ref, wcim_ref, k0_ref)

    @pl.when(ci == 0)
    def _():
        cre_ref[...] = jnp.zeros_like(cre_ref)
        cim_ref[...] = jnp.zeros_like(cim_ref)
        yhat_ref[...] = jnp.zeros_like(yhat_ref)

    half = u_ref.shape[0] // 2
    lane_slabs = S5_W // LANES
    u_all = u_ref[...].astype(F32)
    for s in range(lane_slabs):
        par_ref[s] = u_all[:, s * LANES:(s + 1) * LANES]
    parity = lambda p: jnp.concatenate([par_ref[s, pl.ds(p, half, stride=2), :] for s in range(lane_slabs)], axis=1)
    u_e32, u_o32 = parity(0), parity(1)
    u_e, u_o = u_e32.astype(BF16), u_o32.astype(BF16)
    spb = S5_SLABS // S5_JB
    for j in range(S5_JB):
        blk = slice(j * S5_BLK_IN, (j + 1) * S5_BLK_IN)
        res = jnp.dot(jnp.concatenate([u_e[:, blk], u_o[:, blk]], axis=1), w2_ref[j], preferred_element_type=F32)
        for s in range(spb):
            bure_ref[j * spb + s] = res[:, s * LANES:(s + 1) * LANES]
            buim_ref[j * spb + s] = res[:, (spb + s) * LANES:(spb + s + 1) * LANES]

    for g0 in range(0, S5_SLABS, slab_group):
        slabs = list(range(g0, g0 + slab_group))
        a_re = [jnp.broadcast_to(pre_ref[sl, 0:1, :], (S5_SEGS, LANES)) for sl in slabs]
        a_im = [jnp.broadcast_to(pim_ref[sl, 0:1, :], (S5_SEGS, LANES)) for sl in slabs]

        def scan_body(tau, carry, slabs=slabs, a_re=a_re, a_im=a_im):
            xr, xi = carry
            nr, ni = [], []
            for n, sl in enumerate(slabs):
                rows = pl.ds(tau, S5_SEGS, stride=tseg)
                br = bure_ref[sl, rows, :]
                bi = buim_ref[sl, rows, :]
                r = a_re[n] * xr[n] - a_im[n] * xi[n] + br
                i = a_re[n] * xi[n] + a_im[n] * xr[n] + bi
                bure_ref[sl, rows, :] = r
                buim_ref[sl, rows, :] = i
                nr.append(r)
                ni.append(i)
            return tuple(nr), tuple(ni)

        zero = tuple(jnp.zeros((S5_SEGS, LANES), F32) for _ in slabs)
        end_re, end_im = lax.fori_loop(0, tseg, scan_body, (zero, zero),
                                       unroll=S5_SCAN_UNROLL if tseg % S5_SCAN_UNROLL == 0 else 1)

        for n, sl in enumerate(slabs):
            at_re = pre_ref[sl, tseg - 1:tseg, :]
            at_im = pim_ref[sl, tseg - 1:tseg, :]
            xr = cre_ref[sl, 0:1, :]
            xi = cim_ref[sl, 0:1, :]
            rows_re, rows_im = [], []
            for s in range(S5_SEGS):
                rows_re.append(xr)
                rows_im.append(xi)
                er = end_re[n][s:s + 1]
                ei = end_im[n][s:s + 1]
                xr, xi = er + at_re * xr - at_im * xi, ei + at_re * xi + at_im * xr
            cre_ref[sl] = jnp.broadcast_to(xr, (S5_SEGS, LANES))
            cim_ref[sl] = jnp.broadcast_to(xi, (S5_SEGS, LANES))
            xinre_ref[sl] = jnp.concatenate(rows_re, axis=0)
            xinim_ref[sl] = jnp.concatenate(rows_im, axis=0)

    def fix_body(sl, carry):
        xin_re = xinre_ref[sl]
        xin_im = xinim_ref[sl]
        for tau in range(tseg):
            rows = pl.ds(tau, S5_SEGS, stride=tseg)
            pr = pre_ref[sl, tau:tau + 1, :]
            pi = pim_ref[sl, tau:tau + 1, :]
            bure_ref[sl, rows, :] = bure_ref[sl, rows, :] + (pr * xin_re - pi * xin_im)
            buim_ref[sl, rows, :] = buim_ref[sl, rows, :] + (pr * xin_im + pi * xin_re)
        return carry

    lax.fori_loop(0, S5_SLABS, fix_body, 0)

    nt_dims = (((1,), (1,)), ((), ()))
    y_odd, y_hat = [], []
    for j in range(S5_JB):
        x_re = jnp.concatenate([bure_ref[j * spb + s] for s in range(spb)], axis=1).astype(BF16)
        x_im = jnp.concatenate([buim_ref[j * spb + s] for s in range(spb)], axis=1).astype(BF16)
        both = (lax.dot_general(x_re, wcre_ref[j], nt_dims, preferred_element_type=F32)
                - lax.dot_general(x_im, wcim_ref[j], nt_dims, preferred_element_type=F32))
        y_odd.append(both[:, :S5_BLK_IN])
        y_hat.append(both[:, S5_BLK_IN:])
    y_dir = [jnp.dot(u_e[:, j * S5_K0_IN:(j + 1) * S5_K0_IN], k0_ref[j], preferred_element_type=F32)
             for j in range(S5_W // S5_K0_IN)]
    y_hat = jnp.concatenate(y_hat, axis=1)
    ridx = lax.broadcasted_iota(jnp.int32, (y_hat.shape[0], 1), 0)
    y_even = jnp.where(ridx == 0, yhat_ref[...], pltpu.roll(y_hat, 1, 0)) + jnp.concatenate(y_dir, axis=1)
    yhat_ref[...] = y_hat[y_hat.shape[0] - 1:, :]
    halves = ((y_even, u_e32), (jnp.concatenate(y_odd, axis=1), u_o32))
    for p, (y, u) in enumerate(halves):
        y = _gelu_tanh(y + d_ref[...] * u)
        gate = jnp.dot(y.astype(BF16), wglu_ref[...], preferred_element_type=F32)
        y = y * (1.0 / (1.0 + jnp.exp(-gate)))
        for s in range(lane_slabs):
            par_ref[s, pl.ds(p, half, stride=2), :] = y[:, s * LANES:(s + 1) * LANES]
    y_all = jnp.concatenate([par_ref[s] for s in range(lane_slabs)], axis=1)
    o_ref[...] = (y_all * z_ref[...].astype(F32)).astype(o_ref.dtype)


def _s5_time_tile(rows_per_batch):
    for tseg in (44, 66, 22, 6, 2):
        t = 2 * S5_SEGS * tseg
        if rows_per_batch % t == 0 and (S5_SEGS * tseg) % 16 == 0:
            return t, tseg
    raise ValueError(f"no S5 time tile for {rows_per_batch}")


def _s5(proj, bsz, params, d, wglu, t, tseg):
    m = proj.shape[0]
    rows = t // 2
    blk = S5_F // S5_JB
    nt = m // bsz // t
    ucol = (2 * RET_QK + 2 * RET_W) // S5_W
    const = lambda a: pl.BlockSpec(a.shape, lambda b, i: (0,) * a.ndim, pipeline_mode=pl.Buffered(1))
    tile = lambda col: pl.BlockSpec((t, S5_W), lambda b, i: (b * nt + i, col))
    d2 = d.reshape(1, S5_W)
    return pl.pallas_call(
        functools.partial(_s5_kernel, tseg=tseg, slab_group=8),
        grid=(bsz, nt),
        in_specs=[tile(ucol), tile(ucol + 1)] + [const(a) for a in params] + [const(d2), const(wglu)],
        out_specs=tile(0),
        out_shape=jax.ShapeDtypeStruct((m, S5_W), BF16),
        scratch_shapes=[pltpu.VMEM((S5_JB, 2 * S5_BLK_IN, 2 * blk), BF16),
                        pltpu.VMEM((S5_JB, 2 * S5_BLK_IN, blk), BF16), pltpu.VMEM((S5_JB, 2 * S5_BLK_IN, blk), BF16),
                        pltpu.VMEM((S5_W // S5_K0_IN, S5_K0_IN, S5_K0_IN), BF16),
                        pltpu.VMEM((S5_SLABS, tseg, LANES), F32), pltpu.VMEM((S5_SLABS, tseg, LANES), F32)]
        + [pltpu.VMEM((S5_SLABS, rows, LANES), F32), pltpu.VMEM((S5_SLABS, rows, LANES), F32)]
        + [pltpu.VMEM((S5_SLABS, S5_SEGS, LANES), F32)] * 4
        + [pltpu.VMEM((1, S5_W), F32), pltpu.VMEM((S5_W // LANES, t, LANES), F32)],
        compiler_params=_cparams(("arbitrary", "arbitrary")),
        name="s5",
    )(proj, proj, *params, d2, wglu)


def _out_ab_kernel(lead_ref, xa_ref, xb_ref, xc_ref, oa_ref, ob_ref, w_ref, nw_ref, wg_ref,
                   h1_ref, xn_ref, gl_ref):
    t = pl.program_id(1)
    first = jnp.where(t == 0, lead_ref[...], xa_ref[...])
    h0 = (first, xb_ref[...], xc_ref[...])
    sls = [slice(r * CHUNK, (r + 1) * CHUNK) for r in range(ROW_CHUNKS)]
    accs = [jnp.dot(oa_ref[sl, :], w_ref[:RET_W, :], preferred_element_type=F32)
            + jnp.dot(ob_ref[sl, :], w_ref[RET_W:, :], preferred_element_type=F32) for sl in sls]
    h1s = [h0[r] + accs[r] for r in range(ROW_CHUNKS)]
    xns = [_rms_scale(h1, nw_ref[...]).astype(BF16) for h1 in h1s]
    gls = [jnp.dot(xn, wg_ref[...], preferred_element_type=F32) for xn in xns]
    for r, sl in enumerate(sls):
        h1_ref[sl, :] = h1s[r]
        xn_ref[sl, :] = xns[r]
        gl_ref[sl, :] = gls[r]


def _out_ab(x, lead, oa, ob, w, norm_w, wg, nt):
    bsz, _, d = x.shape
    tile = ROW_CHUNKS * CHUNK
    m = bsz * nt * tile
    row = lambda width: pl.BlockSpec((tile, width), lambda b, t: (b * nt + t, 0))
    const = lambda a: pl.BlockSpec(a.shape, lambda b, t: (0, 0))
    out = jax.ShapeDtypeStruct
    return pl.pallas_call(
        _out_ab_kernel,
        grid=(bsz, nt),
        in_specs=[pl.BlockSpec((CHUNK, d), lambda b, t: (0, 0))] + _chunk_specs(d, 1)
        + [row(RET_W), row(S5_W), const(w), pl.BlockSpec((1, d), lambda b, t: (0, 0)), const(wg)],
        out_specs=[row(d), row(d), row(GATE_PAD)],
        out_shape=(out((m, d), F32), out((m, d), BF16), out((m, GATE_PAD), F32)),
        compiler_params=_cparams(("parallel", "parallel")),
        name="out_proj_ab",
    )(lead, x, x, x, oa, ob, w, norm_w.reshape(1, d), wg)


LOG2E = 1.4426950408889634


def _split2(x):
    x1 = x.astype(BF16)
    return x1, (x - x1.astype(F32)).astype(BF16)


def _neg_abs(x):
    return -jnp.abs(x)


def _block_ref_rows(b, level):
    rows, width = b.shape
    half = 1 << level
    blk = 2 * half
    if blk >= 8:
        b3 = b.reshape(rows // blk, blk, width)
        ref = jnp.broadcast_to(b3[:, half - 1:half, :], b3.shape)
        return ref.reshape(rows, width)
    b3 = b.reshape(rows // 8, 8, width)
    sub = lax.broadcasted_iota(jnp.int32, b3.shape, 1)
    pick = lambda r: jnp.broadcast_to(b3[:, r:r + 1, :], b3.shape)
    if blk == 4:
        ref = jnp.where(sub < 4, pick(1), pick(5))
    else:
        ref = jnp.where(sub < 2, pick(0), jnp.where(sub < 4, pick(2), jnp.where(sub < 6, pick(4), pick(6))))
    return ref.reshape(rows, width)


def _gla_kernel(q_ref, k_ref, v_ref, z_ref, gl_ref, wg_ref, bg_ref, nw_ref, h_ref, wo_ref, fw_ref,
                y_ref, st_ref, mix_ref, wobf_ref, *, nc, nchunks):
    c = pl.program_id(0)
    n = jnp.minimum(c, nchunks - 1) % nc

    @pl.when(c == 0)
    def _():
        mix_ref[...] = jnp.zeros_like(mix_ref)
        wobf_ref[...] = wo_ref[...].astype(BF16)

    @pl.when(n == 0)
    def _():
        st_ref[...] = jnp.zeros_like(st_ref)

    ridx = lax.broadcasted_iota(jnp.int32, (CHUNK, 1), 0)
    ii = lax.broadcasted_iota(jnp.int32, (CHUNK, CHUNK), 0)
    jj = lax.broadcasted_iota(jnp.int32, (CHUNK, CHUNK), 1)
    pair_code = jnp.where(ii > jj, ii ^ jj, 0)
    eye = ii == jj
    tri = (ii >= jj).astype(BF16)

    gl1, gl2 = _split2(gl_ref[:, :GLA_RANK])
    wg1, wg2 = _split2(wg_ref[...])
    x = jnp.dot(jnp.concatenate([gl1, gl2, gl1], axis=1), jnp.concatenate([wg1, wg1, wg2], axis=0),
                preferred_element_type=F32) + bg_ref[...]
    log_a = (jnp.minimum(x, 0.0) - jnp.log(1.0 + jnp.exp(-jnp.abs(x)))) * (LOG2E / GLA_TAU)
    log_a = jnp.where(ridx >= jnp.where(n > 0, 0, PAD), log_a, 0.0)
    g1, g2 = _split2(log_a)
    b_all = jnp.dot(jnp.concatenate([tri, tri], axis=1), jnp.concatenate([g1, g2], axis=0),
                    preferred_element_type=F32)
    odd = (ridx & 1) == 1
    h2 = h_ref[...]

    def finish(heads, qs, ks, bs, scores):
        for i, h in enumerate(heads):
            vc = slice(h * GLA_DV, (h + 1) * GLA_DV)
            q, k, b = qs[i], ks[i], bs[i]
            v = v_ref[:, vc]
            b_last = b[CHUNK - 1:CHUNK, :]
            o = jnp.dot(scores[i].astype(BF16), v, preferred_element_type=F32)
            st = st_ref[h]
            qe = (q * jnp.exp2(b)).astype(BF16)
            o = o + lax.dot_general(qe, st.astype(BF16), (((1,), (1,)), ((), ())), preferred_element_type=F32)
            ke = (k * jnp.exp2(b_last - b)).astype(BF16)
            kv_t = lax.dot_general(v, ke, (((0,), (0,)), ((), ())), preferred_element_type=F32)
            st_ref[h] = st * jnp.exp2(b_last) + kv_t
            mix_ref[:, vc] = (_rms_scale(o, nw_ref[:, vc]) * z_ref[:, vc].astype(F32)).astype(BF16)

    pending = []
    for h0 in range(0, GLA_HEADS, 2):
        heads = (h0, h0 + 1)
        qs, ks, bs, scores = [], [], [], []
        for h in heads:
            kc = slice(h * GLA_DK, (h + 1) * GLA_DK)
            vc = slice(h * GLA_DV, (h + 1) * GLA_DV)
            h2 = h2 + jnp.dot(mix_ref[:, vc], wobf_ref[vc, :], preferred_element_type=F32)
            qs.append(q_ref[:, kc].astype(F32))
            ks.append(k_ref[:, kc].astype(F32))
            bs.append(b_all[:, kc])
            scores.append(jnp.where(eye, jnp.sum(qs[-1] * ks[-1], axis=1, keepdims=True), 0.0))
        for lv in range(GLA_LEVELS):
            half = 1 << lv
            if lv == GLA_LEVELS // 2 and pending:
                finish(*pending.pop())
            zz = []
            for q, k, b in zip(qs, ks, bs):
                if lv == 0:
                    expo = jnp.where(odd, b - pltpu.roll(b, 1, 0), 0.0)
                else:
                    expo = _neg_abs(b - _block_ref_rows(b, lv))
                if half >= 8:
                    sel = jnp.concatenate([(q if r % 2 else k)[r * half:(r + 1) * half]
                                           for r in range(CHUNK // half)], axis=0)
                else:
                    sel = jnp.where(((ridx >> lv) & 1) == 1, q, k)
                zz.append((sel * jnp.exp2(expo)).astype(BF16))
            z2 = jnp.concatenate(zz, axis=0)
            gram = lax.dot_general(z2, z2, (((1,), (1,)), ((), ())), preferred_element_type=F32)
            mask = (pair_code >> lv) == 1
            for i in range(2):
                blk = gram[i * CHUNK:(i + 1) * CHUNK, i * CHUNK:(i + 1) * CHUNK]
                scores[i] = scores[i] + jnp.where(mask, blk, 0.0)
        pending.append((heads, qs, ks, bs, scores))
    finish(*pending.pop())

    y_ref[...] = _rms_scale(h2, fw_ref[...])


def _gla(proj, glow, w_gate, b_gate, norm_w, h1, w_out, final_w, bsz, nc):
    m, d = h1.shape
    nchunks = m // CHUNK
    const = lambda *shape: pl.BlockSpec(shape, lambda c: (0,) * len(shape))
    cur = lambda width, col: pl.BlockSpec((CHUNK, width), lambda c: (jnp.minimum(c, nchunks - 1), col))
    prev = lambda c: jnp.maximum(c - 1, 0)
    return pl.pallas_call(
        functools.partial(_gla_kernel, nc=nc, nchunks=nchunks),
        grid=(nchunks + 1,),
        in_specs=[cur(GLA_QK, 0), cur(GLA_QK, 1), cur(GLA_W, 1), cur(GLA_W, 2), cur(GATE_PAD, 0),
                  const(GLA_RANK, GLA_QK), const(1, GLA_QK), const(1, GLA_W),
                  pl.BlockSpec((CHUNK, d), lambda c: (prev(c), 0)),
                  pl.BlockSpec((None, GLA_W, d), lambda c: (0, 0, 0), pipeline_mode=pl.Buffered(1)),
                  const(1, d)],
        out_specs=pl.BlockSpec((None, CHUNK, d),
                               lambda c: (prev(c) // nc, jnp.maximum(prev(c) % nc - 1, 0), 0)),
        out_shape=jax.ShapeDtypeStruct((bsz, (nc - 1) * CHUNK, d), F32),
        scratch_shapes=[pltpu.VMEM((GLA_HEADS, GLA_DV, GLA_DK), F32), pltpu.VMEM((CHUNK, GLA_W), BF16),
                        pltpu.VMEM((GLA_W, d), BF16)],
        compiler_params=_cparams(("arbitrary",)),
        name="gla_out",
    )(proj, proj, proj, proj, glow, w_gate.astype(F32), b_gate.reshape(1, GLA_QK).astype(F32),
      norm_w.reshape(1, GLA_W).astype(F32), h1, w_out, final_w.reshape(1, d).astype(F32))


def _rope_tables(bsz, rows_per_batch):
    pos = np.maximum(np.arange(rows_per_batch, dtype=np.float64) - PAD, 0.0)
    inv_freq = np.power(ROPE_BASE, -np.arange(0, RET_DK, 2, dtype=np.float64) / RET_DK)
    ang = pos[:, None] * inv_freq[None, :]
    cos, sin = np.cos(ang), np.sin(ang)
    cos2 = np.tile(np.concatenate([cos, cos], axis=1), (bsz, 1)).astype(np.float32)
    sin2 = np.tile(np.concatenate([-sin, sin], axis=1), (bsz, 1)).astype(np.float32)
    return jnp.asarray(cos2), jnp.asarray(sin2)


def kernel(x, meta, norm_ab_w, w_in_ab, ret_norm_w, s5_lam_re, s5_lam_im, s5_log_dt, s5_b_re, s5_b_im,
           s5_c_re, s5_c_im, s5_d, s5_w_glu, w_out_ab, norm_c_w, w_in_c, gla_w_gate, gla_b_gate,
           gla_norm_w, w_out_c, final_norm_w):
    bsz, seq, d = x.shape
    assert seq % CHUNK == 0 and w_in_ab.shape[0] == 1 and w_in_c.shape[0] == 1
    rpb = seq + CHUNK
    nc = rpb // CHUNK
    assert nc % ROW_CHUNKS == 0
    nt = nc // ROW_CHUNKS
    m = bsz * rpb
    tm = _pick_tile(m, (2112, 1408, 768, 384, 256, 128))
    lead = jnp.concatenate([jnp.zeros((PAD, d), x.dtype), meta.astype(x.dtype)], axis=0)

    xn0 = _embed_norm(x, lead, norm_ab_w[0], nt)
    plain, silu = ("plain", 1.0), ("silu", 1.0)
    modes_ab = [("rope", 1.0), ("rope", RET_DK ** -0.5), plain, plain, silu, silu, plain, silu]
    proj0 = _in_proj(xn0, w_in_ab, modes_ab, tm, rope=_rope_tables(bsz, rpb))
    o_a = _retention(proj0, ret_norm_w[0], bsz, nt)
    t, tseg = _s5_time_tile(rpb)
    s5_params = _s5_params(s5_lam_re[0], s5_lam_im[0], s5_log_dt[0], s5_b_re[0], s5_b_im[0],
                           s5_c_re[0], s5_c_im[0])
    o_b = _s5(proj0, bsz, s5_params, s5_d[0], s5_w_glu[0].astype(BF16), t, tseg)

    w_in_c_t = jnp.swapaxes(w_in_c, 1, 2)
    n_main = 2 * GLA_QK + 2 * GLA_W
    wg = jnp.pad(w_in_c_t[0, n_main:, :], ((0, GATE_PAD - GLA_RANK), (0, 0))).astype(BF16).T
    h1, xn1, glow = _out_ab(x, lead, o_a, o_b, w_out_ab[0].astype(BF16), norm_c_w[0], wg, nt)

    modes_c = [("plain", GLA_DK ** -0.5), plain, plain, plain, silu, silu]
    proj1 = _in_proj(xn1, w_in_c_t, modes_c, tm, w_is_transposed=True)
    return _gla(proj1, glow, gla_w_gate[0], gla_b_gate[0], gla_norm_w[0], h1, w_out_c,
                final_norm_w, bsz, nc)
```

```python
import functools
import math

import numpy as np
import jax
import jax.numpy as jnp
from jax import lax
from jax.experimental import pallas as pl
from jax.experimental.pallas import tpu as pltpu

F32 = jnp.float32
BF16 = jnp.bfloat16

N_META = 16
LANES = 128
SUBLANES = 8
CHUNK = 128
PAD = CHUNK - N_META
EPS = 1e-6
ROW_CHUNKS = 3

RET_HEADS = 8
RET_DK = 128
RET_DV = 256
RET_QK = RET_HEADS * RET_DK
RET_W = RET_HEADS * RET_DV
ROPE_BASE = 10000.0

S5_W = 1024
S5_GH = 16
S5_G = 64
S5_P = 64
S5_F = S5_G * S5_P
S5_SLABS = S5_F // LANES
MXU_TILE = 256
S5_JB = 8
S5_BLK_IN = S5_W // S5_JB
S5_K0_IN = MXU_TILE
S5_SEGS = SUBLANES
S5_SCAN_UNROLL = 4

GLA_HEADS = 4
GLA_DK = 256
GLA_DV = 512
GLA_QK = GLA_HEADS * GLA_DK
GLA_W = GLA_HEADS * GLA_DV
GLA_RANK = 16
GLA_TAU = 16.0
GLA_LEVELS = 7
GATE_PAD = LANES

PROJ_TN = 1024
PROJ_ROW_SPLIT = 4
VMEM_LIMIT = 56 * 1024 * 1024


def _cparams(sem):
    return pltpu.CompilerParams(dimension_semantics=sem, vmem_limit_bytes=VMEM_LIMIT)


def _silu(x):
    return x * (1.0 / (1.0 + jnp.exp(-x)))


def _pick_tile(n, candidates):
    for c in candidates:
        if n % c == 0:
            return c
    raise ValueError(f"no tile for {n}")


def _rms_scale(x, w):
    return x * lax.rsqrt(jnp.mean(x * x, axis=-1, keepdims=True) + EPS) * w


def _chunk_specs(d, n_before):
    return [pl.BlockSpec((None, CHUNK, d),
                         lambda b, t, k=k: (b, jnp.maximum(ROW_CHUNKS * t + k - n_before, 0), 0))
            for k in range(ROW_CHUNKS)]


def _embed_norm_kernel(lead_ref, xa_ref, xb_ref, xc_ref, w_ref, o_ref):
    t = pl.program_id(1)
    first = jnp.where(t == 0, lead_ref[...], xa_ref[...])
    for r, rows in enumerate((first, xb_ref[...], xc_ref[...])):
        o_ref[r * CHUNK:(r + 1) * CHUNK, :] = _rms_scale(rows, w_ref[...]).astype(o_ref.dtype)


def _embed_norm(x, lead, w, nt):
    bsz, _, d = x.shape
    tile = ROW_CHUNKS * CHUNK
    return pl.pallas_call(
        _embed_norm_kernel,
        grid=(bsz, nt),
        in_specs=[pl.BlockSpec((CHUNK, d), lambda b, t: (0, 0))] + _chunk_specs(d, 1)
        + [pl.BlockSpec((1, d), lambda b, t: (0, 0))],
        out_specs=pl.BlockSpec((tile, d), lambda b, t: (b * nt + t, 0)),
        out_shape=jax.ShapeDtypeStruct((bsz * nt * tile, d), BF16),
        compiler_params=_cparams(("parallel", "parallel")),
        name="embed_norm",
    )(lead, x, x, x, w.reshape(1, d))


def _rope_rows(a, cos, sin, scale):
    outs = []
    for h in range(a.shape[1] // RET_DK):
        blk = a[:, h * RET_DK:(h + 1) * RET_DK]
        r = blk * cos + pltpu.roll(blk, RET_DK // 2, 1) * sin
        outs.append(r * scale if scale != 1.0 else r)
    return jnp.concatenate(outs, axis=1)


def _in_proj_kernel(*refs, modes, has_rope, w_is_transposed):
    if has_rope:
        x_ref, w_ref, cos_ref, sin_ref, o_ref, wbf_ref = refs
    else:
        x_ref, w_ref, o_ref, wbf_ref = refs
    j = pl.program_id(0)

    @pl.when(pl.program_id(1) == 0)
    def _():
        wbf_ref[...] = w_ref[...].astype(BF16)

    contract = (((1,), (1 if w_is_transposed else 0,)), ((), ()))
    for mode in sorted(set(modes)):
        cond = functools.reduce(jnp.logical_or, [j == jj for jj, mm in enumerate(modes) if mm == mode])

        @pl.when(cond)
        def _(mode=mode):
            kind, scale = mode
            rows = x_ref.shape[0] // PROJ_ROW_SPLIT
            sls = [slice(r * rows, (r + 1) * rows) for r in range(PROJ_ROW_SPLIT)]
            accs = [lax.dot_general(x_ref[sl, :], wbf_ref[...], contract, preferred_element_type=F32)
                    for sl in sls]
            for sl, acc in zip(sls, accs):
                if kind == "rope":
                    acc = _rope_rows(acc, cos_ref[sl, :], sin_ref[sl, :], scale)
                elif kind == "silu":
                    acc = _silu(acc)
                elif scale != 1.0:
                    acc = acc * scale
                o_ref[sl, :] = acc.astype(o_ref.dtype)


def _in_proj(x, w, modes, tm, rope=None, w_is_transposed=False):
    m, k = x.shape
    nt = len(modes)
    assert m % tm == 0 and tm % (PROJ_ROW_SPLIT * 16) == 0
    if w_is_transposed:
        w_spec = pl.BlockSpec((None, PROJ_TN, k), lambda j, i: (0, j, 0))
        w_scratch = pltpu.VMEM((PROJ_TN, k), BF16)
    else:
        w_spec = pl.BlockSpec((None, k, PROJ_TN), lambda j, i: (0, 0, j))
        w_scratch = pltpu.VMEM((k, PROJ_TN), BF16)
    in_specs = [pl.BlockSpec((tm, k), lambda j, i: (i, 0)), w_spec]
    args = [x, w]
    if rope is not None:
        in_specs += [pl.BlockSpec((tm, RET_DK), lambda j, i: (i, 0))] * 2
        args += list(rope)
    return pl.pallas_call(
        functools.partial(_in_proj_kernel, modes=tuple(modes), has_rope=rope is not None,
                          w_is_transposed=w_is_transposed),
        grid=(nt, m // tm),
        in_specs=in_specs,
        out_specs=pl.BlockSpec((tm, PROJ_TN), lambda j, i: (i, j)),
        out_shape=jax.ShapeDtypeStruct((m, nt * PROJ_TN), BF16),
        scratch_shapes=[w_scratch],
        compiler_params=_cparams(("arbitrary", "arbitrary")),
        name="in_proj",
    )(*args)


def _ret_log_decay(h):
    return math.log1p(-(2.0 ** (-5.0 - h)))


def _retention_kernel(q_ref, k_ref, v_ref, z_ref, nw_ref, o_ref, s_ref, dec_ref, qs_ref, ks_ref):
    first_step = (pl.program_id(0) == 0) & (pl.program_id(1) == 0)

    @pl.when(first_step)
    def _():
        ii = lax.broadcasted_iota(jnp.int32, (CHUNK, CHUNK), 0)
        jj = lax.broadcasted_iota(jnp.int32, (CHUNK, CHUNK), 1)
        diff = (ii - jj).astype(F32)
        row = ii.astype(F32)
        for h in range(RET_HEADS):
            lg = _ret_log_decay(h)
            dec_ref[h] = jnp.where(ii >= jj, jnp.exp(lg * jnp.maximum(diff, 0.0)), 0.0)
            qs_ref[h] = jnp.exp(lg * (row + 1.0))
            ks_ref[h] = jnp.exp(lg * (CHUNK - 1.0 - row))

    @pl.when(pl.program_id(1) == 0)
    def _():
        s_ref[...] = jnp.zeros_like(s_ref)

    heads = range(RET_HEADS)
    vcs = [slice(h * RET_DV, (h + 1) * RET_DV) for h in heads]
    nt_dims = (((1,), (1,)), ((), ()))
    tn_dims = (((0,), (0,)), ((), ()))
    states = [s_ref[h] for h in heads]
    for r in range(q_ref.shape[0] // CHUNK):
        rows = slice(r * CHUNK, (r + 1) * CHUNK)
        qs = [q_ref[rows, h * RET_DK:(h + 1) * RET_DK] for h in heads]
        ks = [k_ref[rows, h * RET_DK:(h + 1) * RET_DK] for h in heads]
        vs = [v_ref[rows, vc] for vc in vcs]
        scores = [lax.dot_general(qs[h], ks[h], nt_dims, preferred_element_type=F32) for h in heads]
        kvs = [lax.dot_general((ks[h].astype(F32) * ks_ref[h]).astype(BF16), vs[h], tn_dims,
                               preferred_element_type=F32) for h in heads]
        lhs = [jnp.concatenate([(scores[h] * dec_ref[h]).astype(BF16),
                                (qs[h].astype(F32) * qs_ref[h]).astype(BF16)], axis=1) for h in heads]
        outs = [jnp.dot(lhs[h], jnp.concatenate([vs[h], states[h].astype(BF16)], axis=0),
                        preferred_element_type=F32) for h in heads]
        states = [states[h] * math.exp(_ret_log_decay(h) * CHUNK) + kvs[h] for h in heads]
        for h in heads:
            o_ref[rows, vcs[h]] = (_rms_scale(outs[h], nw_ref[:, vcs[h]])
                                   * z_ref[rows, vcs[h]].astype(F32)).astype(o_ref.dtype)
    for h in heads:
        s_ref[h] = states[h]


def _retention(proj, norm_w, bsz, nt):
    m = proj.shape[0]
    tile = ROW_CHUNKS * CHUNK
    tab = pltpu.VMEM((RET_HEADS, CHUNK, CHUNK), F32)
    row = lambda width, col: pl.BlockSpec((tile, width), lambda b, t: (b * nt + t, col))
    return pl.pallas_call(
        _retention_kernel,
        grid=(bsz, nt),
        in_specs=[row(RET_QK, 0), row(RET_QK, 1), row(RET_W, 1), row(RET_W, 2),
                  pl.BlockSpec((1, RET_W), lambda b, t: (0, 0))],
        out_specs=row(RET_W, 0),
        out_shape=jax.ShapeDtypeStruct((m, RET_W), BF16),
        scratch_shapes=[pltpu.VMEM((RET_HEADS, RET_DK, RET_DV), F32), tab, tab, tab],
        compiler_params=_cparams(("arbitrary", "arbitrary")),
        name="retention",
    )(proj, proj, proj, proj, norm_w.reshape(1, RET_W))


def _s5_prep(lre_ref, lim_ref, ldt_ref, bre_ref, bim_ref, cre_ref, cim_ref,
             pre_ref, pim_ref, w2_ref, wcre_ref, wcim_ref, k0_ref):
    lre = lre_ref[...]
    lim = lim_ref[...]
    dt = jnp.exp(ldt_ref[...])
    tseg = pre_ref.shape[1]
    n = 2.0 * (lax.broadcasted_iota(jnp.int32, (tseg, 1), 0).astype(F32) + 1.0)
    mag = jnp.exp(n * (lre * dt))
    ang = n * (lim * dt)
    p_re = mag * jnp.cos(ang)
    p_im = mag * jnp.sin(ang)
    for sl in range(S5_SLABS):
        pre_ref[sl] = p_re[:, sl * LANES:(sl + 1) * LANES]
        pim_ref[sl] = p_im[:, sl * LANES:(sl + 1) * LANES]
    mag1 = jnp.exp(lre * dt)
    a_re = mag1 * jnp.cos(lim * dt)
    a_im = mag1 * jnp.sin(lim * dt)
    den = lre * lre + lim * lim
    nr = a_re - 1.0
    f_re = (nr * lre + a_im * lim) / den
    f_im = (a_im * lre - nr * lim) / den
    b_re = bre_ref[...]
    b_im = bim_ref[...]
    bb_re = f_re * b_re - f_im * b_im
    bb_im = f_re * b_im + f_im * b_re
    bba_re = a_re * bb_re - a_im * bb_im
    bba_im = a_re * bb_im + a_im * bb_re
    c_re = cre_ref[...]
    c_im = cim_ref[...]
    ca_re = c_re * a_re - c_im * a_im
    ca_im = c_re * a_im + c_im * a_re
    blk = S5_F // S5_JB
    rows = lax.broadcasted_iota(jnp.int32, (S5_BLK_IN, blk), 0)
    cols = lax.broadcasted_iota(jnp.int32, (S5_BLK_IN, blk), 1)
    same_group = (rows // S5_GH) == (cols // S5_P)

    def block_diag(a, j):
        piece = a[:, j * blk:(j + 1) * blk]
        return jnp.where(same_group, jnp.concatenate([piece] * (S5_G // S5_JB), axis=0), 0.0)

    k0_ref[...] = jnp.zeros_like(k0_ref)
    per_k0 = S5_K0_IN // S5_BLK_IN
    for j in range(S5_JB):
        bd_bre, bd_bim = block_diag(bb_re, j), block_diag(bb_im, j)
        bd_cre, bd_cim = block_diag(c_re, j), block_diag(c_im, j)
        w2_ref[j, :S5_BLK_IN, :blk] = block_diag(bba_re, j).astype(BF16)
        w2_ref[j, :S5_BLK_IN, blk:] = block_diag(bba_im, j).astype(BF16)
        w2_ref[j, S5_BLK_IN:, :blk] = bd_bre.astype(BF16)
        w2_ref[j, S5_BLK_IN:, blk:] = bd_bim.astype(BF16)
        wcre_ref[j, :S5_BLK_IN, :] = bd_cre.astype(BF16)
        wcim_ref[j, :S5_BLK_IN, :] = bd_cim.astype(BF16)
        wcre_ref[j, S5_BLK_IN:, :] = block_diag(ca_re, j).astype(BF16)
        wcim_ref[j, S5_BLK_IN:, :] = block_diag(ca_im, j).astype(BF16)
        nt_dims = (((1,), (1,)), ((), ()))
        k0 = (lax.dot_general(bd_bre, bd_cre, nt_dims, preferred_element_type=F32)
              - lax.dot_general(bd_bim, bd_cim, nt_dims, preferred_element_type=F32))
        r0 = (j % per_k0) * S5_BLK_IN
        k0_ref[j // per_k0, r0:r0 + S5_BLK_IN, r0:r0 + S5_BLK_IN] = k0.astype(BF16)


def _s5_params(lam_re, lam_im, log_dt, b_re, b_im, c_re, c_im):
    flat = lambda a: a.reshape(1, S5_F)
    ldt = jnp.broadcast_to(log_dt[:, None], (S5_G, S5_P))
    bt = lambda a: a.reshape(S5_F, S5_GH).T
    ct = lambda a: a.transpose(1, 0, 2).reshape(S5_GH, S5_F)
    return (flat(lam_re), flat(lam_im), flat(ldt), bt(b_re), bt(b_im), ct(c_re.astype(F32)), ct(c_im.astype(F32)))


def _gelu_tanh(x):
    return 0.5 * x * (1.0 + jnp.tanh(math.sqrt(2.0 / math.pi) * (x + 0.044715 * (x * x * x))))


def _s5_kernel(u_ref, z_ref, lre_ref, lim_ref, ldt_ref, bpre_ref, bpim_ref, cpre_ref, cpim_ref, d_ref, wglu_ref,
               o_ref, w2_ref, wcre_ref, wcim_ref, k0_ref, pre_ref, pim_ref, wglubf_ref,
               bure_ref, buim_ref, cre_ref, cim_ref, xinre_ref, xinim_ref, yhat_ref, par_ref,
               *, tseg, slab_group):
    ci = pl.program_id(1)

    @pl.when((pl.program_id(0) == 0) & (ci == 0))
    def _():
        _s5_prep(lre_ref, lim_ref, ldt_ref, bpre_ref, bpim_ref, cpre_ref, cpim_ref,
                 pre_ref, pim_ref, w2_ref, wcre_ref, wcim_ref, k0_ref)
        wglubf_ref[...] = wglu_ref[...].astype(BF16)

    @pl.when(ci == 0)
    def _():
        cre_ref[...] = jnp.zeros_like(cre_ref)
        cim_ref[...] = jnp.zeros_like(cim_ref)
        yhat_ref[...] = jnp.zeros_like(yhat_ref)

    half = u_ref.shape[0] // 2
    lane_slabs = S5_W // LANES
    u_all = u_ref[...].astype(F32)
    for s in range(lane_slabs):
        par_ref[s] = u_all[:, s * LANES:(s + 1) * LANES]
    parity = lambda p: jnp.concatenate([par_ref[s, pl.ds(p, half, stride=2), :] for s in range(lane_slabs)], axis=1)
    u_e32, u_o32 = parity(0), parity(1)
    u_e, u_o = u_e32.astype(BF16), u_o32.astype(BF16)
    spb = S5_SLABS // S5_JB
    for j in range(S5_JB):
        blk = slice(j * S5_BLK_IN, (j + 1) * S5_BLK_IN)
        res = jnp.dot(jnp.concatenate([u_e[:, blk], u_o[:, blk]], axis=1), w2_ref[j], preferred_element_type=F32)
        for s in range(spb):
            bure_ref[j * spb + s] = res[:, s * LANES:(s + 1) * LANES]
            buim_ref[j * spb + s] = res[:, (spb + s) * LANES:(spb + s + 1) * LANES]

    for g0 in range(0, S5_SLABS, slab_group):
        slabs = list(range(g0, g0 + slab_group))
        a_re = [jnp.broadcast_to(pre_ref[sl, 0:1, :], (S5_SEGS, LANES)) for sl in slabs]
        a_im = [jnp.broadcast_to(pim_ref[sl, 0:1, :], (S5_SEGS, LANES)) for sl in slabs]

        def scan_body(tau, carry, slabs=slabs, a_re=a_re, a_im=a_im):
            xr, xi = carry
            nr, ni = [], []
            for n, sl in enumerate(slabs):
                rows = pl.ds(tau, S5_SEGS, stride=tseg)
                br = bure_ref[sl, rows, :]
                bi = buim_ref[sl, rows, :]
                r = a_re[n] * xr[n] - a_im[n] * xi[n] + br
                i = a_re[n] * xi[n] + a_im[n] * xr[n] + bi
                bure_ref[sl, rows, :] = r
                buim_ref[sl, rows, :] = i
                nr.append(r)
                ni.append(i)
            return tuple(nr), tuple(ni)

        zero = tuple(jnp.zeros((S5_SEGS, LANES), F32) for _ in slabs)
        end_re, end_im = lax.fori_loop(0, tseg, scan_body, (zero, zero),
                                       unroll=S5_SCAN_UNROLL if tseg % S5_SCAN_UNROLL == 0 else 1)

        for n, sl in enumerate(slabs):
            at_re = pre_ref[sl, tseg - 1:tseg, :]
            at_im = pim_ref[sl, tseg - 1:tseg, :]
            xr = cre_ref[sl, 0:1, :]
            xi = cim_ref[sl, 0:1, :]
            rows_re, rows_im = [], []
            for s in range(S5_SEGS):
                rows_re.append(xr)
                rows_im.append(xi)
                er = end_re[n][s:s + 1]
                ei = end_im[n][s:s + 1]
                xr, xi = er + at_re * xr - at_im * xi, ei + at_re * xi + at_im * xr
            cre_ref[sl] = jnp.broadcast_to(xr, (S5_SEGS, LANES))
            cim_ref[sl] = jnp.broadcast_to(xi, (S5_SEGS, LANES))
            xinre_ref[sl] = jnp.concatenate(rows_re, axis=0)
            xinim_ref[sl] = jnp.concatenate(rows_im, axis=0)

    def fix_body(sl, carry):
        xin_re = xinre_ref[sl]
        xin_im = xinim_ref[sl]
        for tau in range(tseg):
            rows = pl.ds(tau, S5_SEGS, stride=tseg)
            pr = pre_ref[sl, tau:tau + 1, :]
            pi = pim_ref[sl, tau:tau + 1, :]
            bure_ref[sl, rows, :] = bure_ref[sl, rows, :] + (pr * xin_re - pi * xin_im)
            buim_ref[sl, rows, :] = buim_ref[sl, rows, :] + (pr * xin_im + pi * xin_re)
        return carry

    lax.fori_loop(0, S5_SLABS, fix_body, 0)

    nt_dims = (((1,), (1,)), ((), ()))
    y_odd, y_hat = [], []
    for j in range(S5_JB):
        x_re = jnp.concatenate([bure_ref[j * spb + s] for s in range(spb)], axis=1).astype(BF16)
        x_im = jnp.concatenate([buim_ref[j * spb + s] for s in range(spb)], axis=1).astype(BF16)
        both = (lax.dot_general(x_re, wcre_ref[j], nt_dims, preferred_element_type=F32)
                - lax.dot_general(x_im, wcim_ref[j], nt_dims, preferred_element_type=F32))
        y_odd.append(both[:, :S5_BLK_IN])
        y_hat.append(both[:, S5_BLK_IN:])
    y_dir = [jnp.dot(u_e[:, j * S5_K0_IN:(j + 1) * S5_K0_IN], k0_ref[j], preferred_element_type=F32)
             for j in range(S5_W // S5_K0_IN)]
    y_hat = jnp.concatenate(y_hat, axis=1)
    ridx = lax.broadcasted_iota(jnp.int32, (y_hat.shape[0], 1), 0)
    y_even = jnp.where(ridx == 0, yhat_ref[...], pltpu.roll(y_hat, 1, 0)) + jnp.concatenate(y_dir, axis=1)
    yhat_ref[...] = y_hat[y_hat.shape[0] - 1:, :]
    halves = ((y_even, u_e32), (jnp.concatenate(y_odd, axis=1), u_o32))
    for p, (y, u) in enumerate(halves):
        y = _gelu_tanh(y + d_ref[...] * u)
        gate = jnp.dot(y.astype(BF16), wglubf_ref[...], preferred_element_type=F32)
        y = y * (1.0 / (1.0 + jnp.exp(-gate)))
        for s in range(lane_slabs):
            par_ref[s, pl.ds(p, half, stride=2), :] = y[:, s * LANES:(s + 1) * LANES]
    y_all = jnp.concatenate([par_ref[s] for s in range(lane_slabs)], axis=1)
    o_ref[...] = (y_all * z_ref[...].astype(F32)).astype(o_ref.dtype)


def _s5_time_tile(rows_per_batch):
    for tseg in (44, 66, 22, 6, 2):
        t = 2 * S5_SEGS * tseg
        if rows_per_batch % t == 0 and (S5_SEGS * tseg) % 16 == 0:
            return t, tseg
    raise ValueError(f"no S5 time tile for {rows_per_batch}")


def _s5(proj, bsz, params, d, wglu, t, tseg):
    m = proj.shape[0]
    rows = t // 2
    blk = S5_F // S5_JB
    nt = m // bsz // t
    ucol = (2 * RET_QK + 2 * RET_W) // S5_W
    const = lambda a: pl.BlockSpec(a.shape, lambda b, i: (0,) * a.ndim, pipeline_mode=pl.Buffered(1))
    tile = lambda col: pl.BlockSpec((t, S5_W), lambda b, i: (b * nt + i, col))
    d2 = d.reshape(1, S5_W)
    return pl.pallas_call(
        functools.partial(_s5_kernel, tseg=tseg, slab_group=8),
        grid=(bsz, nt),
        in_specs=[tile(ucol), tile(ucol + 1)] + [const(a) for a in params] + [const(d2), const(wglu)],
        out_specs=tile(0),
        out_shape=jax.ShapeDtypeStruct((m, S5_W), BF16),
        scratch_shapes=[pltpu.VMEM((S5_JB, 2 * S5_BLK_IN, 2 * blk), BF16),
                        pltpu.VMEM((S5_JB, 2 * S5_BLK_IN, blk), BF16), pltpu.VMEM((S5_JB, 2 * S5_BLK_IN, blk), BF16),
                        pltpu.VMEM((S5_W // S5_K0_IN, S5_K0_IN, S5_K0_IN), BF16),
                        pltpu.VMEM((S5_SLABS, tseg, LANES), F32), pltpu.VMEM((S5_SLABS, tseg, LANES), F32),
                        pltpu.VMEM((S5_W, S5_W), BF16)]
        + [pltpu.VMEM((S5_SLABS, rows, LANES), F32), pltpu.VMEM((S5_SLABS, rows, LANES), F32)]
        + [pltpu.VMEM((S5_SLABS, S5_SEGS, LANES), F32)] * 4
        + [pltpu.VMEM((1, S5_W), F32), pltpu.VMEM((S5_W // LANES, t, LANES), F32)],
        compiler_params=_cparams(("arbitrary", "arbitrary")),
        name="s5",
    )(proj, proj, *params, d2, wglu)


def _out_ab_kernel(lead_ref, xa_ref, xb_ref, xc_ref, oa_ref, ob_ref, w_ref, nw_ref, wg_ref,
                   h1_ref, xn_ref, gl_ref):
    t = pl.program_id(1)
    first = jnp.where(t == 0, lead_ref[...], xa_ref[...])
    h0 = (first, xb_ref[...], xc_ref[...])
    sls = [slice(r * CHUNK, (r + 1) * CHUNK) for r in range(ROW_CHUNKS)]
    accs = [jnp.dot(oa_ref[sl, :], w_ref[:RET_W, :], preferred_element_type=F32)
            + jnp.dot(ob_ref[sl, :], w_ref[RET_W:, :], preferred_element_type=F32) for sl in sls]
    h1s = [h0[r] + accs[r] for r in range(ROW_CHUNKS)]
    xns = [_rms_scale(h1, nw_ref[...]).astype(BF16) for h1 in h1s]
    gls = [jnp.dot(xn, wg_ref[...], preferred_element_type=F32) for xn in xns]
    for r, sl in enumerate(sls):
        h1_ref[sl, :] = h1s[r]
        xn_ref[sl, :] = xns[r]
        gl_ref[sl, :] = gls[r]


def _out_ab(x, lead, oa, ob, w, norm_w, wg, nt):
    bsz, _, d = x.shape
    tile = ROW_CHUNKS * CHUNK
    m = bsz * nt * tile
    row = lambda width: pl.BlockSpec((tile, width), lambda b, t: (b * nt + t, 0))
    const = lambda a: pl.BlockSpec(a.shape, lambda b, t: (0, 0))
    out = jax.ShapeDtypeStruct
    return pl.pallas_call(
        _out_ab_kernel,
        grid=(bsz, nt),
        in_specs=[pl.BlockSpec((CHUNK, d), lambda b, t: (0, 0))] + _chunk_specs(d, 1)
        + [row(RET_W), row(S5_W), const(w), pl.BlockSpec((1, d), lambda b, t: (0, 0)), const(wg)],
        out_specs=[row(d), row(d), row(GATE_PAD)],
        out_shape=(out((m, d), F32), out((m, d), BF16), out((m, GATE_PAD), F32)),
        compiler_params=_cparams(("parallel", "parallel")),
        name="out_proj_ab",
    )(lead, x, x, x, oa, ob, w, norm_w.reshape(1, d), wg)


LOG2E = 1.4426950408889634


def _split2(x):
    x1 = x.astype(BF16)
    return x1, (x - x1.astype(F32)).astype(BF16)


def _neg_abs(x):
    return -jnp.abs(x)


def _block_ref_rows(b, level):
    rows, width = b.shape
    half = 1 << level
    blk = 2 * half
    if blk >= 8:
        b3 = b.reshape(rows // blk, blk, width)
        ref = jnp.broadcast_to(b3[:, half - 1:half, :], b3.shape)
        return ref.reshape(rows, width)
    b3 = b.reshape(rows // 8, 8, width)
    sub = lax.broadcasted_iota(jnp.int32, b3.shape, 1)
    pick = lambda r: jnp.broadcast_to(b3[:, r:r + 1, :], b3.shape)
    if blk == 4:
        ref = jnp.where(sub < 4, pick(1), pick(5))
    else:
        ref = jnp.where(sub < 2, pick(0), jnp.where(sub < 4, pick(2), jnp.where(sub < 6, pick(4), pick(6))))
    return ref.reshape(rows, width)


def _gla_kernel(q_ref, k_ref, v_ref, z_ref, gl_ref, wg_ref, bg_ref, nw_ref, h_ref, wo_ref, fw_ref,
                y_ref, st_ref, mix_ref, wobf_ref, *, nc, nchunks):
    c = pl.program_id(0)
    n = jnp.minimum(c, nchunks - 1) % nc

    @pl.when(c == 0)
    def _():
        mix_ref[...] = jnp.zeros_like(mix_ref)
        wobf_ref[...] = wo_ref[...].astype(BF16)

    @pl.when(n == 0)
    def _():
        st_ref[...] = jnp.zeros_like(st_ref)

    ridx = lax.broadcasted_iota(jnp.int32, (CHUNK, 1), 0)
    ii = lax.broadcasted_iota(jnp.int32, (CHUNK, CHUNK), 0)
    jj = lax.broadcasted_iota(jnp.int32, (CHUNK, CHUNK), 1)
    pair_code = jnp.where(ii > jj, ii ^ jj, 0)
    eye = ii == jj
    tri = (ii >= jj).astype(BF16)

    gl1, gl2 = _split2(gl_ref[:, :GLA_RANK])
    wg1, wg2 = _split2(wg_ref[...])
    x = jnp.dot(jnp.concatenate([gl1, gl2, gl1], axis=1), jnp.concatenate([wg1, wg1, wg2], axis=0),
                preferred_element_type=F32) + bg_ref[...]
    log_a = (jnp.minimum(x, 0.0) - jnp.log(1.0 + jnp.exp(-jnp.abs(x)))) * (LOG2E / GLA_TAU)
    log_a = jnp.where(ridx >= jnp.where(n > 0, 0, PAD), log_a, 0.0)
    g1, g2 = _split2(log_a)
    b_all = jnp.dot(jnp.concatenate([tri, tri], axis=1), jnp.concatenate([g1, g2], axis=0),
                    preferred_element_type=F32)
    odd = (ridx & 1) == 1
    h2 = h_ref[...]

    def finish(heads, qs, ks, bs, scores):
        for i, h in enumerate(heads):
            vc = slice(h * GLA_DV, (h + 1) * GLA_DV)
            q, k, b = qs[i], ks[i], bs[i]
            v = v_ref[:, vc]
            b_last = b[CHUNK - 1:CHUNK, :]
            o = jnp.dot(scores[i].astype(BF16), v, preferred_element_type=F32)
            st = st_ref[h]
            qe = (q * jnp.exp2(b)).astype(BF16)
            o = o + lax.dot_general(qe, st.astype(BF16), (((1,), (1,)), ((), ())), preferred_element_type=F32)
            ke = (k * jnp.exp2(b_last - b)).astype(BF16)
            kv_t = lax.dot_general(v, ke, (((0,), (0,)), ((), ())), preferred_element_type=F32)
            st_ref[h] = st * jnp.exp2(b_last) + kv_t
            mix_ref[:, vc] = (_rms_scale(o, nw_ref[:, vc]) * z_ref[:, vc].astype(F32)).astype(BF16)

    pending = []
    for h0 in range(0, GLA_HEADS, 2):
        heads = (h0, h0 + 1)
        qs, ks, bs, scores = [], [], [], []
        for h in heads:
            kc = slice(h * GLA_DK, (h + 1) * GLA_DK)
            vc = slice(h * GLA_DV, (h + 1) * GLA_DV)
            h2 = h2 + jnp.dot(mix_ref[:, vc], wobf_ref[vc, :], preferred_element_type=F32)
            qs.append(q_ref[:, kc].astype(F32))
            ks.append(k_ref[:, kc].astype(F32))
            bs.append(b_all[:, kc])
            scores.append(jnp.where(eye, jnp.sum(qs[-1] * ks[-1], axis=1, keepdims=True), 0.0))
        for lv in range(GLA_LEVELS):
            half = 1 << lv
            if lv == GLA_LEVELS // 2 and pending:
                finish(*pending.pop())
            zz = []
            for q, k, b in zip(qs, ks, bs):
                if lv == 0:
                    expo = jnp.where(odd, b - pltpu.roll(b, 1, 0), 0.0)
                else:
                    expo = _neg_abs(b - _block_ref_rows(b, lv))
                if half >= 8:
                    sel = jnp.concatenate([(q if r % 2 else k)[r * half:(r + 1) * half]
                                           for r in range(CHUNK // half)], axis=0)
                else:
                    sel = jnp.where(((ridx >> lv) & 1) == 1, q, k)
                zz.append((sel * jnp.exp2(expo)).astype(BF16))
            z2 = jnp.concatenate(zz, axis=0)
            gram = lax.dot_general(z2, z2, (((1,), (1,)), ((), ())), preferred_element_type=F32)
            mask = (pair_code >> lv) == 1
            for i in range(2):
                blk = gram[i * CHUNK:(i + 1) * CHUNK, i * CHUNK:(i + 1) * CHUNK]
                scores[i] = scores[i] + jnp.where(mask, blk, 0.0)
        pending.append((heads, qs, ks, bs, scores))
    finish(*pending.pop())

    y_ref[...] = _rms_scale(h2, fw_ref[...])


def _gla(proj, glow, w_gate, b_gate, norm_w, h1, w_out, final_w, bsz, nc):
    m, d = h1.shape
    nchunks = m // CHUNK
    const = lambda *shape: pl.BlockSpec(shape, lambda c: (0,) * len(shape))
    cur = lambda width, col: pl.BlockSpec((CHUNK, width), lambda c: (jnp.minimum(c, nchunks - 1), col))
    prev = lambda c: jnp.maximum(c - 1, 0)
    return pl.pallas_call(
        functools.partial(_gla_kernel, nc=nc, nchunks=nchunks),
        grid=(nchunks + 1,),
        in_specs=[cur(GLA_QK, 0), cur(GLA_QK, 1), cur(GLA_W, 1), cur(GLA_W, 2), cur(GATE_PAD, 0),
                  const(GLA_RANK, GLA_QK), const(1, GLA_QK), const(1, GLA_W),
                  pl.BlockSpec((CHUNK, d), lambda c: (prev(c), 0)),
                  pl.BlockSpec((None, GLA_W, d), lambda c: (0, 0, 0), pipeline_mode=pl.Buffered(1)),
                  const(1, d)],
        out_specs=pl.BlockSpec((None, CHUNK, d),
                               lambda c: (prev(c) // nc, jnp.maximum(prev(c) % nc - 1, 0), 0)),
        out_shape=jax.ShapeDtypeStruct((bsz, (nc - 1) * CHUNK, d), F32),
        scratch_shapes=[pltpu.VMEM((GLA_HEADS, GLA_DV, GLA_DK), F32), pltpu.VMEM((CHUNK, GLA_W), BF16),
                        pltpu.VMEM((GLA_W, d), BF16)],
        compiler_params=_cparams(("arbitrary",)),
        name="gla_out",
    )(proj, proj, proj, proj, glow, w_gate.astype(F32), b_gate.reshape(1, GLA_QK).astype(F32),
      norm_w.reshape(1, GLA_W).astype(F32), h1, w_out, final_w.reshape(1, d).astype(F32))


def _rope_tables(bsz, rows_per_batch):
    pos = np.maximum(np.arange(rows_per_batch, dtype=np.float64) - PAD, 0.0)
    inv_freq = np.power(ROPE_BASE, -np.arange(0, RET_DK, 2, dtype=np.float64) / RET_DK)
    ang = pos[:, None] * inv_freq[None, :]
    cos, sin = np.cos(ang), np.sin(ang)
    cos2 = np.tile(np.concatenate([cos, cos], axis=1), (bsz, 1)).astype(np.float32)
    sin2 = np.tile(np.concatenate([-sin, sin], axis=1), (bsz, 1)).astype(np.float32)
    return jnp.asarray(cos2), jnp.asarray(sin2)


def kernel(x, meta, norm_ab_w, w_in_ab, ret_norm_w, s5_lam_re, s5_lam_im, s5_log_dt, s5_b_re, s5_b_im,
           s5_c_re, s5_c_im, s5_d, s5_w_glu, w_out_ab, norm_c_w, w_in_c, gla_w_gate, gla_b_gate,
           gla_norm_w, w_out_c, final_norm_w):
    bsz, seq, d = x.shape
    assert seq % CHUNK == 0 and w_in_ab.shape[0] == 1 and w_in_c.shape[0] == 1
    rpb = seq + CHUNK
    nc = rpb // CHUNK
    assert nc % ROW_CHUNKS == 0
    nt = nc // ROW_CHUNKS
    m = bsz * rpb
    tm = _pick_tile(m, (2112, 1408, 768, 384, 256, 128))
    lead = jnp.concatenate([jnp.zeros((PAD, d), x.dtype), meta.astype(x.dtype)], axis=0)

    xn0 = _embed_norm(x, lead, norm_ab_w[0], nt)
    plain, silu = ("plain", 1.0), ("silu", 1.0)
    modes_ab = [("rope", 1.0), ("rope", RET_DK ** -0.5), plain, plain, silu, silu, plain, silu]
    proj0 = _in_proj(xn0, w_in_ab, modes_ab, tm, rope=_rope_tables(bsz, rpb))
    o_a = _retention(proj0, ret_norm_w[0], bsz, nt)
    t, tseg = _s5_time_tile(rpb)
    s5_params = _s5_params(s5_lam_re[0], s5_lam_im[0], s5_log_dt[0], s5_b_re[0], s5_b_im[0],
                           s5_c_re[0], s5_c_im[0])
    o_b = _s5(proj0, bsz, s5_params, s5_d[0], s5_w_glu[0].astype(F32), t, tseg)

    w_in_c_t = jnp.swapaxes(w_in_c, 1, 2)
    n_main = 2 * GLA_QK + 2 * GLA_W
    wg = jnp.pad(w_in_c_t[0, n_main:, :], ((0, GATE_PAD - GLA_RANK), (0, 0))).astype(BF16).T
    h1, xn1, glow = _out_ab(x, lead, o_a, o_b, w_out_ab[0].astype(BF16), norm_c_w[0], wg, nt)

    modes_c = [("plain", GLA_DK ** -0.5), plain, plain, plain, silu, silu]
    proj1 = _in_proj(xn1, w_in_c_t, modes_c, tm, w_is_transposed=True)
    return _gla(proj1, glow, gla_w_gate[0], gla_b_gate[0], gla_norm_w[0], h1, w_out_c,
                final_norm_w, bsz, nc)
```

```python
import functools
import math

import numpy as np
import jax
import jax.numpy as jnp
from jax import lax
from jax.experimental import pallas as pl
from jax.experimental.pallas import tpu as pltpu

F32 = jnp.float32
BF16 = jnp.bfloat16

N_META = 16
LANES = 128
SUBLANES = 8
CHUNK = 128
PAD = CHUNK - N_META
EPS = 1e-6
ROW_CHUNKS = 3

RET_HEADS = 8
RET_DK = 128
RET_DV = 256
RET_QK = RET_HEADS * RET_DK
RET_W = RET_HEADS * RET_DV
ROPE_BASE = 10000.0

S5_W = 1024
S5_GH = 16
S5_G = 64
S5_P = 64
S5_F = S5_G * S5_P
S5_SLABS = S5_F // LANES
MXU_TILE = 256
S5_JB = 8
S5_BLK_IN = S5_W // S5_JB
S5_K0_IN = MXU_TILE
S5_SEGS = SUBLANES
S5_SCAN_UNROLL = 4

GLA_HEADS = 4
GLA_DK = 256
GLA_DV = 512
GLA_QK = GLA_HEADS * GLA_DK
GLA_W = GLA_HEADS * GLA_DV
GLA_RANK = 16
GLA_TAU = 16.0
GLA_LEVELS = 7
GATE_PAD = LANES

PROJ_TN = 1024
PROJ_ROW_SPLIT = 4
VMEM_LIMIT = 56 * 1024 * 1024


def _cparams(sem):
    return pltpu.CompilerParams(dimension_semantics=sem, vmem_limit_bytes=VMEM_LIMIT)


def _silu(x):
    return x * (1.0 / (1.0 + jnp.exp(-x)))


def _pick_tile(n, candidates):
    for c in candidates:
        if n % c == 0:
            return c
    raise ValueError(f"no tile for {n}")


def _rms_scale(x, w):
    return x * lax.rsqrt(jnp.mean(x * x, axis=-1, keepdims=True) + EPS) * w


def _chunk_specs(d, n_before):
    return [pl.BlockSpec((None, CHUNK, d),
                         lambda b, t, k=k: (b, jnp.maximum(ROW_CHUNKS * t + k - n_before, 0), 0))
            for k in range(ROW_CHUNKS)]


def _embed_norm_kernel(lead_ref, xa_ref, xb_ref, xc_ref, w_ref, o_ref):
    t = pl.program_id(1)
    first = jnp.where(t == 0, lead_ref[...], xa_ref[...])
    for r, rows in enumerate((first, xb_ref[...], xc_ref[...])):
        o_ref[r * CHUNK:(r + 1) * CHUNK, :] = _rms_scale(rows, w_ref[...]).astype(o_ref.dtype)


def _embed_norm(x, lead, w, nt):
    bsz, _, d = x.shape
    tile = ROW_CHUNKS * CHUNK
    return pl.pallas_call(
        _embed_norm_kernel,
        grid=(bsz, nt),
        in_specs=[pl.BlockSpec((CHUNK, d), lambda b, t: (0, 0))] + _chunk_specs(d, 1)
        + [pl.BlockSpec((1, d), lambda b, t: (0, 0))],
        out_specs=pl.BlockSpec((tile, d), lambda b, t: (b * nt + t, 0)),
        out_shape=jax.ShapeDtypeStruct((bsz * nt * tile, d), BF16),
        compiler_params=_cparams(("parallel", "parallel")),
        name="embed_norm",
    )(lead, x, x, x, w.reshape(1, d))


def _rope_rows(a, cos, sin, scale):
    outs = []
    for h in range(a.shape[1] // RET_DK):
        blk = a[:, h * RET_DK:(h + 1) * RET_DK]
        r = blk * cos + pltpu.roll(blk, RET_DK // 2, 1) * sin
        outs.append(r * scale if scale != 1.0 else r)
    return jnp.concatenate(outs, axis=1)


def _in_proj_kernel(*refs, modes, has_rope, w_is_transposed):
    if has_rope:
        x_ref, w_ref, cos_ref, sin_ref, o_ref, wbf_ref = refs
    else:
        x_ref, w_ref, o_ref, wbf_ref = refs
    j = pl.program_id(0)

    @pl.when(pl.program_id(1) == 0)
    def _():
        wbf_ref[...] = w_ref[...].astype(BF16)

    contract = (((1,), (1 if w_is_transposed else 0,)), ((), ()))
    for mode in sorted(set(modes)):
        cond = functools.reduce(jnp.logical_or, [j == jj for jj, mm in enumerate(modes) if mm == mode])

        @pl.when(cond)
        def _(mode=mode):
            kind, scale = mode
            rows = x_ref.shape[0] // PROJ_ROW_SPLIT
            sls = [slice(r * rows, (r + 1) * rows) for r in range(PROJ_ROW_SPLIT)]
            accs = [lax.dot_general(x_ref[sl, :], wbf_ref[...], contract, preferred_element_type=F32)
                    for sl in sls]
            for sl, acc in zip(sls, accs):
                if kind == "rope":
                    acc = _rope_rows(acc, cos_ref[sl, :], sin_ref[sl, :], scale)
                elif kind == "silu":
                    acc = _silu(acc)
                elif scale != 1.0:
                    acc = acc * scale
                o_ref[sl, :] = acc.astype(o_ref.dtype)


def _in_proj(x, w, modes, tm, rope=None, w_is_transposed=False):
    m, k = x.shape
    nt = len(modes)
    assert m % tm == 0 and tm % (PROJ_ROW_SPLIT * 16) == 0
    if w_is_transposed:
        w_spec = pl.BlockSpec((None, PROJ_TN, k), lambda j, i: (0, j, 0))
        w_scratch = pltpu.VMEM((PROJ_TN, k), BF16)
    else:
        w_spec = pl.BlockSpec((None, k, PROJ_TN), lambda j, i: (0, 0, j))
        w_scratch = pltpu.VMEM((k, PROJ_TN), BF16)
    in_specs = [pl.BlockSpec((tm, k), lambda j, i: (i, 0)), w_spec]
    args = [x, w]
    if rope is not None:
        in_specs += [pl.BlockSpec((tm, RET_DK), lambda j, i: (i, 0))] * 2
        args += list(rope)
    return pl.pallas_call(
        functools.partial(_in_proj_kernel, modes=tuple(modes), has_rope=rope is not None,
                          w_is_transposed=w_is_transposed),
        grid=(nt, m // tm),
        in_specs=in_specs,
        out_specs=pl.BlockSpec((tm, PROJ_TN), lambda j, i: (i, j)),
        out_shape=jax.ShapeDtypeStruct((m, nt * PROJ_TN), BF16),
        scratch_shapes=[w_scratch],
        compiler_params=_cparams(("arbitrary", "arbitrary")),
        name="in_proj",
    )(*args)


def _ret_log_decay(h):
    return math.log1p(-(2.0 ** (-5.0 - h)))


def _retention_kernel(q_ref, k_ref, v_ref, z_ref, nw_ref, o_ref, s_ref, dec_ref, qs_ref, ks_ref):
    first_step = (pl.program_id(0) == 0) & (pl.program_id(1) == 0)

    @pl.when(first_step)
    def _():
        ii = lax.broadcasted_iota(jnp.int32, (CHUNK, CHUNK), 0)
        jj = lax.broadcasted_iota(jnp.int32, (CHUNK, CHUNK), 1)
        diff = (ii - jj).astype(F32)
        row = ii.astype(F32)
        for h in range(RET_HEADS):
            lg = _ret_log_decay(h)
            dec_ref[h] = jnp.where(ii >= jj, jnp.exp(lg * jnp.maximum(diff, 0.0)), 0.0)
            qs_ref[h] = jnp.exp(lg * (row + 1.0))
            ks_ref[h] = jnp.exp(lg * (CHUNK - 1.0 - row))

    @pl.when(pl.program_id(1) == 0)
    def _():
        s_ref[...] = jnp.zeros_like(s_ref)

    heads = range(RET_HEADS)
    vcs = [slice(h * RET_DV, (h + 1) * RET_DV) for h in heads]
    nt_dims = (((1,), (1,)), ((), ()))
    tn_dims = (((0,), (0,)), ((), ()))
    states = [s_ref[h] for h in heads]
    for r in range(q_ref.shape[0] // CHUNK):
        rows = slice(r * CHUNK, (r + 1) * CHUNK)
        qs = [q_ref[rows, h * RET_DK:(h + 1) * RET_DK] for h in heads]
        ks = [k_ref[rows, h * RET_DK:(h + 1) * RET_DK] for h in heads]
        vs = [v_ref[rows, vc] for vc in vcs]
        scores = [lax.dot_general(qs[h], ks[h], nt_dims, preferred_element_type=F32) for h in heads]
        kvs = [lax.dot_general((ks[h].astype(F32) * ks_ref[h]).astype(BF16), vs[h], tn_dims,
                               preferred_element_type=F32) for h in heads]
        lhs = [jnp.concatenate([(scores[h] * dec_ref[h]).astype(BF16),
                                (qs[h].astype(F32) * qs_ref[h]).astype(BF16)], axis=1) for h in heads]
        outs = [jnp.dot(lhs[h], jnp.concatenate([vs[h], states[h].astype(BF16)], axis=0),
                        preferred_element_type=F32) for h in heads]
        states = [states[h] * math.exp(_ret_log_decay(h) * CHUNK) + kvs[h] for h in heads]
        for h in heads:
            o_ref[rows, vcs[h]] = (_rms_scale(outs[h], nw_ref[:, vcs[h]])
                                   * z_ref[rows, vcs[h]].astype(F32)).astype(o_ref.dtype)
    for h in heads:
        s_ref[h] = states[h]


def _retention(proj, norm_w, bsz, nt):
    m = proj.shape[0]
    tile = ROW_CHUNKS * CHUNK
    tab = pltpu.VMEM((RET_HEADS, CHUNK, CHUNK), F32)
    row = lambda width, col: pl.BlockSpec((tile, width), lambda b, t: (b * nt + t, col))
    return pl.pallas_call(
        _retention_kernel,
        grid=(bsz, nt),
        in_specs=[row(RET_QK, 0), row(RET_QK, 1), row(RET_W, 1), row(RET_W, 2),
                  pl.BlockSpec((1, RET_W), lambda b, t: (0, 0))],
        out_specs=row(RET_W, 0),
        out_shape=jax.ShapeDtypeStruct((m, RET_W), BF16),
        scratch_shapes=[pltpu.VMEM((RET_HEADS, RET_DK, RET_DV), F32), tab, tab, tab],
        compiler_params=_cparams(("arbitrary", "arbitrary")),
        name="retention",
    )(proj, proj, proj, proj, norm_w.reshape(1, RET_W))


def _s5_prep(lre_ref, lim_ref, ldt_ref, bre_ref, bim_ref, cre_ref, cim_ref,
             pre_ref, pim_ref, w2_ref, wcre_ref, wcim_ref, k0_ref):
    lre = lre_ref[...]
    lim = lim_ref[...]
    dt = jnp.exp(ldt_ref[...])
    tseg = pre_ref.shape[1]
    n = 2.0 * (lax.broadcasted_iota(jnp.int32, (tseg, 1), 0).astype(F32) + 1.0)
    mag = jnp.exp(n * (lre * dt))
    ang = n * (lim * dt)
    p_re = mag * jnp.cos(ang)
    p_im = mag * jnp.sin(ang)
    for sl in range(S5_SLABS):
        pre_ref[sl] = p_re[:, sl * LANES:(sl + 1) * LANES]
        pim_ref[sl] = p_im[:, sl * LANES:(sl + 1) * LANES]
    mag1 = jnp.exp(lre * dt)
    a_re = mag1 * jnp.cos(lim * dt)
    a_im = mag1 * jnp.sin(lim * dt)
    den = lre * lre + lim * lim
    nr = a_re - 1.0
    f_re = (nr * lre + a_im * lim) / den
    f_im = (a_im * lre - nr * lim) / den
    b_re = bre_ref[...]
    b_im = bim_ref[...]
    bb_re = f_re * b_re - f_im * b_im
    bb_im = f_re * b_im + f_im * b_re
    bba_re = a_re * bb_re - a_im * bb_im
    bba_im = a_re * bb_im + a_im * bb_re
    c_re = cre_ref[...]
    c_im = cim_ref[...]
    ca_re = c_re * a_re - c_im * a_im
    ca_im = c_re * a_im + c_im * a_re
    blk = S5_F // S5_JB
    rows = lax.broadcasted_iota(jnp.int32, (S5_BLK_IN, blk), 0)
    cols = lax.broadcasted_iota(jnp.int32, (S5_BLK_IN, blk), 1)
    same_group = (rows // S5_GH) == (cols // S5_P)

    def block_diag(a, j):
        piece = a[:, j * blk:(j + 1) * blk]
        return jnp.where(same_group, jnp.concatenate([piece] * (S5_G // S5_JB), axis=0), 0.0)

    k0_ref[...] = jnp.zeros_like(k0_ref)
    per_k0 = S5_K0_IN // S5_BLK_IN
    for j in range(S5_JB):
        bd_bre, bd_bim = block_diag(bb_re, j), block_diag(bb_im, j)
        bd_cre, bd_cim = block_diag(c_re, j), block_diag(c_im, j)
        w2_ref[j, :S5_BLK_IN, :blk] = block_diag(bba_re, j).astype(BF16)
        w2_ref[j, :S5_BLK_IN, blk:] = block_diag(bba_im, j).astype(BF16)
        w2_ref[j, S5_BLK_IN:, :blk] = bd_bre.astype(BF16)
        w2_ref[j, S5_BLK_IN:, blk:] = bd_bim.astype(BF16)
        wcre_ref[j, :S5_BLK_IN, :] = bd_cre.astype(BF16)
        wcim_ref[j, :S5_BLK_IN, :] = bd_cim.astype(BF16)
        wcre_ref[j, S5_BLK_IN:, :] = block_diag(ca_re, j).astype(BF16)
        wcim_ref[j, S5_BLK_IN:, :] = block_diag(ca_im, j).astype(BF16)
        nt_dims = (((1,), (1,)), ((), ()))
        k0 = (lax.dot_general(bd_bre, bd_cre, nt_dims, preferred_element_type=F32)
              - lax.dot_general(bd_bim, bd_cim, nt_dims, preferred_element_type=F32))
        r0 = (j % per_k0) * S5_BLK_IN
        k0_ref[j // per_k0, r0:r0 + S5_BLK_IN, r0:r0 + S5_BLK_IN] = k0.astype(BF16)


def _s5_params(lam_re, lam_im, log_dt, b_re, b_im, c_re, c_im):
    flat = lambda a: a.reshape(1, S5_F)
    ldt = jnp.broadcast_to(log_dt[:, None], (S5_G, S5_P))
    bt = lambda a: a.reshape(S5_F, S5_GH).T
    ct = lambda a: a.transpose(1, 0, 2).reshape(S5_GH, S5_F)
    return (flat(lam_re), flat(lam_im), flat(ldt), bt(b_re), bt(b_im), ct(c_re.astype(F32)), ct(c_im.astype(F32)))


def _gelu_tanh(x):
    return 0.5 * x * (1.0 + jnp.tanh(math.sqrt(2.0 / math.pi) * (x + 0.044715 * (x * x * x))))


def _s5_kernel(u_ref, z_ref, lre_ref, lim_ref, ldt_ref, bpre_ref, bpim_ref, cpre_ref, cpim_ref, d_ref, wglu_ref,
               wnext_ref, o_ref, wnextbf_ref, w2_ref, wcre_ref, wcim_ref, k0_ref, pre_ref, pim_ref, wglubf_ref,
               bure_ref, buim_ref, cre_ref, cim_ref, xinre_ref, xinim_ref, yhat_ref, par_ref,
               *, tseg, slab_group):
    ci = pl.program_id(1)
    wnextbf_ref[...] = wnext_ref[...].astype(BF16)

    @pl.when((pl.program_id(0) == 0) & (ci == 0))
    def _():
        _s5_prep(lre_ref, lim_ref, ldt_ref, bpre_ref, bpim_ref, cpre_ref, cpim_ref,
                 pre_ref, pim_ref, w2_ref, wcre_ref, wcim_ref, k0_ref)
        wglubf_ref[...] = wglu_ref[...].astype(BF16)

    @pl.when(ci == 0)
    def _():
        cre_ref[...] = jnp.zeros_like(cre_ref)
        cim_ref[...] = jnp.zeros_like(cim_ref)
        yhat_ref[...] = jnp.zeros_like(yhat_ref)

    half = u_ref.shape[0] // 2
    lane_slabs = S5_W // LANES
    u_all = u_ref[...].astype(F32)
    for s in range(lane_slabs):
        par_ref[s] = u_all[:, s * LANES:(s + 1) * LANES]
    parity = lambda p: jnp.concatenate([par_ref[s, pl.ds(p, half, stride=2), :] for s in range(lane_slabs)], axis=1)
    u_e32, u_o32 = parity(0), parity(1)
    u_e, u_o = u_e32.astype(BF16), u_o32.astype(BF16)
    spb = S5_SLABS // S5_JB
    for j in range(S5_JB):
        blk = slice(j * S5_BLK_IN, (j + 1) * S5_BLK_IN)
        res = jnp.dot(jnp.concatenate([u_e[:, blk], u_o[:, blk]], axis=1), w2_ref[j], preferred_element_type=F32)
        for s in range(spb):
            bure_ref[j * spb + s] = res[:, s * LANES:(s + 1) * LANES]
            buim_ref[j * spb + s] = res[:, (spb + s) * LANES:(spb + s + 1) * LANES]

    for g0 in range(0, S5_SLABS, slab_group):
        slabs = list(range(g0, g0 + slab_group))
        a_re = [jnp.broadcast_to(pre_ref[sl, 0:1, :], (S5_SEGS, LANES)) for sl in slabs]
        a_im = [jnp.broadcast_to(pim_ref[sl, 0:1, :], (S5_SEGS, LANES)) for sl in slabs]

        def scan_body(tau, carry, slabs=slabs, a_re=a_re, a_im=a_im):
            xr, xi = carry
            nr, ni = [], []
            for n, sl in enumerate(slabs):
                rows = pl.ds(tau, S5_SEGS, stride=tseg)
                br = bure_ref[sl, rows, :]
                bi = buim_ref[sl, rows, :]
                r = a_re[n] * xr[n] - a_im[n] * xi[n] + br
                i = a_re[n] * xi[n] + a_im[n] * xr[n] + bi
                bure_ref[sl, rows, :] = r
                buim_ref[sl, rows, :] = i
                nr.append(r)
                ni.append(i)
            return tuple(nr), tuple(ni)

        zero = tuple(jnp.zeros((S5_SEGS, LANES), F32) for _ in slabs)
        end_re, end_im = lax.fori_loop(0, tseg, scan_body, (zero, zero),
                                       unroll=S5_SCAN_UNROLL if tseg % S5_SCAN_UNROLL == 0 else 1)

        for n, sl in enumerate(slabs):
            at_re = pre_ref[sl, tseg - 1:tseg, :]
            at_im = pim_ref[sl, tseg - 1:tseg, :]
            xr = cre_ref[sl, 0:1, :]
            xi = cim_ref[sl, 0:1, :]
            rows_re, rows_im = [], []
            for s in range(S5_SEGS):
                rows_re.append(xr)
                rows_im.append(xi)
                er = end_re[n][s:s + 1]
                ei = end_im[n][s:s + 1]
                xr, xi = er + at_re * xr - at_im * xi, ei + at_re * xi + at_im * xr
            cre_ref[sl] = jnp.broadcast_to(xr, (S5_SEGS, LANES))
            cim_ref[sl] = jnp.broadcast_to(xi, (S5_SEGS, LANES))
            xinre_ref[sl] = jnp.concatenate(rows_re, axis=0)
            xinim_ref[sl] = jnp.concatenate(rows_im, axis=0)

    def fix_body(sl, carry):
        xin_re = xinre_ref[sl]
        xin_im = xinim_ref[sl]
        for tau in range(tseg):
            rows = pl.ds(tau, S5_SEGS, stride=tseg)
            pr = pre_ref[sl, tau:tau + 1, :]
            pi = pim_ref[sl, tau:tau + 1, :]
            bure_ref[sl, rows, :] = bure_ref[sl, rows, :] + (pr * xin_re - pi * xin_im)
            buim_ref[sl, rows, :] = buim_ref[sl, rows, :] + (pr * xin_im + pi * xin_re)
        return carry

    lax.fori_loop(0, S5_SLABS, fix_body, 0)

    nt_dims = (((1,), (1,)), ((), ()))
    y_odd, y_hat = [], []
    for j in range(S5_JB):
        x_re = jnp.concatenate([bure_ref[j * spb + s] for s in range(spb)], axis=1).astype(BF16)
        x_im = jnp.concatenate([buim_ref[j * spb + s] for s in range(spb)], axis=1).astype(BF16)
        both = (lax.dot_general(x_re, wcre_ref[j], nt_dims, preferred_element_type=F32)
                - lax.dot_general(x_im, wcim_ref[j], nt_dims, preferred_element_type=F32))
        y_odd.append(both[:, :S5_BLK_IN])
        y_hat.append(both[:, S5_BLK_IN:])
    y_dir = [jnp.dot(u_e[:, j * S5_K0_IN:(j + 1) * S5_K0_IN], k0_ref[j], preferred_element_type=F32)
             for j in range(S5_W // S5_K0_IN)]
    y_hat = jnp.concatenate(y_hat, axis=1)
    ridx = lax.broadcasted_iota(jnp.int32, (y_hat.shape[0], 1), 0)
    y_even = jnp.where(ridx == 0, yhat_ref[...], pltpu.roll(y_hat, 1, 0)) + jnp.concatenate(y_dir, axis=1)
    yhat_ref[...] = y_hat[y_hat.shape[0] - 1:, :]
    halves = ((y_even, u_e32), (jnp.concatenate(y_odd, axis=1), u_o32))
    for p, (y, u) in enumerate(halves):
        y = _gelu_tanh(y + d_ref[...] * u)
        gate = jnp.dot(y.astype(BF16), wglubf_ref[...], preferred_element_type=F32)
        y = y * (1.0 / (1.0 + jnp.exp(-gate)))
        for s in range(lane_slabs):
            par_ref[s, pl.ds(p, half, stride=2), :] = y[:, s * LANES:(s + 1) * LANES]
    y_all = jnp.concatenate([par_ref[s] for s in range(lane_slabs)], axis=1)
    o_ref[...] = (y_all * z_ref[...].astype(F32)).astype(o_ref.dtype)


def _s5_time_tile(rows_per_batch):
    for tseg in (44, 66, 22, 6, 2):
        t = 2 * S5_SEGS * tseg
        if rows_per_batch % t == 0 and (S5_SEGS * tseg) % 16 == 0:
            return t, tseg
    raise ValueError(f"no S5 time tile for {rows_per_batch}")


def _s5(proj, bsz, params, d, wglu, t, tseg, w_next):
    m = proj.shape[0]
    rows = t // 2
    blk = S5_F // S5_JB
    nt = m // bsz // t
    wrows = w_next.shape[0] // (bsz * nt)
    assert w_next.shape[0] % (bsz * nt) == 0 and wrows % 16 == 0
    wslice = pl.BlockSpec((wrows, w_next.shape[1]), lambda b, i: (b * nt + i, 0))
    ucol = (2 * RET_QK + 2 * RET_W) // S5_W
    const = lambda a: pl.BlockSpec(a.shape, lambda b, i: (0,) * a.ndim, pipeline_mode=pl.Buffered(1))
    tile = lambda col: pl.BlockSpec((t, S5_W), lambda b, i: (b * nt + i, col))
    d2 = d.reshape(1, S5_W)
    return pl.pallas_call(
        functools.partial(_s5_kernel, tseg=tseg, slab_group=8),
        grid=(bsz, nt),
        in_specs=[tile(ucol), tile(ucol + 1)] + [const(a) for a in params] + [const(d2), const(wglu), wslice],
        out_specs=[tile(0), wslice],
        out_shape=(jax.ShapeDtypeStruct((m, S5_W), BF16), jax.ShapeDtypeStruct(w_next.shape, BF16)),
        scratch_shapes=[pltpu.VMEM((S5_JB, 2 * S5_BLK_IN, 2 * blk), BF16),
                        pltpu.VMEM((S5_JB, 2 * S5_BLK_IN, blk), BF16), pltpu.VMEM((S5_JB, 2 * S5_BLK_IN, blk), BF16),
                        pltpu.VMEM((S5_W // S5_K0_IN, S5_K0_IN, S5_K0_IN), BF16),
                        pltpu.VMEM((S5_SLABS, tseg, LANES), F32), pltpu.VMEM((S5_SLABS, tseg, LANES), F32),
                        pltpu.VMEM((S5_W, S5_W), BF16)]
        + [pltpu.VMEM((S5_SLABS, rows, LANES), F32), pltpu.VMEM((S5_SLABS, rows, LANES), F32)]
        + [pltpu.VMEM((S5_SLABS, S5_SEGS, LANES), F32)] * 4
        + [pltpu.VMEM((1, S5_W), F32), pltpu.VMEM((S5_W // LANES, t, LANES), F32)],
        compiler_params=_cparams(("arbitrary", "arbitrary")),
        name="s5",
    )(proj, proj, *params, d2, wglu, w_next)


def _out_ab_kernel(lead_ref, xa_ref, xb_ref, xc_ref, oa_ref, ob_ref, w_ref, nw_ref, wg_ref,
                   h1_ref, xn_ref, gl_ref):
    t = pl.program_id(1)
    first = jnp.where(t == 0, lead_ref[...], xa_ref[...])
    h0 = (first, xb_ref[...], xc_ref[...])
    sls = [slice(r * CHUNK, (r + 1) * CHUNK) for r in range(ROW_CHUNKS)]
    accs = [jnp.dot(oa_ref[sl, :], w_ref[:RET_W, :], preferred_element_type=F32)
            + jnp.dot(ob_ref[sl, :], w_ref[RET_W:, :], preferred_element_type=F32) for sl in sls]
    h1s = [h0[r] + accs[r] for r in range(ROW_CHUNKS)]
    xns = [_rms_scale(h1, nw_ref[...]).astype(BF16) for h1 in h1s]
    gls = [jnp.dot(xn, wg_ref[...], preferred_element_type=F32) for xn in xns]
    for r, sl in enumerate(sls):
        h1_ref[sl, :] = h1s[r]
        xn_ref[sl, :] = xns[r]
        gl_ref[sl, :] = gls[r]


def _out_ab(x, lead, oa, ob, w, norm_w, wg, nt):
    bsz, _, d = x.shape
    tile = ROW_CHUNKS * CHUNK
    m = bsz * nt * tile
    row = lambda width: pl.BlockSpec((tile, width), lambda b, t: (b * nt + t, 0))
    const = lambda a: pl.BlockSpec(a.shape, lambda b, t: (0, 0))
    out = jax.ShapeDtypeStruct
    return pl.pallas_call(
        _out_ab_kernel,
        grid=(bsz, nt),
        in_specs=[pl.BlockSpec((CHUNK, d), lambda b, t: (0, 0))] + _chunk_specs(d, 1)
        + [row(RET_W), row(S5_W), const(w), pl.BlockSpec((1, d), lambda b, t: (0, 0)), const(wg)],
        out_specs=[row(d), row(d), row(GATE_PAD)],
        out_shape=(out((m, d), F32), out((m, d), BF16), out((m, GATE_PAD), F32)),
        compiler_params=_cparams(("parallel", "parallel")),
        name="out_proj_ab",
    )(lead, x, x, x, oa, ob, w, norm_w.reshape(1, d), wg)


LOG2E = 1.4426950408889634


def _split2(x):
    x1 = x.astype(BF16)
    return x1, (x - x1.astype(F32)).astype(BF16)


def _neg_abs(x):
    return -jnp.abs(x)


def _block_ref_rows(b, level):
    rows, width = b.shape
    half = 1 << level
    blk = 2 * half
    if blk >= 8:
        b3 = b.reshape(rows // blk, blk, width)
        ref = jnp.broadcast_to(b3[:, half - 1:half, :], b3.shape)
        return ref.reshape(rows, width)
    b3 = b.reshape(rows // 8, 8, width)
    sub = lax.broadcasted_iota(jnp.int32, b3.shape, 1)
    pick = lambda r: jnp.broadcast_to(b3[:, r:r + 1, :], b3.shape)
    if blk == 4:
        ref = jnp.where(sub < 4, pick(1), pick(5))
    else:
        ref = jnp.where(sub < 2, pick(0), jnp.where(sub < 4, pick(2), jnp.where(sub < 6, pick(4), pick(6))))
    return ref.reshape(rows, width)


def _gla_kernel(q_ref, k_ref, v_ref, z_ref, gl_ref, wg_ref, bg_ref, nw_ref, h_ref, wo_ref, fw_ref,
                y_ref, st_ref, mix_ref, wobf_ref, *, nc, nchunks):
    c = pl.program_id(0)
    n = jnp.minimum(c, nchunks - 1) % nc

    @pl.when(c == 0)
    def _():
        mix_ref[...] = jnp.zeros_like(mix_ref)
        wobf_ref[...] = wo_ref[...].astype(BF16)

    @pl.when(n == 0)
    def _():
        st_ref[...] = jnp.zeros_like(st_ref)

    ridx = lax.broadcasted_iota(jnp.int32, (CHUNK, 1), 0)
    ii = lax.broadcasted_iota(jnp.int32, (CHUNK, CHUNK), 0)
    jj = lax.broadcasted_iota(jnp.int32, (CHUNK, CHUNK), 1)
    pair_code = jnp.where(ii > jj, ii ^ jj, 0)
    eye = ii == jj
    tri = (ii >= jj).astype(BF16)

    gl1, gl2 = _split2(gl_ref[:, :GLA_RANK])
    wg1, wg2 = _split2(wg_ref[...])
    x = jnp.dot(jnp.concatenate([gl1, gl2, gl1], axis=1), jnp.concatenate([wg1, wg1, wg2], axis=0),
                preferred_element_type=F32) + bg_ref[...]
    log_a = (jnp.minimum(x, 0.0) - jnp.log(1.0 + jnp.exp(-jnp.abs(x)))) * (LOG2E / GLA_TAU)
    log_a = jnp.where(ridx >= jnp.where(n > 0, 0, PAD), log_a, 0.0)
    g1, g2 = _split2(log_a)
    b_all = jnp.dot(jnp.concatenate([tri, tri], axis=1), jnp.concatenate([g1, g2], axis=0),
                    preferred_element_type=F32)
    odd = (ridx & 1) == 1
    h2 = h_ref[...]

    def finish(heads, qs, ks, bs, scores):
        for i, h in enumerate(heads):
            vc = slice(h * GLA_DV, (h + 1) * GLA_DV)
            q, k, b = qs[i], ks[i], bs[i]
            v = v_ref[:, vc]
            b_last = b[CHUNK - 1:CHUNK, :]
            o = jnp.dot(scores[i].astype(BF16), v, preferred_element_type=F32)
            st = st_ref[h]
            qe = (q * jnp.exp2(b)).astype(BF16)
            o = o + lax.dot_general(qe, st.astype(BF16), (((1,), (1,)), ((), ())), preferred_element_type=F32)
            ke = (k * jnp.exp2(b_last - b)).astype(BF16)
            kv_t = lax.dot_general(v, ke, (((0,), (0,)), ((), ())), preferred_element_type=F32)
            st_ref[h] = st * jnp.exp2(b_last) + kv_t
            mix_ref[:, vc] = (_rms_scale(o, nw_ref[:, vc]) * z_ref[:, vc].astype(F32)).astype(BF16)

    pending = []
    for h0 in range(0, GLA_HEADS, 2):
        heads = (h0, h0 + 1)
        qs, ks, bs, scores = [], [], [], []
        for h in heads:
            kc = slice(h * GLA_DK, (h + 1) * GLA_DK)
            vc = slice(h * GLA_DV, (h + 1) * GLA_DV)
            h2 = h2 + jnp.dot(mix_ref[:, vc], wobf_ref[vc, :], preferred_element_type=F32)
            qs.append(q_ref[:, kc].astype(F32))
            ks.append(k_ref[:, kc].astype(F32))
            bs.append(b_all[:, kc])
            scores.append(jnp.where(eye, jnp.sum(qs[-1] * ks[-1], axis=1, keepdims=True), 0.0))
        for lv in range(GLA_LEVELS):
            half = 1 << lv
            if lv == GLA_LEVELS // 2 and pending:
                finish(*pending.pop())
            zz = []
            for q, k, b in zip(qs, ks, bs):
                if lv == 0:
                    expo = jnp.where(odd, b - pltpu.roll(b, 1, 0), 0.0)
                else:
                    expo = _neg_abs(b - _block_ref_rows(b, lv))
                if half >= 8:
                    sel = jnp.concatenate([(q if r % 2 else k)[r * half:(r + 1) * half]
                                           for r in range(CHUNK // half)], axis=0)
                else:
                    sel = jnp.where(((ridx >> lv) & 1) == 1, q, k)
                zz.append((sel * jnp.exp2(expo)).astype(BF16))
            z2 = jnp.concatenate(zz, axis=0)
            gram = lax.dot_general(z2, z2, (((1,), (1,)), ((), ())), preferred_element_type=F32)
            mask = (pair_code >> lv) == 1
            for i in range(2):
                blk = gram[i * CHUNK:(i + 1) * CHUNK, i * CHUNK:(i + 1) * CHUNK]
                scores[i] = scores[i] + jnp.where(mask, blk, 0.0)
        pending.append((heads, qs, ks, bs, scores))
    finish(*pending.pop())

    y_ref[...] = _rms_scale(h2, fw_ref[...])


def _gla(proj, glow, w_gate, b_gate, norm_w, h1, w_out, final_w, bsz, nc):
    m, d = h1.shape
    nchunks = m // CHUNK
    const = lambda *shape: pl.BlockSpec(shape, lambda c: (0,) * len(shape))
    cur = lambda width, col: pl.BlockSpec((CHUNK, width), lambda c: (jnp.minimum(c, nchunks - 1), col))
    prev = lambda c: jnp.maximum(c - 1, 0)
    return pl.pallas_call(
        functools.partial(_gla_kernel, nc=nc, nchunks=nchunks),
        grid=(nchunks + 1,),
        in_specs=[cur(GLA_QK, 0), cur(GLA_QK, 1), cur(GLA_W, 1), cur(GLA_W, 2), cur(GATE_PAD, 0),
                  const(GLA_RANK, GLA_QK), const(1, GLA_QK), const(1, GLA_W),
                  pl.BlockSpec((CHUNK, d), lambda c: (prev(c), 0)),
                  pl.BlockSpec((None, GLA_W, d), lambda c: (0, 0, 0), pipeline_mode=pl.Buffered(1)),
                  const(1, d)],
        out_specs=pl.BlockSpec((None, CHUNK, d),
                               lambda c: (prev(c) // nc, jnp.maximum(prev(c) % nc - 1, 0), 0)),
        out_shape=jax.ShapeDtypeStruct((bsz, (nc - 1) * CHUNK, d), F32),
        scratch_shapes=[pltpu.VMEM((GLA_HEADS, GLA_DV, GLA_DK), F32), pltpu.VMEM((CHUNK, GLA_W), BF16),
                        pltpu.VMEM((GLA_W, d), BF16)],
        compiler_params=_cparams(("arbitrary",)),
        name="gla_out",
    )(proj, proj, proj, proj, glow, w_gate.astype(F32), b_gate.reshape(1, GLA_QK).astype(F32),
      norm_w.reshape(1, GLA_W).astype(F32), h1, w_out, final_w.reshape(1, d).astype(F32))


def _rope_tables(bsz, rows_per_batch):
    pos = np.maximum(np.arange(rows_per_batch, dtype=np.float64) - PAD, 0.0)
    inv_freq = np.power(ROPE_BASE, -np.arange(0, RET_DK, 2, dtype=np.float64) / RET_DK)
    ang = pos[:, None] * inv_freq[None, :]
    cos, sin = np.cos(ang), np.sin(ang)
    cos2 = np.tile(np.concatenate([cos, cos], axis=1), (bsz, 1)).astype(np.float32)
    sin2 = np.tile(np.concatenate([-sin, sin], axis=1), (bsz, 1)).astype(np.float32)
    return jnp.asarray(cos2), jnp.asarray(sin2)


def kernel(x, meta, norm_ab_w, w_in_ab, ret_norm_w, s5_lam_re, s5_lam_im, s5_log_dt, s5_b_re, s5_b_im,
           s5_c_re, s5_c_im, s5_d, s5_w_glu, w_out_ab, norm_c_w, w_in_c, gla_w_gate, gla_b_gate,
           gla_norm_w, w_out_c, final_norm_w):
    bsz, seq, d = x.shape
    assert seq % CHUNK == 0 and w_in_ab.shape[0] == 1 and w_in_c.shape[0] == 1
    rpb = seq + CHUNK
    nc = rpb // CHUNK
    assert nc % ROW_CHUNKS == 0
    nt = nc // ROW_CHUNKS
    m = bsz * rpb
    tm = _pick_tile(m, (2112, 1408, 768, 384, 256, 128))
    lead = jnp.concatenate([jnp.zeros((PAD, d), x.dtype), meta.astype(x.dtype)], axis=0)

    xn0 = _embed_norm(x, lead, norm_ab_w[0], nt)
    plain, silu = ("plain", 1.0), ("silu", 1.0)
    modes_ab = [("rope", 1.0), ("rope", RET_DK ** -0.5), plain, plain, silu, silu, plain, silu]
    proj0 = _in_proj(xn0, w_in_ab, modes_ab, tm, rope=_rope_tables(bsz, rpb))
    o_a = _retention(proj0, ret_norm_w[0], bsz, nt)
    t, tseg = _s5_time_tile(rpb)
    s5_params = _s5_params(s5_lam_re[0], s5_lam_im[0], s5_log_dt[0], s5_b_re[0], s5_b_im[0],
                           s5_c_re[0], s5_c_im[0])
    o_b, w_out_ab_bf = _s5(proj0, bsz, s5_params, s5_d[0], s5_w_glu[0].astype(F32), t, tseg,
                           w_out_ab[0].astype(F32))

    w_in_c_t = jnp.swapaxes(w_in_c, 1, 2)
    n_main = 2 * GLA_QK + 2 * GLA_W
    wg = jnp.pad(w_in_c_t[0, n_main:, :], ((0, GATE_PAD - GLA_RANK), (0, 0))).astype(BF16).T
    h1, xn1, glow = _out_ab(x, lead, o_a, o_b, w_out_ab_bf, norm_c_w[0], wg, nt)

    modes_c = [("plain", GLA_DK ** -0.5), plain, plain, plain, silu, silu]
    proj1 = _in_proj(xn1, w_in_c_t, modes_c, tm, w_is_transposed=True)
    return _gla(proj1, glow, gla_w_gate[0], gla_b_gate[0], gla_norm_w[0], h1, w_out_c,
                final_norm_w, bsz, nc)
```

```python
import functools
import math

import numpy as np
import jax
import jax.numpy as jnp
from jax import lax
from jax.experimental import pallas as pl
from jax.experimental.pallas import tpu as pltpu

F32 = jnp.float32
BF16 = jnp.bfloat16

N_META = 16
LANES = 128
SUBLANES = 8
CHUNK = 128
PAD = CHUNK - N_META
EPS = 1e-6
ROW_CHUNKS = 3

RET_HEADS = 8
RET_DK = 128
RET_DV = 256
RET_QK = RET_HEADS * RET_DK
RET_W = RET_HEADS * RET_DV
ROPE_BASE = 10000.0

S5_W = 1024
S5_GH = 16
S5_G = 64
S5_P = 64
S5_F = S5_G * S5_P
S5_SLABS = S5_F // LANES
MXU_TILE = 256
S5_JB = 8
S5_BLK_IN = S5_W // S5_JB
S5_K0_IN = MXU_TILE
S5_SEGS = SUBLANES
S5_SCAN_UNROLL = 4

GLA_HEADS = 4
GLA_DK = 256
GLA_DV = 512
GLA_QK = GLA_HEADS * GLA_DK
GLA_W = GLA_HEADS * GLA_DV
GLA_RANK = 16
GLA_TAU = 16.0
GLA_LEVELS = 7
GATE_PAD = LANES

PROJ_TN = 1024
PROJ_ROW_SPLIT = 4
VMEM_LIMIT = 56 * 1024 * 1024
S5_VMEM_LIMIT = 60 * 1024 * 1024


def _cparams(sem, vmem_limit=VMEM_LIMIT):
    return pltpu.CompilerParams(dimension_semantics=sem, vmem_limit_bytes=vmem_limit)


def _silu(x):
    return x * (1.0 / (1.0 + jnp.exp(-x)))


def _pick_tile(n, candidates):
    for c in candidates:
        if n % c == 0:
            return c
    raise ValueError(f"no tile for {n}")


def _rms_scale(x, w):
    return x * lax.rsqrt(jnp.mean(x * x, axis=-1, keepdims=True) + EPS) * w


def _chunk_specs(d, n_before):
    return [pl.BlockSpec((None, CHUNK, d),
                         lambda b, t, k=k: (b, jnp.maximum(ROW_CHUNKS * t + k - n_before, 0), 0))
            for k in range(ROW_CHUNKS)]


def _embed_norm_kernel(lead_ref, xa_ref, xb_ref, xc_ref, w_ref, o_ref):
    t = pl.program_id(1)
    first = jnp.where(t == 0, lead_ref[...], xa_ref[...])
    for r, rows in enumerate((first, xb_ref[...], xc_ref[...])):
        o_ref[r * CHUNK:(r + 1) * CHUNK, :] = _rms_scale(rows, w_ref[...]).astype(o_ref.dtype)


def _embed_norm(x, lead, w, nt):
    bsz, _, d = x.shape
    tile = ROW_CHUNKS * CHUNK
    return pl.pallas_call(
        _embed_norm_kernel,
        grid=(bsz, nt),
        in_specs=[pl.BlockSpec((CHUNK, d), lambda b, t: (0, 0))] + _chunk_specs(d, 1)
        + [pl.BlockSpec((1, d), lambda b, t: (0, 0))],
        out_specs=pl.BlockSpec((tile, d), lambda b, t: (b * nt + t, 0)),
        out_shape=jax.ShapeDtypeStruct((bsz * nt * tile, d), BF16),
        compiler_params=_cparams(("parallel", "parallel")),
        name="embed_norm",
    )(lead, x, x, x, w.reshape(1, d))


def _rope_rows(a, cos, sin, scale):
    outs = []
    for h in range(a.shape[1] // RET_DK):
        blk = a[:, h * RET_DK:(h + 1) * RET_DK]
        r = blk * cos + pltpu.roll(blk, RET_DK // 2, 1) * sin
        outs.append(r * scale if scale != 1.0 else r)
    return jnp.concatenate(outs, axis=1)


def _in_proj_kernel(*refs, modes, has_rope, w_is_transposed):
    if has_rope:
        x_ref, w_ref, cos_ref, sin_ref, o_ref, wbf_ref = refs
    else:
        x_ref, w_ref, o_ref, wbf_ref = refs
    j = pl.program_id(0)

    @pl.when(pl.program_id(1) == 0)
    def _():
        wbf_ref[...] = w_ref[...].astype(BF16)

    contract = (((1,), (1 if w_is_transposed else 0,)), ((), ()))
    for mode in sorted(set(modes)):
        cond = functools.reduce(jnp.logical_or, [j == jj for jj, mm in enumerate(modes) if mm == mode])

        @pl.when(cond)
        def _(mode=mode):
            kind, scale = mode
            rows = x_ref.shape[0] // PROJ_ROW_SPLIT
            sls = [slice(r * rows, (r + 1) * rows) for r in range(PROJ_ROW_SPLIT)]
            accs = [lax.dot_general(x_ref[sl, :], wbf_ref[...], contract, preferred_element_type=F32)
                    for sl in sls]
            for sl, acc in zip(sls, accs):
                if kind == "rope":
                    acc = _rope_rows(acc, cos_ref[sl, :], sin_ref[sl, :], scale)
                elif kind == "silu":
                    acc = _silu(acc)
                elif scale != 1.0:
                    acc = acc * scale
                o_ref[sl, :] = acc.astype(o_ref.dtype)


def _in_proj(x, w, modes, tm, rope=None, w_is_transposed=False):
    m, k = x.shape
    nt = len(modes)
    assert m % tm == 0 and tm % (PROJ_ROW_SPLIT * 16) == 0
    if w_is_transposed:
        w_spec = pl.BlockSpec((None, PROJ_TN, k), lambda j, i: (0, j, 0))
        w_scratch = pltpu.VMEM((PROJ_TN, k), BF16)
    else:
        w_spec = pl.BlockSpec((None, k, PROJ_TN), lambda j, i: (0, 0, j))
        w_scratch = pltpu.VMEM((k, PROJ_TN), BF16)
    in_specs = [pl.BlockSpec((tm, k), lambda j, i: (i, 0)), w_spec]
    args = [x, w]
    if rope is not None:
        in_specs += [pl.BlockSpec((tm, RET_DK), lambda j, i: (i, 0))] * 2
        args += list(rope)
    return pl.pallas_call(
        functools.partial(_in_proj_kernel, modes=tuple(modes), has_rope=rope is not None,
                          w_is_transposed=w_is_transposed),
        grid=(nt, m // tm),
        in_specs=in_specs,
        out_specs=pl.BlockSpec((tm, PROJ_TN), lambda j, i: (i, j)),
        out_shape=jax.ShapeDtypeStruct((m, nt * PROJ_TN), BF16),
        scratch_shapes=[w_scratch],
        compiler_params=_cparams(("arbitrary", "arbitrary")),
        name="in_proj",
    )(*args)


def _ret_log_decay(h):
    return math.log1p(-(2.0 ** (-5.0 - h)))


def _retention_kernel(q_ref, k_ref, v_ref, z_ref, nw_ref, o_ref, s_ref, dec_ref, qs_ref, ks_ref):
    first_step = (pl.program_id(0) == 0) & (pl.program_id(1) == 0)

    @pl.when(first_step)
    def _():
        ii = lax.broadcasted_iota(jnp.int32, (CHUNK, CHUNK), 0)
        jj = lax.broadcasted_iota(jnp.int32, (CHUNK, CHUNK), 1)
        diff = (ii - jj).astype(F32)
        row = ii.astype(F32)
        for h in range(RET_HEADS):
            lg = _ret_log_decay(h)
            dec_ref[h] = jnp.where(ii >= jj, jnp.exp(lg * jnp.maximum(diff, 0.0)), 0.0)
            qs_ref[h] = jnp.exp(lg * (row + 1.0))
            ks_ref[h] = jnp.exp(lg * (CHUNK - 1.0 - row))

    @pl.when(pl.program_id(1) == 0)
    def _():
        s_ref[...] = jnp.zeros_like(s_ref)

    heads = range(RET_HEADS)
    vcs = [slice(h * RET_DV, (h + 1) * RET_DV) for h in heads]
    nt_dims = (((1,), (1,)), ((), ()))
    tn_dims = (((0,), (0,)), ((), ()))
    states = [s_ref[h] for h in heads]
    for r in range(q_ref.shape[0] // CHUNK):
        rows = slice(r * CHUNK, (r + 1) * CHUNK)
        qs = [q_ref[rows, h * RET_DK:(h + 1) * RET_DK] for h in heads]
        ks = [k_ref[rows, h * RET_DK:(h + 1) * RET_DK] for h in heads]
        vs = [v_ref[rows, vc] for vc in vcs]
        scores = [lax.dot_general(qs[h], ks[h], nt_dims, preferred_element_type=F32) for h in heads]
        kvs = [lax.dot_general((ks[h].astype(F32) * ks_ref[h]).astype(BF16), vs[h], tn_dims,
                               preferred_element_type=F32) for h in heads]
        lhs = [jnp.concatenate([(scores[h] * dec_ref[h]).astype(BF16),
                                (qs[h].astype(F32) * qs_ref[h]).astype(BF16)], axis=1) for h in heads]
        outs = [jnp.dot(lhs[h], jnp.concatenate([vs[h], states[h].astype(BF16)], axis=0),
                        preferred_element_type=F32) for h in heads]
        states = [states[h] * math.exp(_ret_log_decay(h) * CHUNK) + kvs[h] for h in heads]
        for h in heads:
            o_ref[rows, vcs[h]] = (_rms_scale(outs[h], nw_ref[:, vcs[h]])
                                   * z_ref[rows, vcs[h]].astype(F32)).astype(o_ref.dtype)
    for h in heads:
        s_ref[h] = states[h]


def _retention(proj, norm_w, bsz, nt):
    m = proj.shape[0]
    tile = ROW_CHUNKS * CHUNK
    tab = pltpu.VMEM((RET_HEADS, CHUNK, CHUNK), F32)
    row = lambda width, col: pl.BlockSpec((tile, width), lambda b, t: (b * nt + t, col))
    return pl.pallas_call(
        _retention_kernel,
        grid=(bsz, nt),
        in_specs=[row(RET_QK, 0), row(RET_QK, 1), row(RET_W, 1), row(RET_W, 2),
                  pl.BlockSpec((1, RET_W), lambda b, t: (0, 0))],
        out_specs=row(RET_W, 0),
        out_shape=jax.ShapeDtypeStruct((m, RET_W), BF16),
        scratch_shapes=[pltpu.VMEM((RET_HEADS, RET_DK, RET_DV), F32), tab, tab, tab],
        compiler_params=_cparams(("arbitrary", "arbitrary")),
        name="retention",
    )(proj, proj, proj, proj, norm_w.reshape(1, RET_W))


def _s5_prep(lre_ref, lim_ref, ldt_ref, bre_ref, bim_ref, cre_ref, cim_ref,
             pre_ref, pim_ref, w2_ref, wcre_ref, wcim_ref, k0_ref):
    lre = lre_ref[...]
    lim = lim_ref[...]
    dt = jnp.exp(ldt_ref[...])
    tseg = pre_ref.shape[1]
    n = 2.0 * (lax.broadcasted_iota(jnp.int32, (tseg, 1), 0).astype(F32) + 1.0)
    mag = jnp.exp(n * (lre * dt))
    ang = n * (lim * dt)
    p_re = mag * jnp.cos(ang)
    p_im = mag * jnp.sin(ang)
    for sl in range(S5_SLABS):
        pre_ref[sl] = p_re[:, sl * LANES:(sl + 1) * LANES]
        pim_ref[sl] = p_im[:, sl * LANES:(sl + 1) * LANES]
    mag1 = jnp.exp(lre * dt)
    a_re = mag1 * jnp.cos(lim * dt)
    a_im = mag1 * jnp.sin(lim * dt)
    den = lre * lre + lim * lim
    nr = a_re - 1.0
    f_re = (nr * lre + a_im * lim) / den
    f_im = (a_im * lre - nr * lim) / den
    b_re = bre_ref[...]
    b_im = bim_ref[...]
    bb_re = f_re * b_re - f_im * b_im
    bb_im = f_re * b_im + f_im * b_re
    bba_re = a_re * bb_re - a_im * bb_im
    bba_im = a_re * bb_im + a_im * bb_re
    c_re = cre_ref[...]
    c_im = cim_ref[...]
    ca_re = c_re * a_re - c_im * a_im
    ca_im = c_re * a_im + c_im * a_re
    blk = S5_F // S5_JB
    rows = lax.broadcasted_iota(jnp.int32, (S5_BLK_IN, blk), 0)
    cols = lax.broadcasted_iota(jnp.int32, (S5_BLK_IN, blk), 1)
    same_group = (rows // S5_GH) == (cols // S5_P)

    def block_diag(a, j):
        piece = a[:, j * blk:(j + 1) * blk]
        return jnp.where(same_group, jnp.concatenate([piece] * (S5_G // S5_JB), axis=0), 0.0)

    k0_ref[...] = jnp.zeros_like(k0_ref)
    per_k0 = S5_K0_IN // S5_BLK_IN
    for j in range(S5_JB):
        bd_bre, bd_bim = block_diag(bb_re, j), block_diag(bb_im, j)
        bd_cre, bd_cim = block_diag(c_re, j), block_diag(c_im, j)
        w2_ref[j, :S5_BLK_IN, :blk] = block_diag(bba_re, j).astype(BF16)
        w2_ref[j, :S5_BLK_IN, blk:] = block_diag(bba_im, j).astype(BF16)
        w2_ref[j, S5_BLK_IN:, :blk] = bd_bre.astype(BF16)
        w2_ref[j, S5_BLK_IN:, blk:] = bd_bim.astype(BF16)
        wcre_ref[j, :S5_BLK_IN, :] = bd_cre.astype(BF16)
        wcim_ref[j, :S5_BLK_IN, :] = bd_cim.astype(BF16)
        wcre_ref[j, S5_BLK_IN:, :] = block_diag(ca_re, j).astype(BF16)
        wcim_ref[j, S5_BLK_IN:, :] = block_diag(ca_im, j).astype(BF16)
        nt_dims = (((1,), (1,)), ((), ()))
        k0 = (lax.dot_general(bd_bre, bd_cre, nt_dims, preferred_element_type=F32)
              - lax.dot_general(bd_bim, bd_cim, nt_dims, preferred_element_type=F32))
        r0 = (j % per_k0) * S5_BLK_IN
        k0_ref[j // per_k0, r0:r0 + S5_BLK_IN, r0:r0 + S5_BLK_IN] = k0.astype(BF16)


def _s5_params(lam_re, lam_im, log_dt, b_re, b_im, c_re, c_im):
    flat = lambda a: a.reshape(1, S5_F)
    ldt = jnp.broadcast_to(log_dt[:, None], (S5_G, S5_P))
    bt = lambda a: a.reshape(S5_F, S5_GH).T
    ct = lambda a: a.transpose(1, 0, 2).reshape(S5_GH, S5_F)
    return (flat(lam_re), flat(lam_im), flat(ldt), bt(b_re), bt(b_im), ct(c_re.astype(F32)), ct(c_im.astype(F32)))


def _gelu_tanh(x):
    return 0.5 * x * (1.0 + jnp.tanh(math.sqrt(2.0 / math.pi) * (x + 0.044715 * (x * x * x))))


def _s5_kernel(u_ref, z_ref, lre_ref, lim_ref, ldt_ref, bpre_ref, bpim_ref, cpre_ref, cpim_ref, d_ref, wglu_ref,
               wnext_ref, wlast_ref, o_ref, wnextbf_ref, wlastbf_ref, w2_ref, wcre_ref, wcim_ref, k0_ref, pre_ref, pim_ref, wglubf_ref,
               bure_ref, buim_ref, cre_ref, cim_ref, xinre_ref, xinim_ref, yhat_ref, par_ref,
               *, tseg, slab_group):
    ci = pl.program_id(1)
    wnextbf_ref[...] = wnext_ref[...].astype(BF16)
    wlastbf_ref[...] = wlast_ref[...].astype(BF16)

    @pl.when((pl.program_id(0) == 0) & (ci == 0))
    def _():
        _s5_prep(lre_ref, lim_ref, ldt_ref, bpre_ref, bpim_ref, cpre_ref, cpim_ref,
                 pre_ref, pim_ref, w2_ref, wcre_ref, wcim_ref, k0_ref)
        wglubf_ref[...] = wglu_ref[...].astype(BF16)

    @pl.when(ci == 0)
    def _():
        cre_ref[...] = jnp.zeros_like(cre_ref)
        cim_ref[...] = jnp.zeros_like(cim_ref)
        yhat_ref[...] = jnp.zeros_like(yhat_ref)

    half = u_ref.shape[0] // 2
    lane_slabs = S5_W // LANES
    u_all = u_ref[...].astype(F32)
    for s in range(lane_slabs):
        par_ref[s] = u_all[:, s * LANES:(s + 1) * LANES]
    parity = lambda p: jnp.concatenate([par_ref[s, pl.ds(p, half, stride=2), :] for s in range(lane_slabs)], axis=1)
    u_e32, u_o32 = parity(0), parity(1)
    u_e, u_o = u_e32.astype(BF16), u_o32.astype(BF16)
    spb = S5_SLABS // S5_JB
    for j in range(S5_JB):
        blk = slice(j * S5_BLK_IN, (j + 1) * S5_BLK_IN)
        res = jnp.dot(jnp.concatenate([u_e[:, blk], u_o[:, blk]], axis=1), w2_ref[j], preferred_element_type=F32)
        for s in range(spb):
            bure_ref[j * spb + s] = res[:, s * LANES:(s + 1) * LANES]
            buim_ref[j * spb + s] = res[:, (spb + s) * LANES:(spb + s + 1) * LANES]

    for g0 in range(0, S5_SLABS, slab_group):
        slabs = list(range(g0, g0 + slab_group))
        a_re = [jnp.broadcast_to(pre_ref[sl, 0:1, :], (S5_SEGS, LANES)) for sl in slabs]
        a_im = [jnp.broadcast_to(pim_ref[sl, 0:1, :], (S5_SEGS, LANES)) for sl in slabs]

        def scan_body(tau, carry, slabs=slabs, a_re=a_re, a_im=a_im):
            xr, xi = carry
            nr, ni = [], []
            for n, sl in enumerate(slabs):
                rows = pl.ds(tau, S5_SEGS, stride=tseg)
                br = bure_ref[sl, rows, :]
                bi = buim_ref[sl, rows, :]
                r = a_re[n] * xr[n] - a_im[n] * xi[n] + br
                i = a_re[n] * xi[n] + a_im[n] * xr[n] + bi
                bure_ref[sl, rows, :] = r
                buim_ref[sl, rows, :] = i
                nr.append(r)
                ni.append(i)
            return tuple(nr), tuple(ni)

        zero = tuple(jnp.zeros((S5_SEGS, LANES), F32) for _ in slabs)
        end_re, end_im = lax.fori_loop(0, tseg, scan_body, (zero, zero),
                                       unroll=S5_SCAN_UNROLL if tseg % S5_SCAN_UNROLL == 0 else 1)

        for n, sl in enumerate(slabs):
            at_re = pre_ref[sl, tseg - 1:tseg, :]
            at_im = pim_ref[sl, tseg - 1:tseg, :]
            xr = cre_ref[sl, 0:1, :]
            xi = cim_ref[sl, 0:1, :]
            rows_re, rows_im = [], []
            for s in range(S5_SEGS):
                rows_re.append(xr)
                rows_im.append(xi)
                er = end_re[n][s:s + 1]
                ei = end_im[n][s:s + 1]
                xr, xi = er + at_re * xr - at_im * xi, ei + at_re * xi + at_im * xr
            cre_ref[sl] = jnp.broadcast_to(xr, (S5_SEGS, LANES))
            cim_ref[sl] = jnp.broadcast_to(xi, (S5_SEGS, LANES))
            xinre_ref[sl] = jnp.concatenate(rows_re, axis=0)
            xinim_ref[sl] = jnp.concatenate(rows_im, axis=0)

    def fix_body(sl, carry):
        xin_re = xinre_ref[sl]
        xin_im = xinim_ref[sl]
        for tau in range(tseg):
            rows = pl.ds(tau, S5_SEGS, stride=tseg)
            pr = pre_ref[sl, tau:tau + 1, :]
            pi = pim_ref[sl, tau:tau + 1, :]
            bure_ref[sl, rows, :] = bure_ref[sl, rows, :] + (pr * xin_re - pi * xin_im)
            buim_ref[sl, rows, :] = buim_ref[sl, rows, :] + (pr * xin_im + pi * xin_re)
        return carry

    lax.fori_loop(0, S5_SLABS, fix_body, 0)

    nt_dims = (((1,), (1,)), ((), ()))
    y_odd, y_hat = [], []
    for j in range(S5_JB):
        x_re = jnp.concatenate([bure_ref[j * spb + s] for s in range(spb)], axis=1).astype(BF16)
        x_im = jnp.concatenate([buim_ref[j * spb + s] for s in range(spb)], axis=1).astype(BF16)
        both = (lax.dot_general(x_re, wcre_ref[j], nt_dims, preferred_element_type=F32)
                - lax.dot_general(x_im, wcim_ref[j], nt_dims, preferred_element_type=F32))
        y_odd.append(both[:, :S5_BLK_IN])
        y_hat.append(both[:, S5_BLK_IN:])
    y_dir = [jnp.dot(u_e[:, j * S5_K0_IN:(j + 1) * S5_K0_IN], k0_ref[j], preferred_element_type=F32)
             for j in range(S5_W // S5_K0_IN)]
    y_hat = jnp.concatenate(y_hat, axis=1)
    ridx = lax.broadcasted_iota(jnp.int32, (y_hat.shape[0], 1), 0)
    y_even = jnp.where(ridx == 0, yhat_ref[...], pltpu.roll(y_hat, 1, 0)) + jnp.concatenate(y_dir, axis=1)
    yhat_ref[...] = y_hat[y_hat.shape[0] - 1:, :]
    halves = ((y_even, u_e32), (jnp.concatenate(y_odd, axis=1), u_o32))
    for p, (y, u) in enumerate(halves):
        y = _gelu_tanh(y + d_ref[...] * u)
        gate = jnp.dot(y.astype(BF16), wglubf_ref[...], preferred_element_type=F32)
        y = y * (1.0 / (1.0 + jnp.exp(-gate)))
        for s in range(lane_slabs):
            par_ref[s, pl.ds(p, half, stride=2), :] = y[:, s * LANES:(s + 1) * LANES]
    y_all = jnp.concatenate([par_ref[s] for s in range(lane_slabs)], axis=1)
    o_ref[...] = (y_all * z_ref[...].astype(F32)).astype(o_ref.dtype)


def _s5_time_tile(rows_per_batch):
    for tseg in (44, 66, 22, 6, 2):
        t = 2 * S5_SEGS * tseg
        if rows_per_batch % t == 0 and (S5_SEGS * tseg) % 16 == 0:
            return t, tseg
    raise ValueError(f"no S5 time tile for {rows_per_batch}")


def _s5(proj, bsz, params, d, wglu, t, tseg, w_next, w_last):
    m = proj.shape[0]
    rows = t // 2
    blk = S5_F // S5_JB
    nt = m // bsz // t

    def wslice(a):
        fits = [n for n in range(1, bsz * nt + 1) if a.shape[0] % n == 0 and (a.shape[0] // n) % 16 == 0]
        nblk = max(fits)
        return pl.BlockSpec((a.shape[0] // nblk, a.shape[1]), lambda b, i: (jnp.minimum(b * nt + i, nblk - 1), 0))
    ucol = (2 * RET_QK + 2 * RET_W) // S5_W
    const = lambda a: pl.BlockSpec(a.shape, lambda b, i: (0,) * a.ndim, pipeline_mode=pl.Buffered(1))
    tile = lambda col: pl.BlockSpec((t, S5_W), lambda b, i: (b * nt + i, col))
    d2 = d.reshape(1, S5_W)
    return pl.pallas_call(
        functools.partial(_s5_kernel, tseg=tseg, slab_group=8),
        grid=(bsz, nt),
        in_specs=[tile(ucol), tile(ucol + 1)] + [const(a) for a in params] + [const(d2), const(wglu), wslice(w_next), wslice(w_last)],
        out_specs=[tile(0), wslice(w_next), wslice(w_last)],
        out_shape=(jax.ShapeDtypeStruct((m, S5_W), BF16), jax.ShapeDtypeStruct(w_next.shape, BF16),
                   jax.ShapeDtypeStruct(w_last.shape, BF16)),
        scratch_shapes=[pltpu.VMEM((S5_JB, 2 * S5_BLK_IN, 2 * blk), BF16),
                        pltpu.VMEM((S5_JB, 2 * S5_BLK_IN, blk), BF16), pltpu.VMEM((S5_JB, 2 * S5_BLK_IN, blk), BF16),
                        pltpu.VMEM((S5_W // S5_K0_IN, S5_K0_IN, S5_K0_IN), BF16),
                        pltpu.VMEM((S5_SLABS, tseg, LANES), F32), pltpu.VMEM((S5_SLABS, tseg, LANES), F32),
                        pltpu.VMEM((S5_W, S5_W), BF16)]
        + [pltpu.VMEM((S5_SLABS, rows, LANES), F32), pltpu.VMEM((S5_SLABS, rows, LANES), F32)]
        + [pltpu.VMEM((S5_SLABS, S5_SEGS, LANES), F32)] * 4
        + [pltpu.VMEM((1, S5_W), F32), pltpu.VMEM((S5_W // LANES, t, LANES), F32)],
        compiler_params=_cparams(("arbitrary", "arbitrary"), S5_VMEM_LIMIT),
        name="s5",
    )(proj, proj, *params, d2, wglu, w_next, w_last)


def _out_ab_kernel(lead_ref, xa_ref, xb_ref, xc_ref, oa_ref, ob_ref, w_ref, nw_ref, wg_ref,
                   h1_ref, xn_ref, gl_ref):
    t = pl.program_id(1)
    first = jnp.where(t == 0, lead_ref[...], xa_ref[...])
    h0 = (first, xb_ref[...], xc_ref[...])
    sls = [slice(r * CHUNK, (r + 1) * CHUNK) for r in range(ROW_CHUNKS)]
    accs = [jnp.dot(oa_ref[sl, :], w_ref[:RET_W, :], preferred_element_type=F32)
            + jnp.dot(ob_ref[sl, :], w_ref[RET_W:, :], preferred_element_type=F32) for sl in sls]
    h1s = [h0[r] + accs[r] for r in range(ROW_CHUNKS)]
    xns = [_rms_scale(h1, nw_ref[...]).astype(BF16) for h1 in h1s]
    gls = [jnp.dot(xn, wg_ref[...], preferred_element_type=F32) for xn in xns]
    for r, sl in enumerate(sls):
        h1_ref[sl, :] = h1s[r]
        xn_ref[sl, :] = xns[r]
        gl_ref[sl, :] = gls[r]


def _out_ab(x, lead, oa, ob, w, norm_w, wg, nt):
    bsz, _, d = x.shape
    tile = ROW_CHUNKS * CHUNK
    m = bsz * nt * tile
    row = lambda width: pl.BlockSpec((tile, width), lambda b, t: (b * nt + t, 0))
    const = lambda a: pl.BlockSpec(a.shape, lambda b, t: (0, 0))
    out = jax.ShapeDtypeStruct
    return pl.pallas_call(
        _out_ab_kernel,
        grid=(bsz, nt),
        in_specs=[pl.BlockSpec((CHUNK, d), lambda b, t: (0, 0))] + _chunk_specs(d, 1)
        + [row(RET_W), row(S5_W), const(w), pl.BlockSpec((1, d), lambda b, t: (0, 0)), const(wg)],
        out_specs=[row(d), row(d), row(GATE_PAD)],
        out_shape=(out((m, d), F32), out((m, d), BF16), out((m, GATE_PAD), F32)),
        compiler_params=_cparams(("parallel", "parallel")),
        name="out_proj_ab",
    )(lead, x, x, x, oa, ob, w, norm_w.reshape(1, d), wg)


LOG2E = 1.4426950408889634


def _split2(x):
    x1 = x.astype(BF16)
    return x1, (x - x1.astype(F32)).astype(BF16)


def _neg_abs(x):
    return -jnp.abs(x)


def _block_ref_rows(b, level):
    rows, width = b.shape
    half = 1 << level
    blk = 2 * half
    if blk >= 8:
        b3 = b.reshape(rows // blk, blk, width)
        ref = jnp.broadcast_to(b3[:, half - 1:half, :], b3.shape)
        return ref.reshape(rows, width)
    b3 = b.reshape(rows // 8, 8, width)
    sub = lax.broadcasted_iota(jnp.int32, b3.shape, 1)
    pick = lambda r: jnp.broadcast_to(b3[:, r:r + 1, :], b3.shape)
    if blk == 4:
        ref = jnp.where(sub < 4, pick(1), pick(5))
    else:
        ref = jnp.where(sub < 2, pick(0), jnp.where(sub < 4, pick(2), jnp.where(sub < 6, pick(4), pick(6))))
    return ref.reshape(rows, width)


def _gla_kernel(q_ref, k_ref, v_ref, z_ref, gl_ref, wg_ref, bg_ref, nw_ref, h_ref, wo_ref, fw_ref,
                y_ref, st_ref, mix_ref, *, nc, nchunks):
    c = pl.program_id(0)
    n = jnp.minimum(c, nchunks - 1) % nc

    @pl.when(c == 0)
    def _():
        mix_ref[...] = jnp.zeros_like(mix_ref)

    @pl.when(n == 0)
    def _():
        st_ref[...] = jnp.zeros_like(st_ref)

    ridx = lax.broadcasted_iota(jnp.int32, (CHUNK, 1), 0)
    ii = lax.broadcasted_iota(jnp.int32, (CHUNK, CHUNK), 0)
    jj = lax.broadcasted_iota(jnp.int32, (CHUNK, CHUNK), 1)
    pair_code = jnp.where(ii > jj, ii ^ jj, 0)
    eye = ii == jj
    tri = (ii >= jj).astype(BF16)

    gl1, gl2 = _split2(gl_ref[:, :GLA_RANK])
    wg1, wg2 = _split2(wg_ref[...])
    x = jnp.dot(jnp.concatenate([gl1, gl2, gl1], axis=1), jnp.concatenate([wg1, wg1, wg2], axis=0),
                preferred_element_type=F32) + bg_ref[...]
    log_a = (jnp.minimum(x, 0.0) - jnp.log(1.0 + jnp.exp(-jnp.abs(x)))) * (LOG2E / GLA_TAU)
    log_a = jnp.where(ridx >= jnp.where(n > 0, 0, PAD), log_a, 0.0)
    g1, g2 = _split2(log_a)
    b_all = jnp.dot(jnp.concatenate([tri, tri], axis=1), jnp.concatenate([g1, g2], axis=0),
                    preferred_element_type=F32)
    odd = (ridx & 1) == 1
    h2 = h_ref[...]

    def finish(heads, qs, ks, bs, scores):
        for i, h in enumerate(heads):
            vc = slice(h * GLA_DV, (h + 1) * GLA_DV)
            q, k, b = qs[i], ks[i], bs[i]
            v = v_ref[:, vc]
            b_last = b[CHUNK - 1:CHUNK, :]
            o = jnp.dot(scores[i].astype(BF16), v, preferred_element_type=F32)
            st = st_ref[h]
            qe = (q * jnp.exp2(b)).astype(BF16)
            o = o + lax.dot_general(qe, st.astype(BF16), (((1,), (1,)), ((), ())), preferred_element_type=F32)
            ke = (k * jnp.exp2(b_last - b)).astype(BF16)
            kv_t = lax.dot_general(v, ke, (((0,), (0,)), ((), ())), preferred_element_type=F32)
            st_ref[h] = st * jnp.exp2(b_last) + kv_t
            mix_ref[:, vc] = (_rms_scale(o, nw_ref[:, vc]) * z_ref[:, vc].astype(F32)).astype(BF16)

    pending = []
    for h0 in range(0, GLA_HEADS, 2):
        heads = (h0, h0 + 1)
        qs, ks, bs, scores = [], [], [], []
        for h in heads:
            kc = slice(h * GLA_DK, (h + 1) * GLA_DK)
            vc = slice(h * GLA_DV, (h + 1) * GLA_DV)
            h2 = h2 + jnp.dot(mix_ref[:, vc], wo_ref[vc, :], preferred_element_type=F32)
            qs.append(q_ref[:, kc].astype(F32))
            ks.append(k_ref[:, kc].astype(F32))
            bs.append(b_all[:, kc])
            scores.append(jnp.where(eye, jnp.sum(qs[-1] * ks[-1], axis=1, keepdims=True), 0.0))
        for lv in range(GLA_LEVELS):
            half = 1 << lv
            if lv == GLA_LEVELS // 2 and pending:
                finish(*pending.pop())
            zz = []
            for q, k, b in zip(qs, ks, bs):
                if lv == 0:
                    expo = jnp.where(odd, b - pltpu.roll(b, 1, 0), 0.0)
                else:
                    expo = _neg_abs(b - _block_ref_rows(b, lv))
                if half >= 8:
                    sel = jnp.concatenate([(q if r % 2 else k)[r * half:(r + 1) * half]
                                           for r in range(CHUNK // half)], axis=0)
                else:
                    sel = jnp.where(((ridx >> lv) & 1) == 1, q, k)
                zz.append((sel * jnp.exp2(expo)).astype(BF16))
            z2 = jnp.concatenate(zz, axis=0)
            gram = lax.dot_general(z2, z2, (((1,), (1,)), ((), ())), preferred_element_type=F32)
            mask = (pair_code >> lv) == 1
            for i in range(2):
                blk = gram[i * CHUNK:(i + 1) * CHUNK, i * CHUNK:(i + 1) * CHUNK]
                scores[i] = scores[i] + jnp.where(mask, blk, 0.0)
        pending.append((heads, qs, ks, bs, scores))
    finish(*pending.pop())

    y_ref[...] = _rms_scale(h2, fw_ref[...])


def _gla(proj, glow, w_gate, b_gate, norm_w, h1, w_out, final_w, bsz, nc):
    m, d = h1.shape
    nchunks = m // CHUNK
    const = lambda *shape: pl.BlockSpec(shape, lambda c: (0,) * len(shape))
    cur = lambda width, col: pl.BlockSpec((CHUNK, width), lambda c: (jnp.minimum(c, nchunks - 1), col))
    prev = lambda c: jnp.maximum(c - 1, 0)
    return pl.pallas_call(
        functools.partial(_gla_kernel, nc=nc, nchunks=nchunks),
        grid=(nchunks + 1,),
        in_specs=[cur(GLA_QK, 0), cur(GLA_QK, 1), cur(GLA_W, 1), cur(GLA_W, 2), cur(GATE_PAD, 0),
                  const(GLA_RANK, GLA_QK), const(1, GLA_QK), const(1, GLA_W),
                  pl.BlockSpec((CHUNK, d), lambda c: (prev(c), 0)),
                  pl.BlockSpec((GLA_W, d), lambda c: (0, 0), pipeline_mode=pl.Buffered(1)),
                  const(1, d)],
        out_specs=pl.BlockSpec((None, CHUNK, d),
                               lambda c: (prev(c) // nc, jnp.maximum(prev(c) % nc - 1, 0), 0)),
        out_shape=jax.ShapeDtypeStruct((bsz, (nc - 1) * CHUNK, d), F32),
        scratch_shapes=[pltpu.VMEM((GLA_HEADS, GLA_DV, GLA_DK), F32), pltpu.VMEM((CHUNK, GLA_W), BF16)],
        compiler_params=_cparams(("arbitrary",)),
        name="gla_out",
    )(proj, proj, proj, proj, glow, w_gate.astype(F32), b_gate.reshape(1, GLA_QK).astype(F32),
      norm_w.reshape(1, GLA_W).astype(F32), h1, w_out, final_w.reshape(1, d).astype(F32))


def _rope_tables(bsz, rows_per_batch):
    pos = np.maximum(np.arange(rows_per_batch, dtype=np.float64) - PAD, 0.0)
    inv_freq = np.power(ROPE_BASE, -np.arange(0, RET_DK, 2, dtype=np.float64) / RET_DK)
    ang = pos[:, None] * inv_freq[None, :]
    cos, sin = np.cos(ang), np.sin(ang)
    cos2 = np.tile(np.concatenate([cos, cos], axis=1), (bsz, 1)).astype(np.float32)
    sin2 = np.tile(np.concatenate([-sin, sin], axis=1), (bsz, 1)).astype(np.float32)
    return jnp.asarray(cos2), jnp.asarray(sin2)


def kernel(x, meta, norm_ab_w, w_in_ab, ret_norm_w, s5_lam_re, s5_lam_im, s5_log_dt, s5_b_re, s5_b_im,
           s5_c_re, s5_c_im, s5_d, s5_w_glu, w_out_ab, norm_c_w, w_in_c, gla_w_gate, gla_b_gate,
           gla_norm_w, w_out_c, final_norm_w):
    bsz, seq, d = x.shape
    assert seq % CHUNK == 0 and w_in_ab.shape[0] == 1 and w_in_c.shape[0] == 1
    rpb = seq + CHUNK
    nc = rpb // CHUNK
    assert nc % ROW_CHUNKS == 0
    nt = nc // ROW_CHUNKS
    m = bsz * rpb
    tm = _pick_tile(m, (2112, 1408, 768, 384, 256, 128))
    lead = jnp.concatenate([jnp.zeros((PAD, d), x.dtype), meta.astype(x.dtype)], axis=0)

    xn0 = _embed_norm(x, lead, norm_ab_w[0], nt)
    plain, silu = ("plain", 1.0), ("silu", 1.0)
    modes_ab = [("rope", 1.0), ("rope", RET_DK ** -0.5), plain, plain, silu, silu, plain, silu]
    proj0 = _in_proj(xn0, w_in_ab, modes_ab, tm, rope=_rope_tables(bsz, rpb))
    o_a = _retention(proj0, ret_norm_w[0], bsz, nt)
    t, tseg = _s5_time_tile(rpb)
    s5_params = _s5_params(s5_lam_re[0], s5_lam_im[0], s5_log_dt[0], s5_b_re[0], s5_b_im[0],
                           s5_c_re[0], s5_c_im[0])
    assert w_out_c.shape[0] == 1
    o_b, w_out_ab_bf, w_out_c_bf = _s5(proj0, bsz, s5_params, s5_d[0], s5_w_glu[0].astype(F32), t, tseg,
                                       w_out_ab[0].astype(F32), w_out_c[0].astype(F32))

    w_in_c_t = jnp.swapaxes(w_in_c, 1, 2)
    n_main = 2 * GLA_QK + 2 * GLA_W
    wg = jnp.pad(w_in_c_t[0, n_main:, :], ((0, GATE_PAD - GLA_RANK), (0, 0))).astype(BF16).T
    h1, xn1, glow = _out_ab(x, lead, o_a, o_b, w_out_ab_bf, norm_c_w[0], wg, nt)

    modes_c = [("plain", GLA_DK ** -0.5), plain, plain, plain, silu, silu]
    proj1 = _in_proj(xn1, w_in_c_t, modes_c, tm, w_is_transposed=True)
    return _gla(proj1, glow, gla_w_gate[0], gla_b_gate[0], gla_norm_w[0], h1, w_out_c_bf,
                final_norm_w, bsz, nc)
```
